```python
import math
import jax, jax.numpy as jnp
from jax import lax
import numpy as np

D_MODEL = 1024
BATCH = 8
SEQ = 2048
DEPTH = 1

MLA_HEADS = 8
MLA_NOPE_DIM = 64
MLA_ROPE_DIM = 32
MLA_QK_DIM = MLA_NOPE_DIM + MLA_ROPE_DIM
MLA_V_DIM = 64
MLA_WIDTH = MLA_HEADS * MLA_V_DIM
MLA_Q_RANK = 384
MLA_KV_RANK = 256
ROPE_THETA = 10000.0
Q_BLOCK = 128
GLA_HEADS = 4
GLA_DV = D_MODEL // 2
GLA_DK = GLA_DV // 2
GLA_HEAD_K = GLA_DK // GLA_HEADS
GLA_HEAD_V = GLA_DV // GLA_HEADS
GLA_GATE_RANK = 16
GLA_GATE_TAU = 16.0
GLA_CHUNK = 64
D_FF = -(-8 * D_MODEL // (3 * 256)) * 256
NORM_EPS = 1e-6
IN_SIZES = (MLA_Q_RANK, MLA_KV_RANK, MLA_ROPE_DIM,
            GLA_DK, GLA_DK, GLA_DV, GLA_GATE_RANK, GLA_DV,
            D_MODEL, D_MODEL)
D_IN = sum(IN_SIZES)

kernel_name = "hybrid_mla_gla_gated_block"


def rmsnorm(x, w):
    xf = x.astype(jnp.float32)
    y = xf * lax.rsqrt(jnp.mean(xf * xf, axis=-1, keepdims=True) + NORM_EPS)
    return (y * w.astype(jnp.float32)).astype(x.dtype)


def rope_tables(positions):
    half = MLA_ROPE_DIM // 2
    inv = 1.0 / (ROPE_THETA ** (jnp.arange(half, dtype=jnp.float32) / half))
    ang = positions.astype(jnp.float32)[..., None] * inv
    return jnp.cos(ang), jnp.sin(ang)


def apply_rope(x, cos, sin):
    x1, x2 = jnp.split(x.astype(jnp.float32), 2, axis=-1)
    return jnp.concatenate([x1 * cos - x2 * sin, x2 * cos + x1 * sin], axis=-1).astype(x.dtype)


def mla(c_q, c_kv, k_rope, cos, sin, norm_cq, w_uq, norm_ckv, w_ukv):
    B, S, _ = c_q.shape
    H = MLA_HEADS
    q = (rmsnorm(c_q, norm_cq) @ w_uq).reshape(B, S, H, MLA_QK_DIM)
    q_nope, q_rope = q[..., :MLA_NOPE_DIM], q[..., MLA_NOPE_DIM:]
    q_rope = apply_rope(q_rope, cos[:, :, None], sin[:, :, None])
    kv = (rmsnorm(c_kv, norm_ckv) @ w_ukv).reshape(B, S, H, MLA_NOPE_DIM + MLA_V_DIM)
    k_nope, v = kv[..., :MLA_NOPE_DIM], kv[..., MLA_NOPE_DIM:]
    k_rope = apply_rope(k_rope, cos, sin)
    k = jnp.concatenate([k_nope, jnp.broadcast_to(k_rope[:, :, None], (B, S, H, MLA_ROPE_DIM)).astype(k_nope.dtype)], axis=-1)
    q = jnp.concatenate([q_nope, q_rope], axis=-1) * (MLA_QK_DIM ** -0.5)
    nb = S // Q_BLOCK
    q_blocks = q.reshape(B, nb, Q_BLOCK, H, MLA_QK_DIM).transpose(1, 0, 2, 3, 4)
    key_idx = jnp.arange(S)

    def attend(args):
        q_blk, start = args
        s = jnp.einsum('bqhd,bkhd->bhqk', q_blk, k).astype(jnp.float32)
        q_idx = start + jnp.arange(Q_BLOCK)
        mask = key_idx[None, :] <= q_idx[:, None]
        s = jnp.where(mask, s, -jnp.inf)
        p = jax.nn.softmax(s, axis=-1).astype(v.dtype)
        return jnp.einsum('bhqk,bkhd->bqhd', p, v)

    o = lax.map(attend, (q_blocks, jnp.arange(nb) * Q_BLOCK))
    return o.transpose(1, 0, 2, 3, 4).reshape(B, S, MLA_WIDTH)


def gla(q, k, v, g_lr, og, w_gate2, b_gate, norm_o):
    B, S, _ = q.shape
    H, C = GLA_HEADS, GLA_CHUNK
    n = S // C
    log_a = jax.nn.log_sigmoid((g_lr @ w_gate2 + b_gate).astype(jnp.float32)) / GLA_GATE_TAU

    def heads(t, d):
        return t.reshape(B, S, H, d).transpose(0, 2, 1, 3).astype(jnp.float32)

    def chunks(t):
        return t.reshape(B, H, n, C, t.shape[-1]).transpose(2, 0, 1, 3, 4)

    qh = heads(q, GLA_HEAD_K) * (GLA_HEAD_K ** -0.5)
    kh = heads(k, GLA_HEAD_K)
    vh = heads(v, GLA_HEAD_V)
    gh = heads(log_a, GLA_HEAD_K)
    bcum = jnp.cumsum(chunks(gh), axis=3)
    causal = jnp.tril(jnp.ones((C, C), dtype=bool))

    def step(state, xs):
        qc, kc, vc, bc = xs
        o_inter = jnp.einsum('bhcd,bhde->bhce', qc * jnp.exp(bc), state)
        diff = bc[:, :, :, None, :] - bc[:, :, None, :, :]
        decay = jnp.exp(jnp.where(causal[:, :, None], diff, -jnp.inf))
        attn = jnp.einsum('bhid,bhjd,bhijd->bhij', qc, kc, decay)
        o_intra = jnp.einsum('bhij,bhje->bhie', attn, vc)
        b_last = bc[:, :, -1:, :]
        k_dec = kc * jnp.exp(b_last - bc)
        state = jnp.exp(b_last[:, :, 0, :])[..., None] * state + jnp.einsum('bhcd,bhce->bhde', k_dec, vc)
        return state, o_inter + o_intra

    state0 = jnp.zeros((B, H, GLA_HEAD_K, GLA_HEAD_V), jnp.float32)
    _, o = lax.scan(step, state0, (chunks(qh), chunks(kh), chunks(vh), bcum))
    o = o.transpose(1, 0, 3, 2, 4).reshape(B, S, H, GLA_HEAD_V)
    o = rmsnorm(o, norm_o).reshape(B, S, GLA_DV)
    o = o * jax.nn.silu(og.astype(jnp.float32))
    return o.astype(q.dtype)


def setup_inputs(seed: int = 0) -> dict:
    key = jax.random.key(seed)
    ks = jax.random.split(key, 24)
    L, D = DEPTH, D_MODEL

    def w(k, shape, fan_in):
        return jax.random.normal(k, shape, jnp.float32) * (fan_in ** -0.5)

    def gain(k, shape):
        return 1.0 + 0.02 * jax.random.normal(k, shape, jnp.float32)

    return {
        "x": jax.random.normal(ks[0], (BATCH, SEQ, D), jnp.float32),
        "positions": jnp.broadcast_to(jnp.arange(SEQ, dtype=jnp.int32), (BATCH, SEQ)),
        "ln_mix": gain(ks[1], (L, D)),
        "w_in": w(ks[2], (L, D, D_IN), D),
        "mla_norm_cq": gain(ks[3], (L, MLA_Q_RANK)),
        "mla_w_uq": w(ks[4], (L, MLA_Q_RANK, MLA_HEADS * MLA_QK_DIM), MLA_Q_RANK),
        "mla_norm_ckv": gain(ks[5], (L, MLA_KV_RANK)),
        "mla_w_ukv": w(ks[6], (L, MLA_KV_RANK, MLA_HEADS * (MLA_NOPE_DIM + MLA_V_DIM)), MLA_KV_RANK),
        "mla_w_o": w(ks[7], (L, MLA_WIDTH, D), MLA_WIDTH),
        "gla_w_gate2": w(ks[8], (L, GLA_GATE_RANK, GLA_DK), GLA_GATE_RANK),
        "gla_b_gate": 0.1 * jax.random.normal(ks[9], (L, GLA_DK), jnp.float32),
        "gla_norm": gain(ks[10], (L, GLA_HEAD_V)),
        "gla_w_o": w(ks[11], (L, GLA_DV, D), GLA_DV),
        "w_out": w(ks[12], (L, D, D), D),
        "ln_ffn": gain(ks[13], (L, D)),
        "ffn_w_gate": w(ks[14], (L, D, D_FF), D),
        "ffn_w_up": w(ks[15], (L, D, D_FF), D),
        "ffn_w_down": w(ks[16], (L, D_FF, D), D_FF),
        "final_norm": gain(ks[17], (D,)),
    }


def reference(x, positions, ln_mix, w_in, mla_norm_cq, mla_w_uq, mla_norm_ckv, mla_w_ukv, mla_w_o,
              gla_w_gate2, gla_b_gate, gla_norm, gla_w_o, w_out, ln_ffn, ffn_w_gate, ffn_w_up,
              ffn_w_down, final_norm):
    offsets = np.cumsum(IN_SIZES)[:-1].tolist()
    cos, sin = rope_tables(positions)
    h = x
    for l in range(DEPTH):
        u = rmsnorm(h, ln_mix[l])
        z = u @ w_in[l]
        c_q, c_kv, k_rope, g_q, g_k, g_v, g_lr, g_og, gate_a, gate_b = jnp.split(z, offsets, axis=-1)
        y_a = mla(c_q, c_kv, k_rope, cos, sin, mla_norm_cq[l], mla_w_uq[l], mla_norm_ckv[l], mla_w_ukv[l]) @ mla_w_o[l]
        y_b = gla(g_q, g_k, g_v, g_lr, g_og, gla_w_gate2[l], gla_b_gate[l], gla_norm[l]) @ gla_w_o[l]
        mix = jax.nn.sigmoid(gate_a) * y_a + jax.nn.sigmoid(gate_b) * y_b
        h = h + mix @ w_out[l]
        u = rmsnorm(h, ln_ffn[l])
        h = h + (jax.nn.silu(u @ ffn_w_gate[l]) * (u @ ffn_w_up[l])) @ ffn_w_down[l]
    return rmsnorm(h, final_norm)
```

```python
import functools

import jax
import jax.numpy as jnp
from jax import lax
from jax.experimental import pallas as pl
from jax.experimental.pallas import tpu as pltpu

F32 = jnp.float32
BF16 = jnp.bfloat16

D_MODEL = 1024
MLA_HEADS = 8
MLA_NOPE = 64
MLA_ROPE = 32
MLA_QK = MLA_NOPE + MLA_ROPE
MLA_V = 64
MLA_WIDTH = MLA_HEADS * MLA_V
MLA_Q_RANK = 384
MLA_KV_RANK = 256
ROPE_THETA = 10000.0
GLA_HEADS = 4
GLA_DV = 512
GLA_DK = 256
GLA_HK = GLA_DK // GLA_HEADS
GLA_HV = GLA_DV // GLA_HEADS
GLA_RANK = 16
GLA_TAU = 16.0
D_FF = 2816
EPS = 1e-6
IN_SIZES = (MLA_Q_RANK, MLA_KV_RANK, MLA_ROPE, GLA_DK, GLA_DK, GLA_DV, GLA_RANK, GLA_DV, D_MODEL, D_MODEL)

LANES = 128
HEAD_PAD = LANES
ROPE_LO = MLA_NOPE
ROPE_HALF = MLA_ROPE // 2
GLR_LO = ROPE_LO + MLA_ROPE

OFF_CQ = 0
OFF_CKV = OFF_CQ + MLA_Q_RANK
OFF_GQ = OFF_CKV + MLA_KV_RANK
OFF_GK = OFF_GQ + GLA_DK
OFF_GV = OFF_GK + GLA_DK
OFF_OG = OFF_GV + GLA_DV
OFF_GA = OFF_OG + GLA_DV
OFF_GB = OFF_GA + D_MODEL
OFF_MISC = OFF_GB + D_MODEL
D_IN_PACKED = OFF_MISC + LANES

GLA_CHUNK = 64
GLA_SUB = 16
VMEM_LIMIT = 56 * 1024 * 1024


def _dot(a, b):
    return jnp.dot(a, b, preferred_element_type=F32)


def _dot_nt(a, b):
    return lax.dot_general(a, b, (((1,), (1,)), ((), ())), preferred_element_type=F32)


def _dot_tn(a, b):
    return lax.dot_general(a, b, (((0,), (0,)), ((), ())), preferred_element_type=F32)


def _rms(x, w):
    return x * lax.rsqrt(jnp.mean(x * x, axis=-1, keepdims=True) + EPS) * w


def _proj_kernel(x_ref, pos_ref, lnmix_ref, win_ref, ncq_ref, wuq_ref, nckv_ref, wk_ref, wv_ref,
                 wg2_ref, bg_ref, invf_ref,
                 q_ref, k_ref, v_ref, gq_ref, gk_ref, gv_ref, la_ref, og_ref, sa_ref, sb_ref):
    ub = _rms(x_ref[...], lnmix_ref[...]).astype(BF16)

    def seg(lo, width):
        return _dot(ub, win_ref[:, lo:lo + width])

    lane = lax.broadcasted_iota(jnp.int32, (1, LANES), 1)
    ang = pos_ref[...] * invf_ref[...]
    cos = jnp.cos(ang)
    sin = jnp.sin(ang)
    in_x1 = (lane >= ROPE_LO) & (lane < ROPE_LO + ROPE_HALF)
    in_x2 = (lane >= ROPE_LO + ROPE_HALF) & (lane < ROPE_LO + MLA_ROPE)
    sin_x1 = jnp.where(in_x1, -sin, 0.0)
    sin_x2 = jnp.where(in_x2, sin, 0.0)

    def rope(t):
        return (t * cos + pltpu.roll(t, LANES - ROPE_HALF, axis=1) * sin_x1
                + pltpu.roll(t, ROPE_HALF, axis=1) * sin_x2)

    misc = seg(OFF_MISC, LANES)
    k_rope = rope(jnp.where(in_x1 | in_x2, misc, 0.0))

    cqn = _rms(seg(OFF_CQ, MLA_Q_RANK), ncq_ref[...]).astype(BF16)
    q = _dot(cqn, wuq_ref[...]) * (MLA_QK ** -0.5)
    ckvn = _rms(seg(OFF_CKV, MLA_KV_RANK), nckv_ref[...]).astype(BF16)
    k_nope = _dot(ckvn, wk_ref[...])
    for h in range(MLA_HEADS):
        sl = slice(h * HEAD_PAD, (h + 1) * HEAD_PAD)
        q_ref[:, sl] = rope(q[:, sl]).astype(BF16)
        k_ref[:, sl] = (k_nope[:, sl] + k_rope).astype(BF16)
    v_ref[...] = _dot(ckvn, wv_ref[...]).astype(BF16)

    gq_ref[...] = (seg(OFF_GQ, GLA_DK) * (GLA_HK ** -0.5)).astype(BF16)
    gk_ref[...] = seg(OFF_GK, GLA_DK).astype(BF16)
    gv_ref[...] = seg(OFF_GV, GLA_DV).astype(BF16)
    og_ref[...] = seg(OFF_OG, GLA_DV).astype(BF16)
    xg = _dot(misc.astype(BF16), wg2_ref[...]) + bg_ref[...]
    la_ref[...] = (jnp.minimum(xg, 0.0) - jnp.log1p(jnp.exp(-jnp.abs(xg)))) * (1.0 / GLA_TAU)
    sa_ref[...] = jax.nn.sigmoid(seg(OFF_GA, D_MODEL)).astype(BF16)
    sb_ref[...] = jax.nn.sigmoid(seg(OFF_GB, D_MODEL)).astype(BF16)


def _const_spec(shape):
    return pl.BlockSpec(shape, lambda *_: (0,) * len(shape))


def _proj_call(x2, pos2, lnmix, winp, ncq, wuqp, nckv, wkp, wvp, wg2p, bg, invf, tm):
    t = x2.shape[0]
    row = lambda w: pl.BlockSpec((tm, w), lambda i: (i, 0))
    consts = (lnmix, winp, ncq, wuqp, nckv, wkp, wvp, wg2p, bg, invf)
    out_widths = (MLA_HEADS * HEAD_PAD, MLA_HEADS * HEAD_PAD, MLA_WIDTH, GLA_DK, GLA_DK, GLA_DV, GLA_DK,
                  GLA_DV, D_MODEL, D_MODEL)
    out_dtypes = (BF16, BF16, BF16, BF16, BF16, BF16, F32, BF16, BF16, BF16)
    return pl.pallas_call(
        _proj_kernel,
        grid=(t // tm,),
        in_specs=[row(D_MODEL), row(1)] + [_const_spec(c.shape) for c in consts],
        out_specs=[row(w) for w in out_widths],
        out_shape=[jax.ShapeDtypeStruct((t, w), d) for w, d in zip(out_widths, out_dtypes)],
        compiler_params=pltpu.CompilerParams(dimension_semantics=("parallel",), vmem_limit_bytes=VMEM_LIMIT),
        name="proj",
    )(x2, pos2, *consts)


def _attn_kernel(q_ref, k_ref, v_ref, o_ref, *, tq):
    i = pl.program_id(1)
    row = lax.broadcasted_iota(jnp.int32, (tq, tq), 0)
    col = lax.broadcasted_iota(jnp.int32, (tq, tq), 1)
    causal = col <= row
    lane = lax.broadcasted_iota(jnp.int32, (1, 2 * MLA_V), 1)

    def one_head(h):
        q = q_ref[:, h * HEAD_PAD:(h + 1) * HEAD_PAD]
        pair = slice((h // 2) * 2 * MLA_V, (h // 2 + 1) * 2 * MLA_V)

        def step(j, carry, masked):
            m, l, acc = carry
            kb = k_ref[pl.ds(j * tq, tq), h * HEAD_PAD:(h + 1) * HEAD_PAD]
            vb = v_ref[pl.ds(j * tq, tq), pair]
            s = _dot_nt(q, kb)
            if masked:
                s = jnp.where(causal, s, -1e30)
            m_new = jnp.maximum(m, jnp.max(s, axis=-1, keepdims=True))
            alpha = jnp.exp(m - m_new)
            p = jnp.exp(s - m_new)
            l = alpha * l + jnp.sum(p, axis=-1, keepdims=True)
            acc = alpha * acc + _dot(p.astype(BF16), vb)
            return m_new, l, acc

        init = (jnp.full((tq, 1), -1e30, F32), jnp.zeros((tq, 1), F32), jnp.zeros((tq, 2 * MLA_V), F32))
        carry = lax.fori_loop(0, i, lambda j, c: step(j, c, False), init)
        _, l, acc = step(i, carry, True)
        return acc / l

    for g in range(MLA_HEADS // 2):
        even = one_head(2 * g)
        odd = one_head(2 * g + 1)
        o_ref[:, g * 2 * MLA_V:(g + 1) * 2 * MLA_V] = jnp.where(lane < MLA_V, even, odd).astype(BF16)


def _attn_call(q, k, v, batch, seq, tq):
    nq = seq // tq
    return pl.pallas_call(
        functools.partial(_attn_kernel, tq=tq),
        grid=(batch, nq),
        in_specs=[pl.BlockSpec((tq, MLA_HEADS * HEAD_PAD), lambda b, i: (b * nq + i, 0)),
                  pl.BlockSpec((seq, MLA_HEADS * HEAD_PAD), lambda b, i: (b, 0)),
                  pl.BlockSpec((seq, MLA_WIDTH), lambda b, i: (b, 0))],
        out_specs=pl.BlockSpec((tq, MLA_WIDTH), lambda b, i: (b * nq + i, 0)),
        out_shape=jax.ShapeDtypeStruct((batch * seq, MLA_WIDTH), BF16),
        compiler_params=pltpu.CompilerParams(dimension_semantics=("parallel", "arbitrary"),
                                             vmem_limit_bytes=VMEM_LIMIT),
        name="mla_attn",
    )(q, k, v)


def _gla_kernel(gq_ref, gk_ref, gv_ref, la_ref, og_ref, norm_ref, o_ref, state_ref, kpad_ref, lapad_ref,
                *, chunks_per_step):
    c_, s_ = GLA_CHUNK, GLA_SUB
    nsub = c_ // s_

    @pl.when(pl.program_id(1) == 0)
    def _():
        state_ref[...] = jnp.zeros_like(state_ref)

    kpad_ref[0:s_, :] = jnp.zeros((s_, GLA_DK), F32)
    lapad_ref[0:s_, :] = jnp.zeros((s_, GLA_DK), F32)

    ri = lax.broadcasted_iota(jnp.int32, (c_, c_), 0)
    ci = lax.broadcasted_iota(jnp.int32, (c_, c_), 1)
    tri = jnp.where((ri // s_ == ci // s_) & (ci <= ri), 1.0, 0.0).astype(BF16)
    rl = lax.broadcasted_iota(jnp.int32, (c_, GLA_DK), 0)
    ll = lax.broadcasted_iota(jnp.int32, (c_, GLA_DK), 1)
    sub_pos = rl % s_
    row_minus_col = rl - ll % GLA_HK
    head_of_lane = ll // GLA_HK
    di = lax.broadcasted_iota(jnp.int32, (GLA_DK, GLA_DK), 0) // GLA_HK
    dj = lax.broadcasted_iota(jnp.int32, (GLA_DK, GLA_DK), 1) // GLA_HK
    head_sum = jnp.where(di == dj, 1.0, 0.0).astype(BF16)
    norm_w = norm_ref[...]

    def blocks(fn):
        return jnp.concatenate([fn(b) for b in range(nsub)], axis=0)

    def chunk(c, _):
        r0 = pl.multiple_of(c * c_, c_)
        rows = pl.ds(r0, c_)
        q = gq_ref[rows, :].astype(F32)
        k = gk_ref[rows, :].astype(F32)
        v = gv_ref[rows, :]
        la = la_ref[rows, :]

        hi = la.astype(BF16)
        r1 = la - hi.astype(F32)
        mid = r1.astype(BF16)
        lo = (r1 - mid.astype(F32)).astype(BF16)
        l16 = _dot(tri, hi) + _dot(tri, mid) + _dot(tri, lo)
        tot = [l16[(b + 1) * s_ - 1:(b + 1) * s_, :] for b in range(nsub)]
        q16 = q * jnp.exp(l16)
        k16 = k * jnp.exp(blocks(lambda b: jnp.broadcast_to(tot[b], (s_, GLA_DK))) - l16)

        def sub(t, b):
            return t[b * s_:(b + 1) * s_, :]

        zero = jnp.zeros((s_, GLA_DK), F32)
        e = [jnp.exp(t) for t in tot]
        pre = [None, e[0], e[0] * e[1], e[0] * e[1] * e[2]]
        post = [e[1] * e[2] * e[3], e[2] * e[3], e[3], None]
        qd = blocks(lambda b: sub(q16, b) if pre[b] is None else sub(q16, b) * pre[b]).astype(BF16)
        kd = blocks(lambda b: sub(k16, b) if post[b] is None else sub(k16, b) * post[b]).astype(BF16)
        q_lv = [jnp.concatenate([zero, zero, sub(q16, 2), sub(q16, 3) * e[2]], axis=0),
                jnp.concatenate([zero, sub(q16, 1), zero, zero], axis=0),
                jnp.concatenate([zero, zero, zero, sub(q16, 3)], axis=0)]
        k_lv = [jnp.concatenate([sub(k16, 0) * e[1], sub(k16, 1), zero, zero], axis=0),
                jnp.concatenate([sub(k16, 0), zero, zero, zero], axis=0),
                jnp.concatenate([zero, zero, sub(k16, 2), zero], axis=0)]
        q_lv = [t.astype(BF16) for t in q_lv]
        k_lv = [t.astype(BF16) for t in k_lv]

        kpad_ref[s_:, :] = k
        lapad_ref[s_:, :] = la
        decay = jnp.zeros((c_, GLA_DK), F32)
        prods = []
        for d in range(s_):
            if d > 0:
                decay = decay + lapad_ref[s_ - d + 1:s_ - d + 1 + c_, :]
            k_sh = kpad_ref[s_ - d:s_ - d + c_, :]
            prods.append(jnp.where(sub_pos >= d, q * k_sh * jnp.exp(decay), 0.0).astype(BF16))
        summed = _dot(jnp.concatenate(prods, axis=0), head_sum)
        attn_diag = jnp.zeros((c_, GLA_DK), F32)
        for d in range(s_):
            attn_diag = jnp.where(row_minus_col == d, summed[d * c_:(d + 1) * c_, :], attn_diag)

        state = state_ref[...]
        state_b = state.astype(BF16)
        new_state = state * jnp.exp(tot[0] + tot[1] + tot[2] + tot[3])
        for h in range(GLA_HEADS):
            in_head = head_of_lane == h
            vh = v[:, h * GLA_HV:(h + 1) * GLA_HV]
            attn = attn_diag[:, h * GLA_HK:(h + 1) * GLA_HK]
            for ql, kl in zip(q_lv, k_lv):
                attn = attn + _dot_nt(jnp.where(in_head, ql, 0.0), kl)
            o = _dot(attn.astype(BF16), vh) + _dot_nt(jnp.where(in_head, qd, 0.0), state_b)
            o = _rms(o, norm_w)
            g = og_ref[rows, h * GLA_HV:(h + 1) * GLA_HV].astype(F32)
            o_ref[rows, h * GLA_HV:(h + 1) * GLA_HV] = (o * (g * jax.nn.sigmoid(g))).astype(BF16)
            new_state = new_state + _dot_tn(vh, jnp.where(in_head, kd, 0.0))
        state_ref[...] = new_state
        return 0

    lax.fori_loop(0, chunks_per_step, chunk, 0)


def _gla_call(gq, gk, gv, la, og, norm, batch, seq, rows_per_step):
    ns = seq // rows_per_step
    row = lambda w: pl.BlockSpec((rows_per_step, w), lambda b, i: (b * ns + i, 0))
    return pl.pallas_call(
        functools.partial(_gla_kernel, chunks_per_step=rows_per_step // GLA_CHUNK),
        grid=(batch, ns),
        in_specs=[row(GLA_DK), row(GLA_DK), row(GLA_DV), row(GLA_DK), row(GLA_DV), _const_spec(norm.shape)],
        out_specs=row(GLA_DV),
        out_shape=jax.ShapeDtypeStruct((batch * seq, GLA_DV), BF16),
        scratch_shapes=[pltpu.VMEM((GLA_HV, GLA_DK), F32),
                        pltpu.VMEM((GLA_SUB + GLA_CHUNK, GLA_DK), F32),
                        pltpu.VMEM((GLA_SUB + GLA_CHUNK, GLA_DK), F32)],
        compiler_params=pltpu.CompilerParams(dimension_semantics=("parallel", "arbitrary"),
                                             vmem_limit_bytes=VMEM_LIMIT),
        name="gla",
    )(gq, gk, gv, la, og, norm)


def _post_kernel(x_ref, oa_ref, ob_ref, sa_ref, sb_ref, woa_ref, wob_ref, wout_ref, lnffn_ref,
                 wg_ref, wu_ref, wd_ref, fnorm_ref, out_ref):
    ya = _dot(oa_ref[...], woa_ref[...])
    yb = _dot(ob_ref[...], wob_ref[...])
    mix = sa_ref[...].astype(F32) * ya + sb_ref[...].astype(F32) * yb
    h = x_ref[...] + _dot(mix.astype(BF16), wout_ref[...])
    u = _rms(h, lnffn_ref[...]).astype(BF16)
    g = _dot(u, wg_ref[...])
    act = (g * jax.nn.sigmoid(g) * _dot(u, wu_ref[...])).astype(BF16)
    h = h + _dot(act, wd_ref[...])
    out_ref[...] = _rms(h, fnorm_ref[...])


def _post_call(x2, oa, ob, sa, sb, woa, wob, wout, lnffn, wg, wu, wd, fnorm, tm):
    t = x2.shape[0]
    row = lambda w: pl.BlockSpec((tm, w), lambda i: (i, 0))
    single = lambda a: pl.BlockSpec(a.shape, lambda i: (0,) * a.ndim, pipeline_mode=pl.Buffered(1))
    consts = (woa, wob, wout, lnffn, wg, wu, wd, fnorm)
    return pl.pallas_call(
        _post_kernel,
        grid=(t // tm,),
        in_specs=[row(D_MODEL), row(MLA_WIDTH), row(GLA_DV), row(D_MODEL), row(D_MODEL)]
                 + [single(c) for c in consts],
        out_specs=row(D_MODEL),
        out_shape=jax.ShapeDtypeStruct((t, D_MODEL), F32),
        compiler_params=pltpu.CompilerParams(dimension_semantics=("parallel",), vmem_limit_bytes=VMEM_LIMIT),
        name="post",
    )(x2, oa, ob, sa, sb, *consts)


def _pack_weights(w_in, mla_w_uq, mla_w_ukv, gla_w_gate2):
    offs = [0]
    for s in IN_SIZES:
        offs.append(offs[-1] + s)
    col = lambda idx: w_in[:, offs[idx]:offs[idx + 1]]
    d = w_in.shape[0]
    misc = jnp.concatenate([jnp.zeros((d, ROPE_LO), F32), col(2), col(6),
                            jnp.zeros((d, LANES - GLR_LO - GLA_RANK), F32)], axis=1)
    winp = jnp.concatenate([col(0), col(1), col(3), col(4), col(5), col(7), col(8), col(9), misc],
                           axis=1).astype(BF16)
    wuq = mla_w_uq.reshape(MLA_Q_RANK, MLA_HEADS, MLA_QK)
    wuqp = jnp.pad(wuq, ((0, 0), (0, 0), (0, HEAD_PAD - MLA_QK))).reshape(MLA_Q_RANK, MLA_HEADS * HEAD_PAD)
    wukv = mla_w_ukv.reshape(MLA_KV_RANK, MLA_HEADS, MLA_NOPE + MLA_V)
    wkp = jnp.pad(wukv[:, :, :MLA_NOPE], ((0, 0), (0, 0), (0, HEAD_PAD - MLA_NOPE)))
    wkp = wkp.reshape(MLA_KV_RANK, MLA_HEADS * HEAD_PAD)
    wvp = wukv[:, :, MLA_NOPE:].reshape(MLA_KV_RANK, MLA_WIDTH)
    wg2p = jnp.pad(gla_w_gate2, ((GLR_LO, LANES - GLR_LO - GLA_RANK), (0, 0)))
    return winp, wuqp.astype(BF16), wkp.astype(BF16), wvp.astype(BF16), wg2p.astype(BF16)


def _rope_inv_freq():
    half = ROPE_HALF
    inv = 1.0 / (ROPE_THETA ** (jnp.arange(half, dtype=F32) / half))
    return jnp.concatenate([jnp.zeros((ROPE_LO,), F32), inv, inv,
                            jnp.zeros((LANES - ROPE_LO - MLA_ROPE,), F32)]).reshape(1, LANES)


def _tile(n, pref):
    while n % pref:
        pref //= 2
    return pref


def kernel(x, positions, ln_mix, w_in, mla_norm_cq, mla_w_uq, mla_norm_ckv, mla_w_ukv, mla_w_o, gla_w_gate2,
           gla_b_gate, gla_norm, gla_w_o, w_out, ln_ffn, ffn_w_gate, ffn_w_up, ffn_w_down, final_norm):
    batch, seq, d = x.shape
    assert d == D_MODEL and w_in.shape[0] == 1 and seq % GLA_CHUNK == 0
    t = batch * seq
    x2 = x.reshape(t, d)
    pos2 = positions.reshape(t, 1).astype(F32)
    winp, wuqp, wkp, wvp, wg2p = _pack_weights(w_in[0], mla_w_uq[0], mla_w_ukv[0], gla_w_gate2[0])
    r = lambda a: a.reshape(1, -1)

    q, k, v, gq, gk, gv, la, og, sa, sb = _proj_call(
        x2, pos2, r(ln_mix[0]), winp, r(mla_norm_cq[0]), wuqp, r(mla_norm_ckv[0]), wkp, wvp, wg2p,
        r(gla_b_gate[0]), _rope_inv_freq(), _tile(t, 256))
    oa = _attn_call(q, k, v, batch, seq, _tile(seq, 256))
    ob = _gla_call(gq, gk, gv, la, og, r(gla_norm[0]), batch, seq, _tile(seq, 512))
    out = _post_call(x2, oa, ob, sa, sb, mla_w_o[0].astype(BF16), gla_w_o[0].astype(BF16),
                     w_out[0].astype(BF16), r(ln_ffn[0]), ffn_w_gate[0].astype(BF16),
                     ffn_w_up[0].astype(BF16), ffn_w_down[0].astype(BF16), r(final_norm), _tile(t, 256))
    return out.reshape(batch, seq, d)
```

```python
import functools

import jax
import jax.numpy as jnp
from jax import lax
from jax.experimental import pallas as pl
from jax.experimental.pallas import tpu as pltpu

F32 = jnp.float32
BF16 = jnp.bfloat16

D_MODEL = 1024
MLA_HEADS = 8
MLA_NOPE = 64
MLA_ROPE = 32
MLA_QK = MLA_NOPE + MLA_ROPE
MLA_V = 64
MLA_WIDTH = MLA_HEADS * MLA_V
MLA_Q_RANK = 384
MLA_KV_RANK = 256
ROPE_THETA = 10000.0
GLA_HEADS = 4
GLA_DV = 512
GLA_DK = 256
GLA_HK = GLA_DK // GLA_HEADS
GLA_HV = GLA_DV // GLA_HEADS
GLA_RANK = 16
GLA_TAU = 16.0
D_FF = 2816
EPS = 1e-6
LOG2E = 1.4426950408889634
IN_SIZES = (MLA_Q_RANK, MLA_KV_RANK, MLA_ROPE, GLA_DK, GLA_DK, GLA_DV, GLA_RANK, GLA_DV, D_MODEL, D_MODEL)

LANES = 128
HEAD_PAD = LANES
ROPE_LO = MLA_NOPE
ROPE_HALF = MLA_ROPE // 2
GLR_LO = ROPE_LO + MLA_ROPE

OFF_CQ = 0
OFF_CKV = OFF_CQ + MLA_Q_RANK
OFF_GQ = OFF_CKV + MLA_KV_RANK
OFF_GK = OFF_GQ + GLA_DK
OFF_GV = OFF_GK + GLA_DK
OFF_OG = OFF_GV + GLA_DV
OFF_GA = OFF_OG + GLA_DV
OFF_GB = OFF_GA + D_MODEL
OFF_MISC = OFF_GB + D_MODEL
D_IN_PACKED = OFF_MISC + LANES

ATTN_TILE = 256
ATTN_LOOKAHEAD = 8
GLA_CHUNK = 64
GLA_SUB = 16
VMEM_LIMIT = 56 * 1024 * 1024


def _dot(a, b):
    return jnp.dot(a, b, preferred_element_type=F32)


def _dot_nt(a, b):
    return lax.dot_general(a, b, (((1,), (1,)), ((), ())), preferred_element_type=F32)


def _dot_tn(a, b):
    return lax.dot_general(a, b, (((0,), (0,)), ((), ())), preferred_element_type=F32)


def _rms(x, w):
    return x * lax.rsqrt(jnp.mean(x * x, axis=-1, keepdims=True) + EPS) * w


def _proj_kernel(x_ref, pos_ref, lnmix_ref, win_ref, ncq_ref, wuq_ref, nckv_ref, wk_ref, wvt_ref,
                 wg2_ref, bg_ref, invf_ref,
                 q_ref, k_ref, vt_ref, gq_ref, gk_ref, gv_ref, la_ref, og_ref, sa_ref, sb_ref):
    ub = _rms(x_ref[...], lnmix_ref[...]).astype(BF16)

    def seg(lo, width):
        return _dot(ub, win_ref[:, lo:lo + width])

    lane = lax.broadcasted_iota(jnp.int32, (1, LANES), 1)
    ang = pos_ref[...] * invf_ref[...]
    cos = jnp.cos(ang)
    sin = jnp.sin(ang)
    in_x1 = (lane >= ROPE_LO) & (lane < ROPE_LO + ROPE_HALF)
    in_x2 = (lane >= ROPE_LO + ROPE_HALF) & (lane < ROPE_LO + MLA_ROPE)
    sin_x1 = jnp.where(in_x1, -sin, 0.0)
    sin_x2 = jnp.where(in_x2, sin, 0.0)

    def rope(t):
        return (t * cos + pltpu.roll(t, LANES - ROPE_HALF, axis=1) * sin_x1
                + pltpu.roll(t, ROPE_HALF, axis=1) * sin_x2)

    misc = seg(OFF_MISC, LANES)
    k_rope = rope(jnp.where(in_x1 | in_x2, misc, 0.0))

    cqn = _rms(seg(OFF_CQ, MLA_Q_RANK), ncq_ref[...]).astype(BF16)
    q = _dot(cqn, wuq_ref[...]) * (MLA_QK ** -0.5 * LOG2E)
    ckvn = _rms(seg(OFF_CKV, MLA_KV_RANK), nckv_ref[...]).astype(BF16)
    k_nope = _dot(ckvn, wk_ref[...])
    for h in range(MLA_HEADS):
        sl = slice(h * HEAD_PAD, (h + 1) * HEAD_PAD)
        q_ref[:, sl] = rope(q[:, sl]).astype(BF16)
        k_ref[:, sl] = (k_nope[:, sl] + k_rope).astype(BF16)
    vt_ref[0] = _dot_nt(wvt_ref[...], ckvn).astype(BF16)

    gq_ref[...] = (seg(OFF_GQ, GLA_DK) * (GLA_HK ** -0.5)).astype(BF16)
    gk_ref[...] = seg(OFF_GK, GLA_DK).astype(BF16)
    gv_ref[...] = seg(OFF_GV, GLA_DV).astype(BF16)
    og_ref[...] = seg(OFF_OG, GLA_DV).astype(BF16)
    xg = _dot(misc.astype(BF16), wg2_ref[...]) + bg_ref[...]
    la_ref[...] = (jnp.minimum(xg, 0.0) - jnp.log1p(jnp.exp(-jnp.abs(xg)))) * (1.0 / GLA_TAU)
    sa_ref[...] = jax.nn.sigmoid(seg(OFF_GA, D_MODEL)).astype(BF16)
    sb_ref[...] = jax.nn.sigmoid(seg(OFF_GB, D_MODEL)).astype(BF16)


def _const_spec(shape):
    return pl.BlockSpec(shape, lambda *_: (0,) * len(shape))


def _proj_call(x2, pos2, lnmix, winp, ncq, wuqp, nckv, wkp, wvt, wg2p, bg, invf, tm):
    t = x2.shape[0]
    row = lambda w: pl.BlockSpec((tm, w), lambda i: (i, 0))
    consts = (lnmix, winp, ncq, wuqp, nckv, wkp, wvt, wg2p, bg, invf)
    out_widths = (MLA_HEADS * HEAD_PAD, MLA_HEADS * HEAD_PAD, None, GLA_DK, GLA_DK, GLA_DV, GLA_DK,
                  GLA_DV, D_MODEL, D_MODEL)
    out_dtypes = (BF16, BF16, BF16, BF16, BF16, BF16, F32, BF16, BF16, BF16)
    vt_spec = pl.BlockSpec((1, MLA_WIDTH, tm), lambda i: (i, 0, 0))
    vt_shape = jax.ShapeDtypeStruct((t // tm, MLA_WIDTH, tm), BF16)
    return pl.pallas_call(
        _proj_kernel,
        grid=(t // tm,),
        in_specs=[row(D_MODEL), row(1)] + [_const_spec(c.shape) for c in consts],
        out_specs=[vt_spec if w is None else row(w) for w in out_widths],
        out_shape=[vt_shape if w is None else jax.ShapeDtypeStruct((t, w), d)
                   for w, d in zip(out_widths, out_dtypes)],
        compiler_params=pltpu.CompilerParams(dimension_semantics=("parallel",), vmem_limit_bytes=VMEM_LIMIT),
        name="proj",
    )(x2, pos2, *consts)


def _attn_kernel(q_ref, k_ref, vt_ref, o_ref, ot_ref, m_ref, l_ref, s_ref, *, tq):
    i = pl.program_id(1)
    key = lax.broadcasted_iota(jnp.int32, (tq, tq), 0)
    qry = lax.broadcasted_iota(jnp.int32, (tq, tq), 1)
    causal = key <= qry
    m_ref[...] = jnp.full(m_ref.shape, -1e30, F32)
    l_ref[...] = jnp.zeros(l_ref.shape, F32)
    ot_ref[...] = jnp.zeros(ot_ref.shape, F32)

    def step(j, masked):
        rows = pl.ds(pl.multiple_of(j * tq, tq), tq)

        def scores(h):
            hq = slice(h * HEAD_PAD, (h + 1) * HEAD_PAD)
            s = _dot_nt(k_ref[rows, hq], q_ref[:, hq])
            if masked:
                s = jnp.where(causal, s, -1e30)
            s_ref[h] = s
            return jnp.max(s, axis=0, keepdims=True)

        def accumulate(h, s_max):
            hv = slice(h * MLA_V, (h + 1) * MLA_V)
            m = m_ref[h:h + 1, :]
            m_new = jnp.maximum(m, s_max)
            alpha = jnp.exp2(m - m_new)
            p = jnp.exp2(s_ref[h] - m_new)
            m_ref[h:h + 1, :] = m_new
            l_ref[h:h + 1, :] = alpha * l_ref[h:h + 1, :] + jnp.sum(p, axis=0, keepdims=True)
            ot_ref[hv, :] = alpha * ot_ref[hv, :] + _dot(vt_ref[j, hv, :], p.astype(BF16))

        pending = {}
        for h in range(MLA_HEADS + ATTN_LOOKAHEAD):
            if h < MLA_HEADS:
                pending[h] = scores(h)
            if h >= ATTN_LOOKAHEAD:
                accumulate(h - ATTN_LOOKAHEAD, pending.pop(h - ATTN_LOOKAHEAD))

    def body(j, carry):
        step(j, False)
        return carry

    lax.fori_loop(0, i, body, 0)
    step(i, True)
    for h in range(MLA_HEADS):
        hv = slice(h * MLA_V, (h + 1) * MLA_V)
        ot_ref[hv, :] = ot_ref[hv, :] / l_ref[h:h + 1, :]
    o_ref[...] = ot_ref[...].T.astype(BF16)


def _attn_call(q, k, vt, batch, seq, tq):
    nq = seq // tq
    return pl.pallas_call(
        functools.partial(_attn_kernel, tq=tq),
        grid=(batch, nq),
        in_specs=[pl.BlockSpec((tq, MLA_HEADS * HEAD_PAD), lambda b, i: (b * nq + i, 0)),
                  pl.BlockSpec((seq, MLA_HEADS * HEAD_PAD), lambda b, i: (b, 0)),
                  pl.BlockSpec((nq, MLA_WIDTH, tq), lambda b, i: (b, 0, 0))],
        out_specs=pl.BlockSpec((tq, MLA_WIDTH), lambda b, i: (b * nq + i, 0)),
        out_shape=jax.ShapeDtypeStruct((batch * seq, MLA_WIDTH), BF16),
        scratch_shapes=[pltpu.VMEM((MLA_WIDTH, tq), F32), pltpu.VMEM((MLA_HEADS, tq), F32),
                        pltpu.VMEM((MLA_HEADS, tq), F32), pltpu.VMEM((MLA_HEADS, tq, tq), F32)],
        compiler_params=pltpu.CompilerParams(dimension_semantics=("parallel", "arbitrary"),
                                             vmem_limit_bytes=VMEM_LIMIT),
        name="mla_attn",
    )(q, k, vt)


def _gla_kernel(gq_ref, gk_ref, gv_ref, la_ref, og_ref, norm_ref, o_ref, state_ref, kpad_ref, lapad_ref,
                *, chunks_per_step):
    c_, s_ = GLA_CHUNK, GLA_SUB
    nsub = c_ // s_

    @pl.when(pl.program_id(1) == 0)
    def _():
        state_ref[...] = jnp.zeros_like(state_ref)

    kpad_ref[0:s_, :] = jnp.zeros((s_, GLA_DK), F32)
    lapad_ref[0:s_, :] = jnp.zeros((s_, GLA_DK), F32)

    ri = lax.broadcasted_iota(jnp.int32, (c_, c_), 0)
    ci = lax.broadcasted_iota(jnp.int32, (c_, c_), 1)
    tri = jnp.where((ri // s_ == ci // s_) & (ci <= ri), 1.0, 0.0).astype(BF16)
    rl = lax.broadcasted_iota(jnp.int32, (c_, GLA_DK), 0)
    ll = lax.broadcasted_iota(jnp.int32, (c_, GLA_DK), 1)
    sub_pos = rl % s_
    row_minus_col = rl - ll % GLA_HK
    head_of_lane = ll // GLA_HK
    di = lax.broadcasted_iota(jnp.int32, (GLA_DK, GLA_DK), 0) // GLA_HK
    dj = lax.broadcasted_iota(jnp.int32, (GLA_DK, GLA_DK), 1) // GLA_HK
    head_sum = jnp.where(di == dj, 1.0, 0.0).astype(BF16)
    norm_w = norm_ref[...]

    def blocks(fn):
        return jnp.concatenate([fn(b) for b in range(nsub)], axis=0)

    def chunk(c, _):
        r0 = pl.multiple_of(c * c_, c_)
        rows = pl.ds(r0, c_)
        q = gq_ref[rows, :].astype(F32)
        k = gk_ref[rows, :].astype(F32)
        v = gv_ref[rows, :]
        la = la_ref[rows, :]

        hi = la.astype(BF16)
        r1 = la - hi.astype(F32)
        mid = r1.astype(BF16)
        lo = (r1 - mid.astype(F32)).astype(BF16)
        l16 = _dot(tri, hi) + _dot(tri, mid) + _dot(tri, lo)
        tot = [l16[(b + 1) * s_ - 1:(b + 1) * s_, :] for b in range(nsub)]
        q16 = q * jnp.exp(l16)
        k16 = k * jnp.exp(blocks(lambda b: jnp.broadcast_to(tot[b], (s_, GLA_DK))) - l16)

        def sub(t, b):
            return t[b * s_:(b + 1) * s_, :]

        zero = jnp.zeros((s_, GLA_DK), F32)
        e = [jnp.exp(t) for t in tot]
        pre = [None, e[0], e[0] * e[1], e[0] * e[1] * e[2]]
        post = [e[1] * e[2] * e[3], e[2] * e[3], e[3], None]
        qd = blocks(lambda b: sub(q16, b) if pre[b] is None else sub(q16, b) * pre[b]).astype(BF16)
        kd = blocks(lambda b: sub(k16, b) if post[b] is None else sub(k16, b) * post[b]).astype(BF16)
        q_lv = [jnp.concatenate([zero, zero, sub(q16, 2), sub(q16, 3) * e[2]], axis=0),
                jnp.concatenate([zero, sub(q16, 1), zero, zero], axis=0),
                jnp.concatenate([zero, zero, zero, sub(q16, 3)], axis=0)]
        k_lv = [jnp.concatenate([sub(k16, 0) * e[1], sub(k16, 1), zero, zero], axis=0),
                jnp.concatenate([sub(k16, 0), zero, zero, zero], axis=0),
                jnp.concatenate([zero, zero, sub(k16, 2), zero], axis=0)]
        q_lv = [t.astype(BF16) for t in q_lv]
        k_lv = [t.astype(BF16) for t in k_lv]

        kpad_ref[s_:, :] = k
        lapad_ref[s_:, :] = la
        decay = jnp.zeros((c_, GLA_DK), F32)
        prods = []
        for d in range(s_):
            if d > 0:
                decay = decay + lapad_ref[s_ - d + 1:s_ - d + 1 + c_, :]
            k_sh = kpad_ref[s_ - d:s_ - d + c_, :]
            prods.append(jnp.where(sub_pos >= d, q * k_sh * jnp.exp(decay), 0.0).astype(BF16))
        summed = _dot(jnp.concatenate(prods, axis=0), head_sum)
        attn_diag = jnp.zeros((c_, GLA_DK), F32)
        for d in range(s_):
            attn_diag = jnp.where(row_minus_col == d, summed[d * c_:(d + 1) * c_, :], attn_diag)

        state = state_ref[...]
        state_b = state.astype(BF16)
        new_state = state * jnp.exp(tot[0] + tot[1] + tot[2] + tot[3])
        for h in range(GLA_HEADS):
            in_head = head_of_lane == h
            vh = v[:, h * GLA_HV:(h + 1) * GLA_HV]
            attn = attn_diag[:, h * GLA_HK:(h + 1) * GLA_HK]
            for ql, kl in zip(q_lv, k_lv):
                attn = attn + _dot_nt(jnp.where(in_head, ql, 0.0), kl)
            o = _dot(attn.astype(BF16), vh) + _dot_nt(jnp.where(in_head, qd, 0.0), state_b)
            o = _rms(o, norm_w)
            g = og_ref[rows, h * GLA_HV:(h + 1) * GLA_HV].astype(F32)
            o_ref[rows, h * GLA_HV:(h + 1) * GLA_HV] = (o * (g * jax.nn.sigmoid(g))).astype(BF16)
            new_state = new_state + _dot_tn(vh, jnp.where(in_head, kd, 0.0))
        state_ref[...] = new_state
        return 0

    lax.fori_loop(0, chunks_per_step, chunk, 0)


def _gla_call(gq, gk, gv, la, og, norm, batch, seq, rows_per_step):
    ns = seq // rows_per_step
    row = lambda w: pl.BlockSpec((rows_per_step, w), lambda b, i: (b * ns + i, 0))
    return pl.pallas_call(
        functools.partial(_gla_kernel, chunks_per_step=rows_per_step // GLA_CHUNK),
        grid=(batch, ns),
        in_specs=[row(GLA_DK), row(GLA_DK), row(GLA_DV), row(GLA_DK), row(GLA_DV), _const_spec(norm.shape)],
        out_specs=row(GLA_DV),
        out_shape=jax.ShapeDtypeStruct((batch * seq, GLA_DV), BF16),
        scratch_shapes=[pltpu.VMEM((GLA_HV, GLA_DK), F32),
                        pltpu.VMEM((GLA_SUB + GLA_CHUNK, GLA_DK), F32),
                        pltpu.VMEM((GLA_SUB + GLA_CHUNK, GLA_DK), F32)],
        compiler_params=pltpu.CompilerParams(dimension_semantics=("parallel", "arbitrary"),
                                             vmem_limit_bytes=VMEM_LIMIT),
        name="gla",
    )(gq, gk, gv, la, og, norm)


def _post_kernel(x_ref, oa_ref, ob_ref, sa_ref, sb_ref, woa_ref, wob_ref, wout_ref, lnffn_ref,
                 wg_ref, wu_ref, wd_ref, fnorm_ref, out_ref):
    ya = _dot(oa_ref[...], woa_ref[...])
    yb = _dot(ob_ref[...], wob_ref[...])
    mix = sa_ref[...].astype(F32) * ya + sb_ref[...].astype(F32) * yb
    h = x_ref[...] + _dot(mix.astype(BF16), wout_ref[...])
    u = _rms(h, lnffn_ref[...]).astype(BF16)
    g = _dot(u, wg_ref[...])
    act = (g * jax.nn.sigmoid(g) * _dot(u, wu_ref[...])).astype(BF16)
    h = h + _dot(act, wd_ref[...])
    out_ref[...] = _rms(h, fnorm_ref[...])


def _post_call(x2, oa, ob, sa, sb, woa, wob, wout, lnffn, wg, wu, wd, fnorm, tm):
    t = x2.shape[0]
    row = lambda w: pl.BlockSpec((tm, w), lambda i: (i, 0))
    single = lambda a: pl.BlockSpec(a.shape, lambda i: (0,) * a.ndim, pipeline_mode=pl.Buffered(1))
    consts = (woa, wob, wout, lnffn, wg, wu, wd, fnorm)
    return pl.pallas_call(
        _post_kernel,
        grid=(t // tm,),
        in_specs=[row(D_MODEL), row(MLA_WIDTH), row(GLA_DV), row(D_MODEL), row(D_MODEL)]
                 + [single(c) for c in consts],
        out_specs=row(D_MODEL),
        out_shape=jax.ShapeDtypeStruct((t, D_MODEL), F32),
        compiler_params=pltpu.CompilerParams(dimension_semantics=("parallel",), vmem_limit_bytes=VMEM_LIMIT),
        name="post",
    )(x2, oa, ob, sa, sb, *consts)


def _pack_weights(w_in, mla_w_uq, mla_w_ukv, gla_w_gate2):
    offs = [0]
    for s in IN_SIZES:
        offs.append(offs[-1] + s)
    col = lambda idx: w_in[:, offs[idx]:offs[idx + 1]]
    d = w_in.shape[0]
    misc = jnp.concatenate([jnp.zeros((d, ROPE_LO), F32), col(2), col(6),
                            jnp.zeros((d, LANES - GLR_LO - GLA_RANK), F32)], axis=1)
    winp = jnp.concatenate([col(0), col(1), col(3), col(4), col(5), col(7), col(8), col(9), misc],
                           axis=1).astype(BF16)
    wuq = mla_w_uq.reshape(MLA_Q_RANK, MLA_HEADS, MLA_QK)
    wuqp = jnp.pad(wuq, ((0, 0), (0, 0), (0, HEAD_PAD - MLA_QK))).reshape(MLA_Q_RANK, MLA_HEADS * HEAD_PAD)
    wukv = mla_w_ukv.reshape(MLA_KV_RANK, MLA_HEADS, MLA_NOPE + MLA_V)
    wkp = jnp.pad(wukv[:, :, :MLA_NOPE], ((0, 0), (0, 0), (0, HEAD_PAD - MLA_NOPE)))
    wkp = wkp.reshape(MLA_KV_RANK, MLA_HEADS * HEAD_PAD)
    wvt = wukv[:, :, MLA_NOPE:].reshape(MLA_KV_RANK, MLA_WIDTH).T
    wg2p = jnp.pad(gla_w_gate2, ((GLR_LO, LANES - GLR_LO - GLA_RANK), (0, 0)))
    return winp, wuqp.astype(BF16), wkp.astype(BF16), wvt.astype(BF16), wg2p.astype(BF16)


def _rope_inv_freq():
    half = ROPE_HALF
    inv = 1.0 / (ROPE_THETA ** (jnp.arange(half, dtype=F32) / half))
    return jnp.concatenate([jnp.zeros((ROPE_LO,), F32), inv, inv,
                            jnp.zeros((LANES - ROPE_LO - MLA_ROPE,), F32)]).reshape(1, LANES)


def _tile(n, pref):
    while n % pref:
        pref //= 2
    return pref


def kernel(x, positions, ln_mix, w_in, mla_norm_cq, mla_w_uq, mla_norm_ckv, mla_w_ukv, mla_w_o, gla_w_gate2,
           gla_b_gate, gla_norm, gla_w_o, w_out, ln_ffn, ffn_w_gate, ffn_w_up, ffn_w_down, final_norm):
    batch, seq, d = x.shape
    assert d == D_MODEL and w_in.shape[0] == 1 and seq % GLA_CHUNK == 0
    t = batch * seq
    x2 = x.reshape(t, d)
    pos2 = positions.reshape(t, 1).astype(F32)
    winp, wuqp, wkp, wvt, wg2p = _pack_weights(w_in[0], mla_w_uq[0], mla_w_ukv[0], gla_w_gate2[0])
    r = lambda a: a.reshape(1, -1)

    tq = _tile(seq, ATTN_TILE)
    q, k, vt, gq, gk, gv, la, og, sa, sb = _proj_call(
        x2, pos2, r(ln_mix[0]), winp, r(mla_norm_cq[0]), wuqp, r(mla_norm_ckv[0]), wkp, wvt, wg2p,
        r(gla_b_gate[0]), _rope_inv_freq(), tq)
    oa = _attn_call(q, k, vt, batch, seq, tq)
    ob = _gla_call(gq, gk, gv, la, og, r(gla_norm[0]), batch, seq, _tile(seq, 512))
    out = _post_call(x2, oa, ob, sa, sb, mla_w_o[0].astype(BF16), gla_w_o[0].astype(BF16),
                     w_out[0].astype(BF16), r(ln_ffn[0]), ffn_w_gate[0].astype(BF16),
                     ffn_w_up[0].astype(BF16), ffn_w_down[0].astype(BF16), r(final_norm), _tile(t, 256))
    return out.reshape(batch, seq, d)
```

```python
import functools
import math

import jax
import jax.numpy as jnp
from jax import lax
from jax.experimental import pallas as pl
from jax.experimental.pallas import tpu as pltpu

F32 = jnp.float32
BF16 = jnp.bfloat16

D_MODEL = 1024
MLA_HEADS = 8
MLA_NOPE = 64
MLA_ROPE = 32
MLA_QK = MLA_NOPE + MLA_ROPE
MLA_V = 64
MLA_WIDTH = MLA_HEADS * MLA_V
MLA_Q_RANK = 384
MLA_KV_RANK = 256
ROPE_THETA = 10000.0
GLA_HEADS = 4
GLA_DV = 512
GLA_DK = 256
GLA_HK = GLA_DK // GLA_HEADS
GLA_HV = GLA_DV // GLA_HEADS
GLA_RANK = 16
GLA_TAU = 16.0
D_FF = 2816
EPS = 1e-6
LOG2E = 1.4426950408889634
IN_SIZES = (MLA_Q_RANK, MLA_KV_RANK, MLA_ROPE, GLA_DK, GLA_DK, GLA_DV, GLA_RANK, GLA_DV, D_MODEL, D_MODEL)

LANES = 128
HEAD_PAD = LANES
ROPE_LO = MLA_NOPE
ROPE_HALF = MLA_ROPE // 2
GLR_LO = ROPE_LO + MLA_ROPE

OFF_CQ = 0
OFF_CKV = OFF_CQ + MLA_Q_RANK
OFF_GQ = OFF_CKV + MLA_KV_RANK
OFF_GK = OFF_GQ + GLA_DK
OFF_GV = OFF_GK + GLA_DK
OFF_OG = OFF_GV + GLA_DV
OFF_GA = OFF_OG + GLA_DV
OFF_GB = OFF_GA + D_MODEL
OFF_MISC = OFF_GB + D_MODEL
D_IN_PACKED = OFF_MISC + LANES

ATTN_TILE = 256
ATTN_LOOKAHEAD = 8
GLA_CHUNK = 64
GLA_SUB = 16
GLA_PAR = 4
VMEM_LIMIT = 56 * 1024 * 1024


def _dot(a, b):
    return jnp.dot(a, b, preferred_element_type=F32)


def _dot_nt(a, b):
    return lax.dot_general(a, b, (((1,), (1,)), ((), ())), preferred_element_type=F32)


def _dot_tn(a, b):
    return lax.dot_general(a, b, (((0,), (0,)), ((), ())), preferred_element_type=F32)


def _rms(x, w):
    return x * lax.rsqrt(jnp.mean(x * x, axis=-1, keepdims=True) + EPS) * w


def _proj_kernel(x_ref, pos_ref, lnmix_ref, win_ref, ncq_ref, wuq_ref, nckv_ref, wk_ref, wvt_ref,
                 wg2_ref, bg_ref, invf_ref,
                 q_ref, k_ref, vt_ref, gq_ref, gk_ref, gv_ref, la_ref, og_ref, sa_ref, sb_ref):
    ub = _rms(x_ref[...], lnmix_ref[...]).astype(BF16)

    def seg(lo, width):
        return _dot(ub, win_ref[:, lo:lo + width])

    lane = lax.broadcasted_iota(jnp.int32, (1, LANES), 1)
    ang = pos_ref[...] * invf_ref[...]
    cos = jnp.cos(ang)
    sin = jnp.sin(ang)
    in_x1 = (lane >= ROPE_LO) & (lane < ROPE_LO + ROPE_HALF)
    in_x2 = (lane >= ROPE_LO + ROPE_HALF) & (lane < ROPE_LO + MLA_ROPE)
    sin_x1 = jnp.where(in_x1, -sin, 0.0)
    sin_x2 = jnp.where(in_x2, sin, 0.0)

    def rope(t):
        return (t * cos + pltpu.roll(t, LANES - ROPE_HALF, axis=1) * sin_x1
                + pltpu.roll(t, ROPE_HALF, axis=1) * sin_x2)

    misc = seg(OFF_MISC, LANES)
    k_rope = rope(jnp.where(in_x1 | in_x2, misc, 0.0))

    cqn = _rms(seg(OFF_CQ, MLA_Q_RANK), ncq_ref[...]).astype(BF16)
    q = _dot(cqn, wuq_ref[...]) * (MLA_QK ** -0.5 * LOG2E)
    ckvn = _rms(seg(OFF_CKV, MLA_KV_RANK), nckv_ref[...]).astype(BF16)
    k_nope = _dot(ckvn, wk_ref[...])
    for h in range(MLA_HEADS):
        sl = slice(h * HEAD_PAD, (h + 1) * HEAD_PAD)
        q_ref[:, sl] = rope(q[:, sl]).astype(BF16)
        k_ref[:, sl] = (k_nope[:, sl] + k_rope).astype(BF16)
    vt_ref[0] = _dot_nt(wvt_ref[...], ckvn).astype(BF16)

    gq_ref[...] = (seg(OFF_GQ, GLA_DK) * (GLA_HK ** -0.5)).astype(BF16)
    gk_ref[...] = seg(OFF_GK, GLA_DK).astype(BF16)
    gv_ref[...] = seg(OFF_GV, GLA_DV).astype(BF16)
    og_ref[...] = seg(OFF_OG, GLA_DV).astype(BF16)
    xg = _dot(misc.astype(BF16), wg2_ref[...]) + bg_ref[...]
    la_ref[...] = (jnp.minimum(xg, 0.0) - jnp.log1p(jnp.exp(-jnp.abs(xg)))) * (1.0 / GLA_TAU)
    sa_ref[...] = jax.nn.sigmoid(seg(OFF_GA, D_MODEL)).astype(BF16)
    sb_ref[...] = jax.nn.sigmoid(seg(OFF_GB, D_MODEL)).astype(BF16)


def _const_spec(shape):
    return pl.BlockSpec(shape, lambda *_: (0,) * len(shape))


def _proj_call(x2, pos2, lnmix, winp, ncq, wuqp, nckv, wkp, wvt, wg2p, bg, invf, tm):
    t = x2.shape[0]
    row = lambda w: pl.BlockSpec((tm, w), lambda i: (i, 0))
    consts = (lnmix, winp, ncq, wuqp, nckv, wkp, wvt, wg2p, bg, invf)
    out_widths = (MLA_HEADS * HEAD_PAD, MLA_HEADS * HEAD_PAD, None, GLA_DK, GLA_DK, GLA_DV, GLA_DK,
                  GLA_DV, D_MODEL, D_MODEL)
    out_dtypes = (BF16, BF16, BF16, BF16, BF16, BF16, F32, BF16, BF16, BF16)
    vt_spec = pl.BlockSpec((1, MLA_WIDTH, tm), lambda i: (i, 0, 0))
    vt_shape = jax.ShapeDtypeStruct((t // tm, MLA_WIDTH, tm), BF16)
    return pl.pallas_call(
        _proj_kernel,
        grid=(t // tm,),
        in_specs=[row(D_MODEL), row(1)] + [_const_spec(c.shape) for c in consts],
        out_specs=[vt_spec if w is None else row(w) for w in out_widths],
        out_shape=[vt_shape if w is None else jax.ShapeDtypeStruct((t, w), d)
                   for w, d in zip(out_widths, out_dtypes)],
        compiler_params=pltpu.CompilerParams(dimension_semantics=("parallel",), vmem_limit_bytes=VMEM_LIMIT),
        name="proj",
    )(x2, pos2, *consts)


def _attn_kernel(q_ref, k_ref, vt_ref, o_ref, ot_ref, m_ref, l_ref, s_ref, *, tq):
    i = pl.program_id(1)
    key = lax.broadcasted_iota(jnp.int32, (tq, tq), 0)
    qry = lax.broadcasted_iota(jnp.int32, (tq, tq), 1)
    causal = key <= qry
    m_ref[...] = jnp.full(m_ref.shape, -1e30, F32)
    l_ref[...] = jnp.zeros(l_ref.shape, F32)
    ot_ref[...] = jnp.zeros(ot_ref.shape, F32)

    def step(j, masked):
        rows = pl.ds(pl.multiple_of(j * tq, tq), tq)

        def scores(h):
            hq = slice(h * HEAD_PAD, (h + 1) * HEAD_PAD)
            s = _dot_nt(k_ref[rows, hq], q_ref[:, hq])
            if masked:
                s = jnp.where(causal, s, -1e30)
            s_ref[h] = s
            return jnp.max(s, axis=0, keepdims=True)

        def accumulate(h, s_max):
            hv = slice(h * MLA_V, (h + 1) * MLA_V)
            m = m_ref[h:h + 1, :]
            m_new = jnp.maximum(m, s_max)
            alpha = jnp.exp2(m - m_new)
            p = jnp.exp2(s_ref[h] - m_new)
            m_ref[h:h + 1, :] = m_new
            l_ref[h:h + 1, :] = alpha * l_ref[h:h + 1, :] + jnp.sum(p, axis=0, keepdims=True)
            ot_ref[hv, :] = alpha * ot_ref[hv, :] + _dot(vt_ref[j, hv, :], p.astype(BF16))

        pending = {}
        for h in range(MLA_HEADS + ATTN_LOOKAHEAD):
            if h < MLA_HEADS:
                pending[h] = scores(h)
            if h >= ATTN_LOOKAHEAD:
                accumulate(h - ATTN_LOOKAHEAD, pending.pop(h - ATTN_LOOKAHEAD))

    def body(j, carry):
        step(j, False)
        return carry

    lax.fori_loop(0, i, body, 0)
    step(i, True)
    for h in range(MLA_HEADS):
        hv = slice(h * MLA_V, (h + 1) * MLA_V)
        ot_ref[hv, :] = ot_ref[hv, :] / l_ref[h:h + 1, :]
    o_ref[...] = ot_ref[...].T.astype(BF16)


def _attn_call(q, k, vt, batch, seq, tq):
    nq = seq // tq
    return pl.pallas_call(
        functools.partial(_attn_kernel, tq=tq),
        grid=(batch, nq),
        in_specs=[pl.BlockSpec((tq, MLA_HEADS * HEAD_PAD), lambda b, i: (b * nq + i, 0)),
                  pl.BlockSpec((seq, MLA_HEADS * HEAD_PAD), lambda b, i: (b, 0)),
                  pl.BlockSpec((nq, MLA_WIDTH, tq), lambda b, i: (b, 0, 0))],
        out_specs=pl.BlockSpec((tq, MLA_WIDTH), lambda b, i: (b * nq + i, 0)),
        out_shape=jax.ShapeDtypeStruct((batch * seq, MLA_WIDTH), BF16),
        scratch_shapes=[pltpu.VMEM((MLA_WIDTH, tq), F32), pltpu.VMEM((MLA_HEADS, tq), F32),
                        pltpu.VMEM((MLA_HEADS, tq), F32), pltpu.VMEM((MLA_HEADS, tq, tq), F32)],
        compiler_params=pltpu.CompilerParams(dimension_semantics=("parallel", "arbitrary"),
                                             vmem_limit_bytes=VMEM_LIMIT),
        name="mla_attn",
    )(q, k, vt)


def _gla_kernel(gq_ref, gk_ref, gv_ref, la_ref, og_ref, norm_ref, o_ref, state_ref, kpad_ref, lapad_ref,
                *, slabs_per_step, par):
    c_, s_ = GLA_CHUNK, GLA_SUB
    nsub = c_ // s_
    slab_rows = par * c_
    pair = 2 * GLA_HK

    @pl.when(pl.program_id(1) == 0)
    def _():
        state_ref[...] = jnp.zeros_like(state_ref)

    kpad_ref[0:s_, :] = jnp.zeros((s_, GLA_DK), F32)
    lapad_ref[0:s_, :] = jnp.zeros((s_, GLA_DK), F32)

    ri = lax.broadcasted_iota(jnp.int32, (c_, c_), 0)
    ci = lax.broadcasted_iota(jnp.int32, (c_, c_), 1)
    tri = jnp.where((ri // s_ == ci // s_) & (ci <= ri), 1.0, 0.0).astype(BF16)
    rl = lax.broadcasted_iota(jnp.int32, (c_, GLA_DK), 0)
    ll = lax.broadcasted_iota(jnp.int32, (c_, GLA_DK), 1)
    row_minus_col = rl - ll % GLA_HK
    diag_key = jnp.where((row_minus_col >= 0) & (rl % s_ >= row_minus_col), row_minus_col, -1)
    head_of_lane = ll // GLA_HK
    odd_head_lane = (lax.broadcasted_iota(jnp.int32, (c_, pair), 1) // GLA_HK) == 1
    di = lax.broadcasted_iota(jnp.int32, (GLA_DK, GLA_DK), 0) // GLA_HK
    dj = lax.broadcasted_iota(jnp.int32, (GLA_DK, GLA_DK), 1) // GLA_HK
    head_sum = jnp.where(di == dj, 1.0, 0.0).astype(BF16)
    norm_w = norm_ref[...]

    def blocks(fn):
        return jnp.concatenate([fn(b) for b in range(nsub)], axis=0)

    def sub(t, b):
        return t[b * s_:(b + 1) * s_, :]

    def slab(sidx, _):
        base = pl.multiple_of(sidx * slab_rows, slab_rows)
        kpad_ref[s_:, :] = gk_ref[pl.ds(base, slab_rows), :].astype(F32)
        lapad_ref[s_:, :] = la_ref[pl.ds(base, slab_rows), :]
        chunks = [dict(rows=pl.ds(pl.multiple_of(base + c * c_, c_), c_), off=s_ + c * c_) for c in range(par)]

        for ch in chunks:
            la = la_ref[ch["rows"], :]
            hi = la.astype(BF16)
            r1 = la - hi.astype(F32)
            mid = r1.astype(BF16)
            lo = (r1 - mid.astype(F32)).astype(BF16)
            ch["l16"] = _dot(tri, hi) + _dot(tri, mid) + _dot(tri, lo)

        for ch in chunks:
            q = gq_ref[ch["rows"], :].astype(F32)
            k = gk_ref[ch["rows"], :].astype(F32)
            l16 = ch["l16"]
            tot = [l16[(b + 1) * s_ - 1:(b + 1) * s_, :] for b in range(nsub)]
            q16 = q * jnp.exp(l16)
            k16 = k * jnp.exp(blocks(lambda b: jnp.broadcast_to(tot[b], (s_, GLA_DK))) - l16)
            zero = jnp.zeros((s_, GLA_DK), F32)
            e = [jnp.exp(t) for t in tot]
            pre = [None, e[0], e[0] * e[1], e[0] * e[1] * e[2]]
            post = [e[1] * e[2] * e[3], e[2] * e[3], e[3], None]
            ch["qd"] = blocks(lambda b: sub(q16, b) if pre[b] is None else sub(q16, b) * pre[b]).astype(BF16)
            kd = blocks(lambda b: sub(k16, b) if post[b] is None else sub(k16, b) * post[b]).astype(BF16)
            ch["decay"] = jnp.exp(tot[0] + tot[1] + tot[2] + tot[3])
            v = gv_ref[ch["rows"], :]
            ch["upd"] = sum(_dot_tn(v[:, h * GLA_HV:(h + 1) * GLA_HV], jnp.where(head_of_lane == h, kd, 0.0))
                            for h in range(GLA_HEADS))
            ch["q_lv"] = [t.astype(BF16) for t in (
                jnp.concatenate([zero, zero, sub(q16, 2), sub(q16, 3) * e[2]], axis=0),
                jnp.concatenate([zero, sub(q16, 1), zero, zero], axis=0),
                jnp.concatenate([zero, zero, zero, sub(q16, 3)], axis=0))]
            ch["k_lv"] = [t.astype(BF16) for t in (
                jnp.concatenate([sub(k16, 0) * e[1], sub(k16, 1), zero, zero], axis=0),
                jnp.concatenate([sub(k16, 0), zero, zero, zero], axis=0),
                jnp.concatenate([zero, zero, sub(k16, 2), zero], axis=0))]
            decay = jnp.zeros((c_, GLA_DK), F32)
            prods = []
            for d in range(s_):
                if d > 0:
                    decay = decay + lapad_ref[ch["off"] - d + 1:ch["off"] - d + 1 + c_, :]
                k_sh = kpad_ref[ch["off"] - d:ch["off"] - d + c_, :]
                prods.append((q * k_sh * jnp.exp(decay)).astype(BF16))
            ch["summed"] = _dot(jnp.concatenate(prods, axis=0), head_sum)

        for ch in chunks:
            ch["attn_off"] = []
            for h in range(GLA_HEADS):
                lanes = slice((h // 2) * pair, (h // 2 + 1) * pair)
                mine = odd_head_lane if h % 2 else ~odd_head_lane
                qx = jnp.concatenate([jnp.where(mine, t[:, lanes], 0.0) for t in ch["q_lv"]], axis=1)
                kx = jnp.concatenate([t[:, lanes] for t in ch["k_lv"]], axis=1)
                ch["attn_off"].append(_dot_nt(qx, kx))

        state = state_ref[...]
        for ch in chunks:
            ch["state_b"] = state.astype(BF16)
            state = state * ch["decay"] + ch["upd"]
        state_ref[...] = state

        for ch in chunks:
            attn_diag = jnp.zeros((c_, GLA_DK), F32)
            for d in range(s_):
                attn_diag = jnp.where(diag_key == d, ch["summed"][d * c_:(d + 1) * c_, :], attn_diag)
            v = gv_ref[ch["rows"], :]
            for h in range(GLA_HEADS):
                hv = slice(h * GLA_HV, (h + 1) * GLA_HV)
                attn = attn_diag[:, h * GLA_HK:(h + 1) * GLA_HK] + ch["attn_off"][h]
                o = _dot(attn.astype(BF16), v[:, hv])
                o = o + _dot_nt(jnp.where(head_of_lane == h, ch["qd"], 0.0), ch["state_b"])
                o = _rms(o, norm_w)
                g = og_ref[ch["rows"], hv].astype(F32)
                o_ref[ch["rows"], hv] = (o * (g * jax.nn.sigmoid(g))).astype(BF16)
        return 0

    lax.fori_loop(0, slabs_per_step, slab, 0)


def _gla_call(gq, gk, gv, la, og, norm, batch, seq, rows_per_step):
    ns = seq // rows_per_step
    par = math.gcd(rows_per_step // GLA_CHUNK, GLA_PAR)
    row = lambda w: pl.BlockSpec((rows_per_step, w), lambda b, i: (b * ns + i, 0))
    return pl.pallas_call(
        functools.partial(_gla_kernel, slabs_per_step=rows_per_step // (par * GLA_CHUNK), par=par),
        grid=(batch, ns),
        in_specs=[row(GLA_DK), row(GLA_DK), row(GLA_DV), row(GLA_DK), row(GLA_DV), _const_spec(norm.shape)],
        out_specs=row(GLA_DV),
        out_shape=jax.ShapeDtypeStruct((batch * seq, GLA_DV), BF16),
        scratch_shapes=[pltpu.VMEM((GLA_HV, GLA_DK), F32),
                        pltpu.VMEM((GLA_SUB + par * GLA_CHUNK, GLA_DK), F32),
                        pltpu.VMEM((GLA_SUB + par * GLA_CHUNK, GLA_DK), F32)],
        compiler_params=pltpu.CompilerParams(dimension_semantics=("parallel", "arbitrary"),
                                             vmem_limit_bytes=VMEM_LIMIT),
        name="gla",
    )(gq, gk, gv, la, og, norm)


def _post_kernel(x_ref, oa_ref, ob_ref, sa_ref, sb_ref, woa_ref, wob_ref, wout_ref, lnffn_ref,
                 wg_ref, wu_ref, wd_ref, fnorm_ref, out_ref):
    ya = _dot(oa_ref[...], woa_ref[...])
    yb = _dot(ob_ref[...], wob_ref[...])
    mix = sa_ref[...].astype(F32) * ya + sb_ref[...].astype(F32) * yb
    h = x_ref[...] + _dot(mix.astype(BF16), wout_ref[...])
    u = _rms(h, lnffn_ref[...]).astype(BF16)
    g = _dot(u, wg_ref[...])
    act = (g * jax.nn.sigmoid(g) * _dot(u, wu_ref[...])).astype(BF16)
    h = h + _dot(act, wd_ref[...])
    out_ref[...] = _rms(h, fnorm_ref[...])


def _post_call(x2, oa, ob, sa, sb, woa, wob, wout, lnffn, wg, wu, wd, fnorm, tm):
    t = x2.shape[0]
    row = lambda w: pl.BlockSpec((tm, w), lambda i: (i, 0))
    single = lambda a: pl.BlockSpec(a.shape, lambda i: (0,) * a.ndim, pipeline_mode=pl.Buffered(1))
    consts = (woa, wob, wout, lnffn, wg, wu, wd, fnorm)
    return pl.pallas_call(
        _post_kernel,
        grid=(t // tm,),
        in_specs=[row(D_MODEL), row(MLA_WIDTH), row(GLA_DV), row(D_MODEL), row(D_MODEL)]
                 + [single(c) for c in consts],
        out_specs=row(D_MODEL),
        out_shape=jax.ShapeDtypeStruct((t, D_MODEL), F32),
        compiler_params=pltpu.CompilerParams(dimension_semantics=("parallel",), vmem_limit_bytes=VMEM_LIMIT),
        name="post",
    )(x2, oa, ob, sa, sb, *consts)


def _pack_weights(w_in, mla_w_uq, mla_w_ukv, gla_w_gate2):
    offs = [0]
    for s in IN_SIZES:
        offs.append(offs[-1] + s)
    col = lambda idx: w_in[:, offs[idx]:offs[idx + 1]]
    d = w_in.shape[0]
    misc = jnp.concatenate([jnp.zeros((d, ROPE_LO), F32), col(2), col(6),
                            jnp.zeros((d, LANES - GLR_LO - GLA_RANK), F32)], axis=1)
    winp = jnp.concatenate([col(0), col(1), col(3), col(4), col(5), col(7), col(8), col(9), misc],
                           axis=1).astype(BF16)
    wuq = mla_w_uq.reshape(MLA_Q_RANK, MLA_HEADS, MLA_QK)
    wuqp = jnp.pad(wuq, ((0, 0), (0, 0), (0, HEAD_PAD - MLA_QK))).reshape(MLA_Q_RANK, MLA_HEADS * HEAD_PAD)
    wukv = mla_w_ukv.reshape(MLA_KV_RANK, MLA_HEADS, MLA_NOPE + MLA_V)
    wkp = jnp.pad(wukv[:, :, :MLA_NOPE], ((0, 0), (0, 0), (0, HEAD_PAD - MLA_NOPE)))
    wkp = wkp.reshape(MLA_KV_RANK, MLA_HEADS * HEAD_PAD)
    wvt = wukv[:, :, MLA_NOPE:].reshape(MLA_KV_RANK, MLA_WIDTH).T
    wg2p = jnp.pad(gla_w_gate2, ((GLR_LO, LANES - GLR_LO - GLA_RANK), (0, 0)))
    return winp, wuqp.astype(BF16), wkp.astype(BF16), wvt.astype(BF16), wg2p.astype(BF16)


def _rope_inv_freq():
    half = ROPE_HALF
    inv = 1.0 / (ROPE_THETA ** (jnp.arange(half, dtype=F32) / half))
    return jnp.concatenate([jnp.zeros((ROPE_LO,), F32), inv, inv,
                            jnp.zeros((LANES - ROPE_LO - MLA_ROPE,), F32)]).reshape(1, LANES)


def _tile(n, pref):
    while n % pref:
        pref //= 2
    return pref


def kernel(x, positions, ln_mix, w_in, mla_norm_cq, mla_w_uq, mla_norm_ckv, mla_w_ukv, mla_w_o, gla_w_gate2,
           gla_b_gate, gla_norm, gla_w_o, w_out, ln_ffn, ffn_w_gate, ffn_w_up, ffn_w_down, final_norm):
    batch, seq, d = x.shape
    assert d == D_MODEL and w_in.shape[0] == 1 and seq % GLA_CHUNK == 0
    t = batch * seq
    x2 = x.reshape(t, d)
    pos2 = positions.reshape(t, 1).astype(F32)
    winp, wuqp, wkp, wvt, wg2p = _pack_weights(w_in[0], mla_w_uq[0], mla_w_ukv[0], gla_w_gate2[0])
    r = lambda a: a.reshape(1, -1)

    tq = _tile(seq, ATTN_TILE)
    q, k, vt, gq, gk, gv, la, og, sa, sb = _proj_call(
        x2, pos2, r(ln_mix[0]), winp, r(mla_norm_cq[0]), wuqp, r(mla_norm_ckv[0]), wkp, wvt, wg2p,
        r(gla_b_gate[0]), _rope_inv_freq(), tq)
    oa = _attn_call(q, k, vt, batch, seq, tq)
    ob = _gla_call(gq, gk, gv, la, og, r(gla_norm[0]), batch, seq, _tile(seq, 512))
    out = _post_call(x2, oa, ob, sa, sb, mla_w_o[0].astype(BF16), gla_w_o[0].astype(BF16),
                     w_out[0].astype(BF16), r(ln_ffn[0]), ffn_w_gate[0].astype(BF16),
                     ffn_w_up[0].astype(BF16), ffn_w_down[0].astype(BF16), r(final_norm), _tile(t, 256))
    return out.reshape(batch, seq, d)
```

```python
import functools
import math

import jax
import jax.numpy as jnp
import numpy as np
from jax import lax
from jax.experimental import pallas as pl
from jax.experimental.pallas import tpu as pltpu

F32 = jnp.float32
BF16 = jnp.bfloat16

D_MODEL = 1024
MLA_HEADS = 8
MLA_NOPE = 64
MLA_ROPE = 32
MLA_QK = MLA_NOPE + MLA_ROPE
MLA_V = 64
MLA_WIDTH = MLA_HEADS * MLA_V
MLA_Q_RANK = 384
MLA_KV_RANK = 256
ROPE_THETA = 10000.0
GLA_HEADS = 4
GLA_DV = 512
GLA_DK = 256
GLA_HK = GLA_DK // GLA_HEADS
GLA_HV = GLA_DV // GLA_HEADS
GLA_RANK = 16
GLA_TAU = 16.0
D_FF = 2816
EPS = 1e-6
LOG2E = 1.4426950408889634
IN_SIZES = (MLA_Q_RANK, MLA_KV_RANK, MLA_ROPE, GLA_DK, GLA_DK, GLA_DV, GLA_RANK, GLA_DV, D_MODEL, D_MODEL)

LANES = 128
HEAD_PAD = LANES
ROPE_LO = MLA_NOPE
ROPE_HALF = MLA_ROPE // 2
GLR_LO = ROPE_LO + MLA_ROPE

OFF_CQ = 0
OFF_CKV = OFF_CQ + MLA_Q_RANK
OFF_GQ = OFF_CKV + MLA_KV_RANK
OFF_GK = OFF_GQ + GLA_DK
OFF_GV = OFF_GK + GLA_DK
OFF_OG = OFF_GV + GLA_DV
OFF_GA = OFF_OG + GLA_DV
OFF_GB = OFF_GA + D_MODEL
OFF_MISC = OFF_GB + D_MODEL
D_IN_PACKED = OFF_MISC + LANES

PROJ_TILE = 512
ATTN_TILE = 256
ATTN_LOOKAHEAD = 8
GLA_CHUNK = 64
GLA_SUB = 16
GLA_PAR = 4
VMEM_LIMIT = 56 * 1024 * 1024


def _dot(a, b):
    return jnp.dot(a, b, preferred_element_type=F32)


def _dot_nt(a, b):
    return lax.dot_general(a, b, (((1,), (1,)), ((), ())), preferred_element_type=F32)


def _dot_tn(a, b):
    return lax.dot_general(a, b, (((0,), (0,)), ((), ())), preferred_element_type=F32)


def _rms(x, w):
    return x * lax.rsqrt(jnp.mean(x * x, axis=-1, keepdims=True) + EPS) * w


def _proj_kernel(x_ref, pos_ref, lnmix_ref, win_ref, ncq_ref, wuq_ref, nckv_ref, wk_ref, wvt_ref,
                 wg2_ref, bg_ref, invf_ref, place_ref,
                 q_ref, k_ref, vt_ref, gq_ref, gk_ref, gv_ref, la_ref, og_ref, sa_ref, sb_ref, *, tkv):
    tm = x_ref.shape[0]
    ub = _rms(x_ref[...], lnmix_ref[...]).astype(BF16)

    def seg(lo, width):
        return _dot(ub, win_ref[:, lo:lo + width])

    misc = seg(OFF_MISC, LANES)
    cq = seg(OFF_CQ, MLA_Q_RANK)
    ckv = seg(OFF_CKV, MLA_KV_RANK)

    ang = invf_ref[...] * pos_ref[0]
    cs = jnp.concatenate([jnp.cos(ang), jnp.sin(ang)], axis=0)
    cs_hi = cs.astype(BF16)
    cs_lo = (cs - cs_hi.astype(F32)).astype(BF16)
    tabs = _dot_tn(jnp.concatenate([cs_hi, cs_lo], axis=0), place_ref[...])
    lane = lax.broadcasted_iota(jnp.int32, (1, LANES), 1)
    in_rope = (lane >= ROPE_LO) & (lane < ROPE_LO + MLA_ROPE)
    cos = tabs[:, :LANES] + jnp.where(in_rope, 0.0, 1.0)
    sin_x1 = tabs[:, LANES:2 * LANES]
    sin_x2 = tabs[:, 2 * LANES:]

    def rope(t):
        return (t * cos + pltpu.roll(t, LANES - ROPE_HALF, axis=1) * sin_x1
                + pltpu.roll(t, ROPE_HALF, axis=1) * sin_x2)

    sa_ref[...] = jax.nn.sigmoid(seg(OFF_GA, D_MODEL)).astype(BF16)
    cqn = _rms(cq, ncq_ref[...]).astype(BF16)
    ckvn = _rms(ckv, nckv_ref[...]).astype(BF16)
    sb_ref[...] = jax.nn.sigmoid(seg(OFF_GB, D_MODEL)).astype(BF16)
    q = _dot(cqn, wuq_ref[...]) * (MLA_QK ** -0.5 * LOG2E)
    k_nope = _dot(ckvn, wk_ref[...])
    for c in range(tm // tkv):
        vt_ref[c] = _dot_nt(wvt_ref[...], ckvn[c * tkv:(c + 1) * tkv, :]).astype(BF16)
    gv_ref[...] = seg(OFF_GV, GLA_DV).astype(BF16)
    og_ref[...] = seg(OFF_OG, GLA_DV).astype(BF16)
    gq_ref[...] = (seg(OFF_GQ, GLA_DK) * (GLA_HK ** -0.5)).astype(BF16)
    gk_ref[...] = seg(OFF_GK, GLA_DK).astype(BF16)

    k_rope = rope(jnp.where(in_rope, misc, 0.0))
    for h in range(MLA_HEADS):
        sl = slice(h * HEAD_PAD, (h + 1) * HEAD_PAD)
        q_ref[:, sl] = rope(q[:, sl]).astype(BF16)
        k_ref[:, sl] = (k_nope[:, sl] + k_rope).astype(BF16)
    xg = _dot(misc.astype(BF16), wg2_ref[...]) + bg_ref[...]
    la_ref[...] = (jnp.minimum(xg, 0.0) - jnp.log1p(jnp.exp(-jnp.abs(xg)))) * (1.0 / GLA_TAU)


def _const_spec(shape):
    return pl.BlockSpec(shape, lambda *_: (0,) * len(shape))


def _single_spec(a):
    return pl.BlockSpec(a.shape, lambda *_: (0,) * a.ndim, pipeline_mode=pl.Buffered(1))


def _proj_call(x2, pos3, lnmix, winp, ncq, wuqp, nckv, wkp, wvt, wg2p, bg, invf, place, tm, tkv):
    t = x2.shape[0]
    row = lambda w: pl.BlockSpec((tm, w), lambda i: (i, 0))
    consts = (lnmix, winp, ncq, wuqp, nckv, wkp, wvt, wg2p, bg, invf, place)
    out_widths = (MLA_HEADS * HEAD_PAD, MLA_HEADS * HEAD_PAD, None, GLA_DK, GLA_DK, GLA_DV, GLA_DK,
                  GLA_DV, D_MODEL, D_MODEL)
    out_dtypes = (BF16, BF16, BF16, BF16, BF16, BF16, F32, BF16, BF16, BF16)
    vt_spec = pl.BlockSpec((tm // tkv, MLA_WIDTH, tkv), lambda i: (i, 0, 0))
    vt_shape = jax.ShapeDtypeStruct((t // tkv, MLA_WIDTH, tkv), BF16)
    return pl.pallas_call(
        functools.partial(_proj_kernel, tkv=tkv),
        grid=(t // tm,),
        in_specs=[row(D_MODEL), pl.BlockSpec((1, 1, tm), lambda i: (i, 0, 0))] + [_single_spec(c) for c in consts],
        out_specs=[vt_spec if w is None else row(w) for w in out_widths],
        out_shape=[vt_shape if w is None else jax.ShapeDtypeStruct((t, w), d)
                   for w, d in zip(out_widths, out_dtypes)],
        compiler_params=pltpu.CompilerParams(dimension_semantics=("parallel",), vmem_limit_bytes=VMEM_LIMIT),
        name="proj",
    )(x2, pos3, *consts)


def _attn_kernel(q_ref, k_ref, vt_ref, o_ref, ot_ref, m_ref, l_ref, s_ref, *, tq):
    i = pl.program_id(1)
    key = lax.broadcasted_iota(jnp.int32, (tq, tq), 0)
    qry = lax.broadcasted_iota(jnp.int32, (tq, tq), 1)
    causal = key <= qry
    m_ref[...] = jnp.full(m_ref.shape, -1e30, F32)
    l_ref[...] = jnp.zeros(l_ref.shape, F32)
    ot_ref[...] = jnp.zeros(ot_ref.shape, F32)

    def step(j, masked):
        rows = pl.ds(pl.multiple_of(j * tq, tq), tq)

        def scores(h):
            hq = slice(h * HEAD_PAD, (h + 1) * HEAD_PAD)
            s = _dot_nt(k_ref[rows, hq], q_ref[:, hq])
            if masked:
                s = jnp.where(causal, s, -1e30)
            s_ref[h] = s
            return jnp.max(s, axis=0, keepdims=True)

        def accumulate(h, s_max):
            hv = slice(h * MLA_V, (h + 1) * MLA_V)
            m = m_ref[h:h + 1, :]
            m_new = jnp.maximum(m, s_max)
            alpha = jnp.exp2(m - m_new)
            p = jnp.exp2(s_ref[h] - m_new)
            m_ref[h:h + 1, :] = m_new
            l_ref[h:h + 1, :] = alpha * l_ref[h:h + 1, :] + jnp.sum(p, axis=0, keepdims=True)
            ot_ref[hv, :] = alpha * ot_ref[hv, :] + _dot(vt_ref[j, hv, :], p.astype(BF16))

        pending = {}
        for h in range(MLA_HEADS + ATTN_LOOKAHEAD):
            if h < MLA_HEADS:
                pending[h] = scores(h)
            if h >= ATTN_LOOKAHEAD:
                accumulate(h - ATTN_LOOKAHEAD, pending.pop(h - ATTN_LOOKAHEAD))

    def body(j, carry):
        step(j, False)
        return carry

    lax.fori_loop(0, i, body, 0)
    step(i, True)
    for h in range(MLA_HEADS):
        hv = slice(h * MLA_V, (h + 1) * MLA_V)
        ot_ref[hv, :] = ot_ref[hv, :] / l_ref[h:h + 1, :]
    o_ref[...] = ot_ref[...].T.astype(BF16)


def _attn_call(q, k, vt, batch, seq, tq):
    nq = seq // tq
    return pl.pallas_call(
        functools.partial(_attn_kernel, tq=tq),
        grid=(batch, nq),
        in_specs=[pl.BlockSpec((tq, MLA_HEADS * HEAD_PAD), lambda b, i: (b * nq + i, 0)),
                  pl.BlockSpec((seq, MLA_HEADS * HEAD_PAD), lambda b, i: (b, 0)),
                  pl.BlockSpec((nq, MLA_WIDTH, tq), lambda b, i: (b, 0, 0))],
        out_specs=pl.BlockSpec((tq, MLA_WIDTH), lambda b, i: (b * nq + i, 0)),
        out_shape=jax.ShapeDtypeStruct((batch * seq, MLA_WIDTH), BF16),
        scratch_shapes=[pltpu.VMEM((MLA_WIDTH, tq), F32), pltpu.VMEM((MLA_HEADS, tq), F32),
                        pltpu.VMEM((MLA_HEADS, tq), F32), pltpu.VMEM((MLA_HEADS, tq, tq), F32)],
        compiler_params=pltpu.CompilerParams(dimension_semantics=("parallel", "arbitrary"),
                                             vmem_limit_bytes=VMEM_LIMIT),
        name="mla_attn",
    )(q, k, vt)


def _gla_kernel(gq_ref, gk_ref, gv_ref, la_ref, og_ref, norm_ref, o_ref, state_ref, kpad_ref, lapad_ref,
                *, slabs_per_step, par):
    c_, s_ = GLA_CHUNK, GLA_SUB
    nsub = c_ // s_
    slab_rows = par * c_
    pair = 2 * GLA_HK

    @pl.when(pl.program_id(1) == 0)
    def _():
        state_ref[...] = jnp.zeros_like(state_ref)

    kpad_ref[0:s_, :] = jnp.zeros((s_, GLA_DK), F32)
    lapad_ref[0:s_, :] = jnp.zeros((s_, GLA_DK), F32)

    ri = lax.broadcasted_iota(jnp.int32, (c_, c_), 0)
    ci = lax.broadcasted_iota(jnp.int32, (c_, c_), 1)
    tri = jnp.where((ri // s_ == ci // s_) & (ci <= ri), 1.0, 0.0).astype(BF16)
    rl = lax.broadcasted_iota(jnp.int32, (c_, GLA_DK), 0)
    ll = lax.broadcasted_iota(jnp.int32, (c_, GLA_DK), 1)
    row_minus_col = rl - ll % GLA_HK
    diag_key = jnp.where((row_minus_col >= 0) & (rl % s_ >= row_minus_col), row_minus_col, -1)
    head_of_lane = ll // GLA_HK
    odd_head_lane = (lax.broadcasted_iota(jnp.int32, (c_, pair), 1) // GLA_HK) == 1
    di = lax.broadcasted_iota(jnp.int32, (GLA_DK, GLA_DK), 0) // GLA_HK
    dj = lax.broadcasted_iota(jnp.int32, (GLA_DK, GLA_DK), 1) // GLA_HK
    head_sum = jnp.where(di == dj, 1.0, 0.0).astype(BF16)
    norm_w = norm_ref[...]

    def blocks(fn):
        return jnp.concatenate([fn(b) for b in range(nsub)], axis=0)

    def sub(t, b):
        return t[b * s_:(b + 1) * s_, :]

    def slab(sidx, _):
        base = pl.multiple_of(sidx * slab_rows, slab_rows)
        kpad_ref[s_:, :] = gk_ref[pl.ds(base, slab_rows), :].astype(F32)
        lapad_ref[s_:, :] = la_ref[pl.ds(base, slab_rows), :]
        chunks = [dict(rows=pl.ds(pl.multiple_of(base + c * c_, c_), c_), off=s_ + c * c_) for c in range(par)]

        for ch in chunks:
            la = la_ref[ch["rows"], :]
            hi = la.astype(BF16)
            r1 = la - hi.astype(F32)
            mid = r1.astype(BF16)
            lo = (r1 - mid.astype(F32)).astype(BF16)
            ch["l16"] = _dot(tri, hi) + _dot(tri, mid) + _dot(tri, lo)

        for ch in chunks:
            q = gq_ref[ch["rows"], :].astype(F32)
            k = gk_ref[ch["rows"], :].astype(F32)
            l16 = ch["l16"]
            tot = [l16[(b + 1) * s_ - 1:(b + 1) * s_, :] for b in range(nsub)]
            q16 = q * jnp.exp(l16)
            k16 = k * jnp.exp(blocks(lambda b: jnp.broadcast_to(tot[b], (s_, GLA_DK))) - l16)
            zero = jnp.zeros((s_, GLA_DK), F32)
            e = [jnp.exp(t) for t in tot]
            pre = [None, e[0], e[0] * e[1], e[0] * e[1] * e[2]]
            post = [e[1] * e[2] * e[3], e[2] * e[3], e[3], None]
            ch["qd"] = blocks(lambda b: sub(q16, b) if pre[b] is None else sub(q16, b) * pre[b]).astype(BF16)
            kd = blocks(lambda b: sub(k16, b) if post[b] is None else sub(k16, b) * post[b]).astype(BF16)
            ch["decay"] = jnp.exp(tot[0] + tot[1] + tot[2] + tot[3])
            v = gv_ref[ch["rows"], :]
            ch["upd"] = sum(_dot_tn(v[:, h * GLA_HV:(h + 1) * GLA_HV], jnp.where(head_of_lane == h, kd, 0.0))
                            for h in range(GLA_HEADS))
            ch["q_lv"] = [t.astype(BF16) for t in (
                jnp.concatenate([zero, zero, sub(q16, 2), sub(q16, 3) * e[2]], axis=0),
                jnp.concatenate([zero, sub(q16, 1), zero, zero], axis=0),
                jnp.concatenate([zero, zero, zero, sub(q16, 3)], axis=0))]
            ch["k_lv"] = [t.astype(BF16) for t in (
                jnp.concatenate([sub(k16, 0) * e[1], sub(k16, 1), zero, zero], axis=0),
                jnp.concatenate([sub(k16, 0), zero, zero, zero], axis=0),
                jnp.concatenate([zero, zero, sub(k16, 2), zero], axis=0))]
            decay = jnp.zeros((c_, GLA_DK), F32)
            prods = []
            for d in range(s_):
                if d > 0:
                    decay = decay + lapad_ref[ch["off"] - d + 1:ch["off"] - d + 1 + c_, :]
                k_sh = kpad_ref[ch["off"] - d:ch["off"] - d + c_, :]
                prods.append((q * k_sh * jnp.exp(decay)).astype(BF16))
            ch["summed"] = _dot(jnp.concatenate(prods, axis=0), head_sum)

        for ch in chunks:
            ch["attn_off"] = []
            for h in range(GLA_HEADS):
                lanes = slice((h // 2) * pair, (h // 2 + 1) * pair)
                mine = odd_head_lane if h % 2 else ~odd_head_lane
                qx = jnp.concatenate([jnp.where(mine, t[:, lanes], 0.0) for t in ch["q_lv"]], axis=1)
                kx = jnp.concatenate([t[:, lanes] for t in ch["k_lv"]], axis=1)
                ch["attn_off"].append(_dot_nt(qx, kx))

        state = state_ref[...]
        for ch in chunks:
            ch["state_b"] = state.astype(BF16)
            state = state * ch["decay"] + ch["upd"]
        state_ref[...] = state

        for ch in chunks:
            attn_diag = jnp.zeros((c_, GLA_DK), F32)
            for d in range(s_):
                attn_diag = jnp.where(diag_key == d, ch["summed"][d * c_:(d + 1) * c_, :], attn_diag)
            v = gv_ref[ch["rows"], :]
            for h in range(GLA_HEADS):
                hv = slice(h * GLA_HV, (h + 1) * GLA_HV)
                attn = attn_diag[:, h * GLA_HK:(h + 1) * GLA_HK] + ch["attn_off"][h]
                o = _dot(attn.astype(BF16), v[:, hv])
                o = o + _dot_nt(jnp.where(head_of_lane == h, ch["qd"], 0.0), ch["state_b"])
                o = _rms(o, norm_w)
                g = og_ref[ch["rows"], hv].astype(F32)
                o_ref[ch["rows"], hv] = (o * (g * jax.nn.sigmoid(g))).astype(BF16)
        return 0

    lax.fori_loop(0, slabs_per_step, slab, 0)


def _gla_call(gq, gk, gv, la, og, norm, batch, seq, rows_per_step):
    ns = seq // rows_per_step
    par = math.gcd(rows_per_step // GLA_CHUNK, GLA_PAR)
    row = lambda w: pl.BlockSpec((rows_per_step, w), lambda b, i: (b * ns + i, 0))
    return pl.pallas_call(
        functools.partial(_gla_kernel, slabs_per_step=rows_per_step // (par * GLA_CHUNK), par=par),
        grid=(batch, ns),
        in_specs=[row(GLA_DK), row(GLA_DK), row(GLA_DV), row(GLA_DK), row(GLA_DV), _const_spec(norm.shape)],
        out_specs=row(GLA_DV),
        out_shape=jax.ShapeDtypeStruct((batch * seq, GLA_DV), BF16),
        scratch_shapes=[pltpu.VMEM((GLA_HV, GLA_DK), F32),
                        pltpu.VMEM((GLA_SUB + par * GLA_CHUNK, GLA_DK), F32),
                        pltpu.VMEM((GLA_SUB + par * GLA_CHUNK, GLA_DK), F32)],
        compiler_params=pltpu.CompilerParams(dimension_semantics=("parallel", "arbitrary"),
                                             vmem_limit_bytes=VMEM_LIMIT),
        name="gla",
    )(gq, gk, gv, la, og, norm)


def _post_kernel(x_ref, oa_ref, ob_ref, sa_ref, sb_ref, woa_ref, wob_ref, wout_ref, lnffn_ref,
                 wg_ref, wu_ref, wd_ref, fnorm_ref, out_ref):
    ya = _dot(oa_ref[...], woa_ref[...])
    yb = _dot(ob_ref[...], wob_ref[...])
    mix = sa_ref[...].astype(F32) * ya + sb_ref[...].astype(F32) * yb
    h = x_ref[...] + _dot(mix.astype(BF16), wout_ref[...])
    u = _rms(h, lnffn_ref[...]).astype(BF16)
    g = _dot(u, wg_ref[...])
    act = (g * jax.nn.sigmoid(g) * _dot(u, wu_ref[...])).astype(BF16)
    h = h + _dot(act, wd_ref[...])
    out_ref[...] = _rms(h, fnorm_ref[...])


def _post_call(x2, oa, ob, sa, sb, woa, wob, wout, lnffn, wg, wu, wd, fnorm, tm):
    t = x2.shape[0]
    row = lambda w: pl.BlockSpec((tm, w), lambda i: (i, 0))
    single = lambda a: pl.BlockSpec(a.shape, lambda i: (0,) * a.ndim, pipeline_mode=pl.Buffered(1))
    consts = (woa, wob, wout, lnffn, wg, wu, wd, fnorm)
    return pl.pallas_call(
        _post_kernel,
        grid=(t // tm,),
        in_specs=[row(D_MODEL), row(MLA_WIDTH), row(GLA_DV), row(D_MODEL), row(D_MODEL)]
                 + [single(c) for c in consts],
        out_specs=row(D_MODEL),
        out_shape=jax.ShapeDtypeStruct((t, D_MODEL), F32),
        compiler_params=pltpu.CompilerParams(dimension_semantics=("parallel",), vmem_limit_bytes=VMEM_LIMIT),
        name="post",
    )(x2, oa, ob, sa, sb, *consts)


def _pack_weights(w_in, mla_w_uq, mla_w_ukv, gla_w_gate2):
    offs = [0]
    for s in IN_SIZES:
        offs.append(offs[-1] + s)
    col = lambda idx: w_in[:, offs[idx]:offs[idx + 1]]
    d = w_in.shape[0]
    misc = jnp.concatenate([jnp.zeros((d, ROPE_LO), F32), col(2), col(6),
                            jnp.zeros((d, LANES - GLR_LO - GLA_RANK), F32)], axis=1)
    winp = jnp.concatenate([col(0), col(1), col(3), col(4), col(5), col(7), col(8), col(9), misc],
                           axis=1).astype(BF16)
    wuq = mla_w_uq.reshape(MLA_Q_RANK, MLA_HEADS, MLA_QK)
    wuqp = jnp.pad(wuq, ((0, 0), (0, 0), (0, HEAD_PAD - MLA_QK))).reshape(MLA_Q_RANK, MLA_HEADS * HEAD_PAD)
    wukv = mla_w_ukv.reshape(MLA_KV_RANK, MLA_HEADS, MLA_NOPE + MLA_V)
    wkp = jnp.pad(wukv[:, :, :MLA_NOPE], ((0, 0), (0, 0), (0, HEAD_PAD - MLA_NOPE)))
    wkp = wkp.reshape(MLA_KV_RANK, MLA_HEADS * HEAD_PAD)
    wvt = wukv[:, :, MLA_NOPE:].reshape(MLA_KV_RANK, MLA_WIDTH).T
    wg2p = jnp.pad(gla_w_gate2, ((GLR_LO, LANES - GLR_LO - GLA_RANK), (0, 0)))
    return winp, wuqp.astype(BF16), wkp.astype(BF16), wvt.astype(BF16), wg2p.astype(BF16)


def _rope_inv_freq():
    half = ROPE_HALF
    inv = 1.0 / (ROPE_THETA ** (jnp.arange(half, dtype=F32) / half))
    return inv.reshape(half, 1)


def _rope_placement():
    place = np.zeros((4 * ROPE_HALF, 3 * LANES), np.float32)
    for piece in range(2):
        for i in range(ROPE_HALF):
            cos_row = 2 * piece * ROPE_HALF + i
            sin_row = cos_row + ROPE_HALF
            place[cos_row, ROPE_LO + i] = 1.0
            place[cos_row, ROPE_LO + ROPE_HALF + i] = 1.0
            place[sin_row, LANES + ROPE_LO + i] = -1.0
            place[sin_row, 2 * LANES + ROPE_LO + ROPE_HALF + i] = 1.0
    return jnp.asarray(place, BF16)


def _tile(n, pref):
    while n % pref:
        pref //= 2
    return pref


def kernel(x, positions, ln_mix, w_in, mla_norm_cq, mla_w_uq, mla_norm_ckv, mla_w_ukv, mla_w_o, gla_w_gate2,
           gla_b_gate, gla_norm, gla_w_o, w_out, ln_ffn, ffn_w_gate, ffn_w_up, ffn_w_down, final_norm):
    batch, seq, d = x.shape
    assert d == D_MODEL and w_in.shape[0] == 1 and seq % GLA_CHUNK == 0
    t = batch * seq
    x2 = x.reshape(t, d)
    winp, wuqp, wkp, wvt, wg2p = _pack_weights(w_in[0], mla_w_uq[0], mla_w_ukv[0], gla_w_gate2[0])
    r = lambda a: a.reshape(1, -1)

    tq = _tile(seq, ATTN_TILE)
    tm = max(tq, _tile(t, PROJ_TILE))
    pos3 = positions.reshape(t // tm, 1, tm).astype(F32)
    q, k, vt, gq, gk, gv, la, og, sa, sb = _proj_call(
        x2, pos3, r(ln_mix[0]), winp, r(mla_norm_cq[0]), wuqp, r(mla_norm_ckv[0]), wkp, wvt, wg2p,
        r(gla_b_gate[0]), _rope_inv_freq(), _rope_placement(), tm, tq)
    oa = _attn_call(q, k, vt, batch, seq, tq)
    ob = _gla_call(gq, gk, gv, la, og, r(gla_norm[0]), batch, seq, _tile(seq, 512))
    out = _post_call(x2, oa, ob, sa, sb, mla_w_o[0].astype(BF16), gla_w_o[0].astype(BF16),
                     w_out[0].astype(BF16), r(ln_ffn[0]), ffn_w_gate[0].astype(BF16),
                     ffn_w_up[0].astype(BF16), ffn_w_down[0].astype(BF16), r(final_norm), _tile(t, 256))
    return out.reshape(batch, seq, d)
```

```python
import functools
import math

import jax
import jax.numpy as jnp
import numpy as np
from jax import lax
from jax.experimental import pallas as pl
from jax.experimental.pallas import tpu as pltpu

F32 = jnp.float32
BF16 = jnp.bfloat16

D_MODEL = 1024
MLA_HEADS = 8
MLA_NOPE = 64
MLA_ROPE = 32
MLA_QK = MLA_NOPE + MLA_ROPE
MLA_V = 64
MLA_WIDTH = MLA_HEADS * MLA_V
MLA_Q_RANK = 384
MLA_KV_RANK = 256
ROPE_THETA = 10000.0
GLA_HEADS = 4
GLA_DV = 512
GLA_DK = 256
GLA_HK = GLA_DK // GLA_HEADS
GLA_HV = GLA_DV // GLA_HEADS
GLA_RANK = 16
GLA_TAU = 16.0
D_FF = 2816
EPS = 1e-6
LOG2E = 1.4426950408889634
IN_SIZES = (MLA_Q_RANK, MLA_KV_RANK, MLA_ROPE, GLA_DK, GLA_DK, GLA_DV, GLA_RANK, GLA_DV, D_MODEL, D_MODEL)

LANES = 128
HEAD_PAD = LANES
ROPE_LO = MLA_NOPE
ROPE_HALF = MLA_ROPE // 2
GLR_LO = ROPE_LO + MLA_ROPE

OFF_CQ = 0
OFF_CKV = OFF_CQ + MLA_Q_RANK
OFF_GQ = OFF_CKV + MLA_KV_RANK
OFF_GK = OFF_GQ + GLA_DK
OFF_GV = OFF_GK + GLA_DK
OFF_OG = OFF_GV + GLA_DV
OFF_GA = OFF_OG + GLA_DV
OFF_GB = OFF_GA + D_MODEL
OFF_MISC = OFF_GB + D_MODEL
D_IN_PACKED = OFF_MISC + LANES

PROJ_TILE = 512
ATTN_TILE = 256
MLA_VA = MLA_V + 16
GLA_CHUNK = 64
GLA_SUB = 16
GLA_PAR = 4
VMEM_LIMIT = 56 * 1024 * 1024


def _dot(a, b):
    return jnp.dot(a, b, preferred_element_type=F32)


def _dot_nt(a, b):
    return lax.dot_general(a, b, (((1,), (1,)), ((), ())), preferred_element_type=F32)


def _dot_tn(a, b):
    return lax.dot_general(a, b, (((0,), (0,)), ((), ())), preferred_element_type=F32)


def _rms(x, w):
    return x * lax.rsqrt(jnp.mean(x * x, axis=-1, keepdims=True) + EPS) * w


def _proj_kernel(x_ref, pos_ref, lnmix_ref, win_ref, ncq_ref, wuq_ref, nckv_ref, wk_ref, wvt_ref,
                 wg2_ref, bg_ref, invf_ref, place_ref,
                 q_ref, k_ref, vt_ref, gq_ref, gk_ref, gv_ref, la_ref, og_ref, sa_ref, sb_ref, *, tkv):
    tm = x_ref.shape[0]
    ub = _rms(x_ref[...], lnmix_ref[...]).astype(BF16)

    def seg(lo, width):
        return _dot(ub, win_ref[:, lo:lo + width])

    misc = seg(OFF_MISC, LANES)
    cq = seg(OFF_CQ, MLA_Q_RANK)
    ckv = seg(OFF_CKV, MLA_KV_RANK)

    ang = invf_ref[...] * pos_ref[0]
    cs = jnp.concatenate([jnp.cos(ang), jnp.sin(ang)], axis=0)
    cs_hi = cs.astype(BF16)
    cs_lo = (cs - cs_hi.astype(F32)).astype(BF16)
    tabs = _dot_tn(jnp.concatenate([cs_hi, cs_lo], axis=0), place_ref[...])
    lane = lax.broadcasted_iota(jnp.int32, (1, LANES), 1)
    in_rope = (lane >= ROPE_LO) & (lane < ROPE_LO + MLA_ROPE)
    cos = tabs[:, :LANES] + jnp.where(in_rope, 0.0, 1.0)
    sin_x1 = tabs[:, LANES:2 * LANES]
    sin_x2 = tabs[:, 2 * LANES:]

    def rope(t):
        return (t * cos + pltpu.roll(t, LANES - ROPE_HALF, axis=1) * sin_x1
                + pltpu.roll(t, ROPE_HALF, axis=1) * sin_x2)

    sa_ref[...] = jax.nn.sigmoid(seg(OFF_GA, D_MODEL)).astype(BF16)
    cqn = _rms(cq, ncq_ref[...]).astype(BF16)
    ckvn = _rms(ckv, nckv_ref[...]).astype(BF16)
    sb_ref[...] = jax.nn.sigmoid(seg(OFF_GB, D_MODEL)).astype(BF16)
    q = _dot(cqn, wuq_ref[...]) * (MLA_QK ** -0.5 * LOG2E)
    k_nope = _dot(ckvn, wk_ref[...])
    for c in range(tm // tkv):
        vt = _dot_nt(wvt_ref[...], ckvn[c * tkv:(c + 1) * tkv, :]).astype(BF16)
        for h in range(MLA_HEADS):
            vt_ref[c, h * MLA_VA:h * MLA_VA + MLA_V, :] = vt[h * MLA_V:(h + 1) * MLA_V, :]
            vt_ref[c, h * MLA_VA + MLA_V:(h + 1) * MLA_VA, :] = jnp.ones((MLA_VA - MLA_V, tkv), BF16)
    gv_ref[...] = seg(OFF_GV, GLA_DV).astype(BF16)
    og_ref[...] = seg(OFF_OG, GLA_DV).astype(BF16)
    gq_ref[...] = (seg(OFF_GQ, GLA_DK) * (GLA_HK ** -0.5)).astype(BF16)
    gk_ref[...] = seg(OFF_GK, GLA_DK).astype(BF16)

    k_rope = rope(jnp.where(in_rope, misc, 0.0))
    for h in range(MLA_HEADS):
        sl = slice(h * HEAD_PAD, (h + 1) * HEAD_PAD)
        q_ref[:, sl] = rope(q[:, sl]).astype(BF16)
        k_ref[:, sl] = (k_nope[:, sl] + k_rope).astype(BF16)
    xg = _dot(misc.astype(BF16), wg2_ref[...]) + bg_ref[...]
    la_ref[...] = (jnp.minimum(xg, 0.0) - jnp.log1p(jnp.exp(-jnp.abs(xg)))) * (1.0 / GLA_TAU)


def _const_spec(shape):
    return pl.BlockSpec(shape, lambda *_: (0,) * len(shape))


def _single_spec(a):
    return pl.BlockSpec(a.shape, lambda *_: (0,) * a.ndim, pipeline_mode=pl.Buffered(1))


def _proj_call(x2, pos3, lnmix, winp, ncq, wuqp, nckv, wkp, wvt, wg2p, bg, invf, place, tm, tkv):
    t = x2.shape[0]
    row = lambda w: pl.BlockSpec((tm, w), lambda i: (i, 0))
    consts = (lnmix, winp, ncq, wuqp, nckv, wkp, wvt, wg2p, bg, invf, place)
    out_widths = (MLA_HEADS * HEAD_PAD, MLA_HEADS * HEAD_PAD, None, GLA_DK, GLA_DK, GLA_DV, GLA_DK,
                  GLA_DV, D_MODEL, D_MODEL)
    out_dtypes = (BF16, BF16, BF16, BF16, BF16, BF16, F32, BF16, BF16, BF16)
    vt_spec = pl.BlockSpec((tm // tkv, MLA_HEADS * MLA_VA, tkv), lambda i: (i, 0, 0))
    vt_shape = jax.ShapeDtypeStruct((t // tkv, MLA_HEADS * MLA_VA, tkv), BF16)
    return pl.pallas_call(
        functools.partial(_proj_kernel, tkv=tkv),
        grid=(t // tm,),
        in_specs=[row(D_MODEL), pl.BlockSpec((1, 1, tm), lambda i: (i, 0, 0))] + [_single_spec(c) for c in consts],
        out_specs=[vt_spec if w is None else row(w) for w in out_widths],
        out_shape=[vt_shape if w is None else jax.ShapeDtypeStruct((t, w), d)
                   for w, d in zip(out_widths, out_dtypes)],
        compiler_params=pltpu.CompilerParams(dimension_semantics=("parallel",), vmem_limit_bytes=VMEM_LIMIT),
        name="proj",
    )(x2, pos3, *consts)


def _attn_kernel(q_ref, k_ref, vt_ref, o_ref, acc_ref, ot_ref, m_ref, s0_ref, s1_ref, smax0_ref, smax1_ref, *, tq):
    i = pl.program_id(1)
    key = lax.broadcasted_iota(jnp.int32, (tq, tq), 0)
    qry = lax.broadcasted_iota(jnp.int32, (tq, tq), 1)
    causal = key <= qry
    m_ref[...] = jnp.full(m_ref.shape, -1e30, F32)
    acc_ref[...] = jnp.zeros(acc_ref.shape, F32)
    slots = ((s0_ref, smax0_ref), (s1_ref, smax1_ref))

    def scores(j, slot, masked=False):
        s_ref, smax_ref = slots[slot]
        rows = pl.ds(pl.multiple_of(j * tq, tq), tq)
        for h in range(MLA_HEADS):
            hq = slice(h * HEAD_PAD, (h + 1) * HEAD_PAD)
            s = _dot_nt(k_ref[rows, hq], q_ref[:, hq])
            if masked:
                s = jnp.where(causal, s, -1e30)
            s_ref[h] = s
            smax_ref[h:h + 1, :] = jnp.max(s, axis=0, keepdims=True)

    def accumulate(j, slot):
        s_ref, smax_ref = slots[slot]
        for h in range(MLA_HEADS):
            hv = slice(h * MLA_VA, (h + 1) * MLA_VA)
            m = m_ref[h:h + 1, :]
            m_new = jnp.maximum(m, smax_ref[h:h + 1, :])
            alpha = jnp.exp2(m - m_new)
            p = jnp.exp2(s_ref[h] - m_new)
            m_ref[h:h + 1, :] = m_new
            acc_ref[hv, :] = alpha * acc_ref[hv, :] + _dot(vt_ref[j, hv, :], p.astype(BF16))

    scores(i, 0, masked=True)

    def pair(t, carry):
        scores(2 * t, 1)
        accumulate(jnp.where(t == 0, i, 2 * t - 1), 0)
        scores(2 * t + 1, 0)
        accumulate(2 * t, 1)
        return carry

    lax.fori_loop(0, i // 2, pair, 0)
    in_slot0 = jnp.where(i < 2, i, i - 1 - i % 2)

    @pl.when(i % 2 == 1)
    def _():
        scores(i - 1, 1)
        accumulate(in_slot0, 0)
        accumulate(i - 1, 1)

    @pl.when(i % 2 == 0)
    def _():
        accumulate(in_slot0, 0)

    for h in range(MLA_HEADS):
        lo = h * MLA_VA
        ot_ref[h * MLA_V:(h + 1) * MLA_V, :] = acc_ref[lo:lo + MLA_V, :] / acc_ref[lo + MLA_V:lo + MLA_V + 1, :]
    o_ref[...] = ot_ref[...].T.astype(BF16)


def _attn_call(q, k, vt, batch, seq, tq):
    nq = seq // tq
    return pl.pallas_call(
        functools.partial(_attn_kernel, tq=tq),
        grid=(batch, nq),
        in_specs=[pl.BlockSpec((tq, MLA_HEADS * HEAD_PAD), lambda b, i: (b * nq + i, 0)),
                  pl.BlockSpec((seq, MLA_HEADS * HEAD_PAD), lambda b, i: (b, 0)),
                  pl.BlockSpec((nq, MLA_HEADS * MLA_VA, tq), lambda b, i: (b, 0, 0))],
        out_specs=pl.BlockSpec((tq, MLA_WIDTH), lambda b, i: (b * nq + i, 0)),
        out_shape=jax.ShapeDtypeStruct((batch * seq, MLA_WIDTH), BF16),
        scratch_shapes=[pltpu.VMEM((MLA_HEADS * MLA_VA, tq), F32), pltpu.VMEM((MLA_WIDTH, tq), F32),
                        pltpu.VMEM((MLA_HEADS, tq), F32),
                        pltpu.VMEM((MLA_HEADS, tq, tq), F32), pltpu.VMEM((MLA_HEADS, tq, tq), F32),
                        pltpu.VMEM((MLA_HEADS, tq), F32), pltpu.VMEM((MLA_HEADS, tq), F32)],
        compiler_params=pltpu.CompilerParams(dimension_semantics=("parallel", "arbitrary"),
                                             vmem_limit_bytes=VMEM_LIMIT),
        name="mla_attn",
    )(q, k, vt)


def _gla_kernel(gq_ref, gk_ref, gv_ref, la_ref, og_ref, norm_ref, o_ref, state_ref, kpad_ref, lapad_ref,
                *, slabs_per_step, par):
    c_, s_ = GLA_CHUNK, GLA_SUB
    nsub = c_ // s_
    slab_rows = par * c_
    pair = 2 * GLA_HK

    @pl.when(pl.program_id(1) == 0)
    def _():
        state_ref[...] = jnp.zeros_like(state_ref)

    kpad_ref[0:s_, :] = jnp.zeros((s_, GLA_DK), F32)
    lapad_ref[0:s_, :] = jnp.zeros((s_, GLA_DK), F32)

    ri = lax.broadcasted_iota(jnp.int32, (c_, c_), 0)
    ci = lax.broadcasted_iota(jnp.int32, (c_, c_), 1)
    tri = jnp.where((ri // s_ == ci // s_) & (ci <= ri), 1.0, 0.0).astype(BF16)
    rl = lax.broadcasted_iota(jnp.int32, (c_, GLA_DK), 0)
    ll = lax.broadcasted_iota(jnp.int32, (c_, GLA_DK), 1)
    row_minus_col = rl - ll % GLA_HK
    diag_key = jnp.where((row_minus_col >= 0) & (rl % s_ >= row_minus_col), row_minus_col, -1)
    head_of_lane = ll // GLA_HK
    odd_head_lane = (lax.broadcasted_iota(jnp.int32, (c_, pair), 1) // GLA_HK) == 1
    di = lax.broadcasted_iota(jnp.int32, (GLA_DK, GLA_DK), 0) // GLA_HK
    dj = lax.broadcasted_iota(jnp.int32, (GLA_DK, GLA_DK), 1) // GLA_HK
    head_sum = jnp.where(di == dj, 1.0, 0.0).astype(BF16)
    norm_w = norm_ref[...]

    def blocks(fn):
        return jnp.concatenate([fn(b) for b in range(nsub)], axis=0)

    def sub(t, b):
        return t[b * s_:(b + 1) * s_, :]

    def slab(sidx, _):
        base = pl.multiple_of(sidx * slab_rows, slab_rows)
        kpad_ref[s_:, :] = gk_ref[pl.ds(base, slab_rows), :].astype(F32)
        lapad_ref[s_:, :] = la_ref[pl.ds(base, slab_rows), :]
        chunks = [dict(rows=pl.ds(pl.multiple_of(base + c * c_, c_), c_), off=s_ + c * c_) for c in range(par)]

        for ch in chunks:
            la = la_ref[ch["rows"], :]
            hi = la.astype(BF16)
            r1 = la - hi.astype(F32)
            mid = r1.astype(BF16)
            lo = (r1 - mid.astype(F32)).astype(BF16)
            ch["l16"] = _dot(tri, hi) + _dot(tri, mid) + _dot(tri, lo)

        for ch in chunks:
            q = gq_ref[ch["rows"], :].astype(F32)
            k = gk_ref[ch["rows"], :].astype(F32)
            l16 = ch["l16"]
            tot = [l16[(b + 1) * s_ - 1:(b + 1) * s_, :] for b in range(nsub)]
            q16 = q * jnp.exp(l16)
            k16 = k * jnp.exp(blocks(lambda b: jnp.broadcast_to(tot[b], (s_, GLA_DK))) - l16)
            zero = jnp.zeros((s_, GLA_DK), F32)
            e = [jnp.exp(t) for t in tot]
            pre = [None, e[0], e[0] * e[1], e[0] * e[1] * e[2]]
            post = [e[1] * e[2] * e[3], e[2] * e[3], e[3], None]
            ch["qd"] = blocks(lambda b: sub(q16, b) if pre[b] is None else sub(q16, b) * pre[b]).astype(BF16)
            kd = blocks(lambda b: sub(k16, b) if post[b] is None else sub(k16, b) * post[b]).astype(BF16)
            ch["decay"] = jnp.exp(tot[0] + tot[1] + tot[2] + tot[3])
            v = gv_ref[ch["rows"], :]
            ch["upd"] = sum(_dot_tn(v[:, h * GLA_HV:(h + 1) * GLA_HV], jnp.where(head_of_lane == h, kd, 0.0))
                            for h in range(GLA_HEADS))
            ch["q_lv"] = [t.astype(BF16) for t in (
                jnp.concatenate([zero, zero, sub(q16, 2), sub(q16, 3) * e[2]], axis=0),
                jnp.concatenate([zero, sub(q16, 1), zero, zero], axis=0),
                jnp.concatenate([zero, zero, zero, sub(q16, 3)], axis=0))]
            ch["k_lv"] = [t.astype(BF16) for t in (
                jnp.concatenate([sub(k16, 0) * e[1], sub(k16, 1), zero, zero], axis=0),
                jnp.concatenate([sub(k16, 0), zero, zero, zero], axis=0),
                jnp.concatenate([zero, zero, sub(k16, 2), zero], axis=0))]
            decay = jnp.zeros((c_, GLA_DK), F32)
            prods = []
            for d in range(s_):
                if d > 0:
                    decay = decay + lapad_ref[ch["off"] - d + 1:ch["off"] - d + 1 + c_, :]
                k_sh = kpad_ref[ch["off"] - d:ch["off"] - d + c_, :]
                prods.append((q * k_sh * jnp.exp(decay)).astype(BF16))
            ch["summed"] = _dot(jnp.concatenate(prods, axis=0), head_sum)

        for ch in chunks:
            ch["attn_off"] = []
            for h in range(GLA_HEADS):
                lanes = slice((h // 2) * pair, (h // 2 + 1) * pair)
                mine = odd_head_lane if h % 2 else ~odd_head_lane
                qx = jnp.concatenate([jnp.where(mine, t[:, lanes], 0.0) for t in ch["q_lv"]], axis=1)
                kx = jnp.concatenate([t[:, lanes] for t in ch["k_lv"]], axis=1)
                ch["attn_off"].append(_dot_nt(qx, kx))

        state = state_ref[...]
        for ch in chunks:
            ch["state_b"] = state.astype(BF16)
            state = state * ch["decay"] + ch["upd"]
        state_ref[...] = state

        for ch in chunks:
            attn_diag = jnp.zeros((c_, GLA_DK), F32)
            for d in range(s_):
                attn_diag = jnp.where(diag_key == d, ch["summed"][d * c_:(d + 1) * c_, :], attn_diag)
            v = gv_ref[ch["rows"], :]
            for h in range(GLA_HEADS):
                hv = slice(h * GLA_HV, (h + 1) * GLA_HV)
                attn = attn_diag[:, h * GLA_HK:(h + 1) * GLA_HK] + ch["attn_off"][h]
                o = _dot(attn.astype(BF16), v[:, hv])
                o = o + _dot_nt(jnp.where(head_of_lane == h, ch["qd"], 0.0), ch["state_b"])
                o = _rms(o, norm_w)
                g = og_ref[ch["rows"], hv].astype(F32)
                o_ref[ch["rows"], hv] = (o * (g * jax.nn.sigmoid(g))).astype(BF16)
        return 0

    lax.fori_loop(0, slabs_per_step, slab, 0)


def _gla_call(gq, gk, gv, la, og, norm, batch, seq, rows_per_step):
    ns = seq // rows_per_step
    par = math.gcd(rows_per_step // GLA_CHUNK, GLA_PAR)
    row = lambda w: pl.BlockSpec((rows_per_step, w), lambda b, i: (b * ns + i, 0))
    return pl.pallas_call(
        functools.partial(_gla_kernel, slabs_per_step=rows_per_step // (par * GLA_CHUNK), par=par),
        grid=(batch, ns),
        in_specs=[row(GLA_DK), row(GLA_DK), row(GLA_DV), row(GLA_DK), row(GLA_DV), _const_spec(norm.shape)],
        out_specs=row(GLA_DV),
        out_shape=jax.ShapeDtypeStruct((batch * seq, GLA_DV), BF16),
        scratch_shapes=[pltpu.VMEM((GLA_HV, GLA_DK), F32),
                        pltpu.VMEM((GLA_SUB + par * GLA_CHUNK, GLA_DK), F32),
                        pltpu.VMEM((GLA_SUB + par * GLA_CHUNK, GLA_DK), F32)],
        compiler_params=pltpu.CompilerParams(dimension_semantics=("parallel", "arbitrary"),
                                             vmem_limit_bytes=VMEM_LIMIT),
        name="gla",
    )(gq, gk, gv, la, og, norm)


def _post_kernel(x_ref, oa_ref, ob_ref, sa_ref, sb_ref, woa_ref, wob_ref, wout_ref, lnffn_ref,
                 wg_ref, wu_ref, wd_ref, fnorm_ref, out_ref):
    ya = _dot(oa_ref[...], woa_ref[...])
    yb = _dot(ob_ref[...], wob_ref[...])
    mix = sa_ref[...].astype(F32) * ya + sb_ref[...].astype(F32) * yb
    h = x_ref[...] + _dot(mix.astype(BF16), wout_ref[...])
    u = _rms(h, lnffn_ref[...]).astype(BF16)
    g = _dot(u, wg_ref[...])
    act = (g * jax.nn.sigmoid(g) * _dot(u, wu_ref[...])).astype(BF16)
    h = h + _dot(act, wd_ref[...])
    out_ref[...] = _rms(h, fnorm_ref[...])


def _post_call(x2, oa, ob, sa, sb, woa, wob, wout, lnffn, wg, wu, wd, fnorm, tm):
    t = x2.shape[0]
    row = lambda w: pl.BlockSpec((tm, w), lambda i: (i, 0))
    single = lambda a: pl.BlockSpec(a.shape, lambda i: (0,) * a.ndim, pipeline_mode=pl.Buffered(1))
    consts = (woa, wob, wout, lnffn, wg, wu, wd, fnorm)
    return pl.pallas_call(
        _post_kernel,
        grid=(t // tm,),
        in_specs=[row(D_MODEL), row(MLA_WIDTH), row(GLA_DV), row(D_MODEL), row(D_MODEL)]
                 + [single(c) for c in consts],
        out_specs=row(D_MODEL),
        out_shape=jax.ShapeDtypeStruct((t, D_MODEL), F32),
        compiler_params=pltpu.CompilerParams(dimension_semantics=("parallel",), vmem_limit_bytes=VMEM_LIMIT),
        name="post",
    )(x2, oa, ob, sa, sb, *consts)


def _pack_weights(w_in, mla_w_uq, mla_w_ukv, gla_w_gate2):
    offs = [0]
    for s in IN_SIZES:
        offs.append(offs[-1] + s)
    col = lambda idx: w_in[:, offs[idx]:offs[idx + 1]]
    d = w_in.shape[0]
    misc = jnp.concatenate([jnp.zeros((d, ROPE_LO), F32), col(2), col(6),
                            jnp.zeros((d, LANES - GLR_LO - GLA_RANK), F32)], axis=1)
    winp = jnp.concatenate([col(0), col(1), col(3), col(4), col(5), col(7), col(8), col(9), misc],
                           axis=1).astype(BF16)
    wuq = mla_w_uq.reshape(MLA_Q_RANK, MLA_HEADS, MLA_QK)
    wuqp = jnp.pad(wuq, ((0, 0), (0, 0), (0, HEAD_PAD - MLA_QK))).reshape(MLA_Q_RANK, MLA_HEADS * HEAD_PAD)
    wukv = mla_w_ukv.reshape(MLA_KV_RANK, MLA_HEADS, MLA_NOPE + MLA_V)
    wkp = jnp.pad(wukv[:, :, :MLA_NOPE], ((0, 0), (0, 0), (0, HEAD_PAD - MLA_NOPE)))
    wkp = wkp.reshape(MLA_KV_RANK, MLA_HEADS * HEAD_PAD)
    wvt = wukv[:, :, MLA_NOPE:].reshape(MLA_KV_RANK, MLA_WIDTH).T
    wg2p = jnp.pad(gla_w_gate2, ((GLR_LO, LANES - GLR_LO - GLA_RANK), (0, 0)))
    return winp, wuqp.astype(BF16), wkp.astype(BF16), wvt.astype(BF16), wg2p.astype(BF16)


def _rope_inv_freq():
    half = ROPE_HALF
    inv = 1.0 / (ROPE_THETA ** (jnp.arange(half, dtype=F32) / half))
    return inv.reshape(half, 1)


def _rope_placement():
    place = np.zeros((4 * ROPE_HALF, 3 * LANES), np.float32)
    for piece in range(2):
        for i in range(ROPE_HALF):
            cos_row = 2 * piece * ROPE_HALF + i
            sin_row = cos_row + ROPE_HALF
            place[cos_row, ROPE_LO + i] = 1.0
            place[cos_row, ROPE_LO + ROPE_HALF + i] = 1.0
            place[sin_row, LANES + ROPE_LO + i] = -1.0
            place[sin_row, 2 * LANES + ROPE_LO + ROPE_HALF + i] = 1.0
    return jnp.asarray(place, BF16)


def _tile(n, pref):
    while n % pref:
        pref //= 2
    return pref


def kernel(x, positions, ln_mix, w_in, mla_norm_cq, mla_w_uq, mla_norm_ckv, mla_w_ukv, mla_w_o, gla_w_gate2,
           gla_b_gate, gla_norm, gla_w_o, w_out, ln_ffn, ffn_w_gate, ffn_w_up, ffn_w_down, final_norm):
    batch, seq, d = x.shape
    assert d == D_MODEL and w_in.shape[0] == 1 and seq % GLA_CHUNK == 0
    t = batch * seq
    x2 = x.reshape(t, d)
    winp, wuqp, wkp, wvt, wg2p = _pack_weights(w_in[0], mla_w_uq[0], mla_w_ukv[0], gla_w_gate2[0])
    r = lambda a: a.reshape(1, -1)

    tq = _tile(seq, ATTN_TILE)
    tm = max(tq, _tile(t, PROJ_TILE))
    pos3 = positions.reshape(t // tm, 1, tm).astype(F32)
    q, k, vt, gq, gk, gv, la, og, sa, sb = _proj_call(
        x2, pos3, r(ln_mix[0]), winp, r(mla_norm_cq[0]), wuqp, r(mla_norm_ckv[0]), wkp, wvt, wg2p,
        r(gla_b_gate[0]), _rope_inv_freq(), _rope_placement(), tm, tq)
    oa = _attn_call(q, k, vt, batch, seq, tq)
    ob = _gla_call(gq, gk, gv, la, og, r(gla_norm[0]), batch, seq, _tile(seq, 512))
    out = _post_call(x2, oa, ob, sa, sb, mla_w_o[0].astype(BF16), gla_w_o[0].astype(BF16),
                     w_out[0].astype(BF16), r(ln_ffn[0]), ffn_w_gate[0].astype(BF16),
                     ffn_w_up[0].astype(BF16), ffn_w_down[0].astype(BF16), r(final_norm), _tile(t, 256))
    return out.reshape(batch, seq, d)
```

```python
import functools
import math

import jax
import jax.numpy as jnp
import numpy as np
from jax import lax
from jax.experimental import pallas as pl
from jax.experimental.pallas import tpu as pltpu

F32 = jnp.float32
BF16 = jnp.bfloat16

D_MODEL = 1024
MLA_HEADS = 8
MLA_NOPE = 64
MLA_ROPE = 32
MLA_QK = MLA_NOPE + MLA_ROPE
MLA_V = 64
MLA_WIDTH = MLA_HEADS * MLA_V
MLA_Q_RANK = 384
MLA_KV_RANK = 256
ROPE_THETA = 10000.0
GLA_HEADS = 4
GLA_DV = 512
GLA_DK = 256
GLA_HK = GLA_DK // GLA_HEADS
GLA_HV = GLA_DV // GLA_HEADS
GLA_RANK = 16
GLA_TAU = 16.0
D_FF = 2816
EPS = 1e-6
LOG2E = 1.4426950408889634
IN_SIZES = (MLA_Q_RANK, MLA_KV_RANK, MLA_ROPE, GLA_DK, GLA_DK, GLA_DV, GLA_RANK, GLA_DV, D_MODEL, D_MODEL)

LANES = 128
HEAD_PAD = LANES
ROPE_LO = MLA_NOPE
ROPE_HALF = MLA_ROPE // 2
GLR_LO = ROPE_LO + MLA_ROPE

OFF_CQ = 0
OFF_CKV = OFF_CQ + MLA_Q_RANK
OFF_GQ = OFF_CKV + MLA_KV_RANK
OFF_GK = OFF_GQ + GLA_DK
OFF_GV = OFF_GK + GLA_DK
OFF_OG = OFF_GV + GLA_DV
OFF_GA = OFF_OG + GLA_DV
OFF_GB = OFF_GA + D_MODEL
OFF_MISC = OFF_GB + D_MODEL
D_IN_PACKED = OFF_MISC + LANES

PROJ_TILE = 512
ATTN_TILE = 256
MLA_VA = MLA_V + 16
GLA_CHUNK = 64
GLA_SUB = 16
GLA_PAR = 4
VMEM_LIMIT = 56 * 1024 * 1024


def _dot(a, b):
    return jnp.dot(a, b, preferred_element_type=F32)


def _dot_nt(a, b):
    return lax.dot_general(a, b, (((1,), (1,)), ((), ())), preferred_element_type=F32)


def _dot_tn(a, b):
    return lax.dot_general(a, b, (((0,), (0,)), ((), ())), preferred_element_type=F32)


def _rms(x, w):
    return x * lax.rsqrt(jnp.mean(x * x, axis=-1, keepdims=True) + EPS) * w


def _single_spec(a):
    return pl.BlockSpec(a.shape, lambda *_: (0,) * a.ndim, pipeline_mode=pl.Buffered(1))


def _gla_constants():
    c_, s_ = GLA_CHUNK, GLA_SUB
    ri = lax.broadcasted_iota(jnp.int32, (c_, c_), 0)
    ci = lax.broadcasted_iota(jnp.int32, (c_, c_), 1)
    rl = lax.broadcasted_iota(jnp.int32, (c_, GLA_DK), 0)
    ll = lax.broadcasted_iota(jnp.int32, (c_, GLA_DK), 1)
    row_minus_col = rl - ll % GLA_HK
    di = lax.broadcasted_iota(jnp.int32, (GLA_DK, GLA_DK), 0) // GLA_HK
    dj = lax.broadcasted_iota(jnp.int32, (GLA_DK, GLA_DK), 1) // GLA_HK
    return dict(
        tri=jnp.where((ri // s_ == ci // s_) & (ci <= ri), 1.0, 0.0).astype(BF16),
        diag_key=jnp.where((row_minus_col >= 0) & (rl % s_ >= row_minus_col), row_minus_col, -1),
        head_of_lane=ll // GLA_HK,
        odd_head_lane=(lax.broadcasted_iota(jnp.int32, (c_, 2 * GLA_HK), 1) // GLA_HK) == 1,
        head_sum=jnp.where(di == dj, 1.0, 0.0).astype(BF16))


def _gla_slab_stages(base, par, consts, gq_ref, gk_ref, gv_ref, la_ref, og_ref, norm_w, o_ref, state_ref,
                     kpad_ref, lapad_ref):
    c_, s_ = GLA_CHUNK, GLA_SUB
    nsub = c_ // s_
    pair = 2 * GLA_HK
    tri, diag_key, head_of_lane = consts["tri"], consts["diag_key"], consts["head_of_lane"]
    odd_head_lane, head_sum = consts["odd_head_lane"], consts["head_sum"]
    chunks = [dict(rows=slice(base + c * c_, base + (c + 1) * c_), off=s_ + c * c_) for c in range(par)]

    def blocks(fn):
        return jnp.concatenate([fn(b) for b in range(nsub)], axis=0)

    def sub(t, b):
        return t[b * s_:(b + 1) * s_, :]

    def cumsums():
        kpad_ref[0:s_, :] = jnp.zeros((s_, GLA_DK), F32)
        lapad_ref[0:s_, :] = jnp.zeros((s_, GLA_DK), F32)
        kpad_ref[s_:, :] = gk_ref[base:base + par * c_, :].astype(F32)
        lapad_ref[s_:, :] = la_ref[base:base + par * c_, :]
        for ch in chunks:
            la = la_ref[ch["rows"], :]
            hi = la.astype(BF16)
            r1 = la - hi.astype(F32)
            mid = r1.astype(BF16)
            lo = (r1 - mid.astype(F32)).astype(BF16)
            ch["l16"] = _dot(tri, hi) + _dot(tri, mid) + _dot(tri, lo)

    def operands():
        for ch in chunks:
            q = gq_ref[ch["rows"], :].astype(F32)
            k = gk_ref[ch["rows"], :].astype(F32)
            l16 = ch["l16"]
            tot = [l16[(b + 1) * s_ - 1:(b + 1) * s_, :] for b in range(nsub)]
            q16 = q * jnp.exp(l16)
            k16 = k * jnp.exp(blocks(lambda b: jnp.broadcast_to(tot[b], (s_, GLA_DK))) - l16)
            zero = jnp.zeros((s_, GLA_DK), F32)
            e = [jnp.exp(t) for t in tot]
            pre = [None, e[0], e[0] * e[1], e[0] * e[1] * e[2]]
            post = [e[1] * e[2] * e[3], e[2] * e[3], e[3], None]
            ch["qd"] = blocks(lambda b: sub(q16, b) if pre[b] is None else sub(q16, b) * pre[b]).astype(BF16)
            kd = blocks(lambda b: sub(k16, b) if post[b] is None else sub(k16, b) * post[b]).astype(BF16)
            ch["decay"] = jnp.exp(tot[0] + tot[1] + tot[2] + tot[3])
            v = gv_ref[ch["rows"], :]
            ch["upd"] = sum(_dot_tn(v[:, h * GLA_HV:(h + 1) * GLA_HV], jnp.where(head_of_lane == h, kd, 0.0))
                            for h in range(GLA_HEADS))
            ch["q_lv"] = [t.astype(BF16) for t in (
                jnp.concatenate([zero, zero, sub(q16, 2), sub(q16, 3) * e[2]], axis=0),
                jnp.concatenate([zero, sub(q16, 1), zero, zero], axis=0),
                jnp.concatenate([zero, zero, zero, sub(q16, 3)], axis=0))]
            ch["k_lv"] = [t.astype(BF16) for t in (
                jnp.concatenate([sub(k16, 0) * e[1], sub(k16, 1), zero, zero], axis=0),
                jnp.concatenate([sub(k16, 0), zero, zero, zero], axis=0),
                jnp.concatenate([zero, zero, sub(k16, 2), zero], axis=0))]
            decay = jnp.zeros((c_, GLA_DK), F32)
            prods = []
            for d in range(s_):
                if d > 0:
                    decay = decay + lapad_ref[ch["off"] - d + 1:ch["off"] - d + 1 + c_, :]
                k_sh = kpad_ref[ch["off"] - d:ch["off"] - d + c_, :]
                prods.append((q * k_sh * jnp.exp(decay)).astype(BF16))
            ch["summed"] = _dot(jnp.concatenate(prods, axis=0), head_sum)

    def pair_scores():
        for ch in chunks:
            ch["attn_off"] = []
            for h in range(GLA_HEADS):
                lanes = slice((h // 2) * pair, (h // 2 + 1) * pair)
                mine = odd_head_lane if h % 2 else ~odd_head_lane
                qx = jnp.concatenate([jnp.where(mine, t[:, lanes], 0.0) for t in ch["q_lv"]], axis=1)
                kx = jnp.concatenate([t[:, lanes] for t in ch["k_lv"]], axis=1)
                ch["attn_off"].append(_dot_nt(qx, kx))

    def outputs():
        state = state_ref[...]
        for ch in chunks:
            ch["state_b"] = state.astype(BF16)
            state = state * ch["decay"] + ch["upd"]
        state_ref[...] = state
        for ch in chunks:
            attn_diag = jnp.zeros((c_, GLA_DK), F32)
            for d in range(s_):
                attn_diag = jnp.where(diag_key == d, ch["summed"][d * c_:(d + 1) * c_, :], attn_diag)
            v = gv_ref[ch["rows"], :]
            for h in range(GLA_HEADS):
                hv = slice(h * GLA_HV, (h + 1) * GLA_HV)
                attn = attn_diag[:, h * GLA_HK:(h + 1) * GLA_HK] + ch["attn_off"][h]
                o = _dot(attn.astype(BF16), v[:, hv])
                o = o + _dot_nt(jnp.where(head_of_lane == h, ch["qd"], 0.0), ch["state_b"])
                o = _rms(o, norm_w)
                g = og_ref[ch["rows"], hv].astype(F32)
                o_ref[ch["rows"], hv] = (o * (g * jax.nn.sigmoid(g))).astype(BF16)

    return [cumsums, operands, pair_scores, outputs]


def _proj_gla_kernel(x_ref, pos_ref, lnmix_ref, win_ref, ncq_ref, wuq_ref, nckv_ref, wk_ref, wvt_ref,
                     wg2_ref, bg_ref, invf_ref, place_ref, gnorm_ref,
                     q_ref, k_ref, vt_ref, sa_ref, sb_ref, ob_ref,
                     gq_ref, gk_ref, gv_ref, la_ref, og_ref, state_ref, kpad_ref, lapad_ref,
                     *, tkv, steps_per_seq, par):
    tm = x_ref.shape[0]

    @pl.when(pl.program_id(0) % steps_per_seq == 0)
    def _():
        state_ref[...] = jnp.zeros_like(state_ref)

    ub = _rms(x_ref[...], lnmix_ref[...]).astype(BF16)

    def seg(lo, width):
        return _dot(ub, win_ref[:, lo:lo + width])

    misc = seg(OFF_MISC, LANES)
    xg = _dot(misc.astype(BF16), wg2_ref[...]) + bg_ref[...]
    la_ref[...] = (jnp.minimum(xg, 0.0) - jnp.log1p(jnp.exp(-jnp.abs(xg)))) * (1.0 / GLA_TAU)
    gq_ref[...] = (seg(OFF_GQ, GLA_DK) * (GLA_HK ** -0.5)).astype(BF16)
    gk_ref[...] = seg(OFF_GK, GLA_DK).astype(BF16)
    gv_ref[...] = seg(OFF_GV, GLA_DV).astype(BF16)
    og_ref[...] = seg(OFF_OG, GLA_DV).astype(BF16)

    consts = _gla_constants()
    slab_rows = par * GLA_CHUNK
    scan = []
    for sl in range(tm // slab_rows):
        scan += _gla_slab_stages(sl * slab_rows, par, consts, gq_ref, gk_ref, gv_ref, la_ref, og_ref,
                                 gnorm_ref[...], ob_ref, state_ref, kpad_ref.at[sl], lapad_ref.at[sl])

    lane = lax.broadcasted_iota(jnp.int32, (1, LANES), 1)
    in_rope = (lane >= ROPE_LO) & (lane < ROPE_LO + MLA_ROPE)
    t = {}

    def latents():
        t["cq"] = seg(OFF_CQ, MLA_Q_RANK)
        t["ckv"] = seg(OFF_CKV, MLA_KV_RANK)

    def rope_tables():
        ang = invf_ref[...] * pos_ref[0]
        cs = jnp.concatenate([jnp.cos(ang), jnp.sin(ang)], axis=0)
        cs_hi = cs.astype(BF16)
        cs_lo = (cs - cs_hi.astype(F32)).astype(BF16)
        tabs = _dot_tn(jnp.concatenate([cs_hi, cs_lo], axis=0), place_ref[...])
        t["cos"] = tabs[:, :LANES] + jnp.where(in_rope, 0.0, 1.0)
        t["sin_x1"] = tabs[:, LANES:2 * LANES]
        t["sin_x2"] = tabs[:, 2 * LANES:]

    def gate_a():
        sa_ref[...] = jax.nn.sigmoid(seg(OFF_GA, D_MODEL)).astype(BF16)

    def gate_b():
        sb_ref[...] = jax.nn.sigmoid(seg(OFF_GB, D_MODEL)).astype(BF16)

    def latent_norms():
        t["cqn"] = _rms(t["cq"], ncq_ref[...]).astype(BF16)
        t["ckvn"] = _rms(t["ckv"], nckv_ref[...]).astype(BF16)

    def up_q():
        t["q"] = _dot(t["cqn"], wuq_ref[...]) * (MLA_QK ** -0.5 * LOG2E)

    def up_kv():
        t["k_nope"] = _dot(t["ckvn"], wk_ref[...])
        for c in range(tm // tkv):
            vt = _dot_nt(wvt_ref[...], t["ckvn"][c * tkv:(c + 1) * tkv, :]).astype(BF16)
            for h in range(MLA_HEADS):
                vt_ref[c, h * MLA_VA:h * MLA_VA + MLA_V, :] = vt[h * MLA_V:(h + 1) * MLA_V, :]
                vt_ref[c, h * MLA_VA + MLA_V:(h + 1) * MLA_VA, :] = jnp.ones((MLA_VA - MLA_V, tkv), BF16)

    def rope_store():
        def rope(a):
            return (a * t["cos"] + pltpu.roll(a, LANES - ROPE_HALF, axis=1) * t["sin_x1"]
                    + pltpu.roll(a, ROPE_HALF, axis=1) * t["sin_x2"])

        k_rope = rope(jnp.where(in_rope, misc, 0.0))
        for h in range(MLA_HEADS):
            sl = slice(h * HEAD_PAD, (h + 1) * HEAD_PAD)
            q_ref[:, sl] = rope(t["q"][:, sl]).astype(BF16)
            k_ref[:, sl] = (t["k_nope"][:, sl] + k_rope).astype(BF16)

    rest = [latents, gate_a, rope_tables, latent_norms, gate_b, up_q, up_kv, rope_store]
    for n in range(max(len(scan), len(rest))):
        if n < len(scan):
            scan[n]()
        if n < len(rest):
            rest[n]()


def _proj_gla_call(x2, pos3, lnmix, winp, ncq, wuqp, nckv, wkp, wvt, wg2p, bg, invf, place, gnorm, seq, tm, tkv):
    t = x2.shape[0]
    par = math.gcd(tm // GLA_CHUNK, GLA_PAR)
    nslab = tm // (par * GLA_CHUNK)
    row = lambda w: pl.BlockSpec((tm, w), lambda i: (i, 0))
    consts = (lnmix, winp, ncq, wuqp, nckv, wkp, wvt, wg2p, bg, invf, place, gnorm)
    out_widths = (MLA_HEADS * HEAD_PAD, MLA_HEADS * HEAD_PAD, None, D_MODEL, D_MODEL, GLA_DV)
    vt_spec = pl.BlockSpec((tm // tkv, MLA_HEADS * MLA_VA, tkv), lambda i: (i, 0, 0))
    vt_shape = jax.ShapeDtypeStruct((t // tkv, MLA_HEADS * MLA_VA, tkv), BF16)
    pad_rows = GLA_SUB + par * GLA_CHUNK
    return pl.pallas_call(
        functools.partial(_proj_gla_kernel, tkv=tkv, steps_per_seq=seq // tm, par=par),
        grid=(t // tm,),
        in_specs=[row(D_MODEL), pl.BlockSpec((1, 1, tm), lambda i: (i, 0, 0))] + [_single_spec(c) for c in consts],
        out_specs=[vt_spec if w is None else row(w) for w in out_widths],
        out_shape=[vt_shape if w is None else jax.ShapeDtypeStruct((t, w), BF16) for w in out_widths],
        scratch_shapes=[pltpu.VMEM((tm, GLA_DK), BF16), pltpu.VMEM((tm, GLA_DK), BF16),
                        pltpu.VMEM((tm, GLA_DV), BF16), pltpu.VMEM((tm, GLA_DK), F32),
                        pltpu.VMEM((tm, GLA_DV), BF16), pltpu.VMEM((GLA_HV, GLA_DK), F32),
                        pltpu.VMEM((nslab, pad_rows, GLA_DK), F32), pltpu.VMEM((nslab, pad_rows, GLA_DK), F32)],
        compiler_params=pltpu.CompilerParams(dimension_semantics=("arbitrary",), vmem_limit_bytes=VMEM_LIMIT),
        name="proj_gla",
    )(x2, pos3, *consts)


def _attn_kernel(q_ref, k_ref, vt_ref, o_ref, acc_ref, ot_ref, m_ref, s0_ref, s1_ref, smax0_ref, smax1_ref, *, tq):
    i = pl.program_id(1)
    key = lax.broadcasted_iota(jnp.int32, (tq, tq), 0)
    qry = lax.broadcasted_iota(jnp.int32, (tq, tq), 1)
    causal = key <= qry
    m_ref[...] = jnp.full(m_ref.shape, -1e30, F32)
    acc_ref[...] = jnp.zeros(acc_ref.shape, F32)
    slots = ((s0_ref, smax0_ref), (s1_ref, smax1_ref))

    def scores(j, slot, masked=False):
        s_ref, smax_ref = slots[slot]
        rows = pl.ds(pl.multiple_of(j * tq, tq), tq)
        for h in range(MLA_HEADS):
            hq = slice(h * HEAD_PAD, (h + 1) * HEAD_PAD)
            s = _dot_nt(k_ref[rows, hq], q_ref[:, hq])
            if masked:
                s = jnp.where(causal, s, -1e30)
            s_ref[h] = s
            smax_ref[h:h + 1, :] = jnp.max(s, axis=0, keepdims=True)

    def accumulate(j, slot):
        s_ref, smax_ref = slots[slot]
        for h in range(MLA_HEADS):
            hv = slice(h * MLA_VA, (h + 1) * MLA_VA)
            m = m_ref[h:h + 1, :]
            m_new = jnp.maximum(m, smax_ref[h:h + 1, :])
            alpha = jnp.exp2(m - m_new)
            p = jnp.exp2(s_ref[h] - m_new)
            m_ref[h:h + 1, :] = m_new
            acc_ref[hv, :] = alpha * acc_ref[hv, :] + _dot(vt_ref[j, hv, :], p.astype(BF16))

    scores(i, 0, masked=True)

    def pair(t, carry):
        scores(2 * t, 1)
        accumulate(jnp.where(t == 0, i, 2 * t - 1), 0)
        scores(2 * t + 1, 0)
        accumulate(2 * t, 1)
        return carry

    lax.fori_loop(0, i // 2, pair, 0)
    in_slot0 = jnp.where(i < 2, i, i - 1 - i % 2)

    @pl.when(i % 2 == 1)
    def _():
        scores(i - 1, 1)
        accumulate(in_slot0, 0)
        accumulate(i - 1, 1)

    @pl.when(i % 2 == 0)
    def _():
        accumulate(in_slot0, 0)

    for h in range(MLA_HEADS):
        lo = h * MLA_VA
        ot_ref[h * MLA_V:(h + 1) * MLA_V, :] = acc_ref[lo:lo + MLA_V, :] / acc_ref[lo + MLA_V:lo + MLA_V + 1, :]
    o_ref[...] = ot_ref[...].T.astype(BF16)


def _attn_call(q, k, vt, batch, seq, tq):
    nq = seq // tq
    return pl.pallas_call(
        functools.partial(_attn_kernel, tq=tq),
        grid=(batch, nq),
        in_specs=[pl.BlockSpec((tq, MLA_HEADS * HEAD_PAD), lambda b, i: (b * nq + i, 0)),
                  pl.BlockSpec((seq, MLA_HEADS * HEAD_PAD), lambda b, i: (b, 0)),
                  pl.BlockSpec((nq, MLA_HEADS * MLA_VA, tq), lambda b, i: (b, 0, 0))],
        out_specs=pl.BlockSpec((tq, MLA_WIDTH), lambda b, i: (b * nq + i, 0)),
        out_shape=jax.ShapeDtypeStruct((batch * seq, MLA_WIDTH), BF16),
        scratch_shapes=[pltpu.VMEM((MLA_HEADS * MLA_VA, tq), F32), pltpu.VMEM((MLA_WIDTH, tq), F32),
                        pltpu.VMEM((MLA_HEADS, tq), F32),
                        pltpu.VMEM((MLA_HEADS, tq, tq), F32), pltpu.VMEM((MLA_HEADS, tq, tq), F32),
                        pltpu.VMEM((MLA_HEADS, tq), F32), pltpu.VMEM((MLA_HEADS, tq), F32)],
        compiler_params=pltpu.CompilerParams(dimension_semantics=("parallel", "arbitrary"),
                                             vmem_limit_bytes=VMEM_LIMIT),
        name="mla_attn",
    )(q, k, vt)


def _post_kernel(x_ref, oa_ref, ob_ref, sa_ref, sb_ref, woa_ref, wob_ref, wout_ref, lnffn_ref,
                 wg_ref, wu_ref, wd_ref, fnorm_ref, out_ref):
    ya = _dot(oa_ref[...], woa_ref[...])
    yb = _dot(ob_ref[...], wob_ref[...])
    mix = sa_ref[...].astype(F32) * ya + sb_ref[...].astype(F32) * yb
    h = x_ref[...] + _dot(mix.astype(BF16), wout_ref[...])
    u = _rms(h, lnffn_ref[...]).astype(BF16)
    g = _dot(u, wg_ref[...])
    act = (g * jax.nn.sigmoid(g) * _dot(u, wu_ref[...])).astype(BF16)
    h = h + _dot(act, wd_ref[...])
    out_ref[...] = _rms(h, fnorm_ref[...])


def _post_call(x2, oa, ob, sa, sb, woa, wob, wout, lnffn, wg, wu, wd, fnorm, tm):
    t = x2.shape[0]
    row = lambda w: pl.BlockSpec((tm, w), lambda i: (i, 0))
    consts = (woa, wob, wout, lnffn, wg, wu, wd, fnorm)
    return pl.pallas_call(
        _post_kernel,
        grid=(t // tm,),
        in_specs=[row(D_MODEL), row(MLA_WIDTH), row(GLA_DV), row(D_MODEL), row(D_MODEL)]
                 + [_single_spec(c) for c in consts],
        out_specs=row(D_MODEL),
        out_shape=jax.ShapeDtypeStruct((t, D_MODEL), F32),
        compiler_params=pltpu.CompilerParams(dimension_semantics=("parallel",), vmem_limit_bytes=VMEM_LIMIT),
        name="post",
    )(x2, oa, ob, sa, sb, *consts)


def _pack_weights(w_in, mla_w_uq, mla_w_ukv, gla_w_gate2):
    offs = [0]
    for s in IN_SIZES:
        offs.append(offs[-1] + s)
    col = lambda idx: w_in[:, offs[idx]:offs[idx + 1]]
    d = w_in.shape[0]
    misc = jnp.concatenate([jnp.zeros((d, ROPE_LO), F32), col(2), col(6),
                            jnp.zeros((d, LANES - GLR_LO - GLA_RANK), F32)], axis=1)
    winp = jnp.concatenate([col(0), col(1), col(3), col(4), col(5), col(7), col(8), col(9), misc],
                           axis=1).astype(BF16)
    wuq = mla_w_uq.reshape(MLA_Q_RANK, MLA_HEADS, MLA_QK)
    wuqp = jnp.pad(wuq, ((0, 0), (0, 0), (0, HEAD_PAD - MLA_QK))).reshape(MLA_Q_RANK, MLA_HEADS * HEAD_PAD)
    wukv = mla_w_ukv.reshape(MLA_KV_RANK, MLA_HEADS, MLA_NOPE + MLA_V)
    wkp = jnp.pad(wukv[:, :, :MLA_NOPE], ((0, 0), (0, 0), (0, HEAD_PAD - MLA_NOPE)))
    wkp = wkp.reshape(MLA_KV_RANK, MLA_HEADS * HEAD_PAD)
    wvt = wukv[:, :, MLA_NOPE:].reshape(MLA_KV_RANK, MLA_WIDTH).T
    wg2p = jnp.pad(gla_w_gate2, ((GLR_LO, LANES - GLR_LO - GLA_RANK), (0, 0)))
    return winp, wuqp.astype(BF16), wkp.astype(BF16), wvt.astype(BF16), wg2p.astype(BF16)


def _rope_inv_freq():
    half = ROPE_HALF
    inv = 1.0 / (ROPE_THETA ** (jnp.arange(half, dtype=F32) / half))
    return inv.reshape(half, 1)


def _rope_placement():
    place = np.zeros((4 * ROPE_HALF, 3 * LANES), np.float32)
    for piece in range(2):
        for i in range(ROPE_HALF):
            cos_row = 2 * piece * ROPE_HALF + i
            sin_row = cos_row + ROPE_HALF
            place[cos_row, ROPE_LO + i] = 1.0
            place[cos_row, ROPE_LO + ROPE_HALF + i] = 1.0
            place[sin_row, LANES + ROPE_LO + i] = -1.0
            place[sin_row, 2 * LANES + ROPE_LO + ROPE_HALF + i] = 1.0
    return jnp.asarray(place, BF16)


def _tile(n, pref):
    while n % pref:
        pref //= 2
    return pref


def kernel(x, positions, ln_mix, w_in, mla_norm_cq, mla_w_uq, mla_norm_ckv, mla_w_ukv, mla_w_o, gla_w_gate2,
           gla_b_gate, gla_norm, gla_w_o, w_out, ln_ffn, ffn_w_gate, ffn_w_up, ffn_w_down, final_norm):
    batch, seq, d = x.shape
    assert d == D_MODEL and w_in.shape[0] == 1 and seq % GLA_CHUNK == 0
    t = batch * seq
    x2 = x.reshape(t, d)
    winp, wuqp, wkp, wvt, wg2p = _pack_weights(w_in[0], mla_w_uq[0], mla_w_ukv[0], gla_w_gate2[0])
    r = lambda a: a.reshape(1, -1)

    tq = _tile(seq, ATTN_TILE)
    tm = max(tq, _tile(seq, PROJ_TILE))
    pos3 = positions.reshape(t // tm, 1, tm).astype(F32)
    q, k, vt, sa, sb, ob = _proj_gla_call(
        x2, pos3, r(ln_mix[0]), winp, r(mla_norm_cq[0]), wuqp, r(mla_norm_ckv[0]), wkp, wvt, wg2p,
        r(gla_b_gate[0]), _rope_inv_freq(), _rope_placement(), r(gla_norm[0]), seq, tm, tq)
    oa = _attn_call(q, k, vt, batch, seq, tq)
    out = _post_call(x2, oa, ob, sa, sb, mla_w_o[0].astype(BF16), gla_w_o[0].astype(BF16),
                     w_out[0].astype(BF16), r(ln_ffn[0]), ffn_w_gate[0].astype(BF16),
                     ffn_w_up[0].astype(BF16), ffn_w_down[0].astype(BF16), r(final_norm), _tile(t, 256))
    return out.reshape(batch, seq, d)
```

```python
import functools
import math

import jax
import jax.numpy as jnp
import numpy as np
from jax import lax
from jax.experimental import pallas as pl
from jax.experimental.pallas import tpu as pltpu

F32 = jnp.float32
BF16 = jnp.bfloat16

D_MODEL = 1024
MLA_HEADS = 8
MLA_NOPE = 64
MLA_ROPE = 32
MLA_QK = MLA_NOPE + MLA_ROPE
MLA_V = 64
MLA_WIDTH = MLA_HEADS * MLA_V
MLA_Q_RANK = 384
MLA_KV_RANK = 256
ROPE_THETA = 10000.0
GLA_HEADS = 4
GLA_DV = 512
GLA_DK = 256
GLA_HK = GLA_DK // GLA_HEADS
GLA_HV = GLA_DV // GLA_HEADS
GLA_RANK = 16
GLA_TAU = 16.0
D_FF = 2816
EPS = 1e-6
LOG2E = 1.4426950408889634
IN_SIZES = (MLA_Q_RANK, MLA_KV_RANK, MLA_ROPE, GLA_DK, GLA_DK, GLA_DV, GLA_RANK, GLA_DV, D_MODEL, D_MODEL)

LANES = 128
HEAD_PAD = LANES
ROPE_LO = MLA_NOPE
ROPE_HALF = MLA_ROPE // 2
GLR_LO = ROPE_LO + MLA_ROPE

OFF_CQ = 0
OFF_CKV = OFF_CQ + MLA_Q_RANK
OFF_GQ = OFF_CKV + MLA_KV_RANK
OFF_GK = OFF_GQ + GLA_DK
OFF_GV = OFF_GK + GLA_DK
OFF_OG = OFF_GV + GLA_DV
OFF_GA = OFF_OG + GLA_DV
OFF_GB = OFF_GA + D_MODEL
OFF_MISC = OFF_GB + D_MODEL
D_IN_PACKED = OFF_MISC + LANES

REPACK_ROWS = 128
PROJ_TILE = 512
POST_TILE = 512
ATTN_TILE = 256
MLA_VA = MLA_V + 16
GLA_CHUNK = 64
GLA_SUB = 16
GLA_PAR = 4
VMEM_LIMIT = 56 * 1024 * 1024


def _dot(a, b):
    return jnp.dot(a, b, preferred_element_type=F32)


def _dot_nt(a, b):
    return lax.dot_general(a, b, (((1,), (1,)), ((), ())), preferred_element_type=F32)


def _dot_tn(a, b):
    return lax.dot_general(a, b, (((0,), (0,)), ((), ())), preferred_element_type=F32)


def _rms(x, w):
    return x * lax.rsqrt(jnp.mean(x * x, axis=-1, keepdims=True) + EPS) * w


def _single_spec(a):
    return pl.BlockSpec(a.shape, lambda *_: (0,) * a.ndim, pipeline_mode=pl.Buffered(1))


def _gla_constants():
    c_, s_ = GLA_CHUNK, GLA_SUB
    ri = lax.broadcasted_iota(jnp.int32, (c_, c_), 0)
    ci = lax.broadcasted_iota(jnp.int32, (c_, c_), 1)
    rl = lax.broadcasted_iota(jnp.int32, (c_, GLA_DK), 0)
    ll = lax.broadcasted_iota(jnp.int32, (c_, GLA_DK), 1)
    row_minus_col = rl - ll % GLA_HK
    di = lax.broadcasted_iota(jnp.int32, (GLA_DK, GLA_DK), 0) // GLA_HK
    dj = lax.broadcasted_iota(jnp.int32, (GLA_DK, GLA_DK), 1) // GLA_HK
    return dict(
        tri=jnp.where((ri // s_ == ci // s_) & (ci <= ri), 1.0, 0.0).astype(BF16),
        diag_key=jnp.where((row_minus_col >= 0) & (rl % s_ >= row_minus_col), row_minus_col, -1),
        head_of_lane=ll // GLA_HK,
        odd_head_lane=(lax.broadcasted_iota(jnp.int32, (c_, 2 * GLA_HK), 1) // GLA_HK) == 1,
        head_sum=jnp.where(di == dj, 1.0, 0.0).astype(BF16))


def _gla_slab_stages(base, par, consts, gq_ref, gk_ref, gv_ref, la_ref, og_ref, norm_w, o_ref, state_ref,
                     kpad_ref, lapad_ref):
    c_, s_ = GLA_CHUNK, GLA_SUB
    nsub = c_ // s_
    pair = 2 * GLA_HK
    tri, diag_key, head_of_lane = consts["tri"], consts["diag_key"], consts["head_of_lane"]
    odd_head_lane, head_sum = consts["odd_head_lane"], consts["head_sum"]
    chunks = [dict(rows=slice(base + c * c_, base + (c + 1) * c_), off=s_ + c * c_) for c in range(par)]

    def blocks(fn):
        return jnp.concatenate([fn(b) for b in range(nsub)], axis=0)

    def sub(t, b):
        return t[b * s_:(b + 1) * s_, :]

    def cumsums():
        kpad_ref[0:s_, :] = jnp.zeros((s_, GLA_DK), F32)
        lapad_ref[0:s_, :] = jnp.zeros((s_, GLA_DK), F32)
        kpad_ref[s_:, :] = gk_ref[base:base + par * c_, :].astype(F32)
        lapad_ref[s_:, :] = la_ref[base:base + par * c_, :]
        for ch in chunks:
            la = la_ref[ch["rows"], :]
            hi = la.astype(BF16)
            r1 = la - hi.astype(F32)
            mid = r1.astype(BF16)
            lo = (r1 - mid.astype(F32)).astype(BF16)
            ch["l16"] = _dot(tri, hi) + _dot(tri, mid) + _dot(tri, lo)

    def operands():
        for ch in chunks:
            q = gq_ref[ch["rows"], :].astype(F32)
            k = gk_ref[ch["rows"], :].astype(F32)
            l16 = ch["l16"]
            tot = [l16[(b + 1) * s_ - 1:(b + 1) * s_, :] for b in range(nsub)]
            q16 = q * jnp.exp(l16)
            k16 = k * jnp.exp(blocks(lambda b: jnp.broadcast_to(tot[b], (s_, GLA_DK))) - l16)
            zero = jnp.zeros((s_, GLA_DK), F32)
            e = [jnp.exp(t) for t in tot]
            pre = [None, e[0], e[0] * e[1], e[0] * e[1] * e[2]]
            post = [e[1] * e[2] * e[3], e[2] * e[3], e[3], None]
            ch["qd"] = blocks(lambda b: sub(q16, b) if pre[b] is None else sub(q16, b) * pre[b]).astype(BF16)
            kd = blocks(lambda b: sub(k16, b) if post[b] is None else sub(k16, b) * post[b]).astype(BF16)
            ch["decay"] = jnp.exp(tot[0] + tot[1] + tot[2] + tot[3])
            v = gv_ref[ch["rows"], :]
            ch["upd"] = sum(_dot_tn(v[:, h * GLA_HV:(h + 1) * GLA_HV], jnp.where(head_of_lane == h, kd, 0.0))
                            for h in range(GLA_HEADS))
            ch["q_lv"] = [t.astype(BF16) for t in (
                jnp.concatenate([zero, zero, sub(q16, 2), sub(q16, 3) * e[2]], axis=0),
                jnp.concatenate([zero, sub(q16, 1), zero, zero], axis=0),
                jnp.concatenate([zero, zero, zero, sub(q16, 3)], axis=0))]
            ch["k_lv"] = [t.astype(BF16) for t in (
                jnp.concatenate([sub(k16, 0) * e[1], sub(k16, 1), zero, zero], axis=0),
                jnp.concatenate([sub(k16, 0), zero, zero, zero], axis=0),
                jnp.concatenate([zero, zero, sub(k16, 2), zero], axis=0))]
            decay = jnp.zeros((c_, GLA_DK), F32)
            prods = []
            for d in range(s_):
                if d > 0:
                    decay = decay + lapad_ref[ch["off"] - d + 1:ch["off"] - d + 1 + c_, :]
                k_sh = kpad_ref[ch["off"] - d:ch["off"] - d + c_, :]
                prods.append((q * k_sh * jnp.exp(decay)).astype(BF16))
            ch["summed"] = _dot(jnp.concatenate(prods, axis=0), head_sum)

    def pair_scores():
        for ch in chunks:
            ch["attn_off"] = []
            for h in range(GLA_HEADS):
                lanes = slice((h // 2) * pair, (h // 2 + 1) * pair)
                mine = odd_head_lane if h % 2 else ~odd_head_lane
                qx = jnp.concatenate([jnp.where(mine, t[:, lanes], 0.0) for t in ch["q_lv"]], axis=1)
                kx = jnp.concatenate([t[:, lanes] for t in ch["k_lv"]], axis=1)
                ch["attn_off"].append(_dot_nt(qx, kx))

    def outputs():
        state = state_ref[...]
        for ch in chunks:
            ch["state_b"] = state.astype(BF16)
            state = state * ch["decay"] + ch["upd"]
        state_ref[...] = state
        for ch in chunks:
            attn_diag = jnp.zeros((c_, GLA_DK), F32)
            for d in range(s_):
                attn_diag = jnp.where(diag_key == d, ch["summed"][d * c_:(d + 1) * c_, :], attn_diag)
            v = gv_ref[ch["rows"], :]
            for h in range(GLA_HEADS):
                hv = slice(h * GLA_HV, (h + 1) * GLA_HV)
                attn = attn_diag[:, h * GLA_HK:(h + 1) * GLA_HK] + ch["attn_off"][h]
                o = _dot(attn.astype(BF16), v[:, hv])
                o = o + _dot_nt(jnp.where(head_of_lane == h, ch["qd"], 0.0), ch["state_b"])
                o = _rms(o, norm_w)
                g = og_ref[ch["rows"], hv].astype(F32)
                o_ref[ch["rows"], hv] = (o * (g * jax.nn.sigmoid(g))).astype(BF16)

    return [cumsums, operands, pair_scores, outputs]


def _proj_gla_kernel(x_ref, pos_ref, lnmix_ref, win_ref, ncq_ref, wuq_ref, nckv_ref, wk_ref, wvt_ref,
                     wg2_ref, bg_ref, invf_ref, place_ref, gnorm_ref,
                     q_ref, k_ref, vt_ref, sa_ref, sb_ref, ob_ref,
                     gq_ref, gk_ref, gv_ref, la_ref, og_ref, state_ref, kpad_ref, lapad_ref,
                     *, tkv, steps_per_seq, par):
    tm = x_ref.shape[0]

    @pl.when(pl.program_id(0) % steps_per_seq == 0)
    def _():
        state_ref[...] = jnp.zeros_like(state_ref)

    ub = _rms(x_ref[...], lnmix_ref[...]).astype(BF16)

    def seg(lo, width):
        return _dot(ub, win_ref[:, lo:lo + width])

    misc = seg(OFF_MISC, LANES)
    xg = _dot(misc.astype(BF16), wg2_ref[...]) + bg_ref[...]
    la_ref[...] = (jnp.minimum(xg, 0.0) - jnp.log1p(jnp.exp(-jnp.abs(xg)))) * (1.0 / GLA_TAU)
    gq_ref[...] = (seg(OFF_GQ, GLA_DK) * (GLA_HK ** -0.5)).astype(BF16)
    gk_ref[...] = seg(OFF_GK, GLA_DK).astype(BF16)
    gv_ref[...] = seg(OFF_GV, GLA_DV).astype(BF16)
    og_ref[...] = seg(OFF_OG, GLA_DV).astype(BF16)

    consts = _gla_constants()
    slab_rows = par * GLA_CHUNK
    scan = []
    for sl in range(tm // slab_rows):
        scan += _gla_slab_stages(sl * slab_rows, par, consts, gq_ref, gk_ref, gv_ref, la_ref, og_ref,
                                 gnorm_ref[...], ob_ref, state_ref, kpad_ref.at[sl], lapad_ref.at[sl])

    lane = lax.broadcasted_iota(jnp.int32, (1, LANES), 1)
    in_rope = (lane >= ROPE_LO) & (lane < ROPE_LO + MLA_ROPE)
    t = {}

    def latents():
        t["cq"] = seg(OFF_CQ, MLA_Q_RANK)
        t["ckv"] = seg(OFF_CKV, MLA_KV_RANK)

    def rope_tables():
        ang = invf_ref[...] * pos_ref[0]
        cs = jnp.concatenate([jnp.cos(ang), jnp.sin(ang)], axis=0)
        cs_hi = cs.astype(BF16)
        cs_lo = (cs - cs_hi.astype(F32)).astype(BF16)
        tabs = _dot_tn(jnp.concatenate([cs_hi, cs_lo], axis=0), place_ref[...])
        t["cos"] = tabs[:, :LANES] + jnp.where(in_rope, 0.0, 1.0)
        t["sin_x1"] = tabs[:, LANES:2 * LANES]
        t["sin_x2"] = tabs[:, 2 * LANES:]

    def gate_a():
        sa_ref[...] = jax.nn.sigmoid(seg(OFF_GA, D_MODEL)).astype(BF16)

    def gate_b():
        sb_ref[...] = jax.nn.sigmoid(seg(OFF_GB, D_MODEL)).astype(BF16)

    def latent_norms():
        t["cqn"] = _rms(t["cq"], ncq_ref[...]).astype(BF16)
        t["ckvn"] = _rms(t["ckv"], nckv_ref[...]).astype(BF16)

    def up_q():
        t["q"] = _dot(t["cqn"], wuq_ref[...]) * (MLA_QK ** -0.5 * LOG2E)

    def up_kv():
        t["k_nope"] = _dot(t["ckvn"], wk_ref[...])
        for c in range(tm // tkv):
            vt = _dot_nt(wvt_ref[...], t["ckvn"][c * tkv:(c + 1) * tkv, :]).astype(BF16)
            for h in range(MLA_HEADS):
                vt_ref[c, h * MLA_VA:h * MLA_VA + MLA_V, :] = vt[h * MLA_V:(h + 1) * MLA_V, :]
                vt_ref[c, h * MLA_VA + MLA_V:(h + 1) * MLA_VA, :] = jnp.ones((MLA_VA - MLA_V, tkv), BF16)

    def rope_store():
        def rope(a):
            return (a * t["cos"] + pltpu.roll(a, LANES - ROPE_HALF, axis=1) * t["sin_x1"]
                    + pltpu.roll(a, ROPE_HALF, axis=1) * t["sin_x2"])

        k_rope = rope(jnp.where(in_rope, misc, 0.0))
        for h in range(MLA_HEADS):
            sl = slice(h * HEAD_PAD, (h + 1) * HEAD_PAD)
            q_ref[:, sl] = rope(t["q"][:, sl]).astype(BF16)
            k_ref[:, sl] = (t["k_nope"][:, sl] + k_rope).astype(BF16)

    rest = [latents, gate_a, rope_tables, latent_norms, gate_b, up_q, up_kv, rope_store]
    for n in range(max(len(scan), len(rest))):
        if n < len(scan):
            scan[n]()
        if n < len(rest):
            rest[n]()


def _proj_gla_call(x2, pos3, lnmix, winp, ncq, wuqp, nckv, wkp, wvt, wg2p, bg, invf, place, gnorm, seq, tm, tkv):
    t = x2.shape[0]
    par = math.gcd(tm // GLA_CHUNK, GLA_PAR)
    nslab = tm // (par * GLA_CHUNK)
    row = lambda w: pl.BlockSpec((tm, w), lambda i: (i, 0))
    consts = (lnmix, winp, ncq, wuqp, nckv, wkp, wvt, wg2p, bg, invf, place, gnorm)
    out_widths = (MLA_HEADS * HEAD_PAD, MLA_HEADS * HEAD_PAD, None, D_MODEL, D_MODEL, GLA_DV)
    vt_spec = pl.BlockSpec((tm // tkv, MLA_HEADS * MLA_VA, tkv), lambda i: (i, 0, 0))
    vt_shape = jax.ShapeDtypeStruct((t // tkv, MLA_HEADS * MLA_VA, tkv), BF16)
    pad_rows = GLA_SUB + par * GLA_CHUNK
    return pl.pallas_call(
        functools.partial(_proj_gla_kernel, tkv=tkv, steps_per_seq=seq // tm, par=par),
        grid=(t // tm,),
        in_specs=[row(D_MODEL), pl.BlockSpec((1, 1, tm), lambda i: (i, 0, 0))] + [_single_spec(c) for c in consts],
        out_specs=[vt_spec if w is None else row(w) for w in out_widths],
        out_shape=[vt_shape if w is None else jax.ShapeDtypeStruct((t, w), BF16) for w in out_widths],
        scratch_shapes=[pltpu.VMEM((tm, GLA_DK), BF16), pltpu.VMEM((tm, GLA_DK), BF16),
                        pltpu.VMEM((tm, GLA_DV), BF16), pltpu.VMEM((tm, GLA_DK), F32),
                        pltpu.VMEM((tm, GLA_DV), BF16), pltpu.VMEM((GLA_HV, GLA_DK), F32),
                        pltpu.VMEM((nslab, pad_rows, GLA_DK), F32), pltpu.VMEM((nslab, pad_rows, GLA_DK), F32)],
        compiler_params=pltpu.CompilerParams(dimension_semantics=("arbitrary",), vmem_limit_bytes=VMEM_LIMIT),
        name="proj_gla",
    )(x2, pos3, *consts)


def _attn_kernel(q_ref, k_ref, vt_ref, o_ref, acc_ref, ot_ref, m_ref, s0_ref, s1_ref, smax0_ref, smax1_ref, *, tq):
    i = pl.program_id(1)
    key = lax.broadcasted_iota(jnp.int32, (tq, tq), 0)
    qry = lax.broadcasted_iota(jnp.int32, (tq, tq), 1)
    causal = key <= qry
    m_ref[...] = jnp.full(m_ref.shape, -1e30, F32)
    acc_ref[...] = jnp.zeros(acc_ref.shape, F32)
    slots = ((s0_ref, smax0_ref), (s1_ref, smax1_ref))

    def scores(j, slot, masked=False):
        s_ref, smax_ref = slots[slot]
        rows = pl.ds(pl.multiple_of(j * tq, tq), tq)
        for h in range(MLA_HEADS):
            hq = slice(h * HEAD_PAD, (h + 1) * HEAD_PAD)
            s = _dot_nt(k_ref[rows, hq], q_ref[:, hq])
            if masked:
                s = jnp.where(causal, s, -1e30)
            s_ref[h] = s
            smax_ref[h:h + 1, :] = jnp.max(s, axis=0, keepdims=True)

    def accumulate(j, slot):
        s_ref, smax_ref = slots[slot]
        for h in range(MLA_HEADS):
            hv = slice(h * MLA_VA, (h + 1) * MLA_VA)
            m = m_ref[h:h + 1, :]
            m_new = jnp.maximum(m, smax_ref[h:h + 1, :])
            alpha = jnp.exp2(m - m_new)
            p = jnp.exp2(s_ref[h] - m_new)
            m_ref[h:h + 1, :] = m_new
            acc_ref[hv, :] = alpha * acc_ref[hv, :] + _dot(vt_ref[j, hv, :], p.astype(BF16))

    scores(i, 0, masked=True)

    def pair(t, carry):
        scores(2 * t, 1)
        accumulate(jnp.where(t == 0, i, 2 * t - 1), 0)
        scores(2 * t + 1, 0)
        accumulate(2 * t, 1)
        return carry

    lax.fori_loop(0, i // 2, pair, 0)
    in_slot0 = jnp.where(i < 2, i, i - 1 - i % 2)

    @pl.when(i % 2 == 1)
    def _():
        scores(i - 1, 1)
        accumulate(in_slot0, 0)
        accumulate(i - 1, 1)

    @pl.when(i % 2 == 0)
    def _():
        accumulate(in_slot0, 0)

    for h in range(MLA_HEADS):
        lo = h * MLA_VA
        ot_ref[h * MLA_V:(h + 1) * MLA_V, :] = acc_ref[lo:lo + MLA_V, :] / acc_ref[lo + MLA_V:lo + MLA_V + 1, :]
    o_ref[...] = ot_ref[...].T.astype(BF16)


def _attn_call(q, k, vt, batch, seq, tq):
    nq = seq // tq
    return pl.pallas_call(
        functools.partial(_attn_kernel, tq=tq),
        grid=(batch, nq),
        in_specs=[pl.BlockSpec((tq, MLA_HEADS * HEAD_PAD), lambda b, i: (b * nq + i, 0)),
                  pl.BlockSpec((seq, MLA_HEADS * HEAD_PAD), lambda b, i: (b, 0)),
                  pl.BlockSpec((nq, MLA_HEADS * MLA_VA, tq), lambda b, i: (b, 0, 0))],
        out_specs=pl.BlockSpec((tq, MLA_WIDTH), lambda b, i: (b * nq + i, 0)),
        out_shape=jax.ShapeDtypeStruct((batch * seq, MLA_WIDTH), BF16),
        scratch_shapes=[pltpu.VMEM((MLA_HEADS * MLA_VA, tq), F32), pltpu.VMEM((MLA_WIDTH, tq), F32),
                        pltpu.VMEM((MLA_HEADS, tq), F32),
                        pltpu.VMEM((MLA_HEADS, tq, tq), F32), pltpu.VMEM((MLA_HEADS, tq, tq), F32),
                        pltpu.VMEM((MLA_HEADS, tq), F32), pltpu.VMEM((MLA_HEADS, tq), F32)],
        compiler_params=pltpu.CompilerParams(dimension_semantics=("parallel", "arbitrary"),
                                             vmem_limit_bytes=VMEM_LIMIT),
        name="mla_attn",
    )(q, k, vt)


def _post_kernel(x_ref, oa_ref, ob_ref, sa_ref, sb_ref, woa_ref, wob_ref, wout_ref, lnffn_ref,
                 wg_ref, wu_ref, wd_ref, fnorm_ref, out_ref):
    ya = _dot(oa_ref[...], woa_ref[...])
    yb = _dot(ob_ref[...], wob_ref[...])
    mix = sa_ref[...].astype(F32) * ya + sb_ref[...].astype(F32) * yb
    h = x_ref[...] + _dot(mix.astype(BF16), wout_ref[...])
    u = _rms(h, lnffn_ref[...]).astype(BF16)
    g = _dot(u, wg_ref[...])
    act = (g * jax.nn.sigmoid(g) * _dot(u, wu_ref[...])).astype(BF16)
    h = h + _dot(act, wd_ref[...])
    out_ref[...] = _rms(h, fnorm_ref[...])


def _post_call(x2, oa, ob, sa, sb, woa, wob, wout, lnffn, wg, wu, wd, fnorm, tm):
    t = x2.shape[0]
    row = lambda w: pl.BlockSpec((tm, w), lambda i: (i, 0))
    consts = (woa, wob, wout, lnffn, wg, wu, wd, fnorm)
    return pl.pallas_call(
        _post_kernel,
        grid=(t // tm,),
        in_specs=[row(D_MODEL), row(MLA_WIDTH), row(GLA_DV), row(D_MODEL), row(D_MODEL)]
                 + [_single_spec(c) for c in consts],
        out_specs=row(D_MODEL),
        out_shape=jax.ShapeDtypeStruct((t, D_MODEL), F32),
        compiler_params=pltpu.CompilerParams(dimension_semantics=("parallel",), vmem_limit_bytes=VMEM_LIMIT),
        name="post",
    )(x2, oa, ob, sa, sb, *consts)


def _repack_kernel(w_ref, o_ref):
    offs = [0]
    for s in IN_SIZES:
        offs.append(offs[-1] + s)
    w = w_ref[...]
    col = lambda idx: w[:, offs[idx]:offs[idx + 1]]
    rows = w.shape[0]
    misc = jnp.concatenate([jnp.zeros((rows, ROPE_LO), F32), col(2), col(6),
                            jnp.zeros((rows, LANES - GLR_LO - GLA_RANK), F32)], axis=1)
    packed = jnp.concatenate([col(0), col(1), col(3), col(4), col(5), col(7), col(8), col(9), misc], axis=1)
    o_ref[...] = packed.astype(BF16)


def _repack_call(w_in):
    d, n = w_in.shape
    rows = _tile(d, REPACK_ROWS)
    return pl.pallas_call(
        _repack_kernel,
        grid=(d // rows,),
        in_specs=[pl.BlockSpec((rows, n), lambda i: (i, 0))],
        out_specs=pl.BlockSpec((rows, D_IN_PACKED), lambda i: (i, 0)),
        out_shape=jax.ShapeDtypeStruct((d, D_IN_PACKED), BF16),
        compiler_params=pltpu.CompilerParams(dimension_semantics=("parallel",), vmem_limit_bytes=VMEM_LIMIT),
        name="repack_w_in",
    )(w_in)


def _pack_weights(w_in, mla_w_uq, mla_w_ukv, gla_w_gate2):
    winp = _repack_call(w_in)
    wuq = mla_w_uq.reshape(MLA_Q_RANK, MLA_HEADS, MLA_QK)
    wuqp = jnp.pad(wuq, ((0, 0), (0, 0), (0, HEAD_PAD - MLA_QK))).reshape(MLA_Q_RANK, MLA_HEADS * HEAD_PAD)
    wukv = mla_w_ukv.reshape(MLA_KV_RANK, MLA_HEADS, MLA_NOPE + MLA_V)
    wkp = jnp.pad(wukv[:, :, :MLA_NOPE], ((0, 0), (0, 0), (0, HEAD_PAD - MLA_NOPE)))
    wkp = wkp.reshape(MLA_KV_RANK, MLA_HEADS * HEAD_PAD)
    wvt = wukv[:, :, MLA_NOPE:].reshape(MLA_KV_RANK, MLA_WIDTH).T
    wg2p = jnp.pad(gla_w_gate2, ((GLR_LO, LANES - GLR_LO - GLA_RANK), (0, 0)))
    return winp, wuqp.astype(BF16), wkp.astype(BF16), wvt.astype(BF16), wg2p.astype(BF16)


def _rope_inv_freq():
    half = ROPE_HALF
    inv = 1.0 / (ROPE_THETA ** (jnp.arange(half, dtype=F32) / half))
    return inv.reshape(half, 1)


def _rope_placement():
    place = np.zeros((4 * ROPE_HALF, 3 * LANES), np.float32)
    for piece in range(2):
        for i in range(ROPE_HALF):
            cos_row = 2 * piece * ROPE_HALF + i
            sin_row = cos_row + ROPE_HALF
            place[cos_row, ROPE_LO + i] = 1.0
            place[cos_row, ROPE_LO + ROPE_HALF + i] = 1.0
            place[sin_row, LANES + ROPE_LO + i] = -1.0
            place[sin_row, 2 * LANES + ROPE_LO + ROPE_HALF + i] = 1.0
    return jnp.asarray(place, BF16)


def _tile(n, pref):
    while n % pref:
        pref //= 2
    return pref


def kernel(x, positions, ln_mix, w_in, mla_norm_cq, mla_w_uq, mla_norm_ckv, mla_w_ukv, mla_w_o, gla_w_gate2,
           gla_b_gate, gla_norm, gla_w_o, w_out, ln_ffn, ffn_w_gate, ffn_w_up, ffn_w_down, final_norm):
    batch, seq, d = x.shape
    assert d == D_MODEL and w_in.shape[0] == 1 and seq % GLA_CHUNK == 0
    t = batch * seq
    x2 = x.reshape(t, d)
    winp, wuqp, wkp, wvt, wg2p = _pack_weights(w_in[0], mla_w_uq[0], mla_w_ukv[0], gla_w_gate2[0])
    r = lambda a: a.reshape(1, -1)

    tq = _tile(seq, ATTN_TILE)
    tm = max(tq, _tile(seq, PROJ_TILE))
    pos3 = positions.reshape(t // tm, 1, tm).astype(F32)
    q, k, vt, sa, sb, ob = _proj_gla_call(
        x2, pos3, r(ln_mix[0]), winp, r(mla_norm_cq[0]), wuqp, r(mla_norm_ckv[0]), wkp, wvt, wg2p,
        r(gla_b_gate[0]), _rope_inv_freq(), _rope_placement(), r(gla_norm[0]), seq, tm, tq)
    oa = _attn_call(q, k, vt, batch, seq, tq)
    out = _post_call(x2, oa, ob, sa, sb, mla_w_o[0].astype(BF16), gla_w_o[0].astype(BF16),
                     w_out[0].astype(BF16), r(ln_ffn[0]), ffn_w_gate[0].astype(BF16),
                     ffn_w_up[0].astype(BF16), ffn_w_down[0].astype(BF16), r(final_norm), _tile(t, POST_TILE))
    return out.reshape(batch, seq, d)
```

```python
import functools
import math

import jax
import jax.numpy as jnp
import numpy as np
from jax import lax
from jax.experimental import pallas as pl
from jax.experimental.pallas import tpu as pltpu

F32 = jnp.float32
BF16 = jnp.bfloat16

D_MODEL = 1024
MLA_HEADS = 8
MLA_NOPE = 64
MLA_ROPE = 32
MLA_QK = MLA_NOPE + MLA_ROPE
MLA_V = 64
MLA_WIDTH = MLA_HEADS * MLA_V
MLA_Q_RANK = 384
MLA_KV_RANK = 256
ROPE_THETA = 10000.0
GLA_HEADS = 4
GLA_DV = 512
GLA_DK = 256
GLA_HK = GLA_DK // GLA_HEADS
GLA_HV = GLA_DV // GLA_HEADS
GLA_RANK = 16
GLA_TAU = 16.0
D_FF = 2816
EPS = 1e-6
LOG2E = 1.4426950408889634
IN_SIZES = (MLA_Q_RANK, MLA_KV_RANK, MLA_ROPE, GLA_DK, GLA_DK, GLA_DV, GLA_RANK, GLA_DV, D_MODEL, D_MODEL)

LANES = 128
HEAD_PAD = LANES
ROPE_LO = MLA_NOPE
ROPE_HALF = MLA_ROPE // 2
GLR_LO = ROPE_LO + MLA_ROPE

OFF_CQ = 0
OFF_CKV = OFF_CQ + MLA_Q_RANK
OFF_GQ = OFF_CKV + MLA_KV_RANK
OFF_GK = OFF_GQ + GLA_DK
OFF_GV = OFF_GK + GLA_DK
OFF_OG = OFF_GV + GLA_DV
OFF_GA = OFF_OG + GLA_DV
OFF_GB = OFF_GA + D_MODEL
OFF_MISC = OFF_GB + D_MODEL
D_IN_PACKED = OFF_MISC + LANES

REPACK_COLS = 256
PROJ_TILE = 512
POST_TILE = 512
ATTN_TILE = 256
ATTN_GROUP = 1
MLA_VA = MLA_V + 16
GLA_CHUNK = 64
GLA_SUB = 16
GLA_PAR = 4
VMEM_LIMIT = 56 * 1024 * 1024


def _dot(a, b):
    return jnp.dot(a, b, preferred_element_type=F32)


def _dot_nt(a, b):
    return lax.dot_general(a, b, (((1,), (1,)), ((), ())), preferred_element_type=F32)


def _dot_tn(a, b):
    return lax.dot_general(a, b, (((0,), (0,)), ((), ())), preferred_element_type=F32)


def _rms(x, w):
    return x * lax.rsqrt(jnp.mean(x * x, axis=-1, keepdims=True) + EPS) * w


def _single_spec(a):
    return pl.BlockSpec(a.shape, lambda *_: (0,) * a.ndim, pipeline_mode=pl.Buffered(1))


def _gla_constants():
    c_, s_ = GLA_CHUNK, GLA_SUB
    ri = lax.broadcasted_iota(jnp.int32, (c_, c_), 0)
    ci = lax.broadcasted_iota(jnp.int32, (c_, c_), 1)
    rl = lax.broadcasted_iota(jnp.int32, (c_, GLA_DK), 0)
    ll = lax.broadcasted_iota(jnp.int32, (c_, GLA_DK), 1)
    row_minus_col = rl - ll % GLA_HK
    di = lax.broadcasted_iota(jnp.int32, (GLA_DK, GLA_DK), 0) // GLA_HK
    dj = lax.broadcasted_iota(jnp.int32, (GLA_DK, GLA_DK), 1) // GLA_HK
    return dict(
        tri=jnp.where((ri // s_ == ci // s_) & (ci <= ri), 1.0, 0.0).astype(BF16),
        diag_key=jnp.where((row_minus_col >= 0) & (rl % s_ >= row_minus_col), row_minus_col, -1),
        head_of_lane=ll // GLA_HK,
        odd_head_lane=(lax.broadcasted_iota(jnp.int32, (c_, 2 * GLA_HK), 1) // GLA_HK) == 1,
        head_sum=jnp.where(di == dj, 1.0, 0.0).astype(BF16))


def _gla_slab_stages(base, par, consts, gq_ref, gk_ref, gv_ref, la_ref, og_ref, norm_w, o_ref, state_ref,
                     kpad_ref, lapad_ref):
    c_, s_ = GLA_CHUNK, GLA_SUB
    nsub = c_ // s_
    pair = 2 * GLA_HK
    tri, diag_key, head_of_lane = consts["tri"], consts["diag_key"], consts["head_of_lane"]
    odd_head_lane, head_sum = consts["odd_head_lane"], consts["head_sum"]
    chunks = [dict(rows=slice(base + c * c_, base + (c + 1) * c_), off=s_ + c * c_) for c in range(par)]

    def blocks(fn):
        return jnp.concatenate([fn(b) for b in range(nsub)], axis=0)

    def sub(t, b):
        return t[b * s_:(b + 1) * s_, :]

    def cumsums():
        kpad_ref[0:s_, :] = jnp.zeros((s_, GLA_DK), F32)
        lapad_ref[0:s_, :] = jnp.zeros((s_, GLA_DK), F32)
        kpad_ref[s_:, :] = gk_ref[base:base + par * c_, :].astype(F32)
        lapad_ref[s_:, :] = la_ref[base:base + par * c_, :]
        for ch in chunks:
            la = la_ref[ch["rows"], :]
            hi = la.astype(BF16)
            r1 = la - hi.astype(F32)
            mid = r1.astype(BF16)
            lo = (r1 - mid.astype(F32)).astype(BF16)
            ch["l16"] = _dot(tri, hi) + _dot(tri, mid) + _dot(tri, lo)

    def operands():
        for ch in chunks:
            q = gq_ref[ch["rows"], :].astype(F32)
            k = gk_ref[ch["rows"], :].astype(F32)
            l16 = ch["l16"]
            tot = [l16[(b + 1) * s_ - 1:(b + 1) * s_, :] for b in range(nsub)]
            q16 = q * jnp.exp(l16)
            k16 = k * jnp.exp(blocks(lambda b: jnp.broadcast_to(tot[b], (s_, GLA_DK))) - l16)
            zero = jnp.zeros((s_, GLA_DK), F32)
            e = [jnp.exp(t) for t in tot]
            pre = [None, e[0], e[0] * e[1], e[0] * e[1] * e[2]]
            post = [e[1] * e[2] * e[3], e[2] * e[3], e[3], None]
            ch["qd"] = blocks(lambda b: sub(q16, b) if pre[b] is None else sub(q16, b) * pre[b]).astype(BF16)
            kd = blocks(lambda b: sub(k16, b) if post[b] is None else sub(k16, b) * post[b]).astype(BF16)
            ch["decay"] = jnp.exp(tot[0] + tot[1] + tot[2] + tot[3])
            v = gv_ref[ch["rows"], :]
            ch["upd"] = sum(_dot_tn(v[:, h * GLA_HV:(h + 1) * GLA_HV], jnp.where(head_of_lane == h, kd, 0.0))
                            for h in range(GLA_HEADS))
            ch["q_lv"] = [t.astype(BF16) for t in (
                jnp.concatenate([zero, zero, sub(q16, 2), sub(q16, 3) * e[2]], axis=0),
                jnp.concatenate([zero, sub(q16, 1), zero, zero], axis=0),
                jnp.concatenate([zero, zero, zero, sub(q16, 3)], axis=0))]
            ch["k_lv"] = [t.astype(BF16) for t in (
                jnp.concatenate([sub(k16, 0) * e[1], sub(k16, 1), zero, zero], axis=0),
                jnp.concatenate([sub(k16, 0), zero, zero, zero], axis=0),
                jnp.concatenate([zero, zero, sub(k16, 2), zero], axis=0))]
            decay = jnp.zeros((c_, GLA_DK), F32)
            prods = []
            for d in range(s_):
                if d > 0:
                    decay = decay + lapad_ref[ch["off"] - d + 1:ch["off"] - d + 1 + c_, :]
                k_sh = kpad_ref[ch["off"] - d:ch["off"] - d + c_, :]
                prods.append((q * k_sh * jnp.exp(decay)).astype(BF16))
            ch["summed"] = _dot(jnp.concatenate(prods, axis=0), head_sum)

    def pair_scores():
        for ch in chunks:
            ch["attn_off"] = []
            for h in range(GLA_HEADS):
                lanes = slice((h // 2) * pair, (h // 2 + 1) * pair)
                mine = odd_head_lane if h % 2 else ~odd_head_lane
                qx = jnp.concatenate([jnp.where(mine, t[:, lanes], 0.0) for t in ch["q_lv"]], axis=1)
                kx = jnp.concatenate([t[:, lanes] for t in ch["k_lv"]], axis=1)
                ch["attn_off"].append(_dot_nt(qx, kx))

    def outputs():
        state = state_ref[...]
        for ch in chunks:
            ch["state_b"] = state.astype(BF16)
            state = state * ch["decay"] + ch["upd"]
        state_ref[...] = state
        for ch in chunks:
            attn_diag = jnp.zeros((c_, GLA_DK), F32)
            for d in range(s_):
                attn_diag = jnp.where(diag_key == d, ch["summed"][d * c_:(d + 1) * c_, :], attn_diag)
            v = gv_ref[ch["rows"], :]
            for h in range(GLA_HEADS):
                hv = slice(h * GLA_HV, (h + 1) * GLA_HV)
                attn = attn_diag[:, h * GLA_HK:(h + 1) * GLA_HK] + ch["attn_off"][h]
                o = _dot(attn.astype(BF16), v[:, hv])
                o = o + _dot_nt(jnp.where(head_of_lane == h, ch["qd"], 0.0), ch["state_b"])
                o = _rms(o, norm_w)
                g = og_ref[ch["rows"], hv].astype(F32)
                o_ref[ch["rows"], hv] = (o * (g * jax.nn.sigmoid(g))).astype(BF16)

    return [cumsums, operands, pair_scores, outputs]


def _proj_gla_kernel(x_ref, pos_ref, lnmix_ref, win_ref, ncq_ref, wuq_ref, nckv_ref, wk_ref, wvt_ref,
                     wg2_ref, bg_ref, invf_ref, place_ref, gnorm_ref,
                     q_ref, k_ref, vt_ref, sa_ref, sb_ref, ob_ref,
                     gq_ref, gk_ref, gv_ref, la_ref, og_ref, state_ref, kpad_ref, lapad_ref,
                     *, tkv, steps_per_seq, par):
    tm = x_ref.shape[0]

    @pl.when(pl.program_id(0) % steps_per_seq == 0)
    def _():
        state_ref[...] = jnp.zeros_like(state_ref)

    ub = _rms(x_ref[...], lnmix_ref[...]).astype(BF16)

    def seg(lo, width):
        return _dot_nt(ub, win_ref[lo:lo + width, :])

    misc = seg(OFF_MISC, LANES)
    xg = _dot(misc.astype(BF16), wg2_ref[...]) + bg_ref[...]
    la_ref[...] = (jnp.minimum(xg, 0.0) - jnp.log1p(jnp.exp(-jnp.abs(xg)))) * (1.0 / GLA_TAU)
    gq_ref[...] = (seg(OFF_GQ, GLA_DK) * (GLA_HK ** -0.5)).astype(BF16)
    gk_ref[...] = seg(OFF_GK, GLA_DK).astype(BF16)
    gv_ref[...] = seg(OFF_GV, GLA_DV).astype(BF16)
    og_ref[...] = seg(OFF_OG, GLA_DV).astype(BF16)

    consts = _gla_constants()
    slab_rows = par * GLA_CHUNK
    scan = []
    for sl in range(tm // slab_rows):
        scan += _gla_slab_stages(sl * slab_rows, par, consts, gq_ref, gk_ref, gv_ref, la_ref, og_ref,
                                 gnorm_ref[...], ob_ref, state_ref, kpad_ref.at[sl], lapad_ref.at[sl])

    lane = lax.broadcasted_iota(jnp.int32, (1, LANES), 1)
    in_rope = (lane >= ROPE_LO) & (lane < ROPE_LO + MLA_ROPE)
    t = {}

    def latents():
        t["cq"] = seg(OFF_CQ, MLA_Q_RANK)
        t["ckv"] = seg(OFF_CKV, MLA_KV_RANK)

    def rope_tables():
        ang = invf_ref[...] * pos_ref[0]
        cs = jnp.concatenate([jnp.cos(ang), jnp.sin(ang)], axis=0)
        cs_hi = cs.astype(BF16)
        cs_lo = (cs - cs_hi.astype(F32)).astype(BF16)
        tabs = _dot_tn(jnp.concatenate([cs_hi, cs_lo], axis=0), place_ref[...])
        t["cos"] = tabs[:, :LANES] + jnp.where(in_rope, 0.0, 1.0)
        t["sin_x1"] = tabs[:, LANES:2 * LANES]
        t["sin_x2"] = tabs[:, 2 * LANES:]

    def gate_a():
        sa_ref[...] = jax.nn.sigmoid(seg(OFF_GA, D_MODEL)).astype(BF16)

    def gate_b():
        sb_ref[...] = jax.nn.sigmoid(seg(OFF_GB, D_MODEL)).astype(BF16)

    def latent_norms():
        t["cqn"] = _rms(t["cq"], ncq_ref[...]).astype(BF16)
        t["ckvn"] = _rms(t["ckv"], nckv_ref[...]).astype(BF16)

    def up_q():
        t["q"] = _dot(t["cqn"], wuq_ref[...]) * (MLA_QK ** -0.5 * LOG2E)

    def up_kv():
        t["k_nope"] = _dot(t["ckvn"], wk_ref[...])
        for c in range(tm // tkv):
            vt = _dot_nt(wvt_ref[...], t["ckvn"][c * tkv:(c + 1) * tkv, :]).astype(BF16)
            for h in range(MLA_HEADS):
                vt_ref[c, h * MLA_VA:h * MLA_VA + MLA_V, :] = vt[h * MLA_V:(h + 1) * MLA_V, :]
                vt_ref[c, h * MLA_VA + MLA_V:(h + 1) * MLA_VA, :] = jnp.ones((MLA_VA - MLA_V, tkv), BF16)

    def rope_store():
        def rope(a):
            return (a * t["cos"] + pltpu.roll(a, LANES - ROPE_HALF, axis=1) * t["sin_x1"]
                    + pltpu.roll(a, ROPE_HALF, axis=1) * t["sin_x2"])

        k_rope = rope(jnp.where(in_rope, misc, 0.0))
        for h in range(MLA_HEADS):
            sl = slice(h * HEAD_PAD, (h + 1) * HEAD_PAD)
            q_ref[:, sl] = rope(t["q"][:, sl]).astype(BF16)
            k_ref[:, sl] = (t["k_nope"][:, sl] + k_rope).astype(BF16)

    rest = [latents, gate_a, rope_tables, latent_norms, gate_b, up_q, up_kv, rope_store]
    for n in range(max(len(scan), len(rest))):
        if n < len(scan):
            scan[n]()
        if n < len(rest):
            rest[n]()


def _proj_gla_call(x2, pos3, lnmix, winp, ncq, wuqp, nckv, wkp, wvt, wg2p, bg, invf, place, gnorm, seq, tm, tkv):
    t = x2.shape[0]
    par = math.gcd(tm // GLA_CHUNK, GLA_PAR)
    nslab = tm // (par * GLA_CHUNK)
    row = lambda w: pl.BlockSpec((tm, w), lambda i: (i, 0))
    consts = (lnmix, winp, ncq, wuqp, nckv, wkp, wvt, wg2p, bg, invf, place, gnorm)
    out_widths = (MLA_HEADS * HEAD_PAD, MLA_HEADS * HEAD_PAD, None, D_MODEL, D_MODEL, GLA_DV)
    vt_spec = pl.BlockSpec((tm // tkv, MLA_HEADS * MLA_VA, tkv), lambda i: (i, 0, 0))
    vt_shape = jax.ShapeDtypeStruct((t // tkv, MLA_HEADS * MLA_VA, tkv), BF16)
    pad_rows = GLA_SUB + par * GLA_CHUNK
    return pl.pallas_call(
        functools.partial(_proj_gla_kernel, tkv=tkv, steps_per_seq=seq // tm, par=par),
        grid=(t // tm,),
        in_specs=[row(D_MODEL), pl.BlockSpec((1, 1, tm), lambda i: (i, 0, 0))] + [_single_spec(c) for c in consts],
        out_specs=[vt_spec if w is None else row(w) for w in out_widths],
        out_shape=[vt_shape if w is None else jax.ShapeDtypeStruct((t, w), BF16) for w in out_widths],
        scratch_shapes=[pltpu.VMEM((tm, GLA_DK), BF16), pltpu.VMEM((tm, GLA_DK), BF16),
                        pltpu.VMEM((tm, GLA_DV), BF16), pltpu.VMEM((tm, GLA_DK), F32),
                        pltpu.VMEM((tm, GLA_DV), BF16), pltpu.VMEM((GLA_HV, GLA_DK), F32),
                        pltpu.VMEM((nslab, pad_rows, GLA_DK), F32), pltpu.VMEM((nslab, pad_rows, GLA_DK), F32)],
        compiler_params=pltpu.CompilerParams(dimension_semantics=("arbitrary",), vmem_limit_bytes=VMEM_LIMIT),
        name="proj_gla",
    )(x2, pos3, *consts)


def _attn_kernel(q_ref, k_ref, vt_ref, o_ref, acc_ref, ot_ref, m_ref, s0_ref, s1_ref, smax0_ref, smax1_ref, *, tq):
    i = pl.program_id(1)
    key = lax.broadcasted_iota(jnp.int32, (tq, tq), 0)
    qry = lax.broadcasted_iota(jnp.int32, (tq, tq), 1)
    causal = key <= qry
    m_ref[...] = jnp.full(m_ref.shape, -1e30, F32)
    acc_ref[...] = jnp.zeros(acc_ref.shape, F32)
    slots = ((s0_ref, smax0_ref), (s1_ref, smax1_ref))

    def scores(j, slot, heads, masked=False):
        s_ref, smax_ref = slots[slot]
        rows = pl.ds(pl.multiple_of(j * tq, tq), tq)
        for h in heads:
            hq = slice(h * HEAD_PAD, (h + 1) * HEAD_PAD)
            s = _dot_nt(k_ref[rows, hq], q_ref[:, hq])
            if masked:
                s = jnp.where(causal, s, -1e30)
            s_ref[h] = s
            smax_ref[h:h + 1, :] = jnp.max(s, axis=0, keepdims=True)

    def accumulate(j, slot, heads):
        s_ref, smax_ref = slots[slot]
        for h in heads:
            hv = slice(h * MLA_VA, (h + 1) * MLA_VA)
            m = m_ref[h:h + 1, :]
            m_new = jnp.maximum(m, smax_ref[h:h + 1, :])
            alpha = jnp.exp2(m - m_new)
            p = jnp.exp2(s_ref[h] - m_new)
            m_ref[h:h + 1, :] = m_new
            acc_ref[hv, :] = alpha * acc_ref[hv, :] + _dot(vt_ref[j, hv, :], p.astype(BF16))

    all_heads = range(MLA_HEADS)

    def overlapped(score_tile, score_slot, acc_tile, acc_slot):
        for g in range(0, MLA_HEADS, ATTN_GROUP):
            heads = range(g, g + ATTN_GROUP)
            scores(score_tile, score_slot, heads)
            accumulate(acc_tile, acc_slot, heads)

    scores(i, 0, all_heads, masked=True)

    def pair(t, carry):
        overlapped(2 * t, 1, jnp.where(t == 0, i, 2 * t - 1), 0)
        overlapped(2 * t + 1, 0, 2 * t, 1)
        return carry

    lax.fori_loop(0, i // 2, pair, 0)
    in_slot0 = jnp.where(i < 2, i, i - 1 - i % 2)

    @pl.when(i % 2 == 1)
    def _():
        overlapped(i - 1, 1, in_slot0, 0)
        accumulate(i - 1, 1, all_heads)

    @pl.when(i % 2 == 0)
    def _():
        accumulate(in_slot0, 0, all_heads)

    for h in range(MLA_HEADS):
        lo = h * MLA_VA
        ot_ref[h * MLA_V:(h + 1) * MLA_V, :] = acc_ref[lo:lo + MLA_V, :] / acc_ref[lo + MLA_V:lo + MLA_V + 1, :]
    o_ref[...] = ot_ref[...].T.astype(BF16)


def _attn_call(q, k, vt, batch, seq, tq):
    nq = seq // tq
    return pl.pallas_call(
        functools.partial(_attn_kernel, tq=tq),
        grid=(batch, nq),
        in_specs=[pl.BlockSpec((tq, MLA_HEADS * HEAD_PAD), lambda b, i: (b * nq + i, 0)),
                  pl.BlockSpec((seq, MLA_HEADS * HEAD_PAD), lambda b, i: (b, 0)),
                  pl.BlockSpec((nq, MLA_HEADS * MLA_VA, tq), lambda b, i: (b, 0, 0))],
        out_specs=pl.BlockSpec((tq, MLA_WIDTH), lambda b, i: (b * nq + i, 0)),
        out_shape=jax.ShapeDtypeStruct((batch * seq, MLA_WIDTH), BF16),
        scratch_shapes=[pltpu.VMEM((MLA_HEADS * MLA_VA, tq), F32), pltpu.VMEM((MLA_WIDTH, tq), F32),
                        pltpu.VMEM((MLA_HEADS, tq), F32),
                        pltpu.VMEM((MLA_HEADS, tq, tq), F32), pltpu.VMEM((MLA_HEADS, tq, tq), F32),
                        pltpu.VMEM((MLA_HEADS, tq), F32), pltpu.VMEM((MLA_HEADS, tq), F32)],
        compiler_params=pltpu.CompilerParams(dimension_semantics=("parallel", "arbitrary"),
                                             vmem_limit_bytes=VMEM_LIMIT),
        name="mla_attn",
    )(q, k, vt)


def _post_kernel(x_ref, oa_ref, ob_ref, sa_ref, sb_ref, woa_ref, wob_ref, wout_ref, lnffn_ref,
                 wg_ref, wu_ref, wd_ref, fnorm_ref, out_ref):
    ya = _dot(oa_ref[...], woa_ref[...])
    yb = _dot(ob_ref[...], wob_ref[...])
    mix = sa_ref[...].astype(F32) * ya + sb_ref[...].astype(F32) * yb
    h = x_ref[...] + _dot(mix.astype(BF16), wout_ref[...])
    u = _rms(h, lnffn_ref[...]).astype(BF16)
    g = _dot(u, wg_ref[...])
    act = (g * jax.nn.sigmoid(g) * _dot(u, wu_ref[...])).astype(BF16)
    h = h + _dot(act, wd_ref[...])
    out_ref[...] = _rms(h, fnorm_ref[...])


def _post_call(x2, oa, ob, sa, sb, woa, wob, wout, lnffn, wg, wu, wd, fnorm, tm):
    t = x2.shape[0]
    row = lambda w: pl.BlockSpec((tm, w), lambda i: (i, 0))
    consts = (woa, wob, wout, lnffn, wg, wu, wd, fnorm)
    return pl.pallas_call(
        _post_kernel,
        grid=(t // tm,),
        in_specs=[row(D_MODEL), row(MLA_WIDTH), row(GLA_DV), row(D_MODEL), row(D_MODEL)]
                 + [_single_spec(c) for c in consts],
        out_specs=row(D_MODEL),
        out_shape=jax.ShapeDtypeStruct((t, D_MODEL), F32),
        compiler_params=pltpu.CompilerParams(dimension_semantics=("parallel",), vmem_limit_bytes=VMEM_LIMIT),
        name="post",
    )(x2, oa, ob, sa, sb, *consts)


def _repack_kernel(wt_ref, o_ref):
    offs = [0]
    for s in IN_SIZES:
        offs.append(offs[-1] + s)
    cols = wt_ref.shape[1]
    dst = 0
    for idx in (0, 1, 3, 4, 5, 7, 8, 9):
        n = IN_SIZES[idx]
        o_ref[dst:dst + n, :] = wt_ref[offs[idx]:offs[idx + 1], :].astype(BF16)
        dst += n
    o_ref[dst:dst + ROPE_LO, :] = jnp.zeros((ROPE_LO, cols), BF16)
    o_ref[dst + ROPE_LO:dst + GLR_LO, :] = wt_ref[offs[2]:offs[3], :].astype(BF16)
    o_ref[dst + GLR_LO:dst + GLR_LO + GLA_RANK, :] = wt_ref[offs[6]:offs[7], :].astype(BF16)
    o_ref[dst + GLR_LO + GLA_RANK:dst + LANES, :] = jnp.zeros((LANES - GLR_LO - GLA_RANK, cols), BF16)


def _repack_call(w_in_t):
    n, d = w_in_t.shape
    cols = _tile(d, REPACK_COLS)
    return pl.pallas_call(
        _repack_kernel,
        grid=(d // cols,),
        in_specs=[pl.BlockSpec((n, cols), lambda i: (0, i))],
        out_specs=pl.BlockSpec((D_IN_PACKED, cols), lambda i: (0, i)),
        out_shape=jax.ShapeDtypeStruct((D_IN_PACKED, d), BF16),
        compiler_params=pltpu.CompilerParams(dimension_semantics=("parallel",), vmem_limit_bytes=VMEM_LIMIT),
        name="repack_w_in",
    )(w_in_t)


def _pack_weights(w_in, mla_w_uq, mla_w_ukv, gla_w_gate2):
    winp = _repack_call(w_in.T)
    wuq = mla_w_uq.reshape(MLA_Q_RANK, MLA_HEADS, MLA_QK)
    wuqp = jnp.pad(wuq, ((0, 0), (0, 0), (0, HEAD_PAD - MLA_QK))).reshape(MLA_Q_RANK, MLA_HEADS * HEAD_PAD)
    wukv = mla_w_ukv.reshape(MLA_KV_RANK, MLA_HEADS, MLA_NOPE + MLA_V)
    wkp = jnp.pad(wukv[:, :, :MLA_NOPE], ((0, 0), (0, 0), (0, HEAD_PAD - MLA_NOPE)))
    wkp = wkp.reshape(MLA_KV_RANK, MLA_HEADS * HEAD_PAD)
    wvt = wukv[:, :, MLA_NOPE:].reshape(MLA_KV_RANK, MLA_WIDTH).T
    wg2p = jnp.pad(gla_w_gate2, ((GLR_LO, LANES - GLR_LO - GLA_RANK), (0, 0)))
    return winp, wuqp.astype(BF16), wkp.astype(BF16), wvt.astype(BF16), wg2p.astype(BF16)


def _rope_inv_freq():
    half = ROPE_HALF
    inv = 1.0 / (ROPE_THETA ** (jnp.arange(half, dtype=F32) / half))
    return inv.reshape(half, 1)


def _rope_placement():
    place = np.zeros((4 * ROPE_HALF, 3 * LANES), np.float32)
    for piece in range(2):
        for i in range(ROPE_HALF):
            cos_row = 2 * piece * ROPE_HALF + i
            sin_row = cos_row + ROPE_HALF
            place[cos_row, ROPE_LO + i] = 1.0
            place[cos_row, ROPE_LO + ROPE_HALF + i] = 1.0
            place[sin_row, LANES + ROPE_LO + i] = -1.0
            place[sin_row, 2 * LANES + ROPE_LO + ROPE_HALF + i] = 1.0
    return jnp.asarray(place, BF16)


def _tile(n, pref):
    while n % pref:
        pref //= 2
    return pref


def kernel(x, positions, ln_mix, w_in, mla_norm_cq, mla_w_uq, mla_norm_ckv, mla_w_ukv, mla_w_o, gla_w_gate2,
           gla_b_gate, gla_norm, gla_w_o, w_out, ln_ffn, ffn_w_gate, ffn_w_up, ffn_w_down, final_norm):
    batch, seq, d = x.shape
    assert d == D_MODEL and w_in.shape[0] == 1 and seq % GLA_CHUNK == 0
    t = batch * seq
    x2 = x.reshape(t, d)
    winp, wuqp, wkp, wvt, wg2p = _pack_weights(w_in[0], mla_w_uq[0], mla_w_ukv[0], gla_w_gate2[0])
    r = lambda a: a.reshape(1, -1)

    tq = _tile(seq, ATTN_TILE)
    tm = max(tq, _tile(seq, PROJ_TILE))
    pos3 = positions.reshape(t // tm, 1, tm).astype(F32)
    q, k, vt, sa, sb, ob = _proj_gla_call(
        x2, pos3, r(ln_mix[0]), winp, r(mla_norm_cq[0]), wuqp, r(mla_norm_ckv[0]), wkp, wvt, wg2p,
        r(gla_b_gate[0]), _rope_inv_freq(), _rope_placement(), r(gla_norm[0]), seq, tm, tq)
    oa = _attn_call(q, k, vt, batch, seq, tq)
    out = _post_call(x2, oa, ob, sa, sb, mla_w_o[0].astype(BF16), gla_w_o[0].astype(BF16),
                     w_out[0].astype(BF16), r(ln_ffn[0]), ffn_w_gate[0].astype(BF16),
                     ffn_w_up[0].astype(BF16), ffn_w_down[0].astype(BF16), r(final_norm), _tile(t, POST_TILE))
    return out.reshape(batch, seq, d)
```

```python
import functools
import math

import jax
import jax.numpy as jnp
import numpy as np
from jax import lax
from jax.experimental import pallas as pl
from jax.experimental.pallas import tpu as pltpu

F32 = jnp.float32
BF16 = jnp.bfloat16

D_MODEL = 1024
MLA_HEADS = 8
MLA_NOPE = 64
MLA_ROPE = 32
MLA_QK = MLA_NOPE + MLA_ROPE
MLA_V = 64
MLA_WIDTH = MLA_HEADS * MLA_V
MLA_Q_RANK = 384
MLA_KV_RANK = 256
ROPE_THETA = 10000.0
GLA_HEADS = 4
GLA_DV = 512
GLA_DK = 256
GLA_HK = GLA_DK // GLA_HEADS
GLA_HV = GLA_DV // GLA_HEADS
GLA_RANK = 16
GLA_TAU = 16.0
D_FF = 2816
EPS = 1e-6
LOG2E = 1.4426950408889634
IN_SIZES = (MLA_Q_RANK, MLA_KV_RANK, MLA_ROPE, GLA_DK, GLA_DK, GLA_DV, GLA_RANK, GLA_DV, D_MODEL, D_MODEL)

LANES = 128
HEAD_PAD = LANES
ROPE_LO = MLA_NOPE
ROPE_HALF = MLA_ROPE // 2
GLR_LO = ROPE_LO + MLA_ROPE

OFF_CQ = 0
OFF_CKV = OFF_CQ + MLA_Q_RANK
OFF_GQ = OFF_CKV + MLA_KV_RANK
OFF_GK = OFF_GQ + GLA_DK
OFF_GV = OFF_GK + GLA_DK
OFF_OG = OFF_GV + GLA_DV
OFF_GA = OFF_OG + GLA_DV
OFF_GB = OFF_GA + D_MODEL
OFF_MISC = OFF_GB + D_MODEL
D_IN_PACKED = OFF_MISC + LANES

REPACK_COLS = 256
PROJ_TILE = 512
POST_TILE = 512
ATTN_TILE = 256
ATTN_GROUP = 1
MLA_VA = MLA_V + 16
GLA_CHUNK = 64
GLA_SUB = 16
GLA_PAR = 4
VMEM_LIMIT = 56 * 1024 * 1024


def _dot(a, b):
    return jnp.dot(a, b, preferred_element_type=F32)


def _dot_nt(a, b):
    return lax.dot_general(a, b, (((1,), (1,)), ((), ())), preferred_element_type=F32)


def _dot_tn(a, b):
    return lax.dot_general(a, b, (((0,), (0,)), ((), ())), preferred_element_type=F32)


def _rms(x, w):
    return x * lax.rsqrt(jnp.mean(x * x, axis=-1, keepdims=True) + EPS) * w


def _single_spec(a):
    return pl.BlockSpec(a.shape, lambda *_: (0,) * a.ndim, pipeline_mode=pl.Buffered(1))


def _gla_constants():
    c_, s_ = GLA_CHUNK, GLA_SUB
    ri = lax.broadcasted_iota(jnp.int32, (c_, c_), 0)
    ci = lax.broadcasted_iota(jnp.int32, (c_, c_), 1)
    rl = lax.broadcasted_iota(jnp.int32, (c_, GLA_DK), 0)
    ll = lax.broadcasted_iota(jnp.int32, (c_, GLA_DK), 1)
    row_minus_col = rl - ll % GLA_HK
    di = lax.broadcasted_iota(jnp.int32, (GLA_DK, GLA_DK), 0) // GLA_HK
    dj = lax.broadcasted_iota(jnp.int32, (GLA_DK, GLA_DK), 1) // GLA_HK
    return dict(
        tri=jnp.where((ri // s_ == ci // s_) & (ci <= ri), 1.0, 0.0).astype(BF16),
        diag_key=jnp.where((row_minus_col >= 0) & (rl % s_ >= row_minus_col), row_minus_col, -1),
        head_of_lane=ll // GLA_HK,
        odd_head_lane=(lax.broadcasted_iota(jnp.int32, (c_, 2 * GLA_HK), 1) // GLA_HK) == 1,
        head_sum=jnp.where(di == dj, 1.0, 0.0).astype(BF16))


def _gla_slab_stages(base, par, consts, gq_ref, gk_ref, gv_ref, la_ref, og_ref, norm_w, o_ref, state_ref,
                     kpad_ref, lapad_ref):
    c_, s_ = GLA_CHUNK, GLA_SUB
    nsub = c_ // s_
    pair = 2 * GLA_HK
    tri, diag_key, head_of_lane = consts["tri"], consts["diag_key"], consts["head_of_lane"]
    odd_head_lane, head_sum = consts["odd_head_lane"], consts["head_sum"]
    chunks = [dict(rows=slice(base + c * c_, base + (c + 1) * c_), off=s_ + c * c_) for c in range(par)]

    def blocks(fn):
        return jnp.concatenate([fn(b) for b in range(nsub)], axis=0)

    def sub(t, b):
        return t[b * s_:(b + 1) * s_, :]

    def cumsums():
        kpad_ref[0:s_, :] = jnp.zeros((s_, GLA_DK), F32)
        lapad_ref[0:s_, :] = jnp.zeros((s_, GLA_DK), F32)
        kpad_ref[s_:, :] = gk_ref[base:base + par * c_, :].astype(F32)
        lapad_ref[s_:, :] = la_ref[base:base + par * c_, :]
        for ch in chunks:
            la = la_ref[ch["rows"], :]
            hi = la.astype(BF16)
            r1 = la - hi.astype(F32)
            mid = r1.astype(BF16)
            lo = (r1 - mid.astype(F32)).astype(BF16)
            ch["l16"] = _dot(tri, hi) + _dot(tri, mid) + _dot(tri, lo)

    def operands():
        for ch in chunks:
            q = gq_ref[ch["rows"], :].astype(F32)
            k = gk_ref[ch["rows"], :].astype(F32)
            l16 = ch["l16"]
            tot = [l16[(b + 1) * s_ - 1:(b + 1) * s_, :] for b in range(nsub)]
            q16 = q * jnp.exp(l16)
            k16 = k * jnp.exp(blocks(lambda b: jnp.broadcast_to(tot[b], (s_, GLA_DK))) - l16)
            zero = jnp.zeros((s_, GLA_DK), F32)
            e = [jnp.exp(t) for t in tot]
            pre = [None, e[0], e[0] * e[1], e[0] * e[1] * e[2]]
            post = [e[1] * e[2] * e[3], e[2] * e[3], e[3], None]
            ch["qd"] = blocks(lambda b: sub(q16, b) if pre[b] is None else sub(q16, b) * pre[b]).astype(BF16)
            kd = blocks(lambda b: sub(k16, b) if post[b] is None else sub(k16, b) * post[b]).astype(BF16)
            ch["decay"] = jnp.exp(tot[0] + tot[1] + tot[2] + tot[3])
            v = gv_ref[ch["rows"], :]
            ch["upd"] = sum(_dot_tn(v[:, h * GLA_HV:(h + 1) * GLA_HV], jnp.where(head_of_lane == h, kd, 0.0))
                            for h in range(GLA_HEADS))
            ch["q_lv"] = [t.astype(BF16) for t in (
                jnp.concatenate([zero, zero, sub(q16, 2), sub(q16, 3) * e[2]], axis=0),
                jnp.concatenate([zero, sub(q16, 1), zero, zero], axis=0),
                jnp.concatenate([zero, zero, zero, sub(q16, 3)], axis=0))]
            ch["k_lv"] = [t.astype(BF16) for t in (
                jnp.concatenate([sub(k16, 0) * e[1], sub(k16, 1), zero, zero], axis=0),
                jnp.concatenate([sub(k16, 0), zero, zero, zero], axis=0),
                jnp.concatenate([zero, zero, sub(k16, 2), zero], axis=0))]
            decay = jnp.zeros((c_, GLA_DK), F32)
            prods = []
            for d in range(s_):
                if d > 0:
                    decay = decay + lapad_ref[ch["off"] - d + 1:ch["off"] - d + 1 + c_, :]
                k_sh = kpad_ref[ch["off"] - d:ch["off"] - d + c_, :]
                prods.append((q * k_sh * jnp.exp(decay)).astype(BF16))
            ch["summed"] = _dot(jnp.concatenate(prods, axis=0), head_sum)

    def pair_scores():
        for ch in chunks:
            ch["attn_off"] = []
            for h in range(GLA_HEADS):
                lanes = slice((h // 2) * pair, (h // 2 + 1) * pair)
                mine = odd_head_lane if h % 2 else ~odd_head_lane
                qx = jnp.concatenate([jnp.where(mine, t[:, lanes], 0.0) for t in ch["q_lv"]], axis=1)
                kx = jnp.concatenate([t[:, lanes] for t in ch["k_lv"]], axis=1)
                ch["attn_off"].append(_dot_nt(qx, kx))

    def outputs():
        state = state_ref[...]
        for ch in chunks:
            ch["state_b"] = state.astype(BF16)
            state = state * ch["decay"] + ch["upd"]
        state_ref[...] = state
        for ch in chunks:
            attn_diag = jnp.zeros((c_, GLA_DK), F32)
            for d in range(s_):
                attn_diag = jnp.where(diag_key == d, ch["summed"][d * c_:(d + 1) * c_, :], attn_diag)
            v = gv_ref[ch["rows"], :]
            for h in range(GLA_HEADS):
                hv = slice(h * GLA_HV, (h + 1) * GLA_HV)
                attn = attn_diag[:, h * GLA_HK:(h + 1) * GLA_HK] + ch["attn_off"][h]
                o = _dot(attn.astype(BF16), v[:, hv])
                o = o + _dot_nt(jnp.where(head_of_lane == h, ch["qd"], 0.0), ch["state_b"])
                o = _rms(o, norm_w)
                g = og_ref[ch["rows"], hv].astype(F32)
                o_ref[ch["rows"], hv] = (o * (g * jax.nn.sigmoid(g))).astype(BF16)

    return [cumsums, operands, pair_scores, outputs]


def _proj_gla_kernel(x_ref, pos_ref, lnmix_ref, win_ref, ncq_ref, wuq_ref, nckv_ref, wk_ref, wvt_ref,
                     wg2_ref, bg_ref, invf_ref, place_ref, gnorm_ref,
                     q_ref, k_ref, vt_ref, sa_ref, sb_ref, ob_ref,
                     gq_ref, gk_ref, gv_ref, la_ref, og_ref, state_ref, kpad_ref, lapad_ref,
                     *, tkv, steps_per_seq, par):
    tm = x_ref.shape[0]

    @pl.when(pl.program_id(0) % steps_per_seq == 0)
    def _():
        state_ref[...] = jnp.zeros_like(state_ref)

    ub = _rms(x_ref[...], lnmix_ref[...]).astype(BF16)

    def seg(lo, width):
        return _dot_nt(ub, win_ref[lo:lo + width, :])

    misc = seg(OFF_MISC, LANES)
    xg = _dot(misc.astype(BF16), wg2_ref[...]) + bg_ref[...]
    la_ref[...] = (jnp.minimum(xg, 0.0) - jnp.log1p(jnp.exp(-jnp.abs(xg)))) * (1.0 / GLA_TAU)
    gq_ref[...] = (seg(OFF_GQ, GLA_DK) * (GLA_HK ** -0.5)).astype(BF16)
    gk_ref[...] = seg(OFF_GK, GLA_DK).astype(BF16)
    gv_ref[...] = seg(OFF_GV, GLA_DV).astype(BF16)
    og_ref[...] = seg(OFF_OG, GLA_DV).astype(BF16)

    consts = _gla_constants()
    slab_rows = par * GLA_CHUNK
    scan = []
    for sl in range(tm // slab_rows):
        scan += _gla_slab_stages(sl * slab_rows, par, consts, gq_ref, gk_ref, gv_ref, la_ref, og_ref,
                                 gnorm_ref[...], ob_ref, state_ref, kpad_ref.at[sl], lapad_ref.at[sl])

    lane = lax.broadcasted_iota(jnp.int32, (1, LANES), 1)
    in_rope = (lane >= ROPE_LO) & (lane < ROPE_LO + MLA_ROPE)
    t = {}

    def latents():
        t["cq"] = seg(OFF_CQ, MLA_Q_RANK)
        t["ckv"] = seg(OFF_CKV, MLA_KV_RANK)

    def rope_tables():
        ang = invf_ref[...] * pos_ref[0]
        cs = jnp.concatenate([jnp.cos(ang), jnp.sin(ang)], axis=0)
        cs_hi = cs.astype(BF16)
        cs_lo = (cs - cs_hi.astype(F32)).astype(BF16)
        tabs = _dot_tn(jnp.concatenate([cs_hi, cs_lo], axis=0), place_ref[...])
        t["cos"] = tabs[:, :LANES] + jnp.where(in_rope, 0.0, 1.0)
        t["sin_x1"] = tabs[:, LANES:2 * LANES]
        t["sin_x2"] = tabs[:, 2 * LANES:]

    def gate_a():
        sa_ref[...] = jax.nn.sigmoid(seg(OFF_GA, D_MODEL)).astype(BF16)

    def gate_b():
        sb_ref[...] = jax.nn.sigmoid(seg(OFF_GB, D_MODEL)).astype(BF16)

    def latent_norms():
        t["cqn"] = _rms(t["cq"], ncq_ref[...]).astype(BF16)
        t["ckvn"] = _rms(t["ckv"], nckv_ref[...]).astype(BF16)

    def up_q():
        t["q"] = _dot(t["cqn"], wuq_ref[...]) * (MLA_QK ** -0.5 * LOG2E)

    def up_kv():
        t["k_nope"] = _dot(t["ckvn"], wk_ref[...])
        for c in range(tm // tkv):
            vt = _dot_nt(wvt_ref[...], t["ckvn"][c * tkv:(c + 1) * tkv, :]).astype(BF16)
            for h in range(MLA_HEADS):
                vt_ref[c, h * MLA_VA:h * MLA_VA + MLA_V, :] = vt[h * MLA_V:(h + 1) * MLA_V, :]
                vt_ref[c, h * MLA_VA + MLA_V:(h + 1) * MLA_VA, :] = jnp.ones((MLA_VA - MLA_V, tkv), BF16)

    def rope_store():
        def rope(a):
            return (a * t["cos"] + pltpu.roll(a, LANES - ROPE_HALF, axis=1) * t["sin_x1"]
                    + pltpu.roll(a, ROPE_HALF, axis=1) * t["sin_x2"])

        k_rope = rope(jnp.where(in_rope, misc, 0.0))
        for h in range(MLA_HEADS):
            sl = slice(h * HEAD_PAD, (h + 1) * HEAD_PAD)
            q_ref[:, sl] = rope(t["q"][:, sl]).astype(BF16)
            k_ref[:, sl] = (t["k_nope"][:, sl] + k_rope).astype(BF16)

    rest = [latents, gate_a, rope_tables, latent_norms, gate_b, up_q, up_kv, rope_store]
    for n in range(max(len(scan), len(rest))):
        if n < len(scan):
            scan[n]()
        if n < len(rest):
            rest[n]()


def _proj_gla_call(x2, pos3, lnmix, winp, ncq, wuqp, nckv, wkp, wvt, wg2p, bg, invf, place, gnorm, seq, tm, tkv):
    t = x2.shape[0]
    par = math.gcd(tm // GLA_CHUNK, GLA_PAR)
    nslab = tm // (par * GLA_CHUNK)
    row = lambda w: pl.BlockSpec((tm, w), lambda i: (i, 0))
    consts = (lnmix, winp, ncq, wuqp, nckv, wkp, wvt, wg2p, bg, invf, place, gnorm)
    out_widths = (MLA_HEADS * HEAD_PAD, MLA_HEADS * HEAD_PAD, None, D_MODEL, D_MODEL, GLA_DV)
    vt_spec = pl.BlockSpec((tm // tkv, MLA_HEADS * MLA_VA, tkv), lambda i: (i, 0, 0))
    vt_shape = jax.ShapeDtypeStruct((t // tkv, MLA_HEADS * MLA_VA, tkv), BF16)
    pad_rows = GLA_SUB + par * GLA_CHUNK
    return pl.pallas_call(
        functools.partial(_proj_gla_kernel, tkv=tkv, steps_per_seq=seq // tm, par=par),
        grid=(t // tm,),
        in_specs=[row(D_MODEL), pl.BlockSpec((1, 1, tm), lambda i: (i, 0, 0))] + [_single_spec(c) for c in consts],
        out_specs=[vt_spec if w is None else row(w) for w in out_widths],
        out_shape=[vt_shape if w is None else jax.ShapeDtypeStruct((t, w), BF16) for w in out_widths],
        scratch_shapes=[pltpu.VMEM((tm, GLA_DK), BF16), pltpu.VMEM((tm, GLA_DK), BF16),
                        pltpu.VMEM((tm, GLA_DV), BF16), pltpu.VMEM((tm, GLA_DK), F32),
                        pltpu.VMEM((tm, GLA_DV), BF16), pltpu.VMEM((GLA_HV, GLA_DK), F32),
                        pltpu.VMEM((nslab, pad_rows, GLA_DK), F32), pltpu.VMEM((nslab, pad_rows, GLA_DK), F32)],
        compiler_params=pltpu.CompilerParams(dimension_semantics=("arbitrary",), vmem_limit_bytes=VMEM_LIMIT),
        name="proj_gla",
    )(x2, pos3, *consts)


def _attn_kernel(qi_ref, kj_ref, q_ref, k_ref, vt_ref, o_ref, acc_ref, ot_ref, m_ref, s0_ref, s1_ref, smax0_ref,
                 smax1_ref, *, tq, npairs):
    nq = q_ref.shape[0] // tq
    key = lax.broadcasted_iota(jnp.int32, (tq, tq), 0)
    qry = lax.broadcasted_iota(jnp.int32, (tq, tq), 1)
    causal = key <= qry
    m_ref[...] = jnp.full(m_ref.shape, -1e30, F32)
    acc_ref[...] = jnp.zeros(acc_ref.shape, F32)
    slots = ((s0_ref, smax0_ref), (s1_ref, smax1_ref))

    def scores(n, slot, heads):
        s_ref, smax_ref = slots[slot]
        qi, kj = qi_ref[n], kj_ref[n]
        qrows = pl.ds(pl.multiple_of(qi * tq, tq), tq)
        krows = pl.ds(pl.multiple_of(kj * tq, tq), tq)
        keep = causal | (kj < qi)
        for h in heads:
            hq = slice(h * HEAD_PAD, (h + 1) * HEAD_PAD)
            s = _dot_nt(k_ref[krows, hq], q_ref[qrows, hq])
            s = jnp.where(keep, s, -1e30)
            s_ref[h] = s
            smax_ref[h:h + 1, :] = jnp.max(s, axis=0, keepdims=True)

    def accumulate(n, slot, heads):
        s_ref, smax_ref = slots[slot]
        qi, kj = qi_ref[n], kj_ref[n]
        for h in heads:
            hv = slice(h * MLA_VA, (h + 1) * MLA_VA)
            m = m_ref[qi, h:h + 1, :]
            m_new = jnp.maximum(m, smax_ref[h:h + 1, :])
            alpha = jnp.exp2(m - m_new)
            p = jnp.exp2(s_ref[h] - m_new)
            m_ref[qi, h:h + 1, :] = m_new
            acc_ref[qi, hv, :] = alpha * acc_ref[qi, hv, :] + _dot(vt_ref[kj, hv, :], p.astype(BF16))

    all_heads = range(MLA_HEADS)

    def overlapped(score_pair, score_slot, acc_pair, acc_slot):
        for g in range(0, MLA_HEADS, ATTN_GROUP):
            heads = range(g, g + ATTN_GROUP)
            scores(score_pair, score_slot, heads)
            accumulate(acc_pair, acc_slot, heads)

    scores(0, 0, all_heads)

    def two_pairs(t, carry):
        overlapped(2 * t + 1, 1, 2 * t, 0)
        overlapped(2 * t + 2, 0, 2 * t + 1, 1)
        return carry

    lax.fori_loop(0, (npairs - 1) // 2, two_pairs, 0)
    if npairs % 2 == 0:
        overlapped(npairs - 1, 1, npairs - 2, 0)
    accumulate(npairs - 1, (npairs - 1) % 2, all_heads)

    for qt in range(nq):
        for h in range(MLA_HEADS):
            lo = h * MLA_VA
            ot_ref[h * MLA_V:(h + 1) * MLA_V, :] = (acc_ref[qt, lo:lo + MLA_V, :]
                                                    / acc_ref[qt, lo + MLA_V:lo + MLA_V + 1, :])
        o_ref[qt * tq:(qt + 1) * tq, :] = ot_ref[...].T.astype(BF16)


def _attn_call(q, k, vt, batch, seq, tq):
    nq = seq // tq
    pairs = [(i, j) for i in range(nq) for j in range(i + 1)]
    qi = jnp.asarray([p[0] for p in pairs], jnp.int32)
    kj = jnp.asarray([p[1] for p in pairs], jnp.int32)
    grid_spec = pltpu.PrefetchScalarGridSpec(
        num_scalar_prefetch=2,
        grid=(batch,),
        in_specs=[pl.BlockSpec((seq, MLA_HEADS * HEAD_PAD), lambda b, *_: (b, 0)),
                  pl.BlockSpec((seq, MLA_HEADS * HEAD_PAD), lambda b, *_: (b, 0)),
                  pl.BlockSpec((nq, MLA_HEADS * MLA_VA, tq), lambda b, *_: (b, 0, 0))],
        out_specs=pl.BlockSpec((seq, MLA_WIDTH), lambda b, *_: (b, 0)),
        scratch_shapes=[pltpu.VMEM((nq, MLA_HEADS * MLA_VA, tq), F32), pltpu.VMEM((MLA_WIDTH, tq), F32),
                        pltpu.VMEM((nq, MLA_HEADS, tq), F32),
                        pltpu.VMEM((MLA_HEADS, tq, tq), F32), pltpu.VMEM((MLA_HEADS, tq, tq), F32),
                        pltpu.VMEM((MLA_HEADS, tq), F32), pltpu.VMEM((MLA_HEADS, tq), F32)])
    return pl.pallas_call(
        functools.partial(_attn_kernel, tq=tq, npairs=len(pairs)),
        grid_spec=grid_spec,
        out_shape=jax.ShapeDtypeStruct((batch * seq, MLA_WIDTH), BF16),
        compiler_params=pltpu.CompilerParams(dimension_semantics=("parallel",), vmem_limit_bytes=VMEM_LIMIT),
        name="mla_attn",
    )(qi, kj, q, k, vt)


def _post_kernel(x_ref, oa_ref, ob_ref, sa_ref, sb_ref, woa_ref, wob_ref, wout_ref, lnffn_ref,
                 wg_ref, wu_ref, wd_ref, fnorm_ref, out_ref):
    ya = _dot(oa_ref[...], woa_ref[...])
    yb = _dot(ob_ref[...], wob_ref[...])
    mix = sa_ref[...].astype(F32) * ya + sb_ref[...].astype(F32) * yb
    h = x_ref[...] + _dot(mix.astype(BF16), wout_ref[...])
    u = _rms(h, lnffn_ref[...]).astype(BF16)
    g = _dot(u, wg_ref[...])
    act = (g * jax.nn.sigmoid(g) * _dot(u, wu_ref[...])).astype(BF16)
    h = h + _dot(act, wd_ref[...])
    out_ref[...] = _rms(h, fnorm_ref[...])


def _post_call(x2, oa, ob, sa, sb, woa, wob, wout, lnffn, wg, wu, wd, fnorm, tm):
    t = x2.shape[0]
    row = lambda w: pl.BlockSpec((tm, w), lambda i: (i, 0))
    consts = (woa, wob, wout, lnffn, wg, wu, wd, fnorm)
    return pl.pallas_call(
        _post_kernel,
        grid=(t // tm,),
        in_specs=[row(D_MODEL), row(MLA_WIDTH), row(GLA_DV), row(D_MODEL), row(D_MODEL)]
                 + [_single_spec(c) for c in consts],
        out_specs=row(D_MODEL),
        out_shape=jax.ShapeDtypeStruct((t, D_MODEL), F32),
        compiler_params=pltpu.CompilerParams(dimension_semantics=("parallel",), vmem_limit_bytes=VMEM_LIMIT),
        name="post",
    )(x2, oa, ob, sa, sb, *consts)


def _repack_kernel(wt_ref, o_ref):
    offs = [0]
    for s in IN_SIZES:
        offs.append(offs[-1] + s)
    cols = wt_ref.shape[1]
    dst = 0
    for idx in (0, 1, 3, 4, 5, 7, 8, 9):
        n = IN_SIZES[idx]
        o_ref[dst:dst + n, :] = wt_ref[offs[idx]:offs[idx + 1], :].astype(BF16)
        dst += n
    o_ref[dst:dst + ROPE_LO, :] = jnp.zeros((ROPE_LO, cols), BF16)
    o_ref[dst + ROPE_LO:dst + GLR_LO, :] = wt_ref[offs[2]:offs[3], :].astype(BF16)
    o_ref[dst + GLR_LO:dst + GLR_LO + GLA_RANK, :] = wt_ref[offs[6]:offs[7], :].astype(BF16)
    o_ref[dst + GLR_LO + GLA_RANK:dst + LANES, :] = jnp.zeros((LANES - GLR_LO - GLA_RANK, cols), BF16)


def _repack_call(w_in_t):
    n, d = w_in_t.shape
    cols = _tile(d, REPACK_COLS)
    return pl.pallas_call(
        _repack_kernel,
        grid=(d // cols,),
        in_specs=[pl.BlockSpec((n, cols), lambda i: (0, i))],
        out_specs=pl.BlockSpec((D_IN_PACKED, cols), lambda i: (0, i)),
        out_shape=jax.ShapeDtypeStruct((D_IN_PACKED, d), BF16),
        compiler_params=pltpu.CompilerParams(dimension_semantics=("parallel",), vmem_limit_bytes=VMEM_LIMIT),
        name="repack_w_in",
    )(w_in_t)


def _pack_weights(w_in, mla_w_uq, mla_w_ukv, gla_w_gate2):
    winp = _repack_call(w_in.T)
    wuq = mla_w_uq.reshape(MLA_Q_RANK, MLA_HEADS, MLA_QK)
    wuqp = jnp.pad(wuq, ((0, 0), (0, 0), (0, HEAD_PAD - MLA_QK))).reshape(MLA_Q_RANK, MLA_HEADS * HEAD_PAD)
    wukv = mla_w_ukv.reshape(MLA_KV_RANK, MLA_HEADS, MLA_NOPE + MLA_V)
    wkp = jnp.pad(wukv[:, :, :MLA_NOPE], ((0, 0), (0, 0), (0, HEAD_PAD - MLA_NOPE)))
    wkp = wkp.reshape(MLA_KV_RANK, MLA_HEADS * HEAD_PAD)
    wvt = wukv[:, :, MLA_NOPE:].reshape(MLA_KV_RANK, MLA_WIDTH).T
    wg2p = jnp.pad(gla_w_gate2, ((GLR_LO, LANES - GLR_LO - GLA_RANK), (0, 0)))
    return winp, wuqp.astype(BF16), wkp.astype(BF16), wvt.astype(BF16), wg2p.astype(BF16)


def _rope_inv_freq():
    half = ROPE_HALF
    inv = 1.0 / (ROPE_THETA ** (jnp.arange(half, dtype=F32) / half))
    return inv.reshape(half, 1)


def _rope_placement():
    place = np.zeros((4 * ROPE_HALF, 3 * LANES), np.float32)
    for piece in range(2):
        for i in range(ROPE_HALF):
            cos_row = 2 * piece * ROPE_HALF + i
            sin_row = cos_row + ROPE_HALF
            place[cos_row, ROPE_LO + i] = 1.0
            place[cos_row, ROPE_LO + ROPE_HALF + i] = 1.0
            place[sin_row, LANES + ROPE_LO + i] = -1.0
            place[sin_row, 2 * LANES + ROPE_LO + ROPE_HALF + i] = 1.0
    return jnp.asarray(place, BF16)


def _tile(n, pref):
    while n % pref:
        pref //= 2
    return pref


def kernel(x, positions, ln_mix, w_in, mla_norm_cq, mla_w_uq, mla_norm_ckv, mla_w_ukv, mla_w_o, gla_w_gate2,
           gla_b_gate, gla_norm, gla_w_o, w_out, ln_ffn, ffn_w_gate, ffn_w_up, ffn_w_down, final_norm):
    batch, seq, d = x.shape
    assert d == D_MODEL and w_in.shape[0] == 1 and seq % GLA_CHUNK == 0
    t = batch * seq
    x2 = x.reshape(t, d)
    winp, wuqp, wkp, wvt, wg2p = _pack_weights(w_in[0], mla_w_uq[0], mla_w_ukv[0], gla_w_gate2[0])
    r = lambda a: a.reshape(1, -1)

    tq = _tile(seq, ATTN_TILE)
    tm = max(tq, _tile(seq, PROJ_TILE))
    pos3 = positions.reshape(t // tm, 1, tm).astype(F32)
    q, k, vt, sa, sb, ob = _proj_gla_call(
        x2, pos3, r(ln_mix[0]), winp, r(mla_norm_cq[0]), wuqp, r(mla_norm_ckv[0]), wkp, wvt, wg2p,
        r(gla_b_gate[0]), _rope_inv_freq(), _rope_placement(), r(gla_norm[0]), seq, tm, tq)
    oa = _attn_call(q, k, vt, batch, seq, tq)
    out = _post_call(x2, oa, ob, sa, sb, mla_w_o[0].astype(BF16), gla_w_o[0].astype(BF16),
                     w_out[0].astype(BF16), r(ln_ffn[0]), ffn_w_gate[0].astype(BF16),
                     ffn_w_up[0].astype(BF16), ffn_w_down[0].astype(BF16), r(final_norm), _tile(t, POST_TILE))
    return out.reshape(batch, seq, d)
```

```python
import functools
import math

import jax
import jax.numpy as jnp
import numpy as np
from jax import lax
from jax.experimental import pallas as pl
from jax.experimental.pallas import tpu as pltpu

F32 = jnp.float32
BF16 = jnp.bfloat16

D_MODEL = 1024
MLA_HEADS = 8
MLA_NOPE = 64
MLA_ROPE = 32
MLA_QK = MLA_NOPE + MLA_ROPE
MLA_V = 64
MLA_WIDTH = MLA_HEADS * MLA_V
MLA_Q_RANK = 384
MLA_KV_RANK = 256
ROPE_THETA = 10000.0
GLA_HEADS = 4
GLA_DV = 512
GLA_DK = 256
GLA_HK = GLA_DK // GLA_HEADS
GLA_HV = GLA_DV // GLA_HEADS
GLA_RANK = 16
GLA_TAU = 16.0
D_FF = 2816
EPS = 1e-6
LOG2E = 1.4426950408889634
IN_SIZES = (MLA_Q_RANK, MLA_KV_RANK, MLA_ROPE, GLA_DK, GLA_DK, GLA_DV, GLA_RANK, GLA_DV, D_MODEL, D_MODEL)

LANES = 128
HEAD_PAD = LANES
ROPE_LO = MLA_NOPE
ROPE_HALF = MLA_ROPE // 2
GLR_LO = ROPE_LO + MLA_ROPE

OFF_CQ = 0
OFF_CKV = OFF_CQ + MLA_Q_RANK
OFF_GQ = OFF_CKV + MLA_KV_RANK
OFF_GK = OFF_GQ + GLA_DK
OFF_GV = OFF_GK + GLA_DK
OFF_OG = OFF_GV + GLA_DV
OFF_GA = OFF_OG + GLA_DV
OFF_GB = OFF_GA + D_MODEL
OFF_MISC = OFF_GB + D_MODEL
D_IN_PACKED = OFF_MISC + LANES

REPACK_COLS = 256
PROJ_TILE = 512
POST_TILE = 512
ATTN_TILE = 256
ATTN_GROUP = 1
MLA_VA = MLA_V + 16
GLA_CHUNK = 64
GLA_SUB = 16
GLA_PAR = 4
VMEM_LIMIT = 56 * 1024 * 1024


def _dot(a, b):
    return jnp.dot(a, b, preferred_element_type=F32)


def _dot_nt(a, b):
    return lax.dot_general(a, b, (((1,), (1,)), ((), ())), preferred_element_type=F32)


def _dot_tn(a, b):
    return lax.dot_general(a, b, (((0,), (0,)), ((), ())), preferred_element_type=F32)


def _rms(x, w):
    return x * lax.rsqrt(jnp.mean(x * x, axis=-1, keepdims=True) + EPS) * w


def _single_spec(a):
    return pl.BlockSpec(a.shape, lambda *_: (0,) * a.ndim, pipeline_mode=pl.Buffered(1))


def _gla_constants():
    c_, s_ = GLA_CHUNK, GLA_SUB
    ri = lax.broadcasted_iota(jnp.int32, (c_, c_), 0)
    ci = lax.broadcasted_iota(jnp.int32, (c_, c_), 1)
    rl = lax.broadcasted_iota(jnp.int32, (c_, GLA_DK), 0)
    ll = lax.broadcasted_iota(jnp.int32, (c_, GLA_DK), 1)
    row_minus_col = rl - ll % GLA_HK
    di = lax.broadcasted_iota(jnp.int32, (GLA_DK, GLA_DK), 0) // GLA_HK
    dj = lax.broadcasted_iota(jnp.int32, (GLA_DK, GLA_DK), 1) // GLA_HK
    return dict(
        tri=jnp.where((ri // s_ == ci // s_) & (ci <= ri), 1.0, 0.0).astype(BF16),
        diag_key=jnp.where((row_minus_col >= 0) & (rl % s_ >= row_minus_col), row_minus_col, -1),
        head_of_lane=ll // GLA_HK,
        odd_head_lane=(lax.broadcasted_iota(jnp.int32, (c_, 2 * GLA_HK), 1) // GLA_HK) == 1,
        head_sum=jnp.where(di == dj, 1.0, 0.0).astype(BF16))


def _gla_slab_stages(base, par, consts, gq_ref, gk_ref, gv_ref, la_ref, og_ref, norm_w, o_ref, state_ref,
                     kpad_ref, lapad_ref):
    c_, s_ = GLA_CHUNK, GLA_SUB
    nsub = c_ // s_
    pair = 2 * GLA_HK
    tri, diag_key, head_of_lane = consts["tri"], consts["diag_key"], consts["head_of_lane"]
    odd_head_lane, head_sum = consts["odd_head_lane"], consts["head_sum"]
    chunks = [dict(rows=slice(base + c * c_, base + (c + 1) * c_), off=s_ + c * c_) for c in range(par)]

    def blocks(fn):
        return jnp.concatenate([fn(b) for b in range(nsub)], axis=0)

    def sub(t, b):
        return t[b * s_:(b + 1) * s_, :]

    def cumsums():
        kpad_ref[0:s_, :] = jnp.zeros((s_, GLA_DK), F32)
        lapad_ref[0:s_, :] = jnp.zeros((s_, GLA_DK), F32)
        kpad_ref[s_:, :] = gk_ref[base:base + par * c_, :].astype(F32)
        lapad_ref[s_:, :] = la_ref[base:base + par * c_, :]
        for ch in chunks:
            la = la_ref[ch["rows"], :]
            hi = la.astype(BF16)
            r1 = la - hi.astype(F32)
            mid = r1.astype(BF16)
            lo = (r1 - mid.astype(F32)).astype(BF16)
            ch["l16"] = _dot(tri, hi) + _dot(tri, mid) + _dot(tri, lo)

    def operands():
        for ch in chunks:
            q = gq_ref[ch["rows"], :].astype(F32)
            k = gk_ref[ch["rows"], :].astype(F32)
            l16 = ch["l16"]
            tot = [l16[(b + 1) * s_ - 1:(b + 1) * s_, :] for b in range(nsub)]
            q16 = q * jnp.exp(l16)
            k16 = k * jnp.exp(blocks(lambda b: jnp.broadcast_to(tot[b], (s_, GLA_DK))) - l16)
            zero = jnp.zeros((s_, GLA_DK), F32)
            e = [jnp.exp(t) for t in tot]
            pre = [None, e[0], e[0] * e[1], e[0] * e[1] * e[2]]
            post = [e[1] * e[2] * e[3], e[2] * e[3], e[3], None]
            ch["qd"] = blocks(lambda b: sub(q16, b) if pre[b] is None else sub(q16, b) * pre[b]).astype(BF16)
            kd = blocks(lambda b: sub(k16, b) if post[b] is None else sub(k16, b) * post[b]).astype(BF16)
            ch["decay"] = jnp.exp(tot[0] + tot[1] + tot[2] + tot[3])
            v = gv_ref[ch["rows"], :]
            ch["upd"] = sum(_dot_tn(v[:, h * GLA_HV:(h + 1) * GLA_HV], jnp.where(head_of_lane == h, kd, 0.0))
                            for h in range(GLA_HEADS))
            ch["q_lv"] = [t.astype(BF16) for t in (
                jnp.concatenate([zero, zero, sub(q16, 2), sub(q16, 3) * e[2]], axis=0),
                jnp.concatenate([zero, sub(q16, 1), zero, zero], axis=0),
                jnp.concatenate([zero, zero, zero, sub(q16, 3)], axis=0))]
            ch["k_lv"] = [t.astype(BF16) for t in (
                jnp.concatenate([sub(k16, 0) * e[1], sub(k16, 1), zero, zero], axis=0),
                jnp.concatenate([sub(k16, 0), zero, zero, zero], axis=0),
                jnp.concatenate([zero, zero, sub(k16, 2), zero], axis=0))]
            decay = jnp.zeros((c_, GLA_DK), F32)
            prods = []
            for d in range(s_):
                if d > 0:
                    decay = decay + lapad_ref[ch["off"] - d + 1:ch["off"] - d + 1 + c_, :]
                k_sh = kpad_ref[ch["off"] - d:ch["off"] - d + c_, :]
                prods.append((q * k_sh * jnp.exp(decay)).astype(BF16))
            ch["summed"] = _dot(jnp.concatenate(prods, axis=0), head_sum)

    def pair_scores():
        for ch in chunks:
            ch["attn_off"] = []
            for h in range(GLA_HEADS):
                lanes = slice((h // 2) * pair, (h // 2 + 1) * pair)
                mine = odd_head_lane if h % 2 else ~odd_head_lane
                qx = jnp.concatenate([jnp.where(mine, t[:, lanes], 0.0) for t in ch["q_lv"]], axis=1)
                kx = jnp.concatenate([t[:, lanes] for t in ch["k_lv"]], axis=1)
                ch["attn_off"].append(_dot_nt(qx, kx))

    def outputs():
        state = state_ref[...]
        for ch in chunks:
            ch["state_b"] = state.astype(BF16)
            state = state * ch["decay"] + ch["upd"]
        state_ref[...] = state
        for ch in chunks:
            attn_diag = jnp.zeros((c_, GLA_DK), F32)
            for d in range(s_):
                attn_diag = jnp.where(diag_key == d, ch["summed"][d * c_:(d + 1) * c_, :], attn_diag)
            v = gv_ref[ch["rows"], :]
            for h in range(GLA_HEADS):
                hv = slice(h * GLA_HV, (h + 1) * GLA_HV)
                attn = attn_diag[:, h * GLA_HK:(h + 1) * GLA_HK] + ch["attn_off"][h]
                o = _dot(attn.astype(BF16), v[:, hv])
                o = o + _dot_nt(jnp.where(head_of_lane == h, ch["qd"], 0.0), ch["state_b"])
                o = _rms(o, norm_w)
                g = og_ref[ch["rows"], hv].astype(F32)
                o_ref[ch["rows"], hv] = (o * (g * jax.nn.sigmoid(g))).astype(BF16)

    return [cumsums, operands, pair_scores, outputs]


def _proj_gla_kernel(x_ref, pos_ref, lnmix_ref, win_ref, ncq_ref, wuq_ref, nckv_ref, wk_ref, wvt_ref,
                     wg2_ref, bg_ref, invf_ref, place_ref, gnorm_ref,
                     q_ref, k_ref, vt_ref, sa_ref, sb_ref, ob_ref,
                     gq_ref, gk_ref, gv_ref, la_ref, og_ref, state_ref, kpad_ref, lapad_ref,
                     *, tkv, steps_per_seq, par):
    tm = x_ref.shape[0]

    @pl.when(pl.program_id(0) % steps_per_seq == 0)
    def _():
        state_ref[...] = jnp.zeros_like(state_ref)

    ub = _rms(x_ref[...], lnmix_ref[...]).astype(BF16)

    def seg(lo, width):
        return _dot_nt(ub, win_ref[lo:lo + width, :])

    misc = seg(OFF_MISC, LANES)
    xg = _dot(misc.astype(BF16), wg2_ref[...]) + bg_ref[...]
    la_ref[...] = (jnp.minimum(xg, 0.0) - jnp.log1p(jnp.exp(-jnp.abs(xg)))) * (1.0 / GLA_TAU)
    gq_ref[...] = (seg(OFF_GQ, GLA_DK) * (GLA_HK ** -0.5)).astype(BF16)
    gk_ref[...] = seg(OFF_GK, GLA_DK).astype(BF16)
    gv_ref[...] = seg(OFF_GV, GLA_DV).astype(BF16)
    og_ref[...] = seg(OFF_OG, GLA_DV).astype(BF16)

    consts = _gla_constants()
    slab_rows = par * GLA_CHUNK
    scan = []
    for sl in range(tm // slab_rows):
        scan += _gla_slab_stages(sl * slab_rows, par, consts, gq_ref, gk_ref, gv_ref, la_ref, og_ref,
                                 gnorm_ref[...], ob_ref, state_ref, kpad_ref.at[sl], lapad_ref.at[sl])

    lane = lax.broadcasted_iota(jnp.int32, (1, LANES), 1)
    in_rope = (lane >= ROPE_LO) & (lane < ROPE_LO + MLA_ROPE)
    t = {}

    def latents():
        t["cq"] = seg(OFF_CQ, MLA_Q_RANK)
        t["ckv"] = seg(OFF_CKV, MLA_KV_RANK)

    def rope_tables():
        ang = invf_ref[...] * pos_ref[0]
        cs = jnp.concatenate([jnp.cos(ang), jnp.sin(ang)], axis=0)
        cs_hi = cs.astype(BF16)
        cs_lo = (cs - cs_hi.astype(F32)).astype(BF16)
        tabs = _dot_tn(jnp.concatenate([cs_hi, cs_lo], axis=0), place_ref[...])
        t["cos"] = tabs[:, :LANES] + jnp.where(in_rope, 0.0, 1.0)
        t["sin_x1"] = tabs[:, LANES:2 * LANES]
        t["sin_x2"] = tabs[:, 2 * LANES:]

    def gate_a():
        sa_ref[...] = jax.nn.sigmoid(seg(OFF_GA, D_MODEL)).astype(BF16)

    def gate_b():
        sb_ref[...] = jax.nn.sigmoid(seg(OFF_GB, D_MODEL)).astype(BF16)

    def latent_norms():
        t["cqn"] = _rms(t["cq"], ncq_ref[...]).astype(BF16)
        t["ckvn"] = _rms(t["ckv"], nckv_ref[...]).astype(BF16)

    def up_q():
        t["q"] = _dot(t["cqn"], wuq_ref[...]) * (MLA_QK ** -0.5 * LOG2E)

    def up_kv():
        t["k_nope"] = _dot(t["ckvn"], wk_ref[...])
        for c in range(tm // tkv):
            vt = _dot_nt(wvt_ref[...], t["ckvn"][c * tkv:(c + 1) * tkv, :]).astype(BF16)
            for h in range(MLA_HEADS):
                vt_ref[c, h * MLA_VA:h * MLA_VA + MLA_V, :] = vt[h * MLA_V:(h + 1) * MLA_V, :]
                vt_ref[c, h * MLA_VA + MLA_V:(h + 1) * MLA_VA, :] = jnp.ones((MLA_VA - MLA_V, tkv), BF16)

    def rope_store():
        def rope(a):
            return (a * t["cos"] + pltpu.roll(a, LANES - ROPE_HALF, axis=1) * t["sin_x1"]
                    + pltpu.roll(a, ROPE_HALF, axis=1) * t["sin_x2"])

        k_rope = rope(jnp.where(in_rope, misc, 0.0))
        for h in range(MLA_HEADS):
            sl = slice(h * HEAD_PAD, (h + 1) * HEAD_PAD)
            q_ref[:, sl] = rope(t["q"][:, sl]).astype(BF16)
            k_ref[:, sl] = (t["k_nope"][:, sl] + k_rope).astype(BF16)

    rest = [latents, gate_a, rope_tables, latent_norms, gate_b, up_q, up_kv, rope_store]
    for n in range(max(len(scan), len(rest))):
        if n < len(scan):
            scan[n]()
        if n < len(rest):
            rest[n]()


def _proj_gla_call(x2, pos3, lnmix, winp, ncq, wuqp, nckv, wkp, wvt, wg2p, bg, invf, place, gnorm, seq, tm, tkv):
    t = x2.shape[0]
    par = math.gcd(tm // GLA_CHUNK, GLA_PAR)
    nslab = tm // (par * GLA_CHUNK)
    row = lambda w: pl.BlockSpec((tm, w), lambda i: (i, 0))
    consts = (lnmix, winp, ncq, wuqp, nckv, wkp, wvt, wg2p, bg, invf, place, gnorm)
    out_widths = (MLA_HEADS * HEAD_PAD, MLA_HEADS * HEAD_PAD, None, D_MODEL, D_MODEL, GLA_DV)
    vt_spec = pl.BlockSpec((tm // tkv, MLA_HEADS * MLA_VA, tkv), lambda i: (i, 0, 0))
    vt_shape = jax.ShapeDtypeStruct((t // tkv, MLA_HEADS * MLA_VA, tkv), BF16)
    pad_rows = GLA_SUB + par * GLA_CHUNK
    return pl.pallas_call(
        functools.partial(_proj_gla_kernel, tkv=tkv, steps_per_seq=seq // tm, par=par),
        grid=(t // tm,),
        in_specs=[row(D_MODEL), pl.BlockSpec((1, 1, tm), lambda i: (i, 0, 0))] + [_single_spec(c) for c in consts],
        out_specs=[vt_spec if w is None else row(w) for w in out_widths],
        out_shape=[vt_shape if w is None else jax.ShapeDtypeStruct((t, w), BF16) for w in out_widths],
        scratch_shapes=[pltpu.VMEM((tm, GLA_DK), BF16), pltpu.VMEM((tm, GLA_DK), BF16),
                        pltpu.VMEM((tm, GLA_DV), BF16), pltpu.VMEM((tm, GLA_DK), F32),
                        pltpu.VMEM((tm, GLA_DV), BF16), pltpu.VMEM((GLA_HV, GLA_DK), F32),
                        pltpu.VMEM((nslab, pad_rows, GLA_DK), F32), pltpu.VMEM((nslab, pad_rows, GLA_DK), F32)],
        compiler_params=pltpu.CompilerParams(dimension_semantics=("arbitrary",), vmem_limit_bytes=VMEM_LIMIT),
        name="proj_gla",
    )(x2, pos3, *consts)


def _attn_kernel(qi_ref, kj_ref, q_ref, k_ref, vt_ref, o_ref, acc_ref, ot_ref, m_ref, s0_ref, s1_ref, smax0_ref,
                 smax1_ref, *, tq, npairs):
    nq = q_ref.shape[0] // tq
    key = lax.broadcasted_iota(jnp.int32, (tq, tq), 0)
    qry = lax.broadcasted_iota(jnp.int32, (tq, tq), 1)
    causal = key <= qry
    slots = ((s0_ref, smax0_ref), (s1_ref, smax1_ref))

    def scores(n, slot, heads, masked):
        s_ref, smax_ref = slots[slot]
        qi, kj = qi_ref[n], kj_ref[n]
        qrows = pl.ds(pl.multiple_of(qi * tq, tq), tq)
        krows = pl.ds(pl.multiple_of(kj * tq, tq), tq)
        for h in heads:
            hq = slice(h * HEAD_PAD, (h + 1) * HEAD_PAD)
            s = _dot_nt(k_ref[krows, hq], q_ref[qrows, hq])
            if masked:
                s = jnp.where(causal, s, -1e30)
            s_ref[h] = s
            smax_ref[h:h + 1, :] = jnp.max(s, axis=0, keepdims=True)

    def accumulate(n, slot, heads, first):
        s_ref, smax_ref = slots[slot]
        qi, kj = qi_ref[n], kj_ref[n]
        for h in heads:
            hv = slice(h * MLA_VA, (h + 1) * MLA_VA)
            m_new = smax_ref[h:h + 1, :]
            if not first:
                m = m_ref[qi, h:h + 1, :]
                m_new = jnp.maximum(m, m_new)
                alpha = jnp.exp2(m - m_new)
            p = jnp.exp2(s_ref[h] - m_new)
            m_ref[qi, h:h + 1, :] = m_new
            pv = _dot(vt_ref[kj, hv, :], p.astype(BF16))
            acc_ref[qi, hv, :] = pv if first else alpha * acc_ref[qi, hv, :] + pv

    all_heads = range(MLA_HEADS)

    def step(k, parity, diag_next, diag_now):
        for g in range(0, MLA_HEADS, ATTN_GROUP):
            heads = range(g, g + ATTN_GROUP)
            scores(k + 1, 1 - parity, heads, diag_next)
            accumulate(k, parity, heads, diag_now)

    def steps(lo, hi, diag_next, diag_now):
        if (hi - lo) % 2:
            step(lo, lo % 2, diag_next, diag_now)
            lo += 1
        if hi > lo:
            def two_steps(t, carry):
                step(lo + 2 * t, lo % 2, diag_next, diag_now)
                step(lo + 2 * t + 1, 1 - lo % 2, diag_next, diag_now)
                return carry

            lax.fori_loop(0, (hi - lo) // 2, two_steps, 0)

    scores(0, 0, all_heads, True)
    steps(0, nq - 1, True, True)
    if npairs > nq:
        steps(nq - 1, nq, False, True)
        steps(nq, npairs - 1, False, False)
    accumulate(npairs - 1, (npairs - 1) % 2, all_heads, npairs == nq)

    for qt in range(nq):
        for h in range(MLA_HEADS):
            lo = h * MLA_VA
            ot_ref[h * MLA_V:(h + 1) * MLA_V, :] = (acc_ref[qt, lo:lo + MLA_V, :]
                                                    / acc_ref[qt, lo + MLA_V:lo + MLA_V + 1, :])
        o_ref[qt * tq:(qt + 1) * tq, :] = ot_ref[...].T.astype(BF16)


def _attn_call(q, k, vt, batch, seq, tq):
    nq = seq // tq
    pairs = [(i, i) for i in range(nq)] + [(i, j) for i in range(nq) for j in range(i)]
    qi = jnp.asarray([p[0] for p in pairs], jnp.int32)
    kj = jnp.asarray([p[1] for p in pairs], jnp.int32)
    grid_spec = pltpu.PrefetchScalarGridSpec(
        num_scalar_prefetch=2,
        grid=(batch,),
        in_specs=[pl.BlockSpec((seq, MLA_HEADS * HEAD_PAD), lambda b, *_: (b, 0)),
                  pl.BlockSpec((seq, MLA_HEADS * HEAD_PAD), lambda b, *_: (b, 0)),
                  pl.BlockSpec((nq, MLA_HEADS * MLA_VA, tq), lambda b, *_: (b, 0, 0))],
        out_specs=pl.BlockSpec((seq, MLA_WIDTH), lambda b, *_: (b, 0)),
        scratch_shapes=[pltpu.VMEM((nq, MLA_HEADS * MLA_VA, tq), F32), pltpu.VMEM((MLA_WIDTH, tq), F32),
                        pltpu.VMEM((nq, MLA_HEADS, tq), F32),
                        pltpu.VMEM((MLA_HEADS, tq, tq), F32), pltpu.VMEM((MLA_HEADS, tq, tq), F32),
                        pltpu.VMEM((MLA_HEADS, tq), F32), pltpu.VMEM((MLA_HEADS, tq), F32)])
    return pl.pallas_call(
        functools.partial(_attn_kernel, tq=tq, npairs=len(pairs)),
        grid_spec=grid_spec,
        out_shape=jax.ShapeDtypeStruct((batch * seq, MLA_WIDTH), BF16),
        compiler_params=pltpu.CompilerParams(dimension_semantics=("parallel",), vmem_limit_bytes=VMEM_LIMIT),
        name="mla_attn",
    )(qi, kj, q, k, vt)


def _post_kernel(x_ref, oa_ref, ob_ref, sa_ref, sb_ref, woa_ref, wob_ref, wout_ref, lnffn_ref,
                 wg_ref, wu_ref, wd_ref, fnorm_ref, out_ref):
    ya = _dot(oa_ref[...], woa_ref[...])
    yb = _dot(ob_ref[...], wob_ref[...])
    mix = sa_ref[...].astype(F32) * ya + sb_ref[...].astype(F32) * yb
    h = x_ref[...] + _dot(mix.astype(BF16), wout_ref[...])
    u = _rms(h, lnffn_ref[...]).astype(BF16)
    g = _dot(u, wg_ref[...])
    act = (g * jax.nn.sigmoid(g) * _dot(u, wu_ref[...])).astype(BF16)
    h = h + _dot(act, wd_ref[...])
    out_ref[...] = _rms(h, fnorm_ref[...])


def _post_call(x2, oa, ob, sa, sb, woa, wob, wout, lnffn, wg, wu, wd, fnorm, tm):
    t = x2.shape[0]
    row = lambda w: pl.BlockSpec((tm, w), lambda i: (i, 0))
    consts = (woa, wob, wout, lnffn, wg, wu, wd, fnorm)
    return pl.pallas_call(
        _post_kernel,
        grid=(t // tm,),
        in_specs=[row(D_MODEL), row(MLA_WIDTH), row(GLA_DV), row(D_MODEL), row(D_MODEL)]
                 + [_single_spec(c) for c in consts],
        out_specs=row(D_MODEL),
        out_shape=jax.ShapeDtypeStruct((t, D_MODEL), F32),
        compiler_params=pltpu.CompilerParams(dimension_semantics=("parallel",), vmem_limit_bytes=VMEM_LIMIT),
        name="post",
    )(x2, oa, ob, sa, sb, *consts)


def _repack_kernel(wt_ref, o_ref):
    offs = [0]
    for s in IN_SIZES:
        offs.append(offs[-1] + s)
    cols = wt_ref.shape[1]
    dst = 0
    for idx in (0, 1, 3, 4, 5, 7, 8, 9):
        n = IN_SIZES[idx]
        o_ref[dst:dst + n, :] = wt_ref[offs[idx]:offs[idx + 1], :].astype(BF16)
        dst += n
    o_ref[dst:dst + ROPE_LO, :] = jnp.zeros((ROPE_LO, cols), BF16)
    o_ref[dst + ROPE_LO:dst + GLR_LO, :] = wt_ref[offs[2]:offs[3], :].astype(BF16)
    o_ref[dst + GLR_LO:dst + GLR_LO + GLA_RANK, :] = wt_ref[offs[6]:offs[7], :].astype(BF16)
    o_ref[dst + GLR_LO + GLA_RANK:dst + LANES, :] = jnp.zeros((LANES - GLR_LO - GLA_RANK, cols), BF16)


def _repack_call(w_in_t):
    n, d = w_in_t.shape
    cols = _tile(d, REPACK_COLS)
    return pl.pallas_call(
        _repack_kernel,
        grid=(d // cols,),
        in_specs=[pl.BlockSpec((n, cols), lambda i: (0, i))],
        out_specs=pl.BlockSpec((D_IN_PACKED, cols), lambda i: (0, i)),
        out_shape=jax.ShapeDtypeStruct((D_IN_PACKED, d), BF16),
        compiler_params=pltpu.CompilerParams(dimension_semantics=("parallel",), vmem_limit_bytes=VMEM_LIMIT),
        name="repack_w_in",
    )(w_in_t)


def _pack_weights(w_in, mla_w_uq, mla_w_ukv, gla_w_gate2):
    winp = _repack_call(w_in.T)
    wuq = mla_w_uq.reshape(MLA_Q_RANK, MLA_HEADS, MLA_QK)
    wuqp = jnp.pad(wuq, ((0, 0), (0, 0), (0, HEAD_PAD - MLA_QK))).reshape(MLA_Q_RANK, MLA_HEADS * HEAD_PAD)
    wukv = mla_w_ukv.reshape(MLA_KV_RANK, MLA_HEADS, MLA_NOPE + MLA_V)
    wkp = jnp.pad(wukv[:, :, :MLA_NOPE], ((0, 0), (0, 0), (0, HEAD_PAD - MLA_NOPE)))
    wkp = wkp.reshape(MLA_KV_RANK, MLA_HEADS * HEAD_PAD)
    wvt = wukv[:, :, MLA_NOPE:].reshape(MLA_KV_RANK, MLA_WIDTH).T
    wg2p = jnp.pad(gla_w_gate2, ((GLR_LO, LANES - GLR_LO - GLA_RANK), (0, 0)))
    return winp, wuqp.astype(BF16), wkp.astype(BF16), wvt.astype(BF16), wg2p.astype(BF16)


def _rope_inv_freq():
    half = ROPE_HALF
    inv = 1.0 / (ROPE_THETA ** (jnp.arange(half, dtype=F32) / half))
    return inv.reshape(half, 1)


def _rope_placement():
    place = np.zeros((4 * ROPE_HALF, 3 * LANES), np.float32)
    for piece in range(2):
        for i in range(ROPE_HALF):
            cos_row = 2 * piece * ROPE_HALF + i
            sin_row = cos_row + ROPE_HALF
            place[cos_row, ROPE_LO + i] = 1.0
            place[cos_row, ROPE_LO + ROPE_HALF + i] = 1.0
            place[sin_row, LANES + ROPE_LO + i] = -1.0
            place[sin_row, 2 * LANES + ROPE_LO + ROPE_HALF + i] = 1.0
    return jnp.asarray(place, BF16)


def _tile(n, pref):
    while n % pref:
        pref //= 2
    return pref


def kernel(x, positions, ln_mix, w_in, mla_norm_cq, mla_w_uq, mla_norm_ckv, mla_w_ukv, mla_w_o, gla_w_gate2,
           gla_b_gate, gla_norm, gla_w_o, w_out, ln_ffn, ffn_w_gate, ffn_w_up, ffn_w_down, final_norm):
    batch, seq, d = x.shape
    assert d == D_MODEL and w_in.shape[0] == 1 and seq % GLA_CHUNK == 0
    t = batch * seq
    x2 = x.reshape(t, d)
    winp, wuqp, wkp, wvt, wg2p = _pack_weights(w_in[0], mla_w_uq[0], mla_w_ukv[0], gla_w_gate2[0])
    r = lambda a: a.reshape(1, -1)

    tq = _tile(seq, ATTN_TILE)
    tm = max(tq, _tile(seq, PROJ_TILE))
    pos3 = positions.reshape(t // tm, 1, tm).astype(F32)
    q, k, vt, sa, sb, ob = _proj_gla_call(
        x2, pos3, r(ln_mix[0]), winp, r(mla_norm_cq[0]), wuqp, r(mla_norm_ckv[0]), wkp, wvt, wg2p,
        r(gla_b_gate[0]), _rope_inv_freq(), _rope_placement(), r(gla_norm[0]), seq, tm, tq)
    oa = _attn_call(q, k, vt, batch, seq, tq)
    out = _post_call(x2, oa, ob, sa, sb, mla_w_o[0].astype(BF16), gla_w_o[0].astype(BF16),
                     w_out[0].astype(BF16), r(ln_ffn[0]), ffn_w_gate[0].astype(BF16),
                     ffn_w_up[0].astype(BF16), ffn_w_down[0].astype(BF16), r(final_norm), _tile(t, POST_TILE))
    return out.reshape(batch, seq, d)
```

```python
import functools
import math

import jax
import jax.numpy as jnp
import numpy as np
from jax import lax
from jax.experimental import pallas as pl
from jax.experimental.pallas import tpu as pltpu

F32 = jnp.float32
BF16 = jnp.bfloat16

D_MODEL = 1024
MLA_HEADS = 8
MLA_NOPE = 64
MLA_ROPE = 32
MLA_QK = MLA_NOPE + MLA_ROPE
MLA_V = 64
MLA_WIDTH = MLA_HEADS * MLA_V
MLA_Q_RANK = 384
MLA_KV_RANK = 256
ROPE_THETA = 10000.0
GLA_HEADS = 4
GLA_DV = 512
GLA_DK = 256
GLA_HK = GLA_DK // GLA_HEADS
GLA_HV = GLA_DV // GLA_HEADS
GLA_RANK = 16
GLA_TAU = 16.0
EPS = 1e-6
LOG2E = 1.4426950408889634
IN_SIZES = (MLA_Q_RANK, MLA_KV_RANK, MLA_ROPE, GLA_DK, GLA_DK, GLA_DV, GLA_RANK, GLA_DV, D_MODEL, D_MODEL)

LANES = 128
HEAD_PAD = LANES
ROPE_LO = MLA_NOPE
ROPE_HALF = MLA_ROPE // 2
GLR_LO = ROPE_LO + MLA_ROPE

OFF_CQ = 0
OFF_CKV = OFF_CQ + MLA_Q_RANK
OFF_GQ = OFF_CKV + MLA_KV_RANK
OFF_GK = OFF_GQ + GLA_DK
OFF_GV = OFF_GK + GLA_DK
OFF_OG = OFF_GV + GLA_DV
OFF_GA = OFF_OG + GLA_DV
OFF_GB = OFF_GA + D_MODEL
OFF_MISC = OFF_GB + D_MODEL
D_IN_PACKED = OFF_MISC + LANES

REPACK_COLS = 256
PROJ_TILE = 512
POST_TILE = 512
ATTN_TILE = 256
MLA_VA = MLA_V + 16
GLA_CHUNK = 64
GLA_SUB = 8
GLA_PAR = 4
VMEM_LIMIT = 56 * 1024 * 1024


def _dot(a, b):
    return jnp.dot(a, b, preferred_element_type=F32)


def _dot_nt(a, b):
    return lax.dot_general(a, b, (((1,), (1,)), ((), ())), preferred_element_type=F32)


def _dot_tn(a, b):
    return lax.dot_general(a, b, (((0,), (0,)), ((), ())), preferred_element_type=F32)


def _rms(x, w):
    return x * lax.rsqrt(jnp.mean(x * x, axis=-1, keepdims=True) + EPS) * w


def _single_spec(a):
    return pl.BlockSpec(a.shape, lambda *_: (0,) * a.ndim, pipeline_mode=pl.Buffered(1))


def _gla_constants():
    c_, s_ = GLA_CHUNK, GLA_SUB
    ri = lax.broadcasted_iota(jnp.int32, (c_, c_), 0)
    ci = lax.broadcasted_iota(jnp.int32, (c_, c_), 1)
    rl = lax.broadcasted_iota(jnp.int32, (c_, GLA_DK), 0)
    ll = lax.broadcasted_iota(jnp.int32, (c_, GLA_DK), 1)
    row_minus_col = rl - ll % GLA_HK
    di = lax.broadcasted_iota(jnp.int32, (GLA_DK, GLA_DK), 0) // GLA_HK
    dj = lax.broadcasted_iota(jnp.int32, (GLA_DK, GLA_DK), 1) // GLA_HK
    return dict(
        tri=jnp.where((ri // s_ == ci // s_) & (ci <= ri), 1.0, 0.0).astype(BF16),
        diag_key=jnp.where((row_minus_col >= 0) & (rl % s_ >= row_minus_col), row_minus_col, -1),
        head_of_lane=ll // GLA_HK,
        odd_head_lane=(lax.broadcasted_iota(jnp.int32, (c_, 2 * GLA_HK), 1) // GLA_HK) == 1,
        head_sum=jnp.where(di == dj, 1.0, 0.0).astype(BF16))


def _gla_slab_stages(base, par, consts, gq_ref, gk_ref, gv_ref, la_ref, og_ref, norm_w, o_ref, state_ref,
                     kpad_ref, lapad_ref):
    c_, s_ = GLA_CHUNK, GLA_SUB
    nsub = c_ // s_
    pair = 2 * GLA_HK
    tri, diag_key, head_of_lane = consts["tri"], consts["diag_key"], consts["head_of_lane"]
    odd_head_lane, head_sum = consts["odd_head_lane"], consts["head_sum"]
    chunks = [dict(rows=slice(base + c * c_, base + (c + 1) * c_), off=s_ + c * c_) for c in range(par)]

    def blocks(fn):
        return jnp.concatenate([fn(b) for b in range(nsub)], axis=0)

    def sub(t, b):
        return t[b * s_:(b + 1) * s_, :]

    def cumsums():
        kpad_ref[0:s_, :] = jnp.zeros((s_, GLA_DK), F32)
        lapad_ref[0:s_, :] = jnp.zeros((s_, GLA_DK), F32)
        kpad_ref[s_:, :] = gk_ref[base:base + par * c_, :].astype(F32)
        lapad_ref[s_:, :] = la_ref[base:base + par * c_, :]
        for ch in chunks:
            la = la_ref[ch["rows"], :]
            hi = la.astype(BF16)
            r1 = la - hi.astype(F32)
            mid = r1.astype(BF16)
            lo = (r1 - mid.astype(F32)).astype(BF16)
            ch["lsub"] = _dot(tri, hi) + _dot(tri, mid) + _dot(tri, lo)

    def operands():
        for ch in chunks:
            q = gq_ref[ch["rows"], :].astype(F32)
            k = gk_ref[ch["rows"], :].astype(F32)
            lsub = ch["lsub"]
            tot = [lsub[(b + 1) * s_ - 1:(b + 1) * s_, :] for b in range(nsub)]
            q_sub = q * jnp.exp2(lsub)
            k_sub = k * jnp.exp2(blocks(lambda b: jnp.broadcast_to(tot[b], (s_, GLA_DK))) - lsub)
            zero = jnp.zeros((s_, GLA_DK), F32)
            e = [jnp.exp2(t) for t in tot]

            def decayed(t, b, over):
                piece = sub(t, b)
                for i in over:
                    piece = piece * e[i]
                return piece

            ch["qd"] = blocks(lambda b: decayed(q_sub, b, range(b))).astype(BF16)
            kd = blocks(lambda b: decayed(k_sub, b, range(b + 1, nsub))).astype(BF16)
            ch["decay"] = jnp.exp2(sum(tot[1:], tot[0]))
            v = gv_ref[ch["rows"], :]
            ch["upd"] = sum(_dot_tn(v[:, h * GLA_HV:(h + 1) * GLA_HV], jnp.where(head_of_lane == h, kd, 0.0))
                            for h in range(GLA_HEADS))
            ch["q_lv"], ch["k_lv"] = [], []
            width = 1
            while width < nsub:
                for odd in range(1, nsub // width, 2):
                    q_lo, k_lo = odd * width, (odd - 1) * width
                    ch["q_lv"].append(blocks(lambda b: decayed(q_sub, b, range(q_lo, b))
                                             if q_lo <= b < q_lo + width else zero).astype(BF16))
                    ch["k_lv"].append(blocks(lambda b: decayed(k_sub, b, range(b + 1, k_lo + width))
                                             if k_lo <= b < k_lo + width else zero).astype(BF16))
                width *= 2
            decay = jnp.zeros((c_, GLA_DK), F32)
            prods = []
            for d in range(s_):
                if d > 0:
                    decay = decay + lapad_ref[ch["off"] - d + 1:ch["off"] - d + 1 + c_, :]
                k_sh = kpad_ref[ch["off"] - d:ch["off"] - d + c_, :]
                prods.append((q * k_sh * jnp.exp2(decay)).astype(BF16))
            ch["summed"] = _dot(jnp.concatenate(prods, axis=0), head_sum)

    def pair_scores():
        for ch in chunks:
            ch["attn_off"] = []
            for h in range(GLA_HEADS):
                lanes = slice((h // 2) * pair, (h // 2 + 1) * pair)
                mine = odd_head_lane if h % 2 else ~odd_head_lane
                qx = jnp.concatenate([jnp.where(mine, t[:, lanes], 0.0) for t in ch["q_lv"]], axis=1)
                kx = jnp.concatenate([t[:, lanes] for t in ch["k_lv"]], axis=1)
                ch["attn_off"].append(_dot_nt(qx, kx))

    def outputs():
        state = state_ref[...]
        for ch in chunks:
            ch["state_b"] = state.astype(BF16)
            state = state * ch["decay"] + ch["upd"]
        state_ref[...] = state
        for ch in chunks:
            attn_diag = jnp.zeros((c_, GLA_DK), F32)
            for d in range(s_):
                attn_diag = jnp.where(diag_key == d, ch["summed"][d * c_:(d + 1) * c_, :], attn_diag)
            v = gv_ref[ch["rows"], :]
            for h in range(GLA_HEADS):
                hv = slice(h * GLA_HV, (h + 1) * GLA_HV)
                attn = attn_diag[:, h * GLA_HK:(h + 1) * GLA_HK] + ch["attn_off"][h]
                o = _dot(attn.astype(BF16), v[:, hv])
                o = o + _dot_nt(jnp.where(head_of_lane == h, ch["qd"], 0.0), ch["state_b"])
                o = _rms(o, norm_w)
                g = og_ref[ch["rows"], hv].astype(F32)
                o_ref[ch["rows"], hv] = (o * (g * jax.nn.sigmoid(g))).astype(BF16)

    return [cumsums, operands, pair_scores, outputs]


def _proj_gla_kernel(x_ref, pos_ref, lnmix_ref, win_ref, ncq_ref, wuq_ref, nckv_ref, wk_ref, wvt_ref,
                     wg2_ref, bg_ref, invf_ref, place_ref, gnorm_ref,
                     q_ref, k_ref, vt_ref, sa_ref, sb_ref, ob_ref,
                     gq_ref, gk_ref, gv_ref, la_ref, og_ref, state_ref, kpad_ref, lapad_ref,
                     *, tkv, steps_per_seq, par):
    tm = x_ref.shape[0]

    @pl.when(pl.program_id(0) % steps_per_seq == 0)
    def _():
        state_ref[...] = jnp.zeros_like(state_ref)

    ub = _rms(x_ref[...], lnmix_ref[...]).astype(BF16)

    def seg(lo, width):
        return _dot_nt(ub, win_ref[lo:lo + width, :])

    misc = seg(OFF_MISC, LANES)
    xg = _dot(misc.astype(BF16), wg2_ref[...]) + bg_ref[...]
    la_ref[...] = (jnp.minimum(xg, 0.0) - jnp.log1p(jnp.exp(-jnp.abs(xg)))) * (LOG2E / GLA_TAU)
    gq_ref[...] = (seg(OFF_GQ, GLA_DK) * (GLA_HK ** -0.5)).astype(BF16)
    gk_ref[...] = seg(OFF_GK, GLA_DK).astype(BF16)
    gv_ref[...] = seg(OFF_GV, GLA_DV).astype(BF16)
    og_ref[...] = seg(OFF_OG, GLA_DV).astype(BF16)

    consts = _gla_constants()
    slab_rows = par * GLA_CHUNK
    scan = []
    for sl in range(tm // slab_rows):
        scan += _gla_slab_stages(sl * slab_rows, par, consts, gq_ref, gk_ref, gv_ref, la_ref, og_ref,
                                 gnorm_ref[...], ob_ref, state_ref, kpad_ref.at[sl], lapad_ref.at[sl])

    lane = lax.broadcasted_iota(jnp.int32, (1, LANES), 1)
    in_rope = (lane >= ROPE_LO) & (lane < ROPE_LO + MLA_ROPE)
    t = {}

    def latents():
        t["cq"] = seg(OFF_CQ, MLA_Q_RANK)
        t["ckv"] = seg(OFF_CKV, MLA_KV_RANK)

    def rope_tables():
        ang = invf_ref[...] * pos_ref[0]
        cs = jnp.concatenate([jnp.cos(ang), jnp.sin(ang)], axis=0)
        cs_hi = cs.astype(BF16)
        cs_lo = (cs - cs_hi.astype(F32)).astype(BF16)
        tabs = _dot_tn(jnp.concatenate([cs_hi, cs_lo], axis=0), place_ref[...])
        t["cos"] = tabs[:, :LANES] + jnp.where(in_rope, 0.0, 1.0)
        t["sin_x1"] = tabs[:, LANES:2 * LANES]
        t["sin_x2"] = tabs[:, 2 * LANES:]

    def gate_a():
        sa_ref[...] = jax.nn.sigmoid(seg(OFF_GA, D_MODEL)).astype(BF16)

    def gate_b():
        sb_ref[...] = jax.nn.sigmoid(seg(OFF_GB, D_MODEL)).astype(BF16)

    def latent_norms():
        t["cqn"] = _rms(t["cq"], ncq_ref[...]).astype(BF16)
        t["ckvn"] = _rms(t["ckv"], nckv_ref[...]).astype(BF16)

    def up_q():
        t["q"] = _dot(t["cqn"], wuq_ref[...]) * (MLA_QK ** -0.5 * LOG2E)

    def up_kv():
        t["k_nope"] = _dot(t["ckvn"], wk_ref[...])
        for c in range(tm // tkv):
            vt = _dot_nt(wvt_ref[...], t["ckvn"][c * tkv:(c + 1) * tkv, :]).astype(BF16)
            for h in range(MLA_HEADS):
                vt_ref[c, h * MLA_VA:h * MLA_VA + MLA_V, :] = vt[h * MLA_V:(h + 1) * MLA_V, :]
                vt_ref[c, h * MLA_VA + MLA_V:(h + 1) * MLA_VA, :] = jnp.ones((MLA_VA - MLA_V, tkv), BF16)

    def rope_store():
        def rope(a):
            return (a * t["cos"] + pltpu.roll(a, LANES - ROPE_HALF, axis=1) * t["sin_x1"]
                    + pltpu.roll(a, ROPE_HALF, axis=1) * t["sin_x2"])

        k_rope = rope(jnp.where(in_rope, misc, 0.0))
        for h in range(MLA_HEADS):
            sl = slice(h * HEAD_PAD, (h + 1) * HEAD_PAD)
            q_ref[:, sl] = rope(t["q"][:, sl]).astype(BF16)
            k_ref[:, sl] = (t["k_nope"][:, sl] + k_rope).astype(BF16)

    rest = [latents, gate_a, rope_tables, latent_norms, gate_b, up_q, up_kv, rope_store]
    for n in range(max(len(scan), len(rest))):
        if n < len(scan):
            scan[n]()
        if n < len(rest):
            rest[n]()


def _proj_gla_call(x2, pos3, lnmix, winp, ncq, wuqp, nckv, wkp, wvt, wg2p, bg, invf, place, gnorm, seq, tm, tkv):
    t = x2.shape[0]
    par = math.gcd(tm // GLA_CHUNK, GLA_PAR)
    nslab = tm // (par * GLA_CHUNK)
    row = lambda w: pl.BlockSpec((tm, w), lambda i: (i, 0))
    consts = (lnmix, winp, ncq, wuqp, nckv, wkp, wvt, wg2p, bg, invf, place, gnorm)
    out_widths = (MLA_HEADS * HEAD_PAD, MLA_HEADS * HEAD_PAD, None, D_MODEL, D_MODEL, GLA_DV)
    vt_spec = pl.BlockSpec((tm // tkv, MLA_HEADS * MLA_VA, tkv), lambda i: (i, 0, 0))
    vt_shape = jax.ShapeDtypeStruct((t // tkv, MLA_HEADS * MLA_VA, tkv), BF16)
    pad_rows = GLA_SUB + par * GLA_CHUNK
    return pl.pallas_call(
        functools.partial(_proj_gla_kernel, tkv=tkv, steps_per_seq=seq // tm, par=par),
        grid=(t // tm,),
        in_specs=[row(D_MODEL), pl.BlockSpec((1, 1, tm), lambda i: (i, 0, 0))] + [_single_spec(c) for c in consts],
        out_specs=[vt_spec if w is None else row(w) for w in out_widths],
        out_shape=[vt_shape if w is None else jax.ShapeDtypeStruct((t, w), BF16) for w in out_widths],
        scratch_shapes=[pltpu.VMEM((tm, GLA_DK), BF16), pltpu.VMEM((tm, GLA_DK), BF16),
                        pltpu.VMEM((tm, GLA_DV), BF16), pltpu.VMEM((tm, GLA_DK), F32),
                        pltpu.VMEM((tm, GLA_DV), BF16), pltpu.VMEM((GLA_HV, GLA_DK), F32),
                        pltpu.VMEM((nslab, pad_rows, GLA_DK), F32), pltpu.VMEM((nslab, pad_rows, GLA_DK), F32)],
        compiler_params=pltpu.CompilerParams(dimension_semantics=("arbitrary",), vmem_limit_bytes=VMEM_LIMIT),
        name="proj_gla",
    )(x2, pos3, *consts)


def _attn_kernel(qi_ref, kj_ref, q_ref, k_ref, vt_ref, o_ref, acc_ref, ot_ref, m_ref, s0_ref, s1_ref, smax0_ref,
                 smax1_ref, *, tq, npairs):
    nq = q_ref.shape[0] // tq
    key = lax.broadcasted_iota(jnp.int32, (tq, tq), 0)
    qry = lax.broadcasted_iota(jnp.int32, (tq, tq), 1)
    causal = key <= qry
    slots = ((s0_ref, smax0_ref), (s1_ref, smax1_ref))

    def scores(n, slot, heads, masked):
        s_ref, smax_ref = slots[slot]
        qi, kj = qi_ref[n], kj_ref[n]
        qrows = pl.ds(pl.multiple_of(qi * tq, tq), tq)
        krows = pl.ds(pl.multiple_of(kj * tq, tq), tq)
        for h in heads:
            hq = slice(h * HEAD_PAD, (h + 1) * HEAD_PAD)
            s = _dot_nt(k_ref[krows, hq], q_ref[qrows, hq])
            if masked:
                s = jnp.where(causal, s, -1e30)
            s_ref[h] = s
            smax_ref[h:h + 1, :] = jnp.max(s, axis=0, keepdims=True)

    def accumulate(n, slot, heads, first):
        s_ref, smax_ref = slots[slot]
        qi, kj = qi_ref[n], kj_ref[n]
        for h in heads:
            hv = slice(h * MLA_VA, (h + 1) * MLA_VA)
            m_new = smax_ref[h:h + 1, :]
            if not first:
                m = m_ref[qi, h:h + 1, :]
                m_new = jnp.maximum(m, m_new)
                alpha = jnp.exp2(m - m_new)
            p = jnp.exp2(s_ref[h] - m_new)
            m_ref[qi, h:h + 1, :] = m_new
            pv = _dot(vt_ref[kj, hv, :], p.astype(BF16))
            acc_ref[qi, hv, :] = pv if first else alpha * acc_ref[qi, hv, :] + pv

    all_heads = range(MLA_HEADS)

    def step(k, parity, diag_next, diag_now):
        for h in all_heads:
            scores(k + 1, 1 - parity, (h,), diag_next)
            accumulate(k, parity, (h,), diag_now)

    def steps(lo, hi, diag_next, diag_now):
        if (hi - lo) % 2:
            step(lo, lo % 2, diag_next, diag_now)
            lo += 1
        if hi > lo:
            def two_steps(t, carry):
                step(lo + 2 * t, lo % 2, diag_next, diag_now)
                step(lo + 2 * t + 1, 1 - lo % 2, diag_next, diag_now)
                return carry

            lax.fori_loop(0, (hi - lo) // 2, two_steps, 0)

    scores(0, 0, all_heads, True)
    steps(0, nq - 1, True, True)
    if npairs > nq:
        steps(nq - 1, nq, False, True)
        steps(nq, npairs - 1, False, False)
    accumulate(npairs - 1, (npairs - 1) % 2, all_heads, npairs == nq)

    for qt in range(nq):
        for h in range(MLA_HEADS):
            lo = h * MLA_VA
            ot_ref[h * MLA_V:(h + 1) * MLA_V, :] = (acc_ref[qt, lo:lo + MLA_V, :]
                                                    / acc_ref[qt, lo + MLA_V:lo + MLA_V + 1, :])
        o_ref[qt * tq:(qt + 1) * tq, :] = ot_ref[...].T.astype(BF16)


def _attn_call(q, k, vt, batch, seq, tq):
    nq = seq // tq
    pairs = [(i, i) for i in range(nq)] + [(i, j) for i in range(nq) for j in range(i)]
    qi = jnp.asarray([p[0] for p in pairs], jnp.int32)
    kj = jnp.asarray([p[1] for p in pairs], jnp.int32)
    grid_spec = pltpu.PrefetchScalarGridSpec(
        num_scalar_prefetch=2,
        grid=(batch,),
        in_specs=[pl.BlockSpec((seq, MLA_HEADS * HEAD_PAD), lambda b, *_: (b, 0)),
                  pl.BlockSpec((seq, MLA_HEADS * HEAD_PAD), lambda b, *_: (b, 0)),
                  pl.BlockSpec((nq, MLA_HEADS * MLA_VA, tq), lambda b, *_: (b, 0, 0))],
        out_specs=pl.BlockSpec((seq, MLA_WIDTH), lambda b, *_: (b, 0)),
        scratch_shapes=[pltpu.VMEM((nq, MLA_HEADS * MLA_VA, tq), F32), pltpu.VMEM((MLA_WIDTH, tq), F32),
                        pltpu.VMEM((nq, MLA_HEADS, tq), F32),
                        pltpu.VMEM((MLA_HEADS, tq, tq), F32), pltpu.VMEM((MLA_HEADS, tq, tq), F32),
                        pltpu.VMEM((MLA_HEADS, tq), F32), pltpu.VMEM((MLA_HEADS, tq), F32)])
    return pl.pallas_call(
        functools.partial(_attn_kernel, tq=tq, npairs=len(pairs)),
        grid_spec=grid_spec,
        out_shape=jax.ShapeDtypeStruct((batch * seq, MLA_WIDTH), BF16),
        compiler_params=pltpu.CompilerParams(dimension_semantics=("parallel",), vmem_limit_bytes=VMEM_LIMIT),
        name="mla_attn",
    )(qi, kj, q, k, vt)


def _post_kernel(x_ref, oa_ref, ob_ref, sa_ref, sb_ref, woa_ref, wob_ref, wout_ref, lnffn_ref,
                 wg_ref, wu_ref, wd_ref, fnorm_ref, out_ref):
    ya = _dot(oa_ref[...], woa_ref[...])
    yb = _dot(ob_ref[...], wob_ref[...])
    mix = sa_ref[...].astype(F32) * ya + sb_ref[...].astype(F32) * yb
    h = x_ref[...] + _dot(mix.astype(BF16), wout_ref[...])
    u = _rms(h, lnffn_ref[...]).astype(BF16)
    g = _dot(u, wg_ref[...])
    act = (g * jax.nn.sigmoid(g) * _dot(u, wu_ref[...])).astype(BF16)
    h = h + _dot(act, wd_ref[...])
    out_ref[...] = _rms(h, fnorm_ref[...])


def _post_call(x2, oa, ob, sa, sb, woa, wob, wout, lnffn, wg, wu, wd, fnorm, tm):
    t = x2.shape[0]
    row = lambda w: pl.BlockSpec((tm, w), lambda i: (i, 0))
    consts = (woa, wob, wout, lnffn, wg, wu, wd, fnorm)
    return pl.pallas_call(
        _post_kernel,
        grid=(t // tm,),
        in_specs=[row(D_MODEL), row(MLA_WIDTH), row(GLA_DV), row(D_MODEL), row(D_MODEL)]
                 + [_single_spec(c) for c in consts],
        out_specs=row(D_MODEL),
        out_shape=jax.ShapeDtypeStruct((t, D_MODEL), F32),
        compiler_params=pltpu.CompilerParams(dimension_semantics=("parallel",), vmem_limit_bytes=VMEM_LIMIT),
        name="post",
    )(x2, oa, ob, sa, sb, *consts)


def _repack_kernel(wt_ref, o_ref):
    offs = [0]
    for s in IN_SIZES:
        offs.append(offs[-1] + s)
    cols = wt_ref.shape[1]
    dst = 0
    for idx in (0, 1, 3, 4, 5, 7, 8, 9):
        n = IN_SIZES[idx]
        o_ref[dst:dst + n, :] = wt_ref[offs[idx]:offs[idx + 1], :].astype(BF16)
        dst += n
    o_ref[dst:dst + ROPE_LO, :] = jnp.zeros((ROPE_LO, cols), BF16)
    o_ref[dst + ROPE_LO:dst + GLR_LO, :] = wt_ref[offs[2]:offs[3], :].astype(BF16)
    o_ref[dst + GLR_LO:dst + GLR_LO + GLA_RANK, :] = wt_ref[offs[6]:offs[7], :].astype(BF16)
    o_ref[dst + GLR_LO + GLA_RANK:dst + LANES, :] = jnp.zeros((LANES - GLR_LO - GLA_RANK, cols), BF16)


def _repack_call(w_in_t):
    n, d = w_in_t.shape
    cols = _tile(d, REPACK_COLS)
    return pl.pallas_call(
        _repack_kernel,
        grid=(d // cols,),
        in_specs=[pl.BlockSpec((n, cols), lambda i: (0, i))],
        out_specs=pl.BlockSpec((D_IN_PACKED, cols), lambda i: (0, i)),
        out_shape=jax.ShapeDtypeStruct((D_IN_PACKED, d), BF16),
        compiler_params=pltpu.CompilerParams(dimension_semantics=("parallel",), vmem_limit_bytes=VMEM_LIMIT),
        name="repack_w_in",
    )(w_in_t)


def _pack_weights(w_in, mla_w_uq, mla_w_ukv, gla_w_gate2):
    winp = _repack_call(w_in.T)
    wuq = mla_w_uq.reshape(MLA_Q_RANK, MLA_HEADS, MLA_QK)
    wuqp = jnp.pad(wuq, ((0, 0), (0, 0), (0, HEAD_PAD - MLA_QK))).reshape(MLA_Q_RANK, MLA_HEADS * HEAD_PAD)
    wukv = mla_w_ukv.reshape(MLA_KV_RANK, MLA_HEADS, MLA_NOPE + MLA_V)
    wkp = jnp.pad(wukv[:, :, :MLA_NOPE], ((0, 0), (0, 0), (0, HEAD_PAD - MLA_NOPE)))
    wkp = wkp.reshape(MLA_KV_RANK, MLA_HEADS * HEAD_PAD)
    wvt = wukv[:, :, MLA_NOPE:].reshape(MLA_KV_RANK, MLA_WIDTH).T
    wg2p = jnp.pad(gla_w_gate2, ((GLR_LO, LANES - GLR_LO - GLA_RANK), (0, 0)))
    return winp, wuqp.astype(BF16), wkp.astype(BF16), wvt.astype(BF16), wg2p.astype(BF16)


def _rope_inv_freq():
    half = ROPE_HALF
    inv = 1.0 / (ROPE_THETA ** (jnp.arange(half, dtype=F32) / half))
    return inv.reshape(half, 1)


def _rope_placement():
    place = np.zeros((4 * ROPE_HALF, 3 * LANES), np.float32)
    for piece in range(2):
        for i in range(ROPE_HALF):
            cos_row = 2 * piece * ROPE_HALF + i
            sin_row = cos_row + ROPE_HALF
            place[cos_row, ROPE_LO + i] = 1.0
            place[cos_row, ROPE_LO + ROPE_HALF + i] = 1.0
            place[sin_row, LANES + ROPE_LO + i] = -1.0
            place[sin_row, 2 * LANES + ROPE_LO + ROPE_HALF + i] = 1.0
    return jnp.asarray(place, BF16)


def _tile(n, pref):
    while n % pref:
        pref //= 2
    return pref


def kernel(x, positions, ln_mix, w_in, mla_norm_cq, mla_w_uq, mla_norm_ckv, mla_w_ukv, mla_w_o, gla_w_gate2,
           gla_b_gate, gla_norm, gla_w_o, w_out, ln_ffn, ffn_w_gate, ffn_w_up, ffn_w_down, final_norm):
    batch, seq, d = x.shape
    assert d == D_MODEL and w_in.shape[0] == 1 and seq % GLA_CHUNK == 0
    t = batch * seq
    x2 = x.reshape(t, d)
    winp, wuqp, wkp, wvt, wg2p = _pack_weights(w_in[0], mla_w_uq[0], mla_w_ukv[0], gla_w_gate2[0])
    r = lambda a: a.reshape(1, -1)

    tq = _tile(seq, ATTN_TILE)
    tm = max(tq, _tile(seq, PROJ_TILE))
    pos3 = positions.reshape(t // tm, 1, tm).astype(F32)
    q, k, vt, sa, sb, ob = _proj_gla_call(
        x2, pos3, r(ln_mix[0]), winp, r(mla_norm_cq[0]), wuqp, r(mla_norm_ckv[0]), wkp, wvt, wg2p,
        r(gla_b_gate[0]), _rope_inv_freq(), _rope_placement(), r(gla_norm[0]), seq, tm, tq)
    oa = _attn_call(q, k, vt, batch, seq, tq)
    out = _post_call(x2, oa, ob, sa, sb, mla_w_o[0].astype(BF16), gla_w_o[0].astype(BF16),
                     w_out[0].astype(BF16), r(ln_ffn[0]), ffn_w_gate[0].astype(BF16),
                     ffn_w_up[0].astype(BF16), ffn_w_down[0].astype(BF16), r(final_norm), _tile(t, POST_TILE))
    return out.reshape(batch, seq, d)
```

```python
import functools
import math

import jax
import jax.numpy as jnp
import numpy as np
from jax import lax
from jax.experimental import pallas as pl
from jax.experimental.pallas import tpu as pltpu

F32 = jnp.float32
BF16 = jnp.bfloat16

D_MODEL = 1024
MLA_HEADS = 8
MLA_NOPE = 64
MLA_ROPE = 32
MLA_QK = MLA_NOPE + MLA_ROPE
MLA_V = 64
MLA_WIDTH = MLA_HEADS * MLA_V
MLA_Q_RANK = 384
MLA_KV_RANK = 256
ROPE_THETA = 10000.0
GLA_HEADS = 4
GLA_DV = 512
GLA_DK = 256
GLA_HK = GLA_DK // GLA_HEADS
GLA_HV = GLA_DV // GLA_HEADS
GLA_RANK = 16
GLA_TAU = 16.0
EPS = 1e-6
LOG2E = 1.4426950408889634
IN_SIZES = (MLA_Q_RANK, MLA_KV_RANK, MLA_ROPE, GLA_DK, GLA_DK, GLA_DV, GLA_RANK, GLA_DV, D_MODEL, D_MODEL)

LANES = 128
HEAD_PAD = LANES
ROPE_LO = MLA_NOPE
ROPE_HALF = MLA_ROPE // 2
GLR_LO = ROPE_LO + MLA_ROPE

OFF_CQ = 0
OFF_CKV = OFF_CQ + MLA_Q_RANK
OFF_GQ = OFF_CKV + MLA_KV_RANK
OFF_GK = OFF_GQ + GLA_DK
OFF_GV = OFF_GK + GLA_DK
OFF_OG = OFF_GV + GLA_DV
OFF_GA = OFF_OG + GLA_DV
OFF_GB = OFF_GA + D_MODEL
OFF_MISC = OFF_GB + D_MODEL
D_IN_PACKED = OFF_MISC + LANES

REPACK_COLS = 256
PROJ_TILE = 512
POST_TILE = 512
ATTN_TILE = 256
MLA_VA = MLA_V + 16
GLA_CHUNK = 64
GLA_SUB = 8
GLA_PAR = 4
VMEM_LIMIT = 56 * 1024 * 1024


def _dot(a, b):
    return jnp.dot(a, b, preferred_element_type=F32)


def _dot_nt(a, b):
    return lax.dot_general(a, b, (((1,), (1,)), ((), ())), preferred_element_type=F32)


def _dot_tn(a, b):
    return lax.dot_general(a, b, (((0,), (0,)), ((), ())), preferred_element_type=F32)


def _rms(x, w):
    return x * lax.rsqrt(jnp.mean(x * x, axis=-1, keepdims=True) + EPS) * w


def _single_spec(a):
    return pl.BlockSpec(a.shape, lambda *_: (0,) * a.ndim, pipeline_mode=pl.Buffered(1))


def _gla_constants():
    c_, s_ = GLA_CHUNK, GLA_SUB
    ri = lax.broadcasted_iota(jnp.int32, (c_, c_), 0)
    ci = lax.broadcasted_iota(jnp.int32, (c_, c_), 1)
    rl = lax.broadcasted_iota(jnp.int32, (c_, GLA_DK), 0)
    ll = lax.broadcasted_iota(jnp.int32, (c_, GLA_DK), 1)
    row_minus_col = rl - ll % GLA_HK
    di = lax.broadcasted_iota(jnp.int32, (GLA_DK, GLA_DK), 0) // GLA_HK
    dj = lax.broadcasted_iota(jnp.int32, (GLA_DK, GLA_DK), 1) // GLA_HK
    return dict(
        tri=jnp.where((ri // s_ == ci // s_) & (ci <= ri), 1.0, 0.0).astype(BF16),
        diag_key=jnp.where((row_minus_col >= 0) & (rl % s_ >= row_minus_col), row_minus_col, -1),
        head_of_lane=ll // GLA_HK,
        odd_head_lane=(lax.broadcasted_iota(jnp.int32, (c_, 2 * GLA_HK), 1) // GLA_HK) == 1,
        head_sum=jnp.where(di == dj, 1.0, 0.0).astype(BF16))


def _gla_slab_stages(base, par, consts, gq_ref, gk_ref, gv_ref, la_ref, og_ref, norm_w, o_ref, state_ref,
                     kpad_ref, lapad_ref):
    c_, s_ = GLA_CHUNK, GLA_SUB
    nsub = c_ // s_
    pair = 2 * GLA_HK
    tri, diag_key, head_of_lane = consts["tri"], consts["diag_key"], consts["head_of_lane"]
    odd_head_lane, head_sum = consts["odd_head_lane"], consts["head_sum"]
    chunks = [dict(rows=slice(base + c * c_, base + (c + 1) * c_), off=s_ + c * c_) for c in range(par)]

    def blocks(fn):
        return jnp.concatenate([fn(b) for b in range(nsub)], axis=0)

    def sub(t, b):
        return t[b * s_:(b + 1) * s_, :]

    def cumsums():
        kpad_ref[0:s_, :] = jnp.zeros((s_, GLA_DK), F32)
        lapad_ref[0:s_, :] = jnp.zeros((s_, GLA_DK), F32)
        kpad_ref[s_:, :] = gk_ref[base:base + par * c_, :].astype(F32)
        lapad_ref[s_:, :] = la_ref[base:base + par * c_, :]
        for ch in chunks:
            la = la_ref[ch["rows"], :]
            hi = la.astype(BF16)
            r1 = la - hi.astype(F32)
            mid = r1.astype(BF16)
            lo = (r1 - mid.astype(F32)).astype(BF16)
            ch["lsub"] = _dot(tri, hi) + _dot(tri, mid) + _dot(tri, lo)

    def operands():
        for ch in chunks:
            q = gq_ref[ch["rows"], :].astype(F32)
            k = gk_ref[ch["rows"], :].astype(F32)
            lsub = ch["lsub"]
            tot = [lsub[(b + 1) * s_ - 1:(b + 1) * s_, :] for b in range(nsub)]
            q_sub = q * jnp.exp2(lsub)
            k_sub = k * jnp.exp2(blocks(lambda b: jnp.broadcast_to(tot[b], (s_, GLA_DK))) - lsub)
            zero = jnp.zeros((s_, GLA_DK), F32)
            e = [jnp.exp2(t) for t in tot]

            def decayed(t, b, over):
                piece = sub(t, b)
                for i in over:
                    piece = piece * e[i]
                return piece

            ch["qd"] = blocks(lambda b: decayed(q_sub, b, range(b))).astype(BF16)
            kd = blocks(lambda b: decayed(k_sub, b, range(b + 1, nsub))).astype(BF16)
            ch["decay"] = jnp.exp2(sum(tot[1:], tot[0]))
            v = gv_ref[ch["rows"], :]
            ch["upd"] = sum(_dot_tn(v[:, h * GLA_HV:(h + 1) * GLA_HV], jnp.where(head_of_lane == h, kd, 0.0))
                            for h in range(GLA_HEADS))
            ch["q_lv"], ch["k_lv"] = [], []
            width = 1
            while width < nsub:
                for odd in range(1, nsub // width, 2):
                    q_lo, k_lo = odd * width, (odd - 1) * width
                    ch["q_lv"].append(blocks(lambda b: decayed(q_sub, b, range(q_lo, b))
                                             if q_lo <= b < q_lo + width else zero))
                    ch["k_lv"].append(blocks(lambda b: decayed(k_sub, b, range(b + 1, k_lo + width))
                                             if k_lo <= b < k_lo + width else zero))
                width *= 2
            decay = jnp.zeros((c_, GLA_DK), F32)
            prods = []
            for d in range(s_):
                if d > 0:
                    decay = decay + lapad_ref[ch["off"] - d + 1:ch["off"] - d + 1 + c_, :]
                k_sh = kpad_ref[ch["off"] - d:ch["off"] - d + c_, :]
                prods.append((q * k_sh * jnp.exp2(decay)).astype(BF16))
            ch["summed"] = _dot(jnp.concatenate(prods, axis=0), head_sum)

    def pair_scores():
        for ch in chunks:
            ch["attn_off"] = []
            for tile in range(GLA_HEADS // 2):
                lanes = slice(tile * pair, (tile + 1) * pair)

                def split(groups):
                    tiles = [g[:, lanes] for g in groups]
                    return tiles[0::2], [pltpu.roll(g, GLA_HK, axis=1) for g in tiles[1::2]]

                q_even, q_odd = split(ch["q_lv"])
                k_even, k_odd = split(ch["k_lv"])
                for own in (~odd_head_lane, odd_head_lane):
                    def packed(even, odd):
                        return jnp.concatenate(
                            [jnp.where(own, a, odd[n] if n < len(odd) else 0.0).astype(BF16)
                             for n, a in enumerate(even)], axis=1)

                    ch["attn_off"].append(_dot_nt(packed(q_even, q_odd), packed(k_even, k_odd)))

    def outputs():
        state = state_ref[...]
        for ch in chunks:
            ch["state_b"] = state.astype(BF16)
            state = state * ch["decay"] + ch["upd"]
        state_ref[...] = state
        for ch in chunks:
            attn_diag = jnp.zeros((c_, GLA_DK), F32)
            for d in range(s_):
                attn_diag = jnp.where(diag_key == d, ch["summed"][d * c_:(d + 1) * c_, :], attn_diag)
            v = gv_ref[ch["rows"], :]
            for h in range(GLA_HEADS):
                hv = slice(h * GLA_HV, (h + 1) * GLA_HV)
                attn = attn_diag[:, h * GLA_HK:(h + 1) * GLA_HK] + ch["attn_off"][h]
                o = _dot(attn.astype(BF16), v[:, hv])
                o = o + _dot_nt(jnp.where(head_of_lane == h, ch["qd"], 0.0), ch["state_b"])
                o = _rms(o, norm_w)
                g = og_ref[ch["rows"], hv].astype(F32)
                o_ref[ch["rows"], hv] = (o * (g * jax.nn.sigmoid(g))).astype(BF16)

    return [cumsums, operands, pair_scores, outputs]


def _proj_gla_kernel(x_ref, pos_ref, lnmix_ref, win_ref, ncq_ref, wuq_ref, nckv_ref, wk_ref, wvt_ref,
                     wg2_ref, bg_ref, invf_ref, place_ref, gnorm_ref,
                     q_ref, k_ref, vt_ref, sa_ref, sb_ref, ob_ref,
                     gq_ref, gk_ref, gv_ref, la_ref, og_ref, state_ref, kpad_ref, lapad_ref,
                     *, tkv, steps_per_seq, par):
    tm = x_ref.shape[0]

    @pl.when(pl.program_id(0) % steps_per_seq == 0)
    def _():
        state_ref[...] = jnp.zeros_like(state_ref)

    ub = _rms(x_ref[...], lnmix_ref[...]).astype(BF16)

    def seg(lo, width):
        return _dot_nt(ub, win_ref[lo:lo + width, :])

    misc = seg(OFF_MISC, LANES)
    xg = _dot(misc.astype(BF16), wg2_ref[...]) + bg_ref[...]
    la_ref[...] = (jnp.minimum(xg, 0.0) - jnp.log1p(jnp.exp(-jnp.abs(xg)))) * (LOG2E / GLA_TAU)
    gq_ref[...] = (seg(OFF_GQ, GLA_DK) * (GLA_HK ** -0.5)).astype(BF16)
    gk_ref[...] = seg(OFF_GK, GLA_DK).astype(BF16)
    gv_ref[...] = seg(OFF_GV, GLA_DV).astype(BF16)
    og_ref[...] = seg(OFF_OG, GLA_DV).astype(BF16)

    consts = _gla_constants()
    slab_rows = par * GLA_CHUNK
    scan = []
    for sl in range(tm // slab_rows):
        scan += _gla_slab_stages(sl * slab_rows, par, consts, gq_ref, gk_ref, gv_ref, la_ref, og_ref,
                                 gnorm_ref[...], ob_ref, state_ref, kpad_ref.at[sl], lapad_ref.at[sl])

    lane = lax.broadcasted_iota(jnp.int32, (1, LANES), 1)
    in_rope = (lane >= ROPE_LO) & (lane < ROPE_LO + MLA_ROPE)
    t = {}

    def latents():
        t["cq"] = seg(OFF_CQ, MLA_Q_RANK)
        t["ckv"] = seg(OFF_CKV, MLA_KV_RANK)

    def rope_tables():
        ang = invf_ref[...] * pos_ref[0]
        cs = jnp.concatenate([jnp.cos(ang), jnp.sin(ang)], axis=0)
        cs_hi = cs.astype(BF16)
        cs_lo = (cs - cs_hi.astype(F32)).astype(BF16)
        tabs = _dot_tn(jnp.concatenate([cs_hi, cs_lo], axis=0), place_ref[...])
        t["cos"] = tabs[:, :LANES] + jnp.where(in_rope, 0.0, 1.0)
        t["sin_x1"] = tabs[:, LANES:2 * LANES]
        t["sin_x2"] = tabs[:, 2 * LANES:]

    def gate_a():
        sa_ref[...] = jax.nn.sigmoid(seg(OFF_GA, D_MODEL)).astype(BF16)

    def gate_b():
        sb_ref[...] = jax.nn.sigmoid(seg(OFF_GB, D_MODEL)).astype(BF16)

    def latent_norms():
        t["cqn"] = _rms(t["cq"], ncq_ref[...]).astype(BF16)
        t["ckvn"] = _rms(t["ckv"], nckv_ref[...]).astype(BF16)

    def up_q():
        t["q"] = _dot(t["cqn"], wuq_ref[...]) * (MLA_QK ** -0.5 * LOG2E)

    def up_kv():
        t["k_nope"] = _dot(t["ckvn"], wk_ref[...])
        for c in range(tm // tkv):
            vt = _dot_nt(wvt_ref[...], t["ckvn"][c * tkv:(c + 1) * tkv, :]).astype(BF16)
            for h in range(MLA_HEADS):
                vt_ref[c, h * MLA_VA:h * MLA_VA + MLA_V, :] = vt[h * MLA_V:(h + 1) * MLA_V, :]
                vt_ref[c, h * MLA_VA + MLA_V:(h + 1) * MLA_VA, :] = jnp.ones((MLA_VA - MLA_V, tkv), BF16)

    def rope_store():
        def rope(a):
            return (a * t["cos"] + pltpu.roll(a, LANES - ROPE_HALF, axis=1) * t["sin_x1"]
                    + pltpu.roll(a, ROPE_HALF, axis=1) * t["sin_x2"])

        k_rope = rope(jnp.where(in_rope, misc, 0.0))
        for h in range(MLA_HEADS):
            sl = slice(h * HEAD_PAD, (h + 1) * HEAD_PAD)
            q_ref[:, sl] = rope(t["q"][:, sl]).astype(BF16)
            k_ref[:, sl] = (t["k_nope"][:, sl] + k_rope).astype(BF16)

    rest = [latents, gate_a, rope_tables, latent_norms, gate_b, up_q, up_kv, rope_store]
    for n in range(max(len(scan), len(rest))):
        if n < len(scan):
            scan[n]()
        if n < len(rest):
            rest[n]()


def _proj_gla_call(x2, pos3, lnmix, winp, ncq, wuqp, nckv, wkp, wvt, wg2p, bg, invf, place, gnorm, seq, tm, tkv):
    t = x2.shape[0]
    par = math.gcd(tm // GLA_CHUNK, GLA_PAR)
    nslab = tm // (par * GLA_CHUNK)
    row = lambda w: pl.BlockSpec((tm, w), lambda i: (i, 0))
    consts = (lnmix, winp, ncq, wuqp, nckv, wkp, wvt, wg2p, bg, invf, place, gnorm)
    out_widths = (MLA_HEADS * HEAD_PAD, MLA_HEADS * HEAD_PAD, None, D_MODEL, D_MODEL, GLA_DV)
    vt_spec = pl.BlockSpec((tm // tkv, MLA_HEADS * MLA_VA, tkv), lambda i: (i, 0, 0))
    vt_shape = jax.ShapeDtypeStruct((t // tkv, MLA_HEADS * MLA_VA, tkv), BF16)
    pad_rows = GLA_SUB + par * GLA_CHUNK
    return pl.pallas_call(
        functools.partial(_proj_gla_kernel, tkv=tkv, steps_per_seq=seq // tm, par=par),
        grid=(t // tm,),
        in_specs=[row(D_MODEL), pl.BlockSpec((1, 1, tm), lambda i: (i, 0, 0))] + [_single_spec(c) for c in consts],
        out_specs=[vt_spec if w is None else row(w) for w in out_widths],
        out_shape=[vt_shape if w is None else jax.ShapeDtypeStruct((t, w), BF16) for w in out_widths],
        scratch_shapes=[pltpu.VMEM((tm, GLA_DK), BF16), pltpu.VMEM((tm, GLA_DK), BF16),
                        pltpu.VMEM((tm, GLA_DV), BF16), pltpu.VMEM((tm, GLA_DK), F32),
                        pltpu.VMEM((tm, GLA_DV), BF16), pltpu.VMEM((GLA_HV, GLA_DK), F32),
                        pltpu.VMEM((nslab, pad_rows, GLA_DK), F32), pltpu.VMEM((nslab, pad_rows, GLA_DK), F32)],
        compiler_params=pltpu.CompilerParams(dimension_semantics=("arbitrary",), vmem_limit_bytes=VMEM_LIMIT),
        name="proj_gla",
    )(x2, pos3, *consts)


def _attn_kernel(qi_ref, kj_ref, q_ref, k_ref, vt_ref, o_ref, acc_ref, ot_ref, m_ref, s0_ref, s1_ref, smax0_ref,
                 smax1_ref, *, tq, npairs):
    nq = q_ref.shape[0] // tq
    key = lax.broadcasted_iota(jnp.int32, (tq, tq), 0)
    qry = lax.broadcasted_iota(jnp.int32, (tq, tq), 1)
    causal = key <= qry
    slots = ((s0_ref, smax0_ref), (s1_ref, smax1_ref))

    def scores(n, slot, heads, masked):
        s_ref, smax_ref = slots[slot]
        qi, kj = qi_ref[n], kj_ref[n]
        qrows = pl.ds(pl.multiple_of(qi * tq, tq), tq)
        krows = pl.ds(pl.multiple_of(kj * tq, tq), tq)
        for h in heads:
            hq = slice(h * HEAD_PAD, (h + 1) * HEAD_PAD)
            s = _dot_nt(k_ref[krows, hq], q_ref[qrows, hq])
            if masked:
                s = jnp.where(causal, s, -1e30)
            s_ref[h] = s
            smax_ref[h:h + 1, :] = jnp.max(s, axis=0, keepdims=True)

    def accumulate(n, slot, heads, first):
        s_ref, smax_ref = slots[slot]
        qi, kj = qi_ref[n], kj_ref[n]
        for h in heads:
            hv = slice(h * MLA_VA, (h + 1) * MLA_VA)
            m_new = smax_ref[h:h + 1, :]
            if not first:
                m = m_ref[qi, h:h + 1, :]
                m_new = jnp.maximum(m, m_new)
                alpha = jnp.exp2(m - m_new)
            p = jnp.exp2(s_ref[h] - m_new)
            m_ref[qi, h:h + 1, :] = m_new
            pv = _dot(vt_ref[kj, hv, :], p.astype(BF16))
            acc_ref[qi, hv, :] = pv if first else alpha * acc_ref[qi, hv, :] + pv

    all_heads = range(MLA_HEADS)

    def step(k, parity, diag_next, diag_now):
        for h in all_heads:
            scores(k + 1, 1 - parity, (h,), diag_next)
            accumulate(k, parity, (h,), diag_now)

    def steps(lo, hi, diag_next, diag_now):
        if (hi - lo) % 2:
            step(lo, lo % 2, diag_next, diag_now)
            lo += 1
        if hi > lo:
            def two_steps(t, carry):
                step(lo + 2 * t, lo % 2, diag_next, diag_now)
                step(lo + 2 * t + 1, 1 - lo % 2, diag_next, diag_now)
                return carry

            lax.fori_loop(0, (hi - lo) // 2, two_steps, 0)

    scores(0, 0, all_heads, True)
    steps(0, nq - 1, True, True)
    if npairs > nq:
        steps(nq - 1, nq, False, True)
        steps(nq, npairs - 1, False, False)
    accumulate(npairs - 1, (npairs - 1) % 2, all_heads, npairs == nq)

    for qt in range(nq):
        for h in range(MLA_HEADS):
            lo = h * MLA_VA
            ot_ref[h * MLA_V:(h + 1) * MLA_V, :] = (acc_ref[qt, lo:lo + MLA_V, :]
                                                    / acc_ref[qt, lo + MLA_V:lo + MLA_V + 1, :])
        o_ref[qt * tq:(qt + 1) * tq, :] = ot_ref[...].T.astype(BF16)


def _attn_call(q, k, vt, batch, seq, tq):
    nq = seq // tq
    pairs = [(i, i) for i in range(nq)] + [(i, j) for i in range(nq) for j in range(i)]
    qi = jnp.asarray([p[0] for p in pairs], jnp.int32)
    kj = jnp.asarray([p[1] for p in pairs], jnp.int32)
    grid_spec = pltpu.PrefetchScalarGridSpec(
        num_scalar_prefetch=2,
        grid=(batch,),
        in_specs=[pl.BlockSpec((seq, MLA_HEADS * HEAD_PAD), lambda b, *_: (b, 0)),
                  pl.BlockSpec((seq, MLA_HEADS * HEAD_PAD), lambda b, *_: (b, 0)),
                  pl.BlockSpec((nq, MLA_HEADS * MLA_VA, tq), lambda b, *_: (b, 0, 0))],
        out_specs=pl.BlockSpec((seq, MLA_WIDTH), lambda b, *_: (b, 0)),
        scratch_shapes=[pltpu.VMEM((nq, MLA_HEADS * MLA_VA, tq), F32), pltpu.VMEM((MLA_WIDTH, tq), F32),
                        pltpu.VMEM((nq, MLA_HEADS, tq), F32),
                        pltpu.VMEM((MLA_HEADS, tq, tq), F32), pltpu.VMEM((MLA_HEADS, tq, tq), F32),
                        pltpu.VMEM((MLA_HEADS, tq), F32), pltpu.VMEM((MLA_HEADS, tq), F32)])
    return pl.pallas_call(
        functools.partial(_attn_kernel, tq=tq, npairs=len(pairs)),
        grid_spec=grid_spec,
        out_shape=jax.ShapeDtypeStruct((batch * seq, MLA_WIDTH), BF16),
        compiler_params=pltpu.CompilerParams(dimension_semantics=("parallel",), vmem_limit_bytes=VMEM_LIMIT),
        name="mla_attn",
    )(qi, kj, q, k, vt)


def _post_kernel(x_ref, oa_ref, ob_ref, sa_ref, sb_ref, woa_ref, wob_ref, wout_ref, lnffn_ref,
                 wg_ref, wu_ref, wd_ref, fnorm_ref, out_ref):
    ya = _dot(oa_ref[...], woa_ref[...])
    yb = _dot(ob_ref[...], wob_ref[...])
    mix = sa_ref[...].astype(F32) * ya + sb_ref[...].astype(F32) * yb
    h = x_ref[...] + _dot(mix.astype(BF16), wout_ref[...])
    u = _rms(h, lnffn_ref[...]).astype(BF16)
    g = _dot(u, wg_ref[...])
    act = (g * jax.nn.sigmoid(g) * _dot(u, wu_ref[...])).astype(BF16)
    h = h + _dot(act, wd_ref[...])
    out_ref[...] = _rms(h, fnorm_ref[...])


def _post_call(x2, oa, ob, sa, sb, woa, wob, wout, lnffn, wg, wu, wd, fnorm, tm):
    t = x2.shape[0]
    row = lambda w: pl.BlockSpec((tm, w), lambda i: (i, 0))
    consts = (woa, wob, wout, lnffn, wg, wu, wd, fnorm)
    return pl.pallas_call(
        _post_kernel,
        grid=(t // tm,),
        in_specs=[row(D_MODEL), row(MLA_WIDTH), row(GLA_DV), row(D_MODEL), row(D_MODEL)]
                 + [_single_spec(c) for c in consts],
        out_specs=row(D_MODEL),
        out_shape=jax.ShapeDtypeStruct((t, D_MODEL), F32),
        compiler_params=pltpu.CompilerParams(dimension_semantics=("parallel",), vmem_limit_bytes=VMEM_LIMIT),
        name="post",
    )(x2, oa, ob, sa, sb, *consts)


def _repack_kernel(wt_ref, o_ref):
    offs = [0]
    for s in IN_SIZES:
        offs.append(offs[-1] + s)
    cols = wt_ref.shape[1]
    dst = 0
    for idx in (0, 1, 3, 4, 5, 7, 8, 9):
        n = IN_SIZES[idx]
        o_ref[dst:dst + n, :] = wt_ref[offs[idx]:offs[idx + 1], :].astype(BF16)
        dst += n
    o_ref[dst:dst + ROPE_LO, :] = jnp.zeros((ROPE_LO, cols), BF16)
    o_ref[dst + ROPE_LO:dst + GLR_LO, :] = wt_ref[offs[2]:offs[3], :].astype(BF16)
    o_ref[dst + GLR_LO:dst + GLR_LO + GLA_RANK, :] = wt_ref[offs[6]:offs[7], :].astype(BF16)
    o_ref[dst + GLR_LO + GLA_RANK:dst + LANES, :] = jnp.zeros((LANES - GLR_LO - GLA_RANK, cols), BF16)


def _repack_call(w_in_t):
    n, d = w_in_t.shape
    cols = _tile(d, REPACK_COLS)
    return pl.pallas_call(
        _repack_kernel,
        grid=(d // cols,),
        in_specs=[pl.BlockSpec((n, cols), lambda i: (0, i))],
        out_specs=pl.BlockSpec((D_IN_PACKED, cols), lambda i: (0, i)),
        out_shape=jax.ShapeDtypeStruct((D_IN_PACKED, d), BF16),
        compiler_params=pltpu.CompilerParams(dimension_semantics=("parallel",), vmem_limit_bytes=VMEM_LIMIT),
        name="repack_w_in",
    )(w_in_t)


def _pack_weights(w_in, mla_w_uq, mla_w_ukv, gla_w_gate2):
    winp = _repack_call(w_in.T)
    wuq = mla_w_uq.reshape(MLA_Q_RANK, MLA_HEADS, MLA_QK)
    wuqp = jnp.pad(wuq, ((0, 0), (0, 0), (0, HEAD_PAD - MLA_QK))).reshape(MLA_Q_RANK, MLA_HEADS * HEAD_PAD)
    wukv = mla_w_ukv.reshape(MLA_KV_RANK, MLA_HEADS, MLA_NOPE + MLA_V)
    wkp = jnp.pad(wukv[:, :, :MLA_NOPE], ((0, 0), (0, 0), (0, HEAD_PAD - MLA_NOPE)))
    wkp = wkp.reshape(MLA_KV_RANK, MLA_HEADS * HEAD_PAD)
    wvt = wukv[:, :, MLA_NOPE:].reshape(MLA_KV_RANK, MLA_WIDTH).T
    wg2p = jnp.pad(gla_w_gate2, ((GLR_LO, LANES - GLR_LO - GLA_RANK), (0, 0)))
    return winp, wuqp.astype(BF16), wkp.astype(BF16), wvt.astype(BF16), wg2p.astype(BF16)


def _rope_inv_freq():
    half = ROPE_HALF
    inv = 1.0 / (ROPE_THETA ** (jnp.arange(half, dtype=F32) / half))
    return inv.reshape(half, 1)


def _rope_placement():
    place = np.zeros((4 * ROPE_HALF, 3 * LANES), np.float32)
    for piece in range(2):
        for i in range(ROPE_HALF):
            cos_row = 2 * piece * ROPE_HALF + i
            sin_row = cos_row + ROPE_HALF
            place[cos_row, ROPE_LO + i] = 1.0
            place[cos_row, ROPE_LO + ROPE_HALF + i] = 1.0
            place[sin_row, LANES + ROPE_LO + i] = -1.0
            place[sin_row, 2 * LANES + ROPE_LO + ROPE_HALF + i] = 1.0
    return jnp.asarray(place, BF16)


def _tile(n, pref):
    while n % pref:
        pref //= 2
    return pref


def kernel(x, positions, ln_mix, w_in, mla_norm_cq, mla_w_uq, mla_norm_ckv, mla_w_ukv, mla_w_o, gla_w_gate2,
           gla_b_gate, gla_norm, gla_w_o, w_out, ln_ffn, ffn_w_gate, ffn_w_up, ffn_w_down, final_norm):
    batch, seq, d = x.shape
    assert d == D_MODEL and w_in.shape[0] == 1 and seq % GLA_CHUNK == 0
    t = batch * seq
    x2 = x.reshape(t, d)
    winp, wuqp, wkp, wvt, wg2p = _pack_weights(w_in[0], mla_w_uq[0], mla_w_ukv[0], gla_w_gate2[0])
    r = lambda a: a.reshape(1, -1)

    tq = _tile(seq, ATTN_TILE)
    tm = max(tq, _tile(seq, PROJ_TILE))
    pos3 = positions.reshape(t // tm, 1, tm).astype(F32)
    q, k, vt, sa, sb, ob = _proj_gla_call(
        x2, pos3, r(ln_mix[0]), winp, r(mla_norm_cq[0]), wuqp, r(mla_norm_ckv[0]), wkp, wvt, wg2p,
        r(gla_b_gate[0]), _rope_inv_freq(), _rope_placement(), r(gla_norm[0]), seq, tm, tq)
    oa = _attn_call(q, k, vt, batch, seq, tq)
    out = _post_call(x2, oa, ob, sa, sb, mla_w_o[0].astype(BF16), gla_w_o[0].astype(BF16),
                     w_out[0].astype(BF16), r(ln_ffn[0]), ffn_w_gate[0].astype(BF16),
                     ffn_w_up[0].astype(BF16), ffn_w_down[0].astype(BF16), r(final_norm), _tile(t, POST_TILE))
    return out.reshape(batch, seq, d)
```

```python
import functools
import math

import jax
import jax.numpy as jnp
import numpy as np
from jax import lax
from jax.experimental import pallas as pl
from jax.experimental.pallas import tpu as pltpu

F32 = jnp.float32
BF16 = jnp.bfloat16

D_MODEL = 1024
MLA_HEADS = 8
MLA_NOPE = 64
MLA_ROPE = 32
MLA_QK = MLA_NOPE + MLA_ROPE
MLA_V = 64
MLA_WIDTH = MLA_HEADS * MLA_V
MLA_Q_RANK = 384
MLA_KV_RANK = 256
ROPE_THETA = 10000.0
GLA_HEADS = 4
GLA_DV = 512
GLA_DK = 256
GLA_HK = GLA_DK // GLA_HEADS
GLA_HV = GLA_DV // GLA_HEADS
GLA_RANK = 16
GLA_TAU = 16.0
EPS = 1e-6
LOG2E = 1.4426950408889634
IN_SIZES = (MLA_Q_RANK, MLA_KV_RANK, MLA_ROPE, GLA_DK, GLA_DK, GLA_DV, GLA_RANK, GLA_DV, D_MODEL, D_MODEL)

LANES = 128
HEAD_PAD = LANES
ROPE_LO = MLA_NOPE
ROPE_HALF = MLA_ROPE // 2
GLR_LO = ROPE_LO + MLA_ROPE

OFF_CQ = 0
OFF_CKV = OFF_CQ + MLA_Q_RANK
OFF_GQ = OFF_CKV + MLA_KV_RANK
OFF_GK = OFF_GQ + GLA_DK
OFF_GV = OFF_GK + GLA_DK
OFF_OG = OFF_GV + GLA_DV
OFF_GA = OFF_OG + GLA_DV
OFF_GB = OFF_GA + D_MODEL
OFF_MISC = OFF_GB + D_MODEL
D_IN_PACKED = OFF_MISC + LANES

REPACK_COLS = 256
PROJ_TILE = 512
POST_TILE = 512
ATTN_TILE = 256
ATTN_UNROLL = 8
MLA_VA = MLA_V + 16
GLA_CHUNK = 64
GLA_SUB = 8
GLA_PAR = 4
VMEM_LIMIT = 56 * 1024 * 1024


def _dot(a, b):
    return jnp.dot(a, b, preferred_element_type=F32)


def _dot_nt(a, b):
    return lax.dot_general(a, b, (((1,), (1,)), ((), ())), preferred_element_type=F32)


def _dot_tn(a, b):
    return lax.dot_general(a, b, (((0,), (0,)), ((), ())), preferred_element_type=F32)


def _rms(x, w):
    return x * lax.rsqrt(jnp.mean(x * x, axis=-1, keepdims=True) + EPS) * w


def _single_spec(a):
    return pl.BlockSpec(a.shape, lambda *_: (0,) * a.ndim, pipeline_mode=pl.Buffered(1))


def _gla_constants():
    c_, s_ = GLA_CHUNK, GLA_SUB
    ri = lax.broadcasted_iota(jnp.int32, (c_, c_), 0)
    ci = lax.broadcasted_iota(jnp.int32, (c_, c_), 1)
    rl = lax.broadcasted_iota(jnp.int32, (c_, GLA_DK), 0)
    ll = lax.broadcasted_iota(jnp.int32, (c_, GLA_DK), 1)
    row_minus_col = rl - ll % GLA_HK
    di = lax.broadcasted_iota(jnp.int32, (GLA_DK, GLA_DK), 0) // GLA_HK
    dj = lax.broadcasted_iota(jnp.int32, (GLA_DK, GLA_DK), 1) // GLA_HK
    return dict(
        tri=jnp.where((ri // s_ == ci // s_) & (ci <= ri), 1.0, 0.0).astype(BF16),
        diag_key=jnp.where((row_minus_col >= 0) & (rl % s_ >= row_minus_col), row_minus_col, -1),
        head_of_lane=ll // GLA_HK,
        odd_head_lane=(lax.broadcasted_iota(jnp.int32, (c_, 2 * GLA_HK), 1) // GLA_HK) == 1,
        head_sum=jnp.where(di == dj, 1.0, 0.0).astype(BF16))


def _gla_slab_stages(base, par, consts, gq_ref, gk_ref, gv_ref, la_ref, og_ref, norm_w, o_ref, state_ref,
                     kpad_ref, lapad_ref):
    c_, s_ = GLA_CHUNK, GLA_SUB
    nsub = c_ // s_
    pair = 2 * GLA_HK
    tri, diag_key, head_of_lane = consts["tri"], consts["diag_key"], consts["head_of_lane"]
    odd_head_lane, head_sum = consts["odd_head_lane"], consts["head_sum"]
    chunks = [dict(rows=slice(base + c * c_, base + (c + 1) * c_), off=s_ + c * c_) for c in range(par)]

    def blocks(fn):
        return jnp.concatenate([fn(b) for b in range(nsub)], axis=0)

    def sub(t, b):
        return t[b * s_:(b + 1) * s_, :]

    def cumsums():
        kpad_ref[0:s_, :] = jnp.zeros((s_, GLA_DK), F32)
        lapad_ref[0:s_, :] = jnp.zeros((s_, GLA_DK), F32)
        kpad_ref[s_:, :] = gk_ref[base:base + par * c_, :].astype(F32)
        lapad_ref[s_:, :] = la_ref[base:base + par * c_, :]
        for ch in chunks:
            la = la_ref[ch["rows"], :]
            hi = la.astype(BF16)
            r1 = la - hi.astype(F32)
            mid = r1.astype(BF16)
            lo = (r1 - mid.astype(F32)).astype(BF16)
            ch["lsub"] = _dot(tri, hi) + _dot(tri, mid) + _dot(tri, lo)

    def operands():
        for ch in chunks:
            q = gq_ref[ch["rows"], :].astype(F32)
            k = gk_ref[ch["rows"], :].astype(F32)
            lsub = ch["lsub"]
            tot = [lsub[(b + 1) * s_ - 1:(b + 1) * s_, :] for b in range(nsub)]
            q_sub = q * jnp.exp2(lsub)
            k_sub = k * jnp.exp2(blocks(lambda b: jnp.broadcast_to(tot[b], (s_, GLA_DK))) - lsub)
            zero = jnp.zeros((s_, GLA_DK), F32)
            e = [jnp.exp2(t) for t in tot]

            def decayed(t, b, over):
                piece = sub(t, b)
                for i in over:
                    piece = piece * e[i]
                return piece

            ch["qd"] = blocks(lambda b: decayed(q_sub, b, range(b))).astype(BF16)
            kd = blocks(lambda b: decayed(k_sub, b, range(b + 1, nsub))).astype(BF16)
            ch["decay"] = jnp.exp2(sum(tot[1:], tot[0]))
            v = gv_ref[ch["rows"], :]
            ch["upd"] = sum(_dot_tn(v[:, h * GLA_HV:(h + 1) * GLA_HV], jnp.where(head_of_lane == h, kd, 0.0))
                            for h in range(GLA_HEADS))
            ch["q_lv"], ch["k_lv"] = [], []
            width = 1
            while width < nsub:
                for odd in range(1, nsub // width, 2):
                    q_lo, k_lo = odd * width, (odd - 1) * width
                    ch["q_lv"].append(blocks(lambda b: decayed(q_sub, b, range(q_lo, b))
                                             if q_lo <= b < q_lo + width else zero))
                    ch["k_lv"].append(blocks(lambda b: decayed(k_sub, b, range(b + 1, k_lo + width))
                                             if k_lo <= b < k_lo + width else zero))
                width *= 2
            decay = jnp.zeros((c_, GLA_DK), F32)
            prods = []
            for d in range(s_):
                if d > 0:
                    decay = decay + lapad_ref[ch["off"] - d + 1:ch["off"] - d + 1 + c_, :]
                k_sh = kpad_ref[ch["off"] - d:ch["off"] - d + c_, :]
                prods.append((q * k_sh * jnp.exp2(decay)).astype(BF16))
            ch["summed"] = _dot(jnp.concatenate(prods, axis=0), head_sum)

    def pair_scores():
        for ch in chunks:
            ch["attn_off"] = []
            for tile in range(GLA_HEADS // 2):
                lanes = slice(tile * pair, (tile + 1) * pair)

                def split(groups):
                    tiles = [g[:, lanes] for g in groups]
                    return tiles[0::2], [pltpu.roll(g, GLA_HK, axis=1) for g in tiles[1::2]]

                q_even, q_odd = split(ch["q_lv"])
                k_even, k_odd = split(ch["k_lv"])
                for own in (~odd_head_lane, odd_head_lane):
                    def packed(even, odd):
                        return jnp.concatenate(
                            [jnp.where(own, a, odd[n] if n < len(odd) else 0.0).astype(BF16)
                             for n, a in enumerate(even)], axis=1)

                    ch["attn_off"].append(_dot_nt(packed(q_even, q_odd), packed(k_even, k_odd)))

    def outputs():
        state = state_ref[...]
        for ch in chunks:
            ch["state_b"] = state.astype(BF16)
            state = state * ch["decay"] + ch["upd"]
        state_ref[...] = state
        for ch in chunks:
            attn_diag = jnp.zeros((c_, GLA_DK), F32)
            for d in range(s_):
                attn_diag = jnp.where(diag_key == d, ch["summed"][d * c_:(d + 1) * c_, :], attn_diag)
            v = gv_ref[ch["rows"], :]
            for h in range(GLA_HEADS):
                hv = slice(h * GLA_HV, (h + 1) * GLA_HV)
                attn = attn_diag[:, h * GLA_HK:(h + 1) * GLA_HK] + ch["attn_off"][h]
                o = _dot(attn.astype(BF16), v[:, hv])
                o = o + _dot_nt(jnp.where(head_of_lane == h, ch["qd"], 0.0), ch["state_b"])
                o = _rms(o, norm_w)
                g = og_ref[ch["rows"], hv].astype(F32)
                o_ref[ch["rows"], hv] = (o * (g * jax.nn.sigmoid(g))).astype(BF16)

    return [cumsums, operands, pair_scores, outputs]


def _proj_gla_kernel(x_ref, pos_ref, lnmix_ref, win_ref, ncq_ref, wuq_ref, nckv_ref, wk_ref, wvt_ref,
                     wg2_ref, bg_ref, invf_ref, place_ref, gnorm_ref,
                     q_ref, k_ref, vt_ref, sa_ref, sb_ref, ob_ref,
                     gq_ref, gk_ref, gv_ref, la_ref, og_ref, state_ref, kpad_ref, lapad_ref,
                     *, tkv, steps_per_seq, par):
    tm = x_ref.shape[0]

    @pl.when(pl.program_id(0) % steps_per_seq == 0)
    def _():
        state_ref[...] = jnp.zeros_like(state_ref)

    ub = _rms(x_ref[...], lnmix_ref[...]).astype(BF16)

    def seg(lo, width):
        return _dot_nt(ub, win_ref[lo:lo + width, :])

    misc = seg(OFF_MISC, LANES)
    xg = _dot(misc.astype(BF16), wg2_ref[...]) + bg_ref[...]
    la_ref[...] = (jnp.minimum(xg, 0.0) - jnp.log1p(jnp.exp(-jnp.abs(xg)))) * (LOG2E / GLA_TAU)
    gq_ref[...] = (seg(OFF_GQ, GLA_DK) * (GLA_HK ** -0.5)).astype(BF16)
    gk_ref[...] = seg(OFF_GK, GLA_DK).astype(BF16)
    gv_ref[...] = seg(OFF_GV, GLA_DV).astype(BF16)
    og_ref[...] = seg(OFF_OG, GLA_DV).astype(BF16)

    consts = _gla_constants()
    slab_rows = par * GLA_CHUNK
    scan = []
    for sl in range(tm // slab_rows):
        scan += _gla_slab_stages(sl * slab_rows, par, consts, gq_ref, gk_ref, gv_ref, la_ref, og_ref,
                                 gnorm_ref[...], ob_ref, state_ref, kpad_ref.at[sl], lapad_ref.at[sl])

    lane = lax.broadcasted_iota(jnp.int32, (1, LANES), 1)
    in_rope = (lane >= ROPE_LO) & (lane < ROPE_LO + MLA_ROPE)
    t = {}

    def latents():
        t["cq"] = seg(OFF_CQ, MLA_Q_RANK)
        t["ckv"] = seg(OFF_CKV, MLA_KV_RANK)

    def rope_tables():
        ang = invf_ref[...] * pos_ref[0]
        cs = jnp.concatenate([jnp.cos(ang), jnp.sin(ang)], axis=0)
        cs_hi = cs.astype(BF16)
        cs_lo = (cs - cs_hi.astype(F32)).astype(BF16)
        tabs = _dot_tn(jnp.concatenate([cs_hi, cs_lo], axis=0), place_ref[...])
        t["cos"] = tabs[:, :LANES] + jnp.where(in_rope, 0.0, 1.0)
        t["sin_x1"] = tabs[:, LANES:2 * LANES]
        t["sin_x2"] = tabs[:, 2 * LANES:]

    def gate_a():
        sa_ref[...] = jax.nn.sigmoid(seg(OFF_GA, D_MODEL)).astype(BF16)

    def gate_b():
        sb_ref[...] = jax.nn.sigmoid(seg(OFF_GB, D_MODEL)).astype(BF16)

    def latent_norms():
        t["cqn"] = _rms(t["cq"], ncq_ref[...]).astype(BF16)
        t["ckvn"] = _rms(t["ckv"], nckv_ref[...]).astype(BF16)

    def up_q():
        t["q"] = _dot(t["cqn"], wuq_ref[...]) * (MLA_QK ** -0.5 * LOG2E)

    def up_kv():
        t["k_nope"] = _dot(t["ckvn"], wk_ref[...])
        for c in range(tm // tkv):
            vt = _dot_nt(wvt_ref[...], t["ckvn"][c * tkv:(c + 1) * tkv, :]).astype(BF16)
            for h in range(MLA_HEADS):
                vt_ref[c, h * MLA_VA:h * MLA_VA + MLA_V, :] = vt[h * MLA_V:(h + 1) * MLA_V, :]
                vt_ref[c, h * MLA_VA + MLA_V:(h + 1) * MLA_VA, :] = jnp.ones((MLA_VA - MLA_V, tkv), BF16)

    def rope_store():
        def rope(a):
            return (a * t["cos"] + pltpu.roll(a, LANES - ROPE_HALF, axis=1) * t["sin_x1"]
                    + pltpu.roll(a, ROPE_HALF, axis=1) * t["sin_x2"])

        k_rope = rope(jnp.where(in_rope, misc, 0.0))
        for h in range(MLA_HEADS):
            sl = slice(h * HEAD_PAD, (h + 1) * HEAD_PAD)
            q_ref[:, sl] = rope(t["q"][:, sl]).astype(BF16)
            k_ref[:, sl] = (t["k_nope"][:, sl] + k_rope).astype(BF16)

    rest = [latents, gate_a, rope_tables, latent_norms, gate_b, up_q, up_kv, rope_store]
    for n in range(max(len(scan), len(rest))):
        if n < len(scan):
            scan[n]()
        if n < len(rest):
            rest[n]()


def _proj_gla_call(x2, pos3, lnmix, winp, ncq, wuqp, nckv, wkp, wvt, wg2p, bg, invf, place, gnorm, seq, tm, tkv):
    t = x2.shape[0]
    par = math.gcd(tm // GLA_CHUNK, GLA_PAR)
    nslab = tm // (par * GLA_CHUNK)
    row = lambda w: pl.BlockSpec((tm, w), lambda i: (i, 0))
    consts = (lnmix, winp, ncq, wuqp, nckv, wkp, wvt, wg2p, bg, invf, place, gnorm)
    out_widths = (MLA_HEADS * HEAD_PAD, MLA_HEADS * HEAD_PAD, None, D_MODEL, D_MODEL, GLA_DV)
    vt_spec = pl.BlockSpec((tm // tkv, MLA_HEADS * MLA_VA, tkv), lambda i: (i, 0, 0))
    vt_shape = jax.ShapeDtypeStruct((t // tkv, MLA_HEADS * MLA_VA, tkv), BF16)
    pad_rows = GLA_SUB + par * GLA_CHUNK
    return pl.pallas_call(
        functools.partial(_proj_gla_kernel, tkv=tkv, steps_per_seq=seq // tm, par=par),
        grid=(t // tm,),
        in_specs=[row(D_MODEL), pl.BlockSpec((1, 1, tm), lambda i: (i, 0, 0))] + [_single_spec(c) for c in consts],
        out_specs=[vt_spec if w is None else row(w) for w in out_widths],
        out_shape=[vt_shape if w is None else jax.ShapeDtypeStruct((t, w), BF16) for w in out_widths],
        scratch_shapes=[pltpu.VMEM((tm, GLA_DK), BF16), pltpu.VMEM((tm, GLA_DK), BF16),
                        pltpu.VMEM((tm, GLA_DV), BF16), pltpu.VMEM((tm, GLA_DK), F32),
                        pltpu.VMEM((tm, GLA_DV), BF16), pltpu.VMEM((GLA_HV, GLA_DK), F32),
                        pltpu.VMEM((nslab, pad_rows, GLA_DK), F32), pltpu.VMEM((nslab, pad_rows, GLA_DK), F32)],
        compiler_params=pltpu.CompilerParams(dimension_semantics=("arbitrary",), vmem_limit_bytes=VMEM_LIMIT),
        name="proj_gla",
    )(x2, pos3, *consts)


def _attn_kernel(qi_ref, kj_ref, q_ref, k_ref, vt_ref, o_ref, acc_ref, ot_ref, m_ref, s0_ref, s1_ref, smax0_ref,
                 smax1_ref, *, tq, npairs):
    nq = q_ref.shape[0] // tq
    key = lax.broadcasted_iota(jnp.int32, (tq, tq), 0)
    qry = lax.broadcasted_iota(jnp.int32, (tq, tq), 1)
    causal = key <= qry
    slots = ((s0_ref, smax0_ref), (s1_ref, smax1_ref))

    def scores(n, slot, heads, masked):
        s_ref, smax_ref = slots[slot]
        qi, kj = qi_ref[n], kj_ref[n]
        qrows = pl.ds(pl.multiple_of(qi * tq, tq), tq)
        krows = pl.ds(pl.multiple_of(kj * tq, tq), tq)
        for h in heads:
            hq = slice(h * HEAD_PAD, (h + 1) * HEAD_PAD)
            s = _dot_nt(k_ref[krows, hq], q_ref[qrows, hq])
            if masked:
                s = jnp.where(causal, s, -1e30)
            s_ref[h] = s
            smax_ref[h:h + 1, :] = jnp.max(s, axis=0, keepdims=True)

    def accumulate(n, slot, heads, first):
        s_ref, smax_ref = slots[slot]
        qi, kj = qi_ref[n], kj_ref[n]
        for h in heads:
            hv = slice(h * MLA_VA, (h + 1) * MLA_VA)
            m_new = smax_ref[h:h + 1, :]
            if not first:
                m = m_ref[qi, h:h + 1, :]
                m_new = jnp.maximum(m, m_new)
                alpha = jnp.exp2(m - m_new)
            p = jnp.exp2(s_ref[h] - m_new)
            m_ref[qi, h:h + 1, :] = m_new
            pv = _dot(vt_ref[kj, hv, :], p.astype(BF16))
            acc_ref[qi, hv, :] = pv if first else alpha * acc_ref[qi, hv, :] + pv

    all_heads = range(MLA_HEADS)

    def step(k, parity, diag_next, diag_now):
        for h in all_heads:
            scores(k + 1, 1 - parity, (h,), diag_next)
            accumulate(k, parity, (h,), diag_now)

    def steps(lo, hi, diag_next, diag_now):
        while (hi - lo) % ATTN_UNROLL:
            step(lo, lo % 2, diag_next, diag_now)
            lo += 1
        if hi > lo:
            def unrolled(t, carry):
                for u in range(ATTN_UNROLL):
                    step(lo + ATTN_UNROLL * t + u, (lo + u) % 2, diag_next, diag_now)
                return carry

            lax.fori_loop(0, (hi - lo) // ATTN_UNROLL, unrolled, 0)

    scores(0, 0, all_heads, True)
    steps(0, nq - 1, True, True)
    if npairs > nq:
        steps(nq - 1, nq, False, True)
        steps(nq, npairs - 1, False, False)
    accumulate(npairs - 1, (npairs - 1) % 2, all_heads, npairs == nq)

    for qt in range(nq):
        for h in range(MLA_HEADS):
            lo = h * MLA_VA
            inv_denominator = 1.0 / acc_ref[qt, lo + MLA_V:lo + MLA_V + 1, :]
            ot_ref[h * MLA_V:(h + 1) * MLA_V, :] = acc_ref[qt, lo:lo + MLA_V, :] * inv_denominator
        o_ref[qt * tq:(qt + 1) * tq, :] = ot_ref[...].T.astype(BF16)


def _attn_call(q, k, vt, batch, seq, tq):
    nq = seq // tq
    pairs = [(i, i) for i in range(nq)] + [(i, j) for i in range(nq) for j in range(i)]
    qi = jnp.asarray([p[0] for p in pairs], jnp.int32)
    kj = jnp.asarray([p[1] for p in pairs], jnp.int32)
    grid_spec = pltpu.PrefetchScalarGridSpec(
        num_scalar_prefetch=2,
        grid=(batch,),
        in_specs=[pl.BlockSpec((seq, MLA_HEADS * HEAD_PAD), lambda b, *_: (b, 0)),
                  pl.BlockSpec((seq, MLA_HEADS * HEAD_PAD), lambda b, *_: (b, 0)),
                  pl.BlockSpec((nq, MLA_HEADS * MLA_VA, tq), lambda b, *_: (b, 0, 0))],
        out_specs=pl.BlockSpec((seq, MLA_WIDTH), lambda b, *_: (b, 0)),
        scratch_shapes=[pltpu.VMEM((nq, MLA_HEADS * MLA_VA, tq), F32), pltpu.VMEM((MLA_WIDTH, tq), F32),
                        pltpu.VMEM((nq, MLA_HEADS, tq), F32),
                        pltpu.VMEM((MLA_HEADS, tq, tq), F32), pltpu.VMEM((MLA_HEADS, tq, tq), F32),
                        pltpu.VMEM((MLA_HEADS, tq), F32), pltpu.VMEM((MLA_HEADS, tq), F32)])
    return pl.pallas_call(
        functools.partial(_attn_kernel, tq=tq, npairs=len(pairs)),
        grid_spec=grid_spec,
        out_shape=jax.ShapeDtypeStruct((batch * seq, MLA_WIDTH), BF16),
        compiler_params=pltpu.CompilerParams(dimension_semantics=("parallel",), vmem_limit_bytes=VMEM_LIMIT),
        name="mla_attn",
    )(qi, kj, q, k, vt)


def _post_kernel(x_ref, oa_ref, ob_ref, sa_ref, sb_ref, woa_ref, wob_ref, wout_ref, lnffn_ref,
                 wg_ref, wu_ref, wd_ref, fnorm_ref, out_ref):
    ya = _dot(oa_ref[...], woa_ref[...])
    yb = _dot(ob_ref[...], wob_ref[...])
    mix = sa_ref[...].astype(F32) * ya + sb_ref[...].astype(F32) * yb
    h = x_ref[...] + _dot(mix.astype(BF16), wout_ref[...])
    u = _rms(h, lnffn_ref[...]).astype(BF16)
    g = _dot(u, wg_ref[...])
    act = (g * jax.nn.sigmoid(g) * _dot(u, wu_ref[...])).astype(BF16)
    h = h + _dot(act, wd_ref[...])
    out_ref[...] = _rms(h, fnorm_ref[...])


def _post_call(x2, oa, ob, sa, sb, woa, wob, wout, lnffn, wg, wu, wd, fnorm, tm):
    t = x2.shape[0]
    row = lambda w: pl.BlockSpec((tm, w), lambda i: (i, 0))
    consts = (woa, wob, wout, lnffn, wg, wu, wd, fnorm)
    return pl.pallas_call(
        _post_kernel,
        grid=(t // tm,),
        in_specs=[row(D_MODEL), row(MLA_WIDTH), row(GLA_DV), row(D_MODEL), row(D_MODEL)]
                 + [_single_spec(c) for c in consts],
        out_specs=row(D_MODEL),
        out_shape=jax.ShapeDtypeStruct((t, D_MODEL), F32),
        compiler_params=pltpu.CompilerParams(dimension_semantics=("parallel",), vmem_limit_bytes=VMEM_LIMIT),
        name="post",
    )(x2, oa, ob, sa, sb, *consts)


def _repack_kernel(wt_ref, o_ref):
    offs = [0]
    for s in IN_SIZES:
        offs.append(offs[-1] + s)
    cols = wt_ref.shape[1]
    dst = 0
    for idx in (0, 1, 3, 4, 5, 7, 8, 9):
        n = IN_SIZES[idx]
        o_ref[dst:dst + n, :] = wt_ref[offs[idx]:offs[idx + 1], :].astype(BF16)
        dst += n
    o_ref[dst:dst + ROPE_LO, :] = jnp.zeros((ROPE_LO, cols), BF16)
    o_ref[dst + ROPE_LO:dst + GLR_LO, :] = wt_ref[offs[2]:offs[3], :].astype(BF16)
    o_ref[dst + GLR_LO:dst + GLR_LO + GLA_RANK, :] = wt_ref[offs[6]:offs[7], :].astype(BF16)
    o_ref[dst + GLR_LO + GLA_RANK:dst + LANES, :] = jnp.zeros((LANES - GLR_LO - GLA_RANK, cols), BF16)


def _repack_call(w_in_t):
    n, d = w_in_t.shape
    cols = _tile(d, REPACK_COLS)
    return pl.pallas_call(
        _repack_kernel,
        grid=(d // cols,),
        in_specs=[pl.BlockSpec((n, cols), lambda i: (0, i))],
        out_specs=pl.BlockSpec((D_IN_PACKED, cols), lambda i: (0, i)),
        out_shape=jax.ShapeDtypeStruct((D_IN_PACKED, d), BF16),
        compiler_params=pltpu.CompilerParams(dimension_semantics=("parallel",), vmem_limit_bytes=VMEM_LIMIT),
        name="repack_w_in",
    )(w_in_t)


def _pack_weights(w_in, mla_w_uq, mla_w_ukv, gla_w_gate2):
    winp = _repack_call(w_in.T)
    wuq = mla_w_uq.reshape(MLA_Q_RANK, MLA_HEADS, MLA_QK)
    wuqp = jnp.pad(wuq, ((0, 0), (0, 0), (0, HEAD_PAD - MLA_QK))).reshape(MLA_Q_RANK, MLA_HEADS * HEAD_PAD)
    wukv = mla_w_ukv.reshape(MLA_KV_RANK, MLA_HEADS, MLA_NOPE + MLA_V)
    wkp = jnp.pad(wukv[:, :, :MLA_NOPE], ((0, 0), (0, 0), (0, HEAD_PAD - MLA_NOPE)))
    wkp = wkp.reshape(MLA_KV_RANK, MLA_HEADS * HEAD_PAD)
    wvt = wukv[:, :, MLA_NOPE:].reshape(MLA_KV_RANK, MLA_WIDTH).T
    wg2p = jnp.pad(gla_w_gate2, ((GLR_LO, LANES - GLR_LO - GLA_RANK), (0, 0)))
    return winp, wuqp.astype(BF16), wkp.astype(BF16), wvt.astype(BF16), wg2p.astype(BF16)


def _rope_inv_freq():
    half = ROPE_HALF
    inv = 1.0 / (ROPE_THETA ** (jnp.arange(half, dtype=F32) / half))
    return inv.reshape(half, 1)


def _rope_placement():
    place = np.zeros((4 * ROPE_HALF, 3 * LANES), np.float32)
    for piece in range(2):
        for i in range(ROPE_HALF):
            cos_row = 2 * piece * ROPE_HALF + i
            sin_row = cos_row + ROPE_HALF
            place[cos_row, ROPE_LO + i] = 1.0
            place[cos_row, ROPE_LO + ROPE_HALF + i] = 1.0
            place[sin_row, LANES + ROPE_LO + i] = -1.0
            place[sin_row, 2 * LANES + ROPE_LO + ROPE_HALF + i] = 1.0
    return jnp.asarray(place, BF16)


def _tile(n, pref):
    while n % pref:
        pref //= 2
    return pref


def kernel(x, positions, ln_mix, w_in, mla_norm_cq, mla_w_uq, mla_norm_ckv, mla_w_ukv, mla_w_o, gla_w_gate2,
           gla_b_gate, gla_norm, gla_w_o, w_out, ln_ffn, ffn_w_gate, ffn_w_up, ffn_w_down, final_norm):
    batch, seq, d = x.shape
    assert d == D_MODEL and w_in.shape[0] == 1 and seq % GLA_CHUNK == 0
    t = batch * seq
    x2 = x.reshape(t, d)
    winp, wuqp, wkp, wvt, wg2p = _pack_weights(w_in[0], mla_w_uq[0], mla_w_ukv[0], gla_w_gate2[0])
    r = lambda a: a.reshape(1, -1)

    tq = _tile(seq, ATTN_TILE)
    tm = max(tq, _tile(seq, PROJ_TILE))
    pos3 = positions.reshape(t // tm, 1, tm).astype(F32)
    q, k, vt, sa, sb, ob = _proj_gla_call(
        x2, pos3, r(ln_mix[0]), winp, r(mla_norm_cq[0]), wuqp, r(mla_norm_ckv[0]), wkp, wvt, wg2p,
        r(gla_b_gate[0]), _rope_inv_freq(), _rope_placement(), r(gla_norm[0]), seq, tm, tq)
    oa = _attn_call(q, k, vt, batch, seq, tq)
    out = _post_call(x2, oa, ob, sa, sb, mla_w_o[0].astype(BF16), gla_w_o[0].astype(BF16),
                     w_out[0].astype(BF16), r(ln_ffn[0]), ffn_w_gate[0].astype(BF16),
                     ffn_w_up[0].astype(BF16), ffn_w_down[0].astype(BF16), r(final_norm), _tile(t, POST_TILE))
    return out.reshape(batch, seq, d)
```

```python
import functools
import math

import jax
import jax.numpy as jnp
import numpy as np
from jax import lax
from jax.experimental import pallas as pl
from jax.experimental.pallas import tpu as pltpu

F32 = jnp.float32
BF16 = jnp.bfloat16

D_MODEL = 1024
MLA_HEADS = 8
MLA_NOPE = 64
MLA_ROPE = 32
MLA_QK = MLA_NOPE + MLA_ROPE
MLA_V = 64
MLA_WIDTH = MLA_HEADS * MLA_V
MLA_Q_RANK = 384
MLA_KV_RANK = 256
ROPE_THETA = 10000.0
GLA_HEADS = 4
GLA_DV = 512
GLA_DK = 256
GLA_HK = GLA_DK // GLA_HEADS
GLA_HV = GLA_DV // GLA_HEADS
GLA_RANK = 16
GLA_TAU = 16.0
EPS = 1e-6
LOG2E = 1.4426950408889634
IN_SIZES = (MLA_Q_RANK, MLA_KV_RANK, MLA_ROPE, GLA_DK, GLA_DK, GLA_DV, GLA_RANK, GLA_DV, D_MODEL, D_MODEL)

LANES = 128
HEAD_PAD = LANES
ROPE_LO = MLA_NOPE
ROPE_HALF = MLA_ROPE // 2
GLR_LO = ROPE_LO + MLA_ROPE

OFF_CQ = 0
OFF_CKV = OFF_CQ + MLA_Q_RANK
OFF_GQ = OFF_CKV + MLA_KV_RANK
OFF_GK = OFF_GQ + GLA_DK
OFF_GV = OFF_GK + GLA_DK
OFF_OG = OFF_GV + GLA_DV
OFF_GA = OFF_OG + GLA_DV
OFF_GB = OFF_GA + D_MODEL
OFF_MISC = OFF_GB + D_MODEL
D_IN_PACKED = OFF_MISC + LANES

REPACK_COLS = 256
PROJ_TILE = 512
POST_TILE = 512
POST_SUB = 256
ATTN_TILE = 256
ATTN_UNROLL = 8
MLA_VA = MLA_V + 16
GLA_CHUNK = 64
GLA_SUB = 8
GLA_PAR = 4
VMEM_LIMIT = 56 * 1024 * 1024


def _dot(a, b):
    return jnp.dot(a, b, preferred_element_type=F32)


def _dot_nt(a, b):
    return lax.dot_general(a, b, (((1,), (1,)), ((), ())), preferred_element_type=F32)


def _dot_tn(a, b):
    return lax.dot_general(a, b, (((0,), (0,)), ((), ())), preferred_element_type=F32)


def _rms(x, w):
    return x * lax.rsqrt(jnp.mean(x * x, axis=-1, keepdims=True) + EPS) * w


def _single_spec(a):
    return pl.BlockSpec(a.shape, lambda *_: (0,) * a.ndim, pipeline_mode=pl.Buffered(1))


def _gla_constants():
    c_, s_ = GLA_CHUNK, GLA_SUB
    ri = lax.broadcasted_iota(jnp.int32, (c_, c_), 0)
    ci = lax.broadcasted_iota(jnp.int32, (c_, c_), 1)
    rl = lax.broadcasted_iota(jnp.int32, (c_, GLA_DK), 0)
    ll = lax.broadcasted_iota(jnp.int32, (c_, GLA_DK), 1)
    row_minus_col = rl - ll % GLA_HK
    di = lax.broadcasted_iota(jnp.int32, (GLA_DK, GLA_DK), 0) // GLA_HK
    dj = lax.broadcasted_iota(jnp.int32, (GLA_DK, GLA_DK), 1) // GLA_HK
    return dict(
        tri=jnp.where((ri // s_ == ci // s_) & (ci <= ri), 1.0, 0.0).astype(BF16),
        diag_key=jnp.where((row_minus_col >= 0) & (rl % s_ >= row_minus_col), row_minus_col, -1),
        head_of_lane=ll // GLA_HK,
        odd_head_lane=(lax.broadcasted_iota(jnp.int32, (c_, 2 * GLA_HK), 1) // GLA_HK) == 1,
        head_sum=jnp.where(di == dj, 1.0, 0.0).astype(BF16))


def _gla_slab_stages(base, par, consts, gq_ref, gk_ref, gv_ref, la_ref, og_ref, norm_w, o_ref, state_ref,
                     kpad_ref, lapad_ref):
    c_, s_ = GLA_CHUNK, GLA_SUB
    nsub = c_ // s_
    pair = 2 * GLA_HK
    tri, diag_key, head_of_lane = consts["tri"], consts["diag_key"], consts["head_of_lane"]
    odd_head_lane, head_sum = consts["odd_head_lane"], consts["head_sum"]
    chunks = [dict(rows=slice(base + c * c_, base + (c + 1) * c_), off=s_ + c * c_) for c in range(par)]

    def blocks(fn):
        return jnp.concatenate([fn(b) for b in range(nsub)], axis=0)

    def sub(t, b):
        return t[b * s_:(b + 1) * s_, :]

    def cumsums():
        kpad_ref[0:s_, :] = jnp.zeros((s_, GLA_DK), F32)
        lapad_ref[0:s_, :] = jnp.zeros((s_, GLA_DK), F32)
        kpad_ref[s_:, :] = gk_ref[base:base + par * c_, :].astype(F32)
        lapad_ref[s_:, :] = la_ref[base:base + par * c_, :]
        for ch in chunks:
            la = la_ref[ch["rows"], :]
            hi = la.astype(BF16)
            r1 = la - hi.astype(F32)
            mid = r1.astype(BF16)
            lo = (r1 - mid.astype(F32)).astype(BF16)
            ch["lsub"] = _dot(tri, hi) + _dot(tri, mid) + _dot(tri, lo)

    def operands():
        for ch in chunks:
            q = gq_ref[ch["rows"], :].astype(F32)
            k = gk_ref[ch["rows"], :].astype(F32)
            lsub = ch["lsub"]
            tot = [lsub[(b + 1) * s_ - 1:(b + 1) * s_, :] for b in range(nsub)]
            q_sub = q * jnp.exp2(lsub)
            k_sub = k * jnp.exp2(blocks(lambda b: jnp.broadcast_to(tot[b], (s_, GLA_DK))) - lsub)
            zero = jnp.zeros((s_, GLA_DK), F32)
            e = [jnp.exp2(t) for t in tot]

            def decayed(t, b, over):
                piece = sub(t, b)
                for i in over:
                    piece = piece * e[i]
                return piece

            ch["qd"] = blocks(lambda b: decayed(q_sub, b, range(b))).astype(BF16)
            kd = blocks(lambda b: decayed(k_sub, b, range(b + 1, nsub))).astype(BF16)
            ch["decay"] = jnp.exp2(sum(tot[1:], tot[0]))
            v = gv_ref[ch["rows"], :]
            ch["upd"] = sum(_dot_tn(v[:, h * GLA_HV:(h + 1) * GLA_HV], jnp.where(head_of_lane == h, kd, 0.0))
                            for h in range(GLA_HEADS))
            ch["q_lv"], ch["k_lv"] = [], []
            width = 1
            while width < nsub:
                for odd in range(1, nsub // width, 2):
                    q_lo, k_lo = odd * width, (odd - 1) * width
                    ch["q_lv"].append(blocks(lambda b: decayed(q_sub, b, range(q_lo, b))
                                             if q_lo <= b < q_lo + width else zero))
                    ch["k_lv"].append(blocks(lambda b: decayed(k_sub, b, range(b + 1, k_lo + width))
                                             if k_lo <= b < k_lo + width else zero))
                width *= 2
            decay = jnp.zeros((c_, GLA_DK), F32)
            prods = []
            for d in range(s_):
                if d > 0:
                    decay = decay + lapad_ref[ch["off"] - d + 1:ch["off"] - d + 1 + c_, :]
                k_sh = kpad_ref[ch["off"] - d:ch["off"] - d + c_, :]
                prods.append((q * k_sh * jnp.exp2(decay)).astype(BF16))
            ch["summed"] = _dot(jnp.concatenate(prods, axis=0), head_sum)

    def pair_scores():
        for ch in chunks:
            ch["attn_off"] = []
            for tile in range(GLA_HEADS // 2):
                lanes = slice(tile * pair, (tile + 1) * pair)

                def split(groups):
                    tiles = [g[:, lanes] for g in groups]
                    return tiles[0::2], [pltpu.roll(g, GLA_HK, axis=1) for g in tiles[1::2]]

                q_even, q_odd = split(ch["q_lv"])
                k_even, k_odd = split(ch["k_lv"])
                for own in (~odd_head_lane, odd_head_lane):
                    def packed(even, odd):
                        return jnp.concatenate(
                            [jnp.where(own, a, odd[n] if n < len(odd) else 0.0).astype(BF16)
                             for n, a in enumerate(even)], axis=1)

                    ch["attn_off"].append(_dot_nt(packed(q_even, q_odd), packed(k_even, k_odd)))

    def outputs():
        state = state_ref[...]
        for ch in chunks:
            ch["state_b"] = state.astype(BF16)
            state = state * ch["decay"] + ch["upd"]
        state_ref[...] = state
        for ch in chunks:
            attn_diag = jnp.zeros((c_, GLA_DK), F32)
            for d in range(s_):
                attn_diag = jnp.where(diag_key == d, ch["summed"][d * c_:(d + 1) * c_, :], attn_diag)
            v = gv_ref[ch["rows"], :]
            for h in range(GLA_HEADS):
                hv = slice(h * GLA_HV, (h + 1) * GLA_HV)
                attn = attn_diag[:, h * GLA_HK:(h + 1) * GLA_HK] + ch["attn_off"][h]
                o = _dot(attn.astype(BF16), v[:, hv])
                o = o + _dot_nt(jnp.where(head_of_lane == h, ch["qd"], 0.0), ch["state_b"])
                o = _rms(o, norm_w)
                g = og_ref[ch["rows"], hv].astype(F32)
                o_ref[ch["rows"], hv] = (o * (g * jax.nn.sigmoid(g))).astype(BF16)

    return [cumsums, operands, pair_scores, outputs]


def _proj_gla_kernel(x_ref, pos_ref, lnmix_ref, win_ref, ncq_ref, wuq_ref, nckv_ref, wk_ref, wvt_ref,
                     wg2_ref, bg_ref, invf_ref, place_ref, gnorm_ref,
                     q_ref, k_ref, vt_ref, sa_ref, sb_ref, ob_ref,
                     gq_ref, gk_ref, gv_ref, la_ref, og_ref, state_ref, kpad_ref, lapad_ref,
                     *, tkv, steps_per_seq, par):
    tm = x_ref.shape[0]

    @pl.when(pl.program_id(0) % steps_per_seq == 0)
    def _():
        state_ref[...] = jnp.zeros_like(state_ref)

    ub = _rms(x_ref[...], lnmix_ref[...]).astype(BF16)

    def seg(lo, width):
        return _dot_nt(ub, win_ref[lo:lo + width, :])

    misc = seg(OFF_MISC, LANES)
    xg = _dot(misc.astype(BF16), wg2_ref[...]) + bg_ref[...]
    la_ref[...] = (jnp.minimum(xg, 0.0) - jnp.log1p(jnp.exp(-jnp.abs(xg)))) * (LOG2E / GLA_TAU)
    gq_ref[...] = (seg(OFF_GQ, GLA_DK) * (GLA_HK ** -0.5)).astype(BF16)
    gk_ref[...] = seg(OFF_GK, GLA_DK).astype(BF16)
    gv_ref[...] = seg(OFF_GV, GLA_DV).astype(BF16)
    og_ref[...] = seg(OFF_OG, GLA_DV).astype(BF16)

    consts = _gla_constants()
    slab_rows = par * GLA_CHUNK
    scan = []
    for sl in range(tm // slab_rows):
        scan += _gla_slab_stages(sl * slab_rows, par, consts, gq_ref, gk_ref, gv_ref, la_ref, og_ref,
                                 gnorm_ref[...], ob_ref, state_ref, kpad_ref.at[sl], lapad_ref.at[sl])

    lane = lax.broadcasted_iota(jnp.int32, (1, LANES), 1)
    in_rope = (lane >= ROPE_LO) & (lane < ROPE_LO + MLA_ROPE)
    t = {}

    def latents():
        t["cq"] = seg(OFF_CQ, MLA_Q_RANK)
        t["ckv"] = seg(OFF_CKV, MLA_KV_RANK)

    def rope_tables():
        ang = invf_ref[...] * pos_ref[0]
        cs = jnp.concatenate([jnp.cos(ang), jnp.sin(ang)], axis=0)
        cs_hi = cs.astype(BF16)
        cs_lo = (cs - cs_hi.astype(F32)).astype(BF16)
        tabs = _dot_tn(jnp.concatenate([cs_hi, cs_lo], axis=0), place_ref[...])
        t["cos"] = tabs[:, :LANES] + jnp.where(in_rope, 0.0, 1.0)
        t["sin_x1"] = tabs[:, LANES:2 * LANES]
        t["sin_x2"] = tabs[:, 2 * LANES:]

    def gate_a():
        sa_ref[...] = jax.nn.sigmoid(seg(OFF_GA, D_MODEL)).astype(BF16)

    def gate_b():
        sb_ref[...] = jax.nn.sigmoid(seg(OFF_GB, D_MODEL)).astype(BF16)

    def latent_norms():
        t["cqn"] = _rms(t["cq"], ncq_ref[...]).astype(BF16)
        t["ckvn"] = _rms(t["ckv"], nckv_ref[...]).astype(BF16)

    def up_q():
        t["q"] = _dot(t["cqn"], wuq_ref[...]) * (MLA_QK ** -0.5 * LOG2E)

    def up_kv():
        t["k_nope"] = _dot(t["ckvn"], wk_ref[...])
        for c in range(tm // tkv):
            vt = _dot_nt(wvt_ref[...], t["ckvn"][c * tkv:(c + 1) * tkv, :]).astype(BF16)
            for h in range(MLA_HEADS):
                vt_ref[c, h * MLA_VA:h * MLA_VA + MLA_V, :] = vt[h * MLA_V:(h + 1) * MLA_V, :]
                vt_ref[c, h * MLA_VA + MLA_V:(h + 1) * MLA_VA, :] = jnp.ones((MLA_VA - MLA_V, tkv), BF16)

    def rope_store():
        def rope(a):
            return (a * t["cos"] + pltpu.roll(a, LANES - ROPE_HALF, axis=1) * t["sin_x1"]
                    + pltpu.roll(a, ROPE_HALF, axis=1) * t["sin_x2"])

        k_rope = rope(jnp.where(in_rope, misc, 0.0))
        for h in range(MLA_HEADS):
            sl = slice(h * HEAD_PAD, (h + 1) * HEAD_PAD)
            q_ref[:, sl] = rope(t["q"][:, sl]).astype(BF16)
            k_ref[:, sl] = (t["k_nope"][:, sl] + k_rope).astype(BF16)

    rest = [latents, gate_a, rope_tables, latent_norms, gate_b, up_q, up_kv, rope_store]
    for n in range(max(len(scan), len(rest))):
        if n < len(scan):
            scan[n]()
        if n < len(rest):
            rest[n]()


def _proj_gla_call(x2, pos3, lnmix, winp, ncq, wuqp, nckv, wkp, wvt, wg2p, bg, invf, place, gnorm, seq, tm, tkv):
    t = x2.shape[0]
    par = math.gcd(tm // GLA_CHUNK, GLA_PAR)
    nslab = tm // (par * GLA_CHUNK)
    row = lambda w: pl.BlockSpec((tm, w), lambda i: (i, 0))
    consts = (lnmix, winp, ncq, wuqp, nckv, wkp, wvt, wg2p, bg, invf, place, gnorm)
    out_widths = (MLA_HEADS * HEAD_PAD, MLA_HEADS * HEAD_PAD, None, D_MODEL, D_MODEL, GLA_DV)
    vt_spec = pl.BlockSpec((tm // tkv, MLA_HEADS * MLA_VA, tkv), lambda i: (i, 0, 0))
    vt_shape = jax.ShapeDtypeStruct((t // tkv, MLA_HEADS * MLA_VA, tkv), BF16)
    pad_rows = GLA_SUB + par * GLA_CHUNK
    return pl.pallas_call(
        functools.partial(_proj_gla_kernel, tkv=tkv, steps_per_seq=seq // tm, par=par),
        grid=(t // tm,),
        in_specs=[row(D_MODEL), pl.BlockSpec((1, 1, tm), lambda i: (i, 0, 0))] + [_single_spec(c) for c in consts],
        out_specs=[vt_spec if w is None else row(w) for w in out_widths],
        out_shape=[vt_shape if w is None else jax.ShapeDtypeStruct((t, w), BF16) for w in out_widths],
        scratch_shapes=[pltpu.VMEM((tm, GLA_DK), BF16), pltpu.VMEM((tm, GLA_DK), BF16),
                        pltpu.VMEM((tm, GLA_DV), BF16), pltpu.VMEM((tm, GLA_DK), F32),
                        pltpu.VMEM((tm, GLA_DV), BF16), pltpu.VMEM((GLA_HV, GLA_DK), F32),
                        pltpu.VMEM((nslab, pad_rows, GLA_DK), F32), pltpu.VMEM((nslab, pad_rows, GLA_DK), F32)],
        compiler_params=pltpu.CompilerParams(dimension_semantics=("arbitrary",), vmem_limit_bytes=VMEM_LIMIT),
        name="proj_gla",
    )(x2, pos3, *consts)


def _attn_kernel(qi_ref, kj_ref, q_ref, k_ref, vt_ref, o_ref, acc_ref, ot_ref, m_ref, s0_ref, s1_ref, smax0_ref,
                 smax1_ref, *, tq, npairs):
    nq = q_ref.shape[0] // tq
    key = lax.broadcasted_iota(jnp.int32, (tq, tq), 0)
    qry = lax.broadcasted_iota(jnp.int32, (tq, tq), 1)
    causal = key <= qry
    slots = ((s0_ref, smax0_ref), (s1_ref, smax1_ref))

    def scores(n, slot, heads, masked):
        s_ref, smax_ref = slots[slot]
        qi, kj = qi_ref[n], kj_ref[n]
        qrows = pl.ds(pl.multiple_of(qi * tq, tq), tq)
        krows = pl.ds(pl.multiple_of(kj * tq, tq), tq)
        for h in heads:
            hq = slice(h * HEAD_PAD, (h + 1) * HEAD_PAD)
            s = _dot_nt(k_ref[krows, hq], q_ref[qrows, hq])
            if masked:
                s = jnp.where(causal, s, -1e30)
            s_ref[h] = s
            smax_ref[h:h + 1, :] = jnp.max(s, axis=0, keepdims=True)

    def accumulate(n, slot, heads, first):
        s_ref, smax_ref = slots[slot]
        qi, kj = qi_ref[n], kj_ref[n]
        for h in heads:
            hv = slice(h * MLA_VA, (h + 1) * MLA_VA)
            m_new = smax_ref[h:h + 1, :]
            if not first:
                m = m_ref[qi, h:h + 1, :]
                m_new = jnp.maximum(m, m_new)
                alpha = jnp.exp2(m - m_new)
            p = jnp.exp2(s_ref[h] - m_new)
            m_ref[qi, h:h + 1, :] = m_new
            pv = _dot(vt_ref[kj, hv, :], p.astype(BF16))
            acc_ref[qi, hv, :] = pv if first else alpha * acc_ref[qi, hv, :] + pv

    all_heads = range(MLA_HEADS)

    def step(k, parity, diag_next, diag_now):
        for h in all_heads:
            scores(k + 1, 1 - parity, (h,), diag_next)
            accumulate(k, parity, (h,), diag_now)

    def steps(lo, hi, diag_next, diag_now):
        while (hi - lo) % ATTN_UNROLL:
            step(lo, lo % 2, diag_next, diag_now)
            lo += 1
        if hi > lo:
            def unrolled(t, carry):
                for u in range(ATTN_UNROLL):
                    step(lo + ATTN_UNROLL * t + u, (lo + u) % 2, diag_next, diag_now)
                return carry

            lax.fori_loop(0, (hi - lo) // ATTN_UNROLL, unrolled, 0)

    scores(0, 0, all_heads, True)
    steps(0, nq - 1, True, True)
    if npairs > nq:
        steps(nq - 1, nq, False, True)
        steps(nq, npairs - 1, False, False)
    accumulate(npairs - 1, (npairs - 1) % 2, all_heads, npairs == nq)

    for qt in range(nq):
        for h in range(MLA_HEADS):
            lo = h * MLA_VA
            inv_denominator = 1.0 / acc_ref[qt, lo + MLA_V:lo + MLA_V + 1, :]
            ot_ref[h * MLA_V:(h + 1) * MLA_V, :] = acc_ref[qt, lo:lo + MLA_V, :] * inv_denominator
        o_ref[qt * tq:(qt + 1) * tq, :] = ot_ref[...].T.astype(BF16)


def _attn_call(q, k, vt, batch, seq, tq):
    nq = seq // tq
    pairs = [(i, i) for i in range(nq)] + [(i, j) for i in range(nq) for j in range(i)]
    qi = jnp.asarray([p[0] for p in pairs], jnp.int32)
    kj = jnp.asarray([p[1] for p in pairs], jnp.int32)
    grid_spec = pltpu.PrefetchScalarGridSpec(
        num_scalar_prefetch=2,
        grid=(batch,),
        in_specs=[pl.BlockSpec((seq, MLA_HEADS * HEAD_PAD), lambda b, *_: (b, 0)),
                  pl.BlockSpec((seq, MLA_HEADS * HEAD_PAD), lambda b, *_: (b, 0)),
                  pl.BlockSpec((nq, MLA_HEADS * MLA_VA, tq), lambda b, *_: (b, 0, 0))],
        out_specs=pl.BlockSpec((seq, MLA_WIDTH), lambda b, *_: (b, 0)),
        scratch_shapes=[pltpu.VMEM((nq, MLA_HEADS * MLA_VA, tq), F32), pltpu.VMEM((MLA_WIDTH, tq), F32),
                        pltpu.VMEM((nq, MLA_HEADS, tq), F32),
                        pltpu.VMEM((MLA_HEADS, tq, tq), F32), pltpu.VMEM((MLA_HEADS, tq, tq), F32),
                        pltpu.VMEM((MLA_HEADS, tq), F32), pltpu.VMEM((MLA_HEADS, tq), F32)])
    return pl.pallas_call(
        functools.partial(_attn_kernel, tq=tq, npairs=len(pairs)),
        grid_spec=grid_spec,
        out_shape=jax.ShapeDtypeStruct((batch * seq, MLA_WIDTH), BF16),
        compiler_params=pltpu.CompilerParams(dimension_semantics=("parallel",), vmem_limit_bytes=VMEM_LIMIT),
        name="mla_attn",
    )(qi, kj, q, k, vt)


def _post_kernel(x_ref, oa_ref, ob_ref, sa_ref, sb_ref, woa_ref, wob_ref, wout_ref, lnffn_ref,
                 wg_ref, wu_ref, wd_ref, fnorm_ref, out_ref):
    tm = x_ref.shape[0]
    sub = math.gcd(tm, POST_SUB)
    subs = [dict(rows=slice(r, r + sub)) for r in range(0, tm, sub)]

    def merge(t):
        ya = _dot(oa_ref[t["rows"], :], woa_ref[...])
        yb = _dot(ob_ref[t["rows"], :], wob_ref[...])
        t["mix"] = (sa_ref[t["rows"], :].astype(F32) * ya + sb_ref[t["rows"], :].astype(F32) * yb).astype(BF16)

    def residual(t):
        t["h"] = x_ref[t["rows"], :] + _dot(t["mix"], wout_ref[...])
        t["u"] = _rms(t["h"], lnffn_ref[...]).astype(BF16)

    def hidden(t):
        g = _dot(t["u"], wg_ref[...])
        t["act"] = (g * jax.nn.sigmoid(g) * _dot(t["u"], wu_ref[...])).astype(BF16)

    def output(t):
        out_ref[t["rows"], :] = _rms(t["h"] + _dot(t["act"], wd_ref[...]), fnorm_ref[...])

    for stage in (merge, residual, hidden, output):
        for t in subs:
            stage(t)


def _post_call(x2, oa, ob, sa, sb, woa, wob, wout, lnffn, wg, wu, wd, fnorm, tm):
    t = x2.shape[0]
    row = lambda w: pl.BlockSpec((tm, w), lambda i: (i, 0))
    consts = (woa, wob, wout, lnffn, wg, wu, wd, fnorm)
    return pl.pallas_call(
        _post_kernel,
        grid=(t // tm,),
        in_specs=[row(D_MODEL), row(MLA_WIDTH), row(GLA_DV), row(D_MODEL), row(D_MODEL)]
                 + [_single_spec(c) for c in consts],
        out_specs=row(D_MODEL),
        out_shape=jax.ShapeDtypeStruct((t, D_MODEL), F32),
        compiler_params=pltpu.CompilerParams(dimension_semantics=("parallel",), vmem_limit_bytes=VMEM_LIMIT),
        name="post",
    )(x2, oa, ob, sa, sb, *consts)


def _repack_kernel(wt_ref, o_ref):
    offs = [0]
    for s in IN_SIZES:
        offs.append(offs[-1] + s)
    cols = wt_ref.shape[1]
    dst = 0
    for idx in (0, 1, 3, 4, 5, 7, 8, 9):
        n = IN_SIZES[idx]
        o_ref[dst:dst + n, :] = wt_ref[offs[idx]:offs[idx + 1], :].astype(BF16)
        dst += n
    o_ref[dst:dst + ROPE_LO, :] = jnp.zeros((ROPE_LO, cols), BF16)
    o_ref[dst + ROPE_LO:dst + GLR_LO, :] = wt_ref[offs[2]:offs[3], :].astype(BF16)
    o_ref[dst + GLR_LO:dst + GLR_LO + GLA_RANK, :] = wt_ref[offs[6]:offs[7], :].astype(BF16)
    o_ref[dst + GLR_LO + GLA_RANK:dst + LANES, :] = jnp.zeros((LANES - GLR_LO - GLA_RANK, cols), BF16)


def _repack_call(w_in_t):
    n, d = w_in_t.shape
    cols = _tile(d, REPACK_COLS)
    return pl.pallas_call(
        _repack_kernel,
        grid=(d // cols,),
        in_specs=[pl.BlockSpec((n, cols), lambda i: (0, i))],
        out_specs=pl.BlockSpec((D_IN_PACKED, cols), lambda i: (0, i)),
        out_shape=jax.ShapeDtypeStruct((D_IN_PACKED, d), BF16),
        compiler_params=pltpu.CompilerParams(dimension_semantics=("parallel",), vmem_limit_bytes=VMEM_LIMIT),
        name="repack_w_in",
    )(w_in_t)


def _pack_weights(w_in, mla_w_uq, mla_w_ukv, gla_w_gate2):
    winp = _repack_call(w_in.T)
    wuq = mla_w_uq.reshape(MLA_Q_RANK, MLA_HEADS, MLA_QK)
    wuqp = jnp.pad(wuq, ((0, 0), (0, 0), (0, HEAD_PAD - MLA_QK))).reshape(MLA_Q_RANK, MLA_HEADS * HEAD_PAD)
    wukv = mla_w_ukv.reshape(MLA_KV_RANK, MLA_HEADS, MLA_NOPE + MLA_V)
    wkp = jnp.pad(wukv[:, :, :MLA_NOPE], ((0, 0), (0, 0), (0, HEAD_PAD - MLA_NOPE)))
    wkp = wkp.reshape(MLA_KV_RANK, MLA_HEADS * HEAD_PAD)
    wvt = wukv[:, :, MLA_NOPE:].reshape(MLA_KV_RANK, MLA_WIDTH).T
    wg2p = jnp.pad(gla_w_gate2, ((GLR_LO, LANES - GLR_LO - GLA_RANK), (0, 0)))
    return winp, wuqp.astype(BF16), wkp.astype(BF16), wvt.astype(BF16), wg2p.astype(BF16)


def _rope_inv_freq():
    half = ROPE_HALF
    inv = 1.0 / (ROPE_THETA ** (jnp.arange(half, dtype=F32) / half))
    return inv.reshape(half, 1)


def _rope_placement():
    place = np.zeros((4 * ROPE_HALF, 3 * LANES), np.float32)
    for piece in range(2):
        for i in range(ROPE_HALF):
            cos_row = 2 * piece * ROPE_HALF + i
            sin_row = cos_row + ROPE_HALF
            place[cos_row, ROPE_LO + i] = 1.0
            place[cos_row, ROPE_LO + ROPE_HALF + i] = 1.0
            place[sin_row, LANES + ROPE_LO + i] = -1.0
            place[sin_row, 2 * LANES + ROPE_LO + ROPE_HALF + i] = 1.0
    return jnp.asarray(place, BF16)


def _tile(n, pref):
    while n % pref:
        pref //= 2
    return pref


def kernel(x, positions, ln_mix, w_in, mla_norm_cq, mla_w_uq, mla_norm_ckv, mla_w_ukv, mla_w_o, gla_w_gate2,
           gla_b_gate, gla_norm, gla_w_o, w_out, ln_ffn, ffn_w_gate, ffn_w_up, ffn_w_down, final_norm):
    batch, seq, d = x.shape
    assert d == D_MODEL and w_in.shape[0] == 1 and seq % GLA_CHUNK == 0
    t = batch * seq
    x2 = x.reshape(t, d)
    winp, wuqp, wkp, wvt, wg2p = _pack_weights(w_in[0], mla_w_uq[0], mla_w_ukv[0], gla_w_gate2[0])
    r = lambda a: a.reshape(1, -1)

    tq = _tile(seq, ATTN_TILE)
    tm = max(tq, _tile(seq, PROJ_TILE))
    pos3 = positions.reshape(t // tm, 1, tm).astype(F32)
    q, k, vt, sa, sb, ob = _proj_gla_call(
        x2, pos3, r(ln_mix[0]), winp, r(mla_norm_cq[0]), wuqp, r(mla_norm_ckv[0]), wkp, wvt, wg2p,
        r(gla_b_gate[0]), _rope_inv_freq(), _rope_placement(), r(gla_norm[0]), seq, tm, tq)
    oa = _attn_call(q, k, vt, batch, seq, tq)
    out = _post_call(x2, oa, ob, sa, sb, mla_w_o[0].astype(BF16), gla_w_o[0].astype(BF16),
                     w_out[0].astype(BF16), r(ln_ffn[0]), ffn_w_gate[0].astype(BF16),
                     ffn_w_up[0].astype(BF16), ffn_w_down[0].astype(BF16), r(final_norm), _tile(t, POST_TILE))
    return out.reshape(batch, seq, d)
```

```python
import functools
import math

import jax
import jax.numpy as jnp
import numpy as np
from jax import lax
from jax.experimental import pallas as pl
from jax.experimental.pallas import tpu as pltpu

F32 = jnp.float32
BF16 = jnp.bfloat16

D_MODEL = 1024
MLA_HEADS = 8
MLA_NOPE = 64
MLA_ROPE = 32
MLA_QK = MLA_NOPE + MLA_ROPE
MLA_V = 64
MLA_WIDTH = MLA_HEADS * MLA_V
MLA_Q_RANK = 384
MLA_KV_RANK = 256
ROPE_THETA = 10000.0
GLA_HEADS = 4
GLA_DV = 512
GLA_DK = 256
GLA_HK = GLA_DK // GLA_HEADS
GLA_HV = GLA_DV // GLA_HEADS
GLA_RANK = 16
GLA_TAU = 16.0
EPS = 1e-6
LOG2E = 1.4426950408889634
IN_SIZES = (MLA_Q_RANK, MLA_KV_RANK, MLA_ROPE, GLA_DK, GLA_DK, GLA_DV, GLA_RANK, GLA_DV, D_MODEL, D_MODEL)

LANES = 128
HEAD_PAD = LANES
ROPE_LO = MLA_NOPE
ROPE_HALF = MLA_ROPE // 2
GLR_LO = ROPE_LO + MLA_ROPE

OFF_CQ = 0
OFF_CKV = OFF_CQ + MLA_Q_RANK
OFF_GQ = OFF_CKV + MLA_KV_RANK
OFF_GK = OFF_GQ + GLA_DK
OFF_GV = OFF_GK + GLA_DK
OFF_OG = OFF_GV + GLA_DV
OFF_GA = OFF_OG + GLA_DV
OFF_GB = OFF_GA + D_MODEL
OFF_MISC = OFF_GB + D_MODEL
D_IN_PACKED = OFF_MISC + LANES

REPACK_COLS = 256
PROJ_TILE = 512
POST_TILE = 512
POST_SUB = 256
ATTN_TILE = 256
ATTN_UNROLL = 8
BF16_ROWS = 16
MLA_VA = MLA_V + BF16_ROWS
MASKED = -1e30
GLA_CHUNK = 64
GLA_SUB = 8
GLA_PAR = 4
VMEM_LIMIT = 56 * 1024 * 1024


def _dot(a, b):
    return jnp.dot(a, b, preferred_element_type=F32)


def _dot_nt(a, b):
    return lax.dot_general(a, b, (((1,), (1,)), ((), ())), preferred_element_type=F32)


def _dot_tn(a, b):
    return lax.dot_general(a, b, (((0,), (0,)), ((), ())), preferred_element_type=F32)


def _rms(x, w):
    return x * lax.rsqrt(jnp.mean(x * x, axis=-1, keepdims=True) + EPS) * w


def _single_spec(a):
    return pl.BlockSpec(a.shape, lambda *_: (0,) * a.ndim, pipeline_mode=pl.Buffered(1))


def _gla_constants():
    c_, s_ = GLA_CHUNK, GLA_SUB
    ri = lax.broadcasted_iota(jnp.int32, (c_, c_), 0)
    ci = lax.broadcasted_iota(jnp.int32, (c_, c_), 1)
    rl = lax.broadcasted_iota(jnp.int32, (c_, GLA_DK), 0)
    ll = lax.broadcasted_iota(jnp.int32, (c_, GLA_DK), 1)
    row_minus_col = rl - ll % GLA_HK
    di = lax.broadcasted_iota(jnp.int32, (GLA_DK, GLA_DK), 0) // GLA_HK
    dj = lax.broadcasted_iota(jnp.int32, (GLA_DK, GLA_DK), 1) // GLA_HK
    return dict(
        tri=jnp.where((ri // s_ == ci // s_) & (ci <= ri), 1.0, 0.0).astype(BF16),
        diag_key=jnp.where((row_minus_col >= 0) & (rl % s_ >= row_minus_col), row_minus_col, -1),
        head_of_lane=ll // GLA_HK,
        odd_head_lane=(lax.broadcasted_iota(jnp.int32, (c_, 2 * GLA_HK), 1) // GLA_HK) == 1,
        head_sum=jnp.where(di == dj, 1.0, 0.0).astype(BF16))


def _gla_slab_stages(base, par, consts, gq_ref, gk_ref, gv_ref, la_ref, og_ref, norm_w, o_ref, state_ref,
                     kpad_ref, lapad_ref):
    c_, s_ = GLA_CHUNK, GLA_SUB
    nsub = c_ // s_
    pair = 2 * GLA_HK
    tri, diag_key, head_of_lane = consts["tri"], consts["diag_key"], consts["head_of_lane"]
    odd_head_lane, head_sum = consts["odd_head_lane"], consts["head_sum"]
    chunks = [dict(rows=slice(base + c * c_, base + (c + 1) * c_), off=s_ + c * c_) for c in range(par)]

    def blocks(fn):
        return jnp.concatenate([fn(b) for b in range(nsub)], axis=0)

    def sub(t, b):
        return t[b * s_:(b + 1) * s_, :]

    def cumsums():
        kpad_ref[0:s_, :] = jnp.zeros((s_, GLA_DK), F32)
        lapad_ref[0:s_, :] = jnp.zeros((s_, GLA_DK), F32)
        kpad_ref[s_:, :] = gk_ref[base:base + par * c_, :].astype(F32)
        lapad_ref[s_:, :] = la_ref[base:base + par * c_, :]
        for ch in chunks:
            la = la_ref[ch["rows"], :]
            hi = la.astype(BF16)
            r1 = la - hi.astype(F32)
            mid = r1.astype(BF16)
            lo = (r1 - mid.astype(F32)).astype(BF16)
            ch["lsub"] = _dot(tri, hi) + _dot(tri, mid) + _dot(tri, lo)

    def operands():
        for ch in chunks:
            q = gq_ref[ch["rows"], :].astype(F32)
            k = gk_ref[ch["rows"], :].astype(F32)
            lsub = ch["lsub"]
            tot = [lsub[(b + 1) * s_ - 1:(b + 1) * s_, :] for b in range(nsub)]
            q_sub = q * jnp.exp2(lsub)
            k_sub = k * jnp.exp2(blocks(lambda b: jnp.broadcast_to(tot[b], (s_, GLA_DK))) - lsub)
            zero = jnp.zeros((s_, GLA_DK), F32)
            e = [jnp.exp2(t) for t in tot]

            def decayed(t, b, over):
                piece = sub(t, b)
                for i in over:
                    piece = piece * e[i]
                return piece

            ch["qd"] = blocks(lambda b: decayed(q_sub, b, range(b))).astype(BF16)
            kd = blocks(lambda b: decayed(k_sub, b, range(b + 1, nsub))).astype(BF16)
            ch["decay"] = jnp.exp2(sum(tot[1:], tot[0]))
            v = gv_ref[ch["rows"], :]
            ch["upd"] = sum(_dot_tn(v[:, h * GLA_HV:(h + 1) * GLA_HV], jnp.where(head_of_lane == h, kd, 0.0))
                            for h in range(GLA_HEADS))
            ch["q_lv"], ch["k_lv"] = [], []
            width = 1
            while width < nsub:
                for odd in range(1, nsub // width, 2):
                    q_lo, k_lo = odd * width, (odd - 1) * width
                    ch["q_lv"].append(blocks(lambda b: decayed(q_sub, b, range(q_lo, b))
                                             if q_lo <= b < q_lo + width else zero))
                    ch["k_lv"].append(blocks(lambda b: decayed(k_sub, b, range(b + 1, k_lo + width))
                                             if k_lo <= b < k_lo + width else zero))
                width *= 2
            decay = jnp.zeros((c_, GLA_DK), F32)
            prods = []
            for d in range(s_):
                if d > 0:
                    decay = decay + lapad_ref[ch["off"] - d + 1:ch["off"] - d + 1 + c_, :]
                k_sh = kpad_ref[ch["off"] - d:ch["off"] - d + c_, :]
                prods.append((q * k_sh * jnp.exp2(decay)).astype(BF16))
            ch["summed"] = _dot(jnp.concatenate(prods, axis=0), head_sum)

    def pair_scores():
        for ch in chunks:
            ch["attn_off"] = []
            for tile in range(GLA_HEADS // 2):
                lanes = slice(tile * pair, (tile + 1) * pair)

                def split(groups):
                    tiles = [g[:, lanes] for g in groups]
                    return tiles[0::2], [pltpu.roll(g, GLA_HK, axis=1) for g in tiles[1::2]]

                q_even, q_odd = split(ch["q_lv"])
                k_even, k_odd = split(ch["k_lv"])
                for own in (~odd_head_lane, odd_head_lane):
                    def packed(even, odd):
                        return jnp.concatenate(
                            [jnp.where(own, a, odd[n] if n < len(odd) else 0.0).astype(BF16)
                             for n, a in enumerate(even)], axis=1)

                    ch["attn_off"].append(_dot_nt(packed(q_even, q_odd), packed(k_even, k_odd)))

    def outputs():
        state = state_ref[...]
        for ch in chunks:
            ch["state_b"] = state.astype(BF16)
            state = state * ch["decay"] + ch["upd"]
        state_ref[...] = state
        for ch in chunks:
            attn_diag = jnp.zeros((c_, GLA_DK), F32)
            for d in range(s_):
                attn_diag = jnp.where(diag_key == d, ch["summed"][d * c_:(d + 1) * c_, :], attn_diag)
            v = gv_ref[ch["rows"], :]
            for h in range(GLA_HEADS):
                hv = slice(h * GLA_HV, (h + 1) * GLA_HV)
                attn = attn_diag[:, h * GLA_HK:(h + 1) * GLA_HK] + ch["attn_off"][h]
                o = _dot(attn.astype(BF16), v[:, hv])
                o = o + _dot_nt(jnp.where(head_of_lane == h, ch["qd"], 0.0), ch["state_b"])
                o = _rms(o, norm_w)
                g = og_ref[ch["rows"], hv].astype(F32)
                o_ref[ch["rows"], hv] = (o * (g * jax.nn.sigmoid(g))).astype(BF16)

    return [cumsums, operands, pair_scores, outputs]


def _proj_gla_kernel(x_ref, pos_ref, lnmix_ref, win_ref, ncq_ref, wuq_ref, nckv_ref, wk_ref, wvt_ref,
                     wg2_ref, bg_ref, invf_ref, place_ref, gnorm_ref,
                     q_ref, k_ref, vt_ref, sa_ref, sb_ref, ob_ref,
                     gq_ref, gk_ref, gv_ref, la_ref, og_ref, state_ref, kpad_ref, lapad_ref,
                     *, tkv, steps_per_seq, par):
    tm = x_ref.shape[0]

    @pl.when(pl.program_id(0) % steps_per_seq == 0)
    def _():
        state_ref[...] = jnp.zeros_like(state_ref)

    ub = _rms(x_ref[...], lnmix_ref[...]).astype(BF16)

    def seg(lo, width):
        return _dot_nt(ub, win_ref[lo:lo + width, :])

    misc = seg(OFF_MISC, LANES)
    xg = _dot(misc.astype(BF16), wg2_ref[...]) + bg_ref[...]
    la_ref[...] = (jnp.minimum(xg, 0.0) - jnp.log1p(jnp.exp(-jnp.abs(xg)))) * (LOG2E / GLA_TAU)
    gq_ref[...] = (seg(OFF_GQ, GLA_DK) * (GLA_HK ** -0.5)).astype(BF16)
    gk_ref[...] = seg(OFF_GK, GLA_DK).astype(BF16)
    gv_ref[...] = seg(OFF_GV, GLA_DV).astype(BF16)
    og_ref[...] = seg(OFF_OG, GLA_DV).astype(BF16)

    consts = _gla_constants()
    slab_rows = par * GLA_CHUNK
    scan = []
    for sl in range(tm // slab_rows):
        scan += _gla_slab_stages(sl * slab_rows, par, consts, gq_ref, gk_ref, gv_ref, la_ref, og_ref,
                                 gnorm_ref[...], ob_ref, state_ref, kpad_ref.at[sl], lapad_ref.at[sl])

    lane = lax.broadcasted_iota(jnp.int32, (1, LANES), 1)
    in_rope = (lane >= ROPE_LO) & (lane < ROPE_LO + MLA_ROPE)
    t = {}

    def latents():
        t["cq"] = seg(OFF_CQ, MLA_Q_RANK)
        t["ckv"] = seg(OFF_CKV, MLA_KV_RANK)

    def rope_tables():
        ang = invf_ref[...] * pos_ref[0]
        cs = jnp.concatenate([jnp.cos(ang), jnp.sin(ang)], axis=0)
        cs_hi = cs.astype(BF16)
        cs_lo = (cs - cs_hi.astype(F32)).astype(BF16)
        tabs = _dot_tn(jnp.concatenate([cs_hi, cs_lo], axis=0), place_ref[...])
        t["cos"] = tabs[:, :LANES] + jnp.where(in_rope, 0.0, 1.0)
        t["sin_x1"] = tabs[:, LANES:2 * LANES]
        t["sin_x2"] = tabs[:, 2 * LANES:]

    def gate_a():
        sa_ref[...] = jax.nn.sigmoid(seg(OFF_GA, D_MODEL)).astype(BF16)

    def gate_b():
        sb_ref[...] = jax.nn.sigmoid(seg(OFF_GB, D_MODEL)).astype(BF16)

    def latent_norms():
        t["cqn"] = _rms(t["cq"], ncq_ref[...]).astype(BF16)
        t["ckvn"] = _rms(t["ckv"], nckv_ref[...]).astype(BF16)

    def up_q():
        t["q"] = _dot(t["cqn"], wuq_ref[...]) * (MLA_QK ** -0.5 * LOG2E)

    def up_kv():
        t["k_nope"] = _dot(t["ckvn"], wk_ref[...])
        for c in range(tm // tkv):
            vt = _dot_nt(wvt_ref[...], t["ckvn"][c * tkv:(c + 1) * tkv, :]).astype(BF16)
            for h in range(MLA_HEADS):
                vt_ref[c, h * MLA_VA:h * MLA_VA + MLA_V, :] = vt[h * MLA_V:(h + 1) * MLA_V, :]
                vt_ref[c, h * MLA_VA + MLA_V:(h + 1) * MLA_VA, :] = jnp.ones((MLA_VA - MLA_V, tkv), BF16)

    def rope_store():
        def rope(a):
            return (a * t["cos"] + pltpu.roll(a, LANES - ROPE_HALF, axis=1) * t["sin_x1"]
                    + pltpu.roll(a, ROPE_HALF, axis=1) * t["sin_x2"])

        k_rope = rope(jnp.where(in_rope, misc, 0.0))
        for h in range(MLA_HEADS):
            sl = slice(h * HEAD_PAD, (h + 1) * HEAD_PAD)
            q_ref[:, sl] = rope(t["q"][:, sl]).astype(BF16)
            k_ref[:, sl] = (t["k_nope"][:, sl] + k_rope).astype(BF16)

    rest = [latents, gate_a, rope_tables, latent_norms, gate_b, up_q, up_kv, rope_store]
    for n in range(max(len(scan), len(rest))):
        if n < len(scan):
            scan[n]()
        if n < len(rest):
            rest[n]()


def _proj_gla_call(x2, pos3, lnmix, winp, ncq, wuqp, nckv, wkp, wvt, wg2p, bg, invf, place, gnorm, seq, tm, tkv):
    t = x2.shape[0]
    par = math.gcd(tm // GLA_CHUNK, GLA_PAR)
    nslab = tm // (par * GLA_CHUNK)
    row = lambda w: pl.BlockSpec((tm, w), lambda i: (i, 0))
    consts = (lnmix, winp, ncq, wuqp, nckv, wkp, wvt, wg2p, bg, invf, place, gnorm)
    out_widths = (MLA_HEADS * HEAD_PAD, MLA_HEADS * HEAD_PAD, None, D_MODEL, D_MODEL, GLA_DV)
    vt_spec = pl.BlockSpec((tm // tkv, MLA_HEADS * MLA_VA, tkv), lambda i: (i, 0, 0))
    vt_shape = jax.ShapeDtypeStruct((t // tkv, MLA_HEADS * MLA_VA, tkv), BF16)
    pad_rows = GLA_SUB + par * GLA_CHUNK
    return pl.pallas_call(
        functools.partial(_proj_gla_kernel, tkv=tkv, steps_per_seq=seq // tm, par=par),
        grid=(t // tm,),
        in_specs=[row(D_MODEL), pl.BlockSpec((1, 1, tm), lambda i: (i, 0, 0))] + [_single_spec(c) for c in consts],
        out_specs=[vt_spec if w is None else row(w) for w in out_widths],
        out_shape=[vt_shape if w is None else jax.ShapeDtypeStruct((t, w), BF16) for w in out_widths],
        scratch_shapes=[pltpu.VMEM((tm, GLA_DK), BF16), pltpu.VMEM((tm, GLA_DK), BF16),
                        pltpu.VMEM((tm, GLA_DV), BF16), pltpu.VMEM((tm, GLA_DK), F32),
                        pltpu.VMEM((tm, GLA_DV), BF16), pltpu.VMEM((GLA_HV, GLA_DK), F32),
                        pltpu.VMEM((nslab, pad_rows, GLA_DK), F32), pltpu.VMEM((nslab, pad_rows, GLA_DK), F32)],
        compiler_params=pltpu.CompilerParams(dimension_semantics=("arbitrary",), vmem_limit_bytes=VMEM_LIMIT),
        name="proj_gla",
    )(x2, pos3, *consts)


def _attn_kernel(qi_ref, kj_ref, q_ref, k_ref, vt_ref, o_ref, acc_ref, ot_ref, m_ref, s0_ref, s1_ref, smax0_ref,
                 smax1_ref, *, tq, npairs):
    nq = q_ref.shape[0] // tq
    key = lax.broadcasted_iota(jnp.int32, (tq, tq), 0)
    qry = lax.broadcasted_iota(jnp.int32, (tq, tq), 1)
    causal = key <= qry
    slots = ((s0_ref, smax0_ref), (s1_ref, smax1_ref))

    def scores(n, slot, heads, masked):
        s_ref, smax_ref = slots[slot]
        qi, kj = qi_ref[n], kj_ref[n]
        qrows = pl.ds(pl.multiple_of(qi * tq, tq), tq)
        krows = pl.ds(pl.multiple_of(kj * tq, tq), tq)
        for h in heads:
            hq = slice(h * HEAD_PAD, (h + 1) * HEAD_PAD)
            s = _dot_nt(k_ref[krows, hq], q_ref[qrows, hq])
            if masked:
                s = jnp.where(causal, s, MASKED)
            s_ref[h] = s
            smax_ref[h:h + 1, :] = jnp.max(s, axis=0, keepdims=True)

    def accumulate(n, slot, heads, first):
        s_ref, smax_ref = slots[slot]
        qi, kj = qi_ref[n], kj_ref[n]
        for h in heads:
            hv = slice(h * MLA_VA, (h + 1) * MLA_VA)
            m_new = smax_ref[h:h + 1, :]
            if not first:
                m = m_ref[qi, h:h + 1, :]
                m_new = jnp.maximum(m, m_new)
                alpha = jnp.exp2(m - m_new)
            p = jnp.exp2(s_ref[h] - m_new)
            m_ref[qi, h:h + 1, :] = m_new
            pv = _dot(vt_ref[kj, hv, :], p.astype(BF16))
            acc_ref[qi, hv, :] = pv if first else alpha * acc_ref[qi, hv, :] + pv

    all_heads = range(MLA_HEADS)

    def step(k, parity, diag_next, diag_now):
        for h in all_heads:
            scores(k + 1, 1 - parity, (h,), diag_next)
            accumulate(k, parity, (h,), diag_now)

    def steps(lo, hi, diag_next, diag_now):
        while (hi - lo) % ATTN_UNROLL:
            step(lo, lo % 2, diag_next, diag_now)
            lo += 1
        if hi > lo:
            def unrolled(t, carry):
                for u in range(ATTN_UNROLL):
                    step(lo + ATTN_UNROLL * t + u, (lo + u) % 2, diag_next, diag_now)
                return carry

            lax.fori_loop(0, (hi - lo) // ATTN_UNROLL, unrolled, 0)

    scores(0, 0, all_heads, True)
    steps(0, nq - 1, True, True)
    if npairs > nq:
        steps(nq - 1, nq, False, True)
        steps(nq, npairs - 1, False, False)
    accumulate(npairs - 1, (npairs - 1) % 2, all_heads, npairs == nq)

    for qt in range(nq):
        for h in range(MLA_HEADS):
            lo = h * MLA_VA
            inv_denominator = 1.0 / acc_ref[qt, lo + MLA_V:lo + MLA_V + 1, :]
            ot_ref[h * MLA_V:(h + 1) * MLA_V, :] = acc_ref[qt, lo:lo + MLA_V, :] * inv_denominator
        o_ref[qt * tq:(qt + 1) * tq, :] = ot_ref[...].T.astype(BF16)


def _attn_call(q, k, vt, batch, seq, tq):
    nq = seq // tq
    pairs = [(i, i) for i in range(nq)] + [(i, j) for i in range(nq) for j in range(i)]
    qi = jnp.asarray([p[0] for p in pairs], jnp.int32)
    kj = jnp.asarray([p[1] for p in pairs], jnp.int32)
    grid_spec = pltpu.PrefetchScalarGridSpec(
        num_scalar_prefetch=2,
        grid=(batch,),
        in_specs=[pl.BlockSpec((seq, MLA_HEADS * HEAD_PAD), lambda b, *_: (b, 0)),
                  pl.BlockSpec((seq, MLA_HEADS * HEAD_PAD), lambda b, *_: (b, 0)),
                  pl.BlockSpec((nq, MLA_HEADS * MLA_VA, tq), lambda b, *_: (b, 0, 0))],
        out_specs=pl.BlockSpec((seq, MLA_WIDTH), lambda b, *_: (b, 0)),
        scratch_shapes=[pltpu.VMEM((nq, MLA_HEADS * MLA_VA, tq), F32), pltpu.VMEM((MLA_WIDTH, tq), F32),
                        pltpu.VMEM((nq, MLA_HEADS, tq), F32),
                        pltpu.VMEM((MLA_HEADS, tq, tq), F32), pltpu.VMEM((MLA_HEADS, tq, tq), F32),
                        pltpu.VMEM((MLA_HEADS, tq), F32), pltpu.VMEM((MLA_HEADS, tq), F32)])
    return pl.pallas_call(
        functools.partial(_attn_kernel, tq=tq, npairs=len(pairs)),
        grid_spec=grid_spec,
        out_shape=jax.ShapeDtypeStruct((batch * seq, MLA_WIDTH), BF16),
        compiler_params=pltpu.CompilerParams(dimension_semantics=("parallel",), vmem_limit_bytes=VMEM_LIMIT),
        name="mla_attn",
    )(qi, kj, q, k, vt)


def _post_kernel(x_ref, oa_ref, ob_ref, sa_ref, sb_ref, woa_ref, wob_ref, wout_ref, lnffn_ref,
                 wg_ref, wu_ref, wd_ref, fnorm_ref, out_ref):
    tm = x_ref.shape[0]
    sub = math.gcd(tm, POST_SUB)
    subs = [dict(rows=slice(r, r + sub)) for r in range(0, tm, sub)]

    def merge(t):
        ya = _dot(oa_ref[t["rows"], :], woa_ref[...])
        yb = _dot(ob_ref[t["rows"], :], wob_ref[...])
        t["mix"] = (sa_ref[t["rows"], :].astype(F32) * ya + sb_ref[t["rows"], :].astype(F32) * yb).astype(BF16)

    def residual(t):
        t["h"] = x_ref[t["rows"], :] + _dot(t["mix"], wout_ref[...])
        t["u"] = _rms(t["h"], lnffn_ref[...]).astype(BF16)

    def hidden(t):
        g = _dot(t["u"], wg_ref[...])
        t["act"] = (g * jax.nn.sigmoid(g) * _dot(t["u"], wu_ref[...])).astype(BF16)

    def output(t):
        out_ref[t["rows"], :] = _rms(t["h"] + _dot(t["act"], wd_ref[...]), fnorm_ref[...])

    for stage in (merge, residual, hidden, output):
        for t in subs:
            stage(t)


def _post_call(x2, oa, ob, sa, sb, woa, wob, wout, lnffn, wg, wu, wd, fnorm, tm):
    t = x2.shape[0]
    row = lambda w: pl.BlockSpec((tm, w), lambda i: (i, 0))
    consts = (woa, wob, wout, lnffn, wg, wu, wd, fnorm)
    return pl.pallas_call(
        _post_kernel,
        grid=(t // tm,),
        in_specs=[row(D_MODEL), row(MLA_WIDTH), row(GLA_DV), row(D_MODEL), row(D_MODEL)]
                 + [_single_spec(c) for c in consts],
        out_specs=row(D_MODEL),
        out_shape=jax.ShapeDtypeStruct((t, D_MODEL), F32),
        compiler_params=pltpu.CompilerParams(dimension_semantics=("parallel",), vmem_limit_bytes=VMEM_LIMIT),
        name="post",
    )(x2, oa, ob, sa, sb, *consts)


def _repack_kernel(wt_ref, o_ref):
    offs = [0]
    for s in IN_SIZES:
        offs.append(offs[-1] + s)
    cols = wt_ref.shape[1]
    dst = 0
    for idx in (0, 1, 3, 4, 5, 7, 8, 9):
        n = IN_SIZES[idx]
        o_ref[dst:dst + n, :] = wt_ref[offs[idx]:offs[idx + 1], :].astype(BF16)
        dst += n
    o_ref[dst:dst + ROPE_LO, :] = jnp.zeros((ROPE_LO, cols), BF16)
    o_ref[dst + ROPE_LO:dst + GLR_LO, :] = wt_ref[offs[2]:offs[3], :].astype(BF16)
    o_ref[dst + GLR_LO:dst + GLR_LO + GLA_RANK, :] = wt_ref[offs[6]:offs[7], :].astype(BF16)
    o_ref[dst + GLR_LO + GLA_RANK:dst + LANES, :] = jnp.zeros((LANES - GLR_LO - GLA_RANK, cols), BF16)


def _repack_call(w_in_t):
    n, d = w_in_t.shape
    cols = _tile(d, REPACK_COLS)
    return pl.pallas_call(
        _repack_kernel,
        grid=(d // cols,),
        in_specs=[pl.BlockSpec((n, cols), lambda i: (0, i))],
        out_specs=pl.BlockSpec((D_IN_PACKED, cols), lambda i: (0, i)),
        out_shape=jax.ShapeDtypeStruct((D_IN_PACKED, d), BF16),
        compiler_params=pltpu.CompilerParams(dimension_semantics=("parallel",), vmem_limit_bytes=VMEM_LIMIT),
        name="repack_w_in",
    )(w_in_t)


def _pack_weights(w_in, mla_w_uq, mla_w_ukv, gla_w_gate2):
    winp = _repack_call(w_in.T)
    wuq = mla_w_uq.reshape(MLA_Q_RANK, MLA_HEADS, MLA_QK)
    wuqp = jnp.pad(wuq, ((0, 0), (0, 0), (0, HEAD_PAD - MLA_QK))).reshape(MLA_Q_RANK, MLA_HEADS * HEAD_PAD)
    wukv = mla_w_ukv.reshape(MLA_KV_RANK, MLA_HEADS, MLA_NOPE + MLA_V)
    wkp = jnp.pad(wukv[:, :, :MLA_NOPE], ((0, 0), (0, 0), (0, HEAD_PAD - MLA_NOPE)))
    wkp = wkp.reshape(MLA_KV_RANK, MLA_HEADS * HEAD_PAD)
    wvt = wukv[:, :, MLA_NOPE:].reshape(MLA_KV_RANK, MLA_WIDTH).T
    wg2p = jnp.pad(gla_w_gate2, ((GLR_LO, LANES - GLR_LO - GLA_RANK), (0, 0)))
    return winp, wuqp.astype(BF16), wkp.astype(BF16), wvt.astype(BF16), wg2p.astype(BF16)


def _rope_inv_freq():
    half = ROPE_HALF
    inv = 1.0 / (ROPE_THETA ** (jnp.arange(half, dtype=F32) / half))
    return inv.reshape(half, 1)


def _rope_placement():
    place = np.zeros((4 * ROPE_HALF, 3 * LANES), np.float32)
    for piece in range(2):
        for i in range(ROPE_HALF):
            cos_row = 2 * piece * ROPE_HALF + i
            sin_row = cos_row + ROPE_HALF
            place[cos_row, ROPE_LO + i] = 1.0
            place[cos_row, ROPE_LO + ROPE_HALF + i] = 1.0
            place[sin_row, LANES + ROPE_LO + i] = -1.0
            place[sin_row, 2 * LANES + ROPE_LO + ROPE_HALF + i] = 1.0
    return jnp.asarray(place, BF16)


def _tile(n, pref):
    while n % pref:
        pref //= 2
    return pref


def kernel(x, positions, ln_mix, w_in, mla_norm_cq, mla_w_uq, mla_norm_ckv, mla_w_ukv, mla_w_o, gla_w_gate2,
           gla_b_gate, gla_norm, gla_w_o, w_out, ln_ffn, ffn_w_gate, ffn_w_up, ffn_w_down, final_norm):
    batch, seq, d = x.shape
    assert d == D_MODEL and w_in.shape[0] == 1 and seq % GLA_CHUNK == 0
    t = batch * seq
    x2 = x.reshape(t, d)
    winp, wuqp, wkp, wvt, wg2p = _pack_weights(w_in[0], mla_w_uq[0], mla_w_ukv[0], gla_w_gate2[0])
    r = lambda a: a.reshape(1, -1)

    tq = _tile(seq, ATTN_TILE)
    tm = max(tq, _tile(seq, PROJ_TILE))
    pos3 = positions.reshape(t // tm, 1, tm).astype(F32)
    q, k, vt, sa, sb, ob = _proj_gla_call(
        x2, pos3, r(ln_mix[0]), winp, r(mla_norm_cq[0]), wuqp, r(mla_norm_ckv[0]), wkp, wvt, wg2p,
        r(gla_b_gate[0]), _rope_inv_freq(), _rope_placement(), r(gla_norm[0]), seq, tm, tq)
    oa = _attn_call(q, k, vt, batch, seq, tq)
    out = _post_call(x2, oa, ob, sa, sb, mla_w_o[0].astype(BF16), gla_w_o[0].astype(BF16),
                     w_out[0].astype(BF16), r(ln_ffn[0]), ffn_w_gate[0].astype(BF16),
                     ffn_w_up[0].astype(BF16), ffn_w_down[0].astype(BF16), r(final_norm), _tile(t, POST_TILE))
    return out.reshape(batch, seq, d)
```

```python
import functools
import math

import jax
import jax.numpy as jnp
import numpy as np
from jax import lax
from jax.experimental import pallas as pl
from jax.experimental.pallas import tpu as pltpu

F32 = jnp.float32
BF16 = jnp.bfloat16

D_MODEL = 1024
MLA_HEADS = 8
MLA_NOPE = 64
MLA_ROPE = 32
MLA_QK = MLA_NOPE + MLA_ROPE
MLA_V = 64
MLA_WIDTH = MLA_HEADS * MLA_V
MLA_Q_RANK = 384
MLA_KV_RANK = 256
ROPE_THETA = 10000.0
GLA_HEADS = 4
GLA_DV = 512
GLA_DK = 256
GLA_HK = GLA_DK // GLA_HEADS
GLA_HV = GLA_DV // GLA_HEADS
GLA_RANK = 16
GLA_TAU = 16.0
EPS = 1e-6
LOG2E = 1.4426950408889634
IN_SIZES = (MLA_Q_RANK, MLA_KV_RANK, MLA_ROPE, GLA_DK, GLA_DK, GLA_DV, GLA_RANK, GLA_DV, D_MODEL, D_MODEL)

LANES = 128
HEAD_PAD = LANES
ROPE_LO = MLA_NOPE
ROPE_HALF = MLA_ROPE // 2
GLR_LO = ROPE_LO + MLA_ROPE

OFF_CQ = 0
OFF_CKV = OFF_CQ + MLA_Q_RANK
OFF_GQ = OFF_CKV + MLA_KV_RANK
OFF_GK = OFF_GQ + GLA_DK
OFF_GV = OFF_GK + GLA_DK
OFF_OG = OFF_GV + GLA_DV
OFF_GA = OFF_OG + GLA_DV
OFF_GB = OFF_GA + D_MODEL
OFF_MISC = OFF_GB + D_MODEL
D_IN_PACKED = OFF_MISC + LANES

REPACK_COLS = 256
PROJ_TILE = 512
POST_TILE = 512
POST_SUB = 256
ATTN_TILE = 256
ATTN_UNROLL = 8
BF16_ROWS = 16
MLA_VA = MLA_V + BF16_ROWS
MASKED = -1e30
GLA_CHUNK = 64
GLA_SUB = 8
GLA_PAR = 4
VMEM_LIMIT = 56 * 1024 * 1024


def _dot(a, b):
    return jnp.dot(a, b, preferred_element_type=F32)


def _dot_nt(a, b):
    return lax.dot_general(a, b, (((1,), (1,)), ((), ())), preferred_element_type=F32)


def _dot_tn(a, b):
    return lax.dot_general(a, b, (((0,), (0,)), ((), ())), preferred_element_type=F32)


def _rms(x, w):
    return x * lax.rsqrt(jnp.mean(x * x, axis=-1, keepdims=True) + EPS) * w


def _single_spec(a):
    return pl.BlockSpec(a.shape, lambda *_: (0,) * a.ndim, pipeline_mode=pl.Buffered(1))


def _gla_constants():
    c_, s_ = GLA_CHUNK, GLA_SUB
    ri = lax.broadcasted_iota(jnp.int32, (c_, c_), 0)
    ci = lax.broadcasted_iota(jnp.int32, (c_, c_), 1)
    rl = lax.broadcasted_iota(jnp.int32, (c_, GLA_DK), 0)
    ll = lax.broadcasted_iota(jnp.int32, (c_, GLA_DK), 1)
    row_minus_col = rl - ll % GLA_HK
    di = lax.broadcasted_iota(jnp.int32, (GLA_DK, GLA_DK), 0) // GLA_HK
    dj = lax.broadcasted_iota(jnp.int32, (GLA_DK, GLA_DK), 1) // GLA_HK
    return dict(
        tri=jnp.where((ri // s_ == ci // s_) & (ci <= ri), 1.0, 0.0).astype(BF16),
        diag_key=jnp.where((row_minus_col >= 0) & (rl % s_ >= row_minus_col), row_minus_col, -1),
        head_of_lane=ll // GLA_HK,
        odd_head_lane=(lax.broadcasted_iota(jnp.int32, (c_, 2 * GLA_HK), 1) // GLA_HK) == 1,
        head_sum=jnp.where(di == dj, 1.0, 0.0).astype(BF16))


def _gla_slab_stages(base, par, consts, gq_ref, gk_ref, gv_ref, la_ref, og_ref, norm_w, o_ref, state_ref,
                     kpad_ref, lapad_ref):
    c_, s_ = GLA_CHUNK, GLA_SUB
    nsub = c_ // s_
    pair = 2 * GLA_HK
    tri, diag_key, head_of_lane = consts["tri"], consts["diag_key"], consts["head_of_lane"]
    odd_head_lane, head_sum = consts["odd_head_lane"], consts["head_sum"]
    chunks = [dict(rows=slice(base + c * c_, base + (c + 1) * c_), off=s_ + c * c_) for c in range(par)]

    def blocks(fn):
        return jnp.concatenate([fn(b) for b in range(nsub)], axis=0)

    def sub(t, b):
        return t[b * s_:(b + 1) * s_, :]

    def cumsums():
        kpad_ref[0:s_, :] = jnp.zeros((s_, GLA_DK), F32)
        lapad_ref[0:s_, :] = jnp.zeros((s_, GLA_DK), F32)
        kpad_ref[s_:, :] = gk_ref[base:base + par * c_, :].astype(F32)
        lapad_ref[s_:, :] = la_ref[base:base + par * c_, :]
        for ch in chunks:
            la = la_ref[ch["rows"], :]
            hi = la.astype(BF16)
            r1 = la - hi.astype(F32)
            mid = r1.astype(BF16)
            lo = (r1 - mid.astype(F32)).astype(BF16)
            ch["lsub"] = _dot(tri, hi) + _dot(tri, mid) + _dot(tri, lo)

    def operands():
        for ch in chunks:
            q = gq_ref[ch["rows"], :].astype(F32)
            k = gk_ref[ch["rows"], :].astype(F32)
            lsub = ch["lsub"]
            tot = [lsub[(b + 1) * s_ - 1:(b + 1) * s_, :] for b in range(nsub)]
            q_sub = q * jnp.exp2(lsub)
            k_sub = k * jnp.exp2(blocks(lambda b: jnp.broadcast_to(tot[b], (s_, GLA_DK))) - lsub)
            zero = jnp.zeros((s_, GLA_DK), F32)
            e = [jnp.exp2(t) for t in tot]

            def decayed(t, b, over):
                piece = sub(t, b)
                for i in over:
                    piece = piece * e[i]
                return piece

            ch["qd"] = blocks(lambda b: decayed(q_sub, b, range(b))).astype(BF16)
            kd = blocks(lambda b: decayed(k_sub, b, range(b + 1, nsub))).astype(BF16)
            ch["decay"] = jnp.exp2(sum(tot[1:], tot[0]))
            v = gv_ref[ch["rows"], :]
            ch["upd"] = sum(_dot_tn(v[:, h * GLA_HV:(h + 1) * GLA_HV], jnp.where(head_of_lane == h, kd, 0.0))
                            for h in range(GLA_HEADS))
            ch["q_lv"], ch["k_lv"] = [], []
            width = 1
            while width < nsub:
                for odd in range(1, nsub // width, 2):
                    q_lo, k_lo = odd * width, (odd - 1) * width
                    ch["q_lv"].append(blocks(lambda b: decayed(q_sub, b, range(q_lo, b))
                                             if q_lo <= b < q_lo + width else zero))
                    ch["k_lv"].append(blocks(lambda b: decayed(k_sub, b, range(b + 1, k_lo + width))
                                             if k_lo <= b < k_lo + width else zero))
                width *= 2
            decay = jnp.zeros((c_, GLA_DK), F32)
            prods = []
            for d in range(s_):
                if d > 0:
                    decay = decay + lapad_ref[ch["off"] - d + 1:ch["off"] - d + 1 + c_, :]
                k_sh = kpad_ref[ch["off"] - d:ch["off"] - d + c_, :]
                prods.append((q * k_sh * jnp.exp2(decay)).astype(BF16))
            ch["summed"] = _dot(jnp.concatenate(prods, axis=0), head_sum)

    def pair_scores():
        for ch in chunks:
            ch["attn_off"] = []
            for tile in range(GLA_HEADS // 2):
                lanes = slice(tile * pair, (tile + 1) * pair)

                def split(groups):
                    tiles = [g[:, lanes] for g in groups]
                    return tiles[0::2], [pltpu.roll(g, GLA_HK, axis=1) for g in tiles[1::2]]

                q_even, q_odd = split(ch["q_lv"])
                k_even, k_odd = split(ch["k_lv"])
                for own in (~odd_head_lane, odd_head_lane):
                    def packed(even, odd):
                        return jnp.concatenate(
                            [jnp.where(own, a, odd[n] if n < len(odd) else 0.0).astype(BF16)
                             for n, a in enumerate(even)], axis=1)

                    ch["attn_off"].append(_dot_nt(packed(q_even, q_odd), packed(k_even, k_odd)))

    def outputs():
        state = state_ref[...]
        for ch in chunks:
            ch["state_b"] = state.astype(BF16)
            state = state * ch["decay"] + ch["upd"]
        state_ref[...] = state
        for ch in chunks:
            attn_diag = jnp.zeros((c_, GLA_DK), F32)
            for d in range(s_):
                attn_diag = jnp.where(diag_key == d, ch["summed"][d * c_:(d + 1) * c_, :], attn_diag)
            v = gv_ref[ch["rows"], :]
            for h in range(GLA_HEADS):
                hv = slice(h * GLA_HV, (h + 1) * GLA_HV)
                attn = attn_diag[:, h * GLA_HK:(h + 1) * GLA_HK] + ch["attn_off"][h]
                o = _dot(attn.astype(BF16), v[:, hv])
                o = o + _dot_nt(jnp.where(head_of_lane == h, ch["qd"], 0.0), ch["state_b"])
                o = _rms(o, norm_w)
                g = og_ref[ch["rows"], hv].astype(F32)
                o_ref[ch["rows"], hv] = (o * (g * jax.nn.sigmoid(g))).astype(BF16)

    return [cumsums, operands, pair_scores, outputs]


def _proj_gla_kernel(x_ref, pos_ref, lnmix_ref, win_ref, ncq_ref, wuq_ref, nckv_ref, wk_ref, wvt_ref,
                     wg2_ref, bg_ref, invf_ref, place_ref, gnorm_ref,
                     q_ref, k_ref, vt_ref, sa_ref, sb_ref, ob_ref,
                     gq_ref, gk_ref, gv_ref, la_ref, og_ref, state_ref, kpad_ref, lapad_ref,
                     *, tkv, steps_per_seq, par):
    tm = x_ref.shape[0]

    @pl.when(pl.program_id(0) % steps_per_seq == 0)
    def _():
        state_ref[...] = jnp.zeros_like(state_ref)

    ub = _rms(x_ref[...], lnmix_ref[...]).astype(BF16)

    def seg(lo, width):
        return _dot_nt(ub, win_ref[lo:lo + width, :])

    misc = seg(OFF_MISC, LANES)
    xg = _dot(misc.astype(BF16), wg2_ref[...]) + bg_ref[...]
    la_ref[...] = (jnp.minimum(xg, 0.0) - jnp.log1p(jnp.exp(-jnp.abs(xg)))) * (LOG2E / GLA_TAU)
    gq_ref[...] = (seg(OFF_GQ, GLA_DK) * (GLA_HK ** -0.5)).astype(BF16)
    gk_ref[...] = seg(OFF_GK, GLA_DK).astype(BF16)
    gv_ref[...] = seg(OFF_GV, GLA_DV).astype(BF16)
    og_ref[...] = seg(OFF_OG, GLA_DV).astype(BF16)

    consts = _gla_constants()
    slab_rows = par * GLA_CHUNK
    scan = []
    for sl in range(tm // slab_rows):
        scan += _gla_slab_stages(sl * slab_rows, par, consts, gq_ref, gk_ref, gv_ref, la_ref, og_ref,
                                 gnorm_ref[...], ob_ref, state_ref, kpad_ref.at[sl], lapad_ref.at[sl])

    lane = lax.broadcasted_iota(jnp.int32, (1, LANES), 1)
    in_rope = (lane >= ROPE_LO) & (lane < ROPE_LO + MLA_ROPE)
    t = {}

    def latents():
        t["cq"] = seg(OFF_CQ, MLA_Q_RANK)
        t["ckv"] = seg(OFF_CKV, MLA_KV_RANK)

    def rope_tables():
        ang = invf_ref[...] * pos_ref[0]
        cs = jnp.concatenate([jnp.cos(ang), jnp.sin(ang)], axis=0)
        cs_hi = cs.astype(BF16)
        cs_lo = (cs - cs_hi.astype(F32)).astype(BF16)
        tabs = _dot_tn(jnp.concatenate([cs_hi, cs_lo], axis=0), place_ref[...])
        t["cos"] = tabs[:, :LANES] + jnp.where(in_rope, 0.0, 1.0)
        t["sin_x1"] = tabs[:, LANES:2 * LANES]
        t["sin_x2"] = tabs[:, 2 * LANES:]

    def gate_a():
        sa_ref[...] = jax.nn.sigmoid(seg(OFF_GA, D_MODEL)).astype(BF16)

    def gate_b():
        sb_ref[...] = jax.nn.sigmoid(seg(OFF_GB, D_MODEL)).astype(BF16)

    def latent_norms():
        t["cqn"] = _rms(t["cq"], ncq_ref[...]).astype(BF16)
        t["ckvn"] = _rms(t["ckv"], nckv_ref[...]).astype(BF16)

    def up_q():
        t["q"] = _dot(t["cqn"], wuq_ref[...]) * (MLA_QK ** -0.5 * LOG2E)

    def up_kv():
        t["k_nope"] = _dot(t["ckvn"], wk_ref[...])
        for c in range(tm // tkv):
            vt = _dot_nt(wvt_ref[...], t["ckvn"][c * tkv:(c + 1) * tkv, :]).astype(BF16)
            for h in range(MLA_HEADS):
                vt_ref[c, h * MLA_VA:h * MLA_VA + MLA_V, :] = vt[h * MLA_V:(h + 1) * MLA_V, :]
                vt_ref[c, h * MLA_VA + MLA_V:(h + 1) * MLA_VA, :] = jnp.ones((MLA_VA - MLA_V, tkv), BF16)

    def rope_store():
        def rope(a):
            return (a * t["cos"] + pltpu.roll(a, LANES - ROPE_HALF, axis=1) * t["sin_x1"]
                    + pltpu.roll(a, ROPE_HALF, axis=1) * t["sin_x2"])

        k_rope = rope(jnp.where(in_rope, misc, 0.0))
        for h in range(MLA_HEADS):
            sl = slice(h * HEAD_PAD, (h + 1) * HEAD_PAD)
            q_ref[:, sl] = rope(t["q"][:, sl]).astype(BF16)
            k_ref[:, sl] = (t["k_nope"][:, sl] + k_rope).astype(BF16)

    rest = [latents, gate_a, rope_tables, latent_norms, gate_b, up_q, up_kv, rope_store]
    for n in range(max(len(scan), len(rest))):
        if n < len(scan):
            scan[n]()
        if n < len(rest):
            rest[n]()


def _proj_gla_call(x2, pos3, lnmix, winp, ncq, wuqp, nckv, wkp, wvt, wg2p, bg, invf, place, gnorm, seq, tm, tkv):
    t = x2.shape[0]
    par = math.gcd(tm // GLA_CHUNK, GLA_PAR)
    nslab = tm // (par * GLA_CHUNK)
    row = lambda w: pl.BlockSpec((tm, w), lambda i: (i, 0))
    consts = (lnmix, winp, ncq, wuqp, nckv, wkp, wvt, wg2p, bg, invf, place, gnorm)
    out_widths = (MLA_HEADS * HEAD_PAD, MLA_HEADS * HEAD_PAD, None, D_MODEL, D_MODEL, GLA_DV)
    vt_spec = pl.BlockSpec((tm // tkv, MLA_HEADS * MLA_VA, tkv), lambda i: (i, 0, 0))
    vt_shape = jax.ShapeDtypeStruct((t // tkv, MLA_HEADS * MLA_VA, tkv), BF16)
    pad_rows = GLA_SUB + par * GLA_CHUNK
    return pl.pallas_call(
        functools.partial(_proj_gla_kernel, tkv=tkv, steps_per_seq=seq // tm, par=par),
        grid=(t // tm,),
        in_specs=[row(D_MODEL), pl.BlockSpec((1, 1, tm), lambda i: (i, 0, 0))] + [_single_spec(c) for c in consts],
        out_specs=[vt_spec if w is None else row(w) for w in out_widths],
        out_shape=[vt_shape if w is None else jax.ShapeDtypeStruct((t, w), BF16) for w in out_widths],
        scratch_shapes=[pltpu.VMEM((tm, GLA_DK), BF16), pltpu.VMEM((tm, GLA_DK), BF16),
                        pltpu.VMEM((tm, GLA_DV), BF16), pltpu.VMEM((tm, GLA_DK), F32),
                        pltpu.VMEM((tm, GLA_DV), BF16), pltpu.VMEM((GLA_HV, GLA_DK), F32),
                        pltpu.VMEM((nslab, pad_rows, GLA_DK), F32), pltpu.VMEM((nslab, pad_rows, GLA_DK), F32)],
        compiler_params=pltpu.CompilerParams(dimension_semantics=("arbitrary",), vmem_limit_bytes=VMEM_LIMIT),
        name="proj_gla",
    )(x2, pos3, *consts)


def _attn_kernel(qi_ref, kj_ref, q_ref, k_ref, vt_ref, *refs, tq, npairs, ncast):
    cast_in, o_ref, cast_out = refs[:ncast], refs[ncast], refs[ncast + 1:2 * ncast + 1]
    acc_ref, ot_ref, m_ref, s0_ref, s1_ref, smax0_ref, smax1_ref = refs[2 * ncast + 1:]
    for src, dst in zip(cast_in, cast_out):
        dst[...] = src[...].astype(BF16)
    nq = q_ref.shape[0] // tq
    key = lax.broadcasted_iota(jnp.int32, (tq, tq), 0)
    qry = lax.broadcasted_iota(jnp.int32, (tq, tq), 1)
    causal = key <= qry
    slots = ((s0_ref, smax0_ref), (s1_ref, smax1_ref))

    def scores(n, slot, heads, masked):
        s_ref, smax_ref = slots[slot]
        qi, kj = qi_ref[n], kj_ref[n]
        qrows = pl.ds(pl.multiple_of(qi * tq, tq), tq)
        krows = pl.ds(pl.multiple_of(kj * tq, tq), tq)
        for h in heads:
            hq = slice(h * HEAD_PAD, (h + 1) * HEAD_PAD)
            s = _dot_nt(k_ref[krows, hq], q_ref[qrows, hq])
            if masked:
                s = jnp.where(causal, s, MASKED)
            s_ref[h] = s
            smax_ref[h:h + 1, :] = jnp.max(s, axis=0, keepdims=True)

    def accumulate(n, slot, heads, first):
        s_ref, smax_ref = slots[slot]
        qi, kj = qi_ref[n], kj_ref[n]
        for h in heads:
            hv = slice(h * MLA_VA, (h + 1) * MLA_VA)
            m_new = smax_ref[h:h + 1, :]
            if not first:
                m = m_ref[qi, h:h + 1, :]
                m_new = jnp.maximum(m, m_new)
                alpha = jnp.exp2(m - m_new)
            p = jnp.exp2(s_ref[h] - m_new)
            m_ref[qi, h:h + 1, :] = m_new
            pv = _dot(vt_ref[kj, hv, :], p.astype(BF16))
            acc_ref[qi, hv, :] = pv if first else alpha * acc_ref[qi, hv, :] + pv

    all_heads = range(MLA_HEADS)

    def step(k, parity, diag_next, diag_now):
        for h in all_heads:
            scores(k + 1, 1 - parity, (h,), diag_next)
            accumulate(k, parity, (h,), diag_now)

    def steps(lo, hi, diag_next, diag_now):
        while (hi - lo) % ATTN_UNROLL:
            step(lo, lo % 2, diag_next, diag_now)
            lo += 1
        if hi > lo:
            def unrolled(t, carry):
                for u in range(ATTN_UNROLL):
                    step(lo + ATTN_UNROLL * t + u, (lo + u) % 2, diag_next, diag_now)
                return carry

            lax.fori_loop(0, (hi - lo) // ATTN_UNROLL, unrolled, 0)

    scores(0, 0, all_heads, True)
    steps(0, nq - 1, True, True)
    if npairs > nq:
        steps(nq - 1, nq, False, True)
        steps(nq, npairs - 1, False, False)
    accumulate(npairs - 1, (npairs - 1) % 2, all_heads, npairs == nq)

    for qt in range(nq):
        for h in range(MLA_HEADS):
            lo = h * MLA_VA
            inv_denominator = 1.0 / acc_ref[qt, lo + MLA_V:lo + MLA_V + 1, :]
            ot_ref[h * MLA_V:(h + 1) * MLA_V, :] = acc_ref[qt, lo:lo + MLA_V, :] * inv_denominator
        o_ref[qt * tq:(qt + 1) * tq, :] = ot_ref[...].T.astype(BF16)


def _attn_call(q, k, vt, casts, batch, seq, tq):
    nq = seq // tq
    for a in casts:
        assert a.shape[0] % (batch * BF16_ROWS) == 0, a.shape
    cast_specs = [pl.BlockSpec((a.shape[0] // batch, a.shape[1]), lambda b, *_: (b, 0)) for a in casts]
    pairs = [(i, i) for i in range(nq)] + [(i, j) for i in range(nq) for j in range(i)]
    qi = jnp.asarray([p[0] for p in pairs], jnp.int32)
    kj = jnp.asarray([p[1] for p in pairs], jnp.int32)
    grid_spec = pltpu.PrefetchScalarGridSpec(
        num_scalar_prefetch=2,
        grid=(batch,),
        in_specs=[pl.BlockSpec((seq, MLA_HEADS * HEAD_PAD), lambda b, *_: (b, 0)),
                  pl.BlockSpec((seq, MLA_HEADS * HEAD_PAD), lambda b, *_: (b, 0)),
                  pl.BlockSpec((nq, MLA_HEADS * MLA_VA, tq), lambda b, *_: (b, 0, 0))] + cast_specs,
        out_specs=[pl.BlockSpec((seq, MLA_WIDTH), lambda b, *_: (b, 0))] + cast_specs,
        scratch_shapes=[pltpu.VMEM((nq, MLA_HEADS * MLA_VA, tq), F32), pltpu.VMEM((MLA_WIDTH, tq), F32),
                        pltpu.VMEM((nq, MLA_HEADS, tq), F32),
                        pltpu.VMEM((MLA_HEADS, tq, tq), F32), pltpu.VMEM((MLA_HEADS, tq, tq), F32),
                        pltpu.VMEM((MLA_HEADS, tq), F32), pltpu.VMEM((MLA_HEADS, tq), F32)])
    return pl.pallas_call(
        functools.partial(_attn_kernel, tq=tq, npairs=len(pairs), ncast=len(casts)),
        grid_spec=grid_spec,
        out_shape=[jax.ShapeDtypeStruct((batch * seq, MLA_WIDTH), BF16)]
                  + [jax.ShapeDtypeStruct(a.shape, BF16) for a in casts],
        compiler_params=pltpu.CompilerParams(dimension_semantics=("parallel",), vmem_limit_bytes=VMEM_LIMIT),
        name="mla_attn",
    )(qi, kj, q, k, vt, *casts)


def _post_kernel(x_ref, oa_ref, ob_ref, sa_ref, sb_ref, woa_ref, wob_ref, wout_ref, lnffn_ref,
                 wg_ref, wu_ref, wd_ref, fnorm_ref, out_ref):
    tm = x_ref.shape[0]
    sub = math.gcd(tm, POST_SUB)
    subs = [dict(rows=slice(r, r + sub)) for r in range(0, tm, sub)]

    def merge(t):
        ya = _dot(oa_ref[t["rows"], :], woa_ref[...])
        yb = _dot(ob_ref[t["rows"], :], wob_ref[...])
        t["mix"] = (sa_ref[t["rows"], :].astype(F32) * ya + sb_ref[t["rows"], :].astype(F32) * yb).astype(BF16)

    def residual(t):
        t["h"] = x_ref[t["rows"], :] + _dot(t["mix"], wout_ref[...])
        t["u"] = _rms(t["h"], lnffn_ref[...]).astype(BF16)

    def hidden(t):
        g = _dot(t["u"], wg_ref[...])
        t["act"] = (g * jax.nn.sigmoid(g) * _dot(t["u"], wu_ref[...])).astype(BF16)

    def output(t):
        out_ref[t["rows"], :] = _rms(t["h"] + _dot(t["act"], wd_ref[...]), fnorm_ref[...])

    for stage in (merge, residual, hidden, output):
        for t in subs:
            stage(t)


def _post_call(x2, oa, ob, sa, sb, woa, wob, wout, lnffn, wg, wu, wd, fnorm, tm):
    t = x2.shape[0]
    row = lambda w: pl.BlockSpec((tm, w), lambda i: (i, 0))
    consts = (woa, wob, wout, lnffn, wg, wu, wd, fnorm)
    return pl.pallas_call(
        _post_kernel,
        grid=(t // tm,),
        in_specs=[row(D_MODEL), row(MLA_WIDTH), row(GLA_DV), row(D_MODEL), row(D_MODEL)]
                 + [_single_spec(c) for c in consts],
        out_specs=row(D_MODEL),
        out_shape=jax.ShapeDtypeStruct((t, D_MODEL), F32),
        compiler_params=pltpu.CompilerParams(dimension_semantics=("parallel",), vmem_limit_bytes=VMEM_LIMIT),
        name="post",
    )(x2, oa, ob, sa, sb, *consts)


def _repack_kernel(wt_ref, o_ref):
    offs = [0]
    for s in IN_SIZES:
        offs.append(offs[-1] + s)
    cols = wt_ref.shape[1]
    dst = 0
    for idx in (0, 1, 3, 4, 5, 7, 8, 9):
        n = IN_SIZES[idx]
        o_ref[dst:dst + n, :] = wt_ref[offs[idx]:offs[idx + 1], :].astype(BF16)
        dst += n
    o_ref[dst:dst + ROPE_LO, :] = jnp.zeros((ROPE_LO, cols), BF16)
    o_ref[dst + ROPE_LO:dst + GLR_LO, :] = wt_ref[offs[2]:offs[3], :].astype(BF16)
    o_ref[dst + GLR_LO:dst + GLR_LO + GLA_RANK, :] = wt_ref[offs[6]:offs[7], :].astype(BF16)
    o_ref[dst + GLR_LO + GLA_RANK:dst + LANES, :] = jnp.zeros((LANES - GLR_LO - GLA_RANK, cols), BF16)


def _repack_call(w_in_t):
    n, d = w_in_t.shape
    cols = _tile(d, REPACK_COLS)
    return pl.pallas_call(
        _repack_kernel,
        grid=(d // cols,),
        in_specs=[pl.BlockSpec((n, cols), lambda i: (0, i))],
        out_specs=pl.BlockSpec((D_IN_PACKED, cols), lambda i: (0, i)),
        out_shape=jax.ShapeDtypeStruct((D_IN_PACKED, d), BF16),
        compiler_params=pltpu.CompilerParams(dimension_semantics=("parallel",), vmem_limit_bytes=VMEM_LIMIT),
        name="repack_w_in",
    )(w_in_t)


def _pack_weights(w_in, mla_w_uq, mla_w_ukv, gla_w_gate2):
    winp = _repack_call(w_in.T)
    wuq = mla_w_uq.reshape(MLA_Q_RANK, MLA_HEADS, MLA_QK)
    wuqp = jnp.pad(wuq, ((0, 0), (0, 0), (0, HEAD_PAD - MLA_QK))).reshape(MLA_Q_RANK, MLA_HEADS * HEAD_PAD)
    wukv = mla_w_ukv.reshape(MLA_KV_RANK, MLA_HEADS, MLA_NOPE + MLA_V)
    wkp = jnp.pad(wukv[:, :, :MLA_NOPE], ((0, 0), (0, 0), (0, HEAD_PAD - MLA_NOPE)))
    wkp = wkp.reshape(MLA_KV_RANK, MLA_HEADS * HEAD_PAD)
    wvt = wukv[:, :, MLA_NOPE:].reshape(MLA_KV_RANK, MLA_WIDTH).T
    wg2p = jnp.pad(gla_w_gate2, ((GLR_LO, LANES - GLR_LO - GLA_RANK), (0, 0)))
    return winp, wuqp.astype(BF16), wkp.astype(BF16), wvt.astype(BF16), wg2p.astype(BF16)


def _rope_inv_freq():
    half = ROPE_HALF
    inv = 1.0 / (ROPE_THETA ** (jnp.arange(half, dtype=F32) / half))
    return inv.reshape(half, 1)


def _rope_placement():
    place = np.zeros((4 * ROPE_HALF, 3 * LANES), np.float32)
    for piece in range(2):
        for i in range(ROPE_HALF):
            cos_row = 2 * piece * ROPE_HALF + i
            sin_row = cos_row + ROPE_HALF
            place[cos_row, ROPE_LO + i] = 1.0
            place[cos_row, ROPE_LO + ROPE_HALF + i] = 1.0
            place[sin_row, LANES + ROPE_LO + i] = -1.0
            place[sin_row, 2 * LANES + ROPE_LO + ROPE_HALF + i] = 1.0
    return jnp.asarray(place, BF16)


def _tile(n, pref):
    while n % pref:
        pref //= 2
    return pref


def kernel(x, positions, ln_mix, w_in, mla_norm_cq, mla_w_uq, mla_norm_ckv, mla_w_ukv, mla_w_o, gla_w_gate2,
           gla_b_gate, gla_norm, gla_w_o, w_out, ln_ffn, ffn_w_gate, ffn_w_up, ffn_w_down, final_norm):
    batch, seq, d = x.shape
    assert d == D_MODEL and w_in.shape[0] == 1 and seq % GLA_CHUNK == 0
    t = batch * seq
    x2 = x.reshape(t, d)
    winp, wuqp, wkp, wvt, wg2p = _pack_weights(w_in[0], mla_w_uq[0], mla_w_ukv[0], gla_w_gate2[0])
    r = lambda a: a.reshape(1, -1)

    tq = _tile(seq, ATTN_TILE)
    tm = max(tq, _tile(seq, PROJ_TILE))
    pos3 = positions.reshape(t // tm, 1, tm).astype(F32)
    q, k, vt, sa, sb, ob = _proj_gla_call(
        x2, pos3, r(ln_mix[0]), winp, r(mla_norm_cq[0]), wuqp, r(mla_norm_ckv[0]), wkp, wvt, wg2p,
        r(gla_b_gate[0]), _rope_inv_freq(), _rope_placement(), r(gla_norm[0]), seq, tm, tq)
    post_weights = (mla_w_o[0], gla_w_o[0], w_out[0], ffn_w_gate[0], ffn_w_up[0], ffn_w_down[0])
    oa, woa, wob, wout, wg, wu, wd = _attn_call(q, k, vt, post_weights, batch, seq, tq)
    out = _post_call(x2, oa, ob, sa, sb, woa, wob, wout, r(ln_ffn[0]), wg, wu, wd, r(final_norm),
                     _tile(t, POST_TILE))
    return out.reshape(batch, seq, d)
```

```python
import functools
import math

import jax
import jax.numpy as jnp
import numpy as np
from jax import lax
from jax.experimental import pallas as pl
from jax.experimental.pallas import tpu as pltpu

F32 = jnp.float32
BF16 = jnp.bfloat16

D_MODEL = 1024
MLA_HEADS = 8
MLA_NOPE = 64
MLA_ROPE = 32
MLA_QK = MLA_NOPE + MLA_ROPE
MLA_V = 64
MLA_WIDTH = MLA_HEADS * MLA_V
MLA_Q_RANK = 384
MLA_KV_RANK = 256
ROPE_THETA = 10000.0
GLA_HEADS = 4
GLA_DV = 512
GLA_DK = 256
GLA_HK = GLA_DK // GLA_HEADS
GLA_HV = GLA_DV // GLA_HEADS
GLA_RANK = 16
GLA_TAU = 16.0
EPS = 1e-6
LOG2E = 1.4426950408889634
IN_SIZES = (MLA_Q_RANK, MLA_KV_RANK, MLA_ROPE, GLA_DK, GLA_DK, GLA_DV, GLA_RANK, GLA_DV, D_MODEL, D_MODEL)

LANES = 128
HEAD_PAD = LANES
ROPE_LO = MLA_NOPE
ROPE_HALF = MLA_ROPE // 2
GLR_LO = ROPE_LO + MLA_ROPE

OFF_CQ = 0
OFF_CKV = OFF_CQ + MLA_Q_RANK
OFF_GQ = OFF_CKV + MLA_KV_RANK
OFF_GK = OFF_GQ + GLA_DK
OFF_GV = OFF_GK + GLA_DK
OFF_OG = OFF_GV + GLA_DV
OFF_GA = OFF_OG + GLA_DV
OFF_GB = OFF_GA + D_MODEL
OFF_MISC = OFF_GB + D_MODEL
D_IN_PACKED = OFF_MISC + LANES

REPACK_COLS = 256
PROJ_TILE = 512
POST_TILE = 512
POST_SUB = 256
ATTN_TILE = 256
ATTN_UNROLL = 12
BF16_ROWS = 16
MLA_VA = MLA_V + BF16_ROWS
MASKED = -1e30
GLA_CHUNK = 64
GLA_SUB = 8
GLA_PAR = 4
VMEM_LIMIT = 56 * 1024 * 1024


def _dot(a, b):
    return jnp.dot(a, b, preferred_element_type=F32)


def _dot_nt(a, b):
    return lax.dot_general(a, b, (((1,), (1,)), ((), ())), preferred_element_type=F32)


def _dot_tn(a, b):
    return lax.dot_general(a, b, (((0,), (0,)), ((), ())), preferred_element_type=F32)


def _rms(x, w):
    return x * lax.rsqrt(jnp.mean(x * x, axis=-1, keepdims=True) + EPS) * w


def _single_spec(a):
    return pl.BlockSpec(a.shape, lambda *_: (0,) * a.ndim, pipeline_mode=pl.Buffered(1))


def _gla_constants():
    c_, s_ = GLA_CHUNK, GLA_SUB
    ri = lax.broadcasted_iota(jnp.int32, (c_, c_), 0)
    ci = lax.broadcasted_iota(jnp.int32, (c_, c_), 1)
    rl = lax.broadcasted_iota(jnp.int32, (c_, GLA_DK), 0)
    ll = lax.broadcasted_iota(jnp.int32, (c_, GLA_DK), 1)
    row_minus_col = rl - ll % GLA_HK
    di = lax.broadcasted_iota(jnp.int32, (GLA_DK, GLA_DK), 0) // GLA_HK
    dj = lax.broadcasted_iota(jnp.int32, (GLA_DK, GLA_DK), 1) // GLA_HK
    return dict(
        tri=jnp.where((ri // s_ == ci // s_) & (ci <= ri), 1.0, 0.0).astype(BF16),
        diag_key=jnp.where((row_minus_col >= 0) & (rl % s_ >= row_minus_col), row_minus_col, -1),
        head_of_lane=ll // GLA_HK,
        odd_head_lane=(lax.broadcasted_iota(jnp.int32, (c_, 2 * GLA_HK), 1) // GLA_HK) == 1,
        head_sum=jnp.where(di == dj, 1.0, 0.0).astype(BF16))


def _gla_slab_stages(base, par, consts, gq_ref, gk_ref, gv_ref, la_ref, og_ref, norm_w, o_ref, state_ref,
                     kpad_ref, lapad_ref):
    c_, s_ = GLA_CHUNK, GLA_SUB
    nsub = c_ // s_
    pair = 2 * GLA_HK
    tri, diag_key, head_of_lane = consts["tri"], consts["diag_key"], consts["head_of_lane"]
    odd_head_lane, head_sum = consts["odd_head_lane"], consts["head_sum"]
    chunks = [dict(rows=slice(base + c * c_, base + (c + 1) * c_), off=s_ + c * c_) for c in range(par)]

    def blocks(fn):
        return jnp.concatenate([fn(b) for b in range(nsub)], axis=0)

    def sub(t, b):
        return t[b * s_:(b + 1) * s_, :]

    def cumsums():
        kpad_ref[0:s_, :] = jnp.zeros((s_, GLA_DK), F32)
        lapad_ref[0:s_, :] = jnp.zeros((s_, GLA_DK), F32)
        kpad_ref[s_:, :] = gk_ref[base:base + par * c_, :].astype(F32)
        lapad_ref[s_:, :] = la_ref[base:base + par * c_, :]
        for ch in chunks:
            la = la_ref[ch["rows"], :]
            hi = la.astype(BF16)
            r1 = la - hi.astype(F32)
            mid = r1.astype(BF16)
            lo = (r1 - mid.astype(F32)).astype(BF16)
            ch["lsub"] = _dot(tri, hi) + _dot(tri, mid) + _dot(tri, lo)

    def operands():
        for ch in chunks:
            q = gq_ref[ch["rows"], :].astype(F32)
            k = gk_ref[ch["rows"], :].astype(F32)
            lsub = ch["lsub"]
            tot = [lsub[(b + 1) * s_ - 1:(b + 1) * s_, :] for b in range(nsub)]
            q_sub = q * jnp.exp2(lsub)
            k_sub = k * jnp.exp2(blocks(lambda b: jnp.broadcast_to(tot[b], (s_, GLA_DK))) - lsub)
            zero = jnp.zeros((s_, GLA_DK), F32)
            e = [jnp.exp2(t) for t in tot]

            def decayed(t, b, over):
                piece = sub(t, b)
                for i in over:
                    piece = piece * e[i]
                return piece

            ch["qd"] = blocks(lambda b: decayed(q_sub, b, range(b))).astype(BF16)
            kd = blocks(lambda b: decayed(k_sub, b, range(b + 1, nsub))).astype(BF16)
            ch["decay"] = jnp.exp2(sum(tot[1:], tot[0]))
            v = gv_ref[ch["rows"], :]
            ch["upd"] = sum(_dot_tn(v[:, h * GLA_HV:(h + 1) * GLA_HV], jnp.where(head_of_lane == h, kd, 0.0))
                            for h in range(GLA_HEADS))
            ch["q_lv"], ch["k_lv"] = [], []
            width = 1
            while width < nsub:
                for odd in range(1, nsub // width, 2):
                    q_lo, k_lo = odd * width, (odd - 1) * width
                    ch["q_lv"].append(blocks(lambda b: decayed(q_sub, b, range(q_lo, b))
                                             if q_lo <= b < q_lo + width else zero))
                    ch["k_lv"].append(blocks(lambda b: decayed(k_sub, b, range(b + 1, k_lo + width))
                                             if k_lo <= b < k_lo + width else zero))
                width *= 2
            decay = jnp.zeros((c_, GLA_DK), F32)
            prods = []
            for d in range(s_):
                if d > 0:
                    decay = decay + lapad_ref[ch["off"] - d + 1:ch["off"] - d + 1 + c_, :]
                k_sh = kpad_ref[ch["off"] - d:ch["off"] - d + c_, :]
                prods.append((q * k_sh * jnp.exp2(decay)).astype(BF16))
            ch["summed"] = _dot(jnp.concatenate(prods, axis=0), head_sum)

    def pair_scores():
        for ch in chunks:
            ch["attn_off"] = []
            for tile in range(GLA_HEADS // 2):
                lanes = slice(tile * pair, (tile + 1) * pair)

                def split(groups):
                    tiles = [g[:, lanes] for g in groups]
                    return tiles[0::2], [pltpu.roll(g, GLA_HK, axis=1) for g in tiles[1::2]]

                q_even, q_odd = split(ch["q_lv"])
                k_even, k_odd = split(ch["k_lv"])
                for own in (~odd_head_lane, odd_head_lane):
                    def packed(even, odd):
                        return jnp.concatenate(
                            [jnp.where(own, a, odd[n] if n < len(odd) else 0.0).astype(BF16)
                             for n, a in enumerate(even)], axis=1)

                    ch["attn_off"].append(_dot_nt(packed(q_even, q_odd), packed(k_even, k_odd)))

    def outputs():
        state = state_ref[...]
        for ch in chunks:
            ch["state_b"] = state.astype(BF16)
            state = state * ch["decay"] + ch["upd"]
        state_ref[...] = state
        for ch in chunks:
            attn_diag = jnp.zeros((c_, GLA_DK), F32)
            for d in range(s_):
                attn_diag = jnp.where(diag_key == d, ch["summed"][d * c_:(d + 1) * c_, :], attn_diag)
            v = gv_ref[ch["rows"], :]
            for h in range(GLA_HEADS):
                hv = slice(h * GLA_HV, (h + 1) * GLA_HV)
                attn = attn_diag[:, h * GLA_HK:(h + 1) * GLA_HK] + ch["attn_off"][h]
                o = _dot(attn.astype(BF16), v[:, hv])
                o = o + _dot_nt(jnp.where(head_of_lane == h, ch["qd"], 0.0), ch["state_b"])
                o = _rms(o, norm_w)
                g = og_ref[ch["rows"], hv].astype(F32)
                o_ref[ch["rows"], hv] = (o * (g * jax.nn.sigmoid(g))).astype(BF16)

    return [cumsums, operands, pair_scores, outputs]


def _proj_gla_kernel(x_ref, pos_ref, lnmix_ref, win_ref, ncq_ref, wuq_ref, nckv_ref, wk_ref, wvt_ref,
                     wg2_ref, bg_ref, invf_ref, place_ref, gnorm_ref,
                     q_ref, k_ref, vt_ref, sa_ref, sb_ref, ob_ref,
                     gq_ref, gk_ref, gv_ref, la_ref, og_ref, state_ref, kpad_ref, lapad_ref,
                     *, tkv, steps_per_seq, par):
    tm = x_ref.shape[0]

    @pl.when(pl.program_id(0) % steps_per_seq == 0)
    def _():
        state_ref[...] = jnp.zeros_like(state_ref)

    ub = _rms(x_ref[...], lnmix_ref[...]).astype(BF16)

    def seg(lo, width):
        return _dot_nt(ub, win_ref[lo:lo + width, :])

    misc = seg(OFF_MISC, LANES)
    xg = _dot(misc.astype(BF16), wg2_ref[...]) + bg_ref[...]
    la_ref[...] = (jnp.minimum(xg, 0.0) - jnp.log1p(jnp.exp(-jnp.abs(xg)))) * (LOG2E / GLA_TAU)
    gq_ref[...] = (seg(OFF_GQ, GLA_DK) * (GLA_HK ** -0.5)).astype(BF16)
    gk_ref[...] = seg(OFF_GK, GLA_DK).astype(BF16)
    gv_ref[...] = seg(OFF_GV, GLA_DV).astype(BF16)
    og_ref[...] = seg(OFF_OG, GLA_DV).astype(BF16)

    consts = _gla_constants()
    slab_rows = par * GLA_CHUNK
    scan = []
    for sl in range(tm // slab_rows):
        scan += _gla_slab_stages(sl * slab_rows, par, consts, gq_ref, gk_ref, gv_ref, la_ref, og_ref,
                                 gnorm_ref[...], ob_ref, state_ref, kpad_ref.at[sl], lapad_ref.at[sl])

    lane = lax.broadcasted_iota(jnp.int32, (1, LANES), 1)
    in_rope = (lane >= ROPE_LO) & (lane < ROPE_LO + MLA_ROPE)
    t = {}

    def latents():
        t["cq"] = seg(OFF_CQ, MLA_Q_RANK)
        t["ckv"] = seg(OFF_CKV, MLA_KV_RANK)

    def rope_tables():
        ang = invf_ref[...] * pos_ref[0]
        cs = jnp.concatenate([jnp.cos(ang), jnp.sin(ang)], axis=0)
        cs_hi = cs.astype(BF16)
        cs_lo = (cs - cs_hi.astype(F32)).astype(BF16)
        tabs = _dot_tn(jnp.concatenate([cs_hi, cs_lo], axis=0), place_ref[...])
        t["cos"] = tabs[:, :LANES] + jnp.where(in_rope, 0.0, 1.0)
        t["sin_x1"] = tabs[:, LANES:2 * LANES]
        t["sin_x2"] = tabs[:, 2 * LANES:]

    def gate_a():
        sa_ref[...] = jax.nn.sigmoid(seg(OFF_GA, D_MODEL)).astype(BF16)

    def gate_b():
        sb_ref[...] = jax.nn.sigmoid(seg(OFF_GB, D_MODEL)).astype(BF16)

    def latent_norms():
        t["cqn"] = _rms(t["cq"], ncq_ref[...]).astype(BF16)
        t["ckvn"] = _rms(t["ckv"], nckv_ref[...]).astype(BF16)

    def up_q():
        t["q"] = _dot(t["cqn"], wuq_ref[...]) * (MLA_QK ** -0.5 * LOG2E)

    def up_kv():
        t["k_nope"] = _dot(t["ckvn"], wk_ref[...])
        for c in range(tm // tkv):
            vt = _dot_nt(wvt_ref[...], t["ckvn"][c * tkv:(c + 1) * tkv, :]).astype(BF16)
            for h in range(MLA_HEADS):
                vt_ref[c, h * MLA_VA:h * MLA_VA + MLA_V, :] = vt[h * MLA_V:(h + 1) * MLA_V, :]
                vt_ref[c, h * MLA_VA + MLA_V:(h + 1) * MLA_VA, :] = jnp.ones((MLA_VA - MLA_V, tkv), BF16)

    def rope_store():
        def rope(a):
            return (a * t["cos"] + pltpu.roll(a, LANES - ROPE_HALF, axis=1) * t["sin_x1"]
                    + pltpu.roll(a, ROPE_HALF, axis=1) * t["sin_x2"])

        k_rope = rope(jnp.where(in_rope, misc, 0.0))
        for h in range(MLA_HEADS):
            sl = slice(h * HEAD_PAD, (h + 1) * HEAD_PAD)
            q_ref[:, sl] = rope(t["q"][:, sl]).astype(BF16)
            k_ref[:, sl] = (t["k_nope"][:, sl] + k_rope).astype(BF16)

    rest = [latents, gate_a, rope_tables, latent_norms, gate_b, up_q, up_kv, rope_store]
    for n in range(max(len(scan), len(rest))):
        if n < len(scan):
            scan[n]()
        if n < len(rest):
            rest[n]()


def _proj_gla_call(x2, pos3, lnmix, winp, ncq, wuqp, nckv, wkp, wvt, wg2p, bg, invf, place, gnorm, seq, tm, tkv):
    t = x2.shape[0]
    par = math.gcd(tm // GLA_CHUNK, GLA_PAR)
    nslab = tm // (par * GLA_CHUNK)
    row = lambda w: pl.BlockSpec((tm, w), lambda i: (i, 0))
    consts = (lnmix, winp, ncq, wuqp, nckv, wkp, wvt, wg2p, bg, invf, place, gnorm)
    out_widths = (MLA_HEADS * HEAD_PAD, MLA_HEADS * HEAD_PAD, None, D_MODEL, D_MODEL, GLA_DV)
    vt_spec = pl.BlockSpec((tm // tkv, MLA_HEADS * MLA_VA, tkv), lambda i: (i, 0, 0))
    vt_shape = jax.ShapeDtypeStruct((t // tkv, MLA_HEADS * MLA_VA, tkv), BF16)
    pad_rows = GLA_SUB + par * GLA_CHUNK
    return pl.pallas_call(
        functools.partial(_proj_gla_kernel, tkv=tkv, steps_per_seq=seq // tm, par=par),
        grid=(t // tm,),
        in_specs=[row(D_MODEL), pl.BlockSpec((1, 1, tm), lambda i: (i, 0, 0))] + [_single_spec(c) for c in consts],
        out_specs=[vt_spec if w is None else row(w) for w in out_widths],
        out_shape=[vt_shape if w is None else jax.ShapeDtypeStruct((t, w), BF16) for w in out_widths],
        scratch_shapes=[pltpu.VMEM((tm, GLA_DK), BF16), pltpu.VMEM((tm, GLA_DK), BF16),
                        pltpu.VMEM((tm, GLA_DV), BF16), pltpu.VMEM((tm, GLA_DK), F32),
                        pltpu.VMEM((tm, GLA_DV), BF16), pltpu.VMEM((GLA_HV, GLA_DK), F32),
                        pltpu.VMEM((nslab, pad_rows, GLA_DK), F32), pltpu.VMEM((nslab, pad_rows, GLA_DK), F32)],
        compiler_params=pltpu.CompilerParams(dimension_semantics=("arbitrary",), vmem_limit_bytes=VMEM_LIMIT),
        name="proj_gla",
    )(x2, pos3, *consts)


def _attn_kernel(qi_ref, kj_ref, q_ref, k_ref, vt_ref, *refs, tq, npairs, ncast):
    cast_in, o_ref, cast_out = refs[:ncast], refs[ncast], refs[ncast + 1:2 * ncast + 1]
    acc_ref, ot_ref, m_ref, s0_ref, s1_ref, smax0_ref, smax1_ref = refs[2 * ncast + 1:]
    for src, dst in zip(cast_in, cast_out):
        dst[...] = src[...].astype(BF16)
    nq = q_ref.shape[0] // tq
    key = lax.broadcasted_iota(jnp.int32, (tq, tq), 0)
    qry = lax.broadcasted_iota(jnp.int32, (tq, tq), 1)
    causal = key <= qry
    slots = ((s0_ref, smax0_ref), (s1_ref, smax1_ref))

    def scores(n, slot, heads, masked):
        s_ref, smax_ref = slots[slot]
        qi, kj = qi_ref[n], kj_ref[n]
        qrows = pl.ds(pl.multiple_of(qi * tq, tq), tq)
        krows = pl.ds(pl.multiple_of(kj * tq, tq), tq)
        for h in heads:
            hq = slice(h * HEAD_PAD, (h + 1) * HEAD_PAD)
            s = _dot_nt(k_ref[krows, hq], q_ref[qrows, hq])
            if masked:
                s = jnp.where(causal, s, MASKED)
            s_ref[h] = s
            smax_ref[h:h + 1, :] = jnp.max(s, axis=0, keepdims=True)

    def accumulate(n, slot, heads, first):
        s_ref, smax_ref = slots[slot]
        qi, kj = qi_ref[n], kj_ref[n]
        for h in heads:
            hv = slice(h * MLA_VA, (h + 1) * MLA_VA)
            m_new = smax_ref[h:h + 1, :]
            if not first:
                m = m_ref[qi, h:h + 1, :]
                m_new = jnp.maximum(m, m_new)
                alpha = jnp.exp2(m - m_new)
            p = jnp.exp2(s_ref[h] - m_new)
            m_ref[qi, h:h + 1, :] = m_new
            pv = _dot(vt_ref[kj, hv, :], p.astype(BF16))
            acc_ref[qi, hv, :] = pv if first else alpha * acc_ref[qi, hv, :] + pv

    all_heads = range(MLA_HEADS)

    def step(k, parity, diag_next, diag_now):
        for h in all_heads:
            scores(k + 1, 1 - parity, (h,), diag_next)
            accumulate(k, parity, (h,), diag_now)

    def steps(lo, hi, diag_next, diag_now):
        while (hi - lo) % ATTN_UNROLL:
            step(lo, lo % 2, diag_next, diag_now)
            lo += 1
        if hi > lo:
            def unrolled(t, carry):
                for u in range(ATTN_UNROLL):
                    step(lo + ATTN_UNROLL * t + u, (lo + u) % 2, diag_next, diag_now)
                return carry

            lax.fori_loop(0, (hi - lo) // ATTN_UNROLL, unrolled, 0)

    scores(0, 0, all_heads, True)
    steps(0, nq - 1, True, True)
    if npairs > nq:
        steps(nq - 1, nq, False, True)
        steps(nq, npairs - 1, False, False)
    accumulate(npairs - 1, (npairs - 1) % 2, all_heads, npairs == nq)

    for qt in range(nq):
        for h in range(MLA_HEADS):
            lo = h * MLA_VA
            inv_denominator = 1.0 / acc_ref[qt, lo + MLA_V:lo + MLA_V + 1, :]
            ot_ref[h * MLA_V:(h + 1) * MLA_V, :] = acc_ref[qt, lo:lo + MLA_V, :] * inv_denominator
        o_ref[qt * tq:(qt + 1) * tq, :] = ot_ref[...].T.astype(BF16)


def _attn_call(q, k, vt, casts, batch, seq, tq):
    nq = seq // tq
    for a in casts:
        assert a.shape[0] % (batch * BF16_ROWS) == 0, a.shape
    cast_specs = [pl.BlockSpec((a.shape[0] // batch, a.shape[1]), lambda b, *_: (b, 0)) for a in casts]
    pairs = [(i, i) for i in range(nq)] + [(i, j) for i in range(nq) for j in range(i)]
    qi = jnp.asarray([p[0] for p in pairs], jnp.int32)
    kj = jnp.asarray([p[1] for p in pairs], jnp.int32)
    grid_spec = pltpu.PrefetchScalarGridSpec(
        num_scalar_prefetch=2,
        grid=(batch,),
        in_specs=[pl.BlockSpec((seq, MLA_HEADS * HEAD_PAD), lambda b, *_: (b, 0)),
                  pl.BlockSpec((seq, MLA_HEADS * HEAD_PAD), lambda b, *_: (b, 0)),
                  pl.BlockSpec((nq, MLA_HEADS * MLA_VA, tq), lambda b, *_: (b, 0, 0))] + cast_specs,
        out_specs=[pl.BlockSpec((seq, MLA_WIDTH), lambda b, *_: (b, 0))] + cast_specs,
        scratch_shapes=[pltpu.VMEM((nq, MLA_HEADS * MLA_VA, tq), F32), pltpu.VMEM((MLA_WIDTH, tq), F32),
                        pltpu.VMEM((nq, MLA_HEADS, tq), F32),
                        pltpu.VMEM((MLA_HEADS, tq, tq), F32), pltpu.VMEM((MLA_HEADS, tq, tq), F32),
                        pltpu.VMEM((MLA_HEADS, tq), F32), pltpu.VMEM((MLA_HEADS, tq), F32)])
    return pl.pallas_call(
        functools.partial(_attn_kernel, tq=tq, npairs=len(pairs), ncast=len(casts)),
        grid_spec=grid_spec,
        out_shape=[jax.ShapeDtypeStruct((batch * seq, MLA_WIDTH), BF16)]
                  + [jax.ShapeDtypeStruct(a.shape, BF16) for a in casts],
        compiler_params=pltpu.CompilerParams(dimension_semantics=("parallel",), vmem_limit_bytes=VMEM_LIMIT),
        name="mla_attn",
    )(qi, kj, q, k, vt, *casts)


def _post_kernel(x_ref, oa_ref, ob_ref, sa_ref, sb_ref, woa_ref, wob_ref, wout_ref, lnffn_ref,
                 wg_ref, wu_ref, wd_ref, fnorm_ref, out_ref):
    tm = x_ref.shape[0]
    sub = math.gcd(tm, POST_SUB)
    subs = [dict(rows=slice(r, r + sub)) for r in range(0, tm, sub)]

    def merge(t):
        ya = _dot(oa_ref[t["rows"], :], woa_ref[...])
        yb = _dot(ob_ref[t["rows"], :], wob_ref[...])
        t["mix"] = (sa_ref[t["rows"], :].astype(F32) * ya + sb_ref[t["rows"], :].astype(F32) * yb).astype(BF16)

    def residual(t):
        t["h"] = x_ref[t["rows"], :] + _dot(t["mix"], wout_ref[...])
        t["u"] = _rms(t["h"], lnffn_ref[...]).astype(BF16)

    def hidden(t):
        g = _dot(t["u"], wg_ref[...])
        t["act"] = (g * jax.nn.sigmoid(g) * _dot(t["u"], wu_ref[...])).astype(BF16)

    def output(t):
        out_ref[t["rows"], :] = _rms(t["h"] + _dot(t["act"], wd_ref[...]), fnorm_ref[...])

    for stage in (merge, residual, hidden, output):
        for t in subs:
            stage(t)


def _post_call(x2, oa, ob, sa, sb, woa, wob, wout, lnffn, wg, wu, wd, fnorm, tm):
    t = x2.shape[0]
    row = lambda w: pl.BlockSpec((tm, w), lambda i: (i, 0))
    consts = (woa, wob, wout, lnffn, wg, wu, wd, fnorm)
    return pl.pallas_call(
        _post_kernel,
        grid=(t // tm,),
        in_specs=[row(D_MODEL), row(MLA_WIDTH), row(GLA_DV), row(D_MODEL), row(D_MODEL)]
                 + [_single_spec(c) for c in consts],
        out_specs=row(D_MODEL),
        out_shape=jax.ShapeDtypeStruct((t, D_MODEL), F32),
        compiler_params=pltpu.CompilerParams(dimension_semantics=("parallel",), vmem_limit_bytes=VMEM_LIMIT),
        name="post",
    )(x2, oa, ob, sa, sb, *consts)


def _repack_kernel(wt_ref, o_ref):
    offs = [0]
    for s in IN_SIZES:
        offs.append(offs[-1] + s)
    cols = wt_ref.shape[1]
    dst = 0
    for idx in (0, 1, 3, 4, 5, 7, 8, 9):
        n = IN_SIZES[idx]
        o_ref[dst:dst + n, :] = wt_ref[offs[idx]:offs[idx + 1], :].astype(BF16)
        dst += n
    o_ref[dst:dst + ROPE_LO, :] = jnp.zeros((ROPE_LO, cols), BF16)
    o_ref[dst + ROPE_LO:dst + GLR_LO, :] = wt_ref[offs[2]:offs[3], :].astype(BF16)
    o_ref[dst + GLR_LO:dst + GLR_LO + GLA_RANK, :] = wt_ref[offs[6]:offs[7], :].astype(BF16)
    o_ref[dst + GLR_LO + GLA_RANK:dst + LANES, :] = jnp.zeros((LANES - GLR_LO - GLA_RANK, cols), BF16)


def _repack_call(w_in_t):
    n, d = w_in_t.shape
    cols = _tile(d, REPACK_COLS)
    return pl.pallas_call(
        _repack_kernel,
        grid=(d // cols,),
        in_specs=[pl.BlockSpec((n, cols), lambda i: (0, i))],
        out_specs=pl.BlockSpec((D_IN_PACKED, cols), lambda i: (0, i)),
        out_shape=jax.ShapeDtypeStruct((D_IN_PACKED, d), BF16),
        compiler_params=pltpu.CompilerParams(dimension_semantics=("parallel",), vmem_limit_bytes=VMEM_LIMIT),
        name="repack_w_in",
    )(w_in_t)


def _pack_weights(w_in, mla_w_uq, mla_w_ukv, gla_w_gate2):
    winp = _repack_call(w_in.T)
    wuq = mla_w_uq.reshape(MLA_Q_RANK, MLA_HEADS, MLA_QK)
    wuqp = jnp.pad(wuq, ((0, 0), (0, 0), (0, HEAD_PAD - MLA_QK))).reshape(MLA_Q_RANK, MLA_HEADS * HEAD_PAD)
    wukv = mla_w_ukv.reshape(MLA_KV_RANK, MLA_HEADS, MLA_NOPE + MLA_V)
    wkp = jnp.pad(wukv[:, :, :MLA_NOPE], ((0, 0), (0, 0), (0, HEAD_PAD - MLA_NOPE)))
    wkp = wkp.reshape(MLA_KV_RANK, MLA_HEADS * HEAD_PAD)
    wvt = wukv[:, :, MLA_NOPE:].reshape(MLA_KV_RANK, MLA_WIDTH).T
    wg2p = jnp.pad(gla_w_gate2, ((GLR_LO, LANES - GLR_LO - GLA_RANK), (0, 0)))
    return winp, wuqp.astype(BF16), wkp.astype(BF16), wvt.astype(BF16), wg2p.astype(BF16)


def _rope_inv_freq():
    half = ROPE_HALF
    inv = 1.0 / (ROPE_THETA ** (jnp.arange(half, dtype=F32) / half))
    return inv.reshape(half, 1)


def _rope_placement():
    place = np.zeros((4 * ROPE_HALF, 3 * LANES), np.float32)
    for piece in range(2):
        for i in range(ROPE_HALF):
            cos_row = 2 * piece * ROPE_HALF + i
            sin_row = cos_row + ROPE_HALF
            place[cos_row, ROPE_LO + i] = 1.0
            place[cos_row, ROPE_LO + ROPE_HALF + i] = 1.0
            place[sin_row, LANES + ROPE_LO + i] = -1.0
            place[sin_row, 2 * LANES + ROPE_LO + ROPE_HALF + i] = 1.0
    return jnp.asarray(place, BF16)


def _tile(n, pref):
    while n % pref:
        pref //= 2
    return pref


def kernel(x, positions, ln_mix, w_in, mla_norm_cq, mla_w_uq, mla_norm_ckv, mla_w_ukv, mla_w_o, gla_w_gate2,
           gla_b_gate, gla_norm, gla_w_o, w_out, ln_ffn, ffn_w_gate, ffn_w_up, ffn_w_down, final_norm):
    batch, seq, d = x.shape
    assert d == D_MODEL and w_in.shape[0] == 1 and seq % GLA_CHUNK == 0
    t = batch * seq
    x2 = x.reshape(t, d)
    winp, wuqp, wkp, wvt, wg2p = _pack_weights(w_in[0], mla_w_uq[0], mla_w_ukv[0], gla_w_gate2[0])
    r = lambda a: a.reshape(1, -1)

    tq = _tile(seq, ATTN_TILE)
    tm = max(tq, _tile(seq, PROJ_TILE))
    pos3 = positions.reshape(t // tm, 1, tm).astype(F32)
    q, k, vt, sa, sb, ob = _proj_gla_call(
        x2, pos3, r(ln_mix[0]), winp, r(mla_norm_cq[0]), wuqp, r(mla_norm_ckv[0]), wkp, wvt, wg2p,
        r(gla_b_gate[0]), _rope_inv_freq(), _rope_placement(), r(gla_norm[0]), seq, tm, tq)
    post_weights = (mla_w_o[0], gla_w_o[0], w_out[0], ffn_w_gate[0], ffn_w_up[0], ffn_w_down[0])
    oa, woa, wob, wout, wg, wu, wd = _attn_call(q, k, vt, post_weights, batch, seq, tq)
    out = _post_call(x2, oa, ob, sa, sb, woa, wob, wout, r(ln_ffn[0]), wg, wu, wd, r(final_norm),
                     _tile(t, POST_TILE))
    return out.reshape(batch, seq, d)
```

```python
import functools
import math

import jax
import jax.numpy as jnp
import numpy as np
from jax import lax
from jax.experimental import pallas as pl
from jax.experimental.pallas import tpu as pltpu

F32 = jnp.float32
BF16 = jnp.bfloat16

D_MODEL = 1024
MLA_HEADS = 8
MLA_NOPE = 64
MLA_ROPE = 32
MLA_QK = MLA_NOPE + MLA_ROPE
MLA_V = 64
MLA_WIDTH = MLA_HEADS * MLA_V
MLA_Q_RANK = 384
MLA_KV_RANK = 256
ROPE_THETA = 10000.0
GLA_HEADS = 4
GLA_DV = 512
GLA_DK = 256
GLA_HK = GLA_DK // GLA_HEADS
GLA_HV = GLA_DV // GLA_HEADS
GLA_RANK = 16
GLA_TAU = 16.0
EPS = 1e-6
LOG2E = 1.4426950408889634
IN_SIZES = (MLA_Q_RANK, MLA_KV_RANK, MLA_ROPE, GLA_DK, GLA_DK, GLA_DV, GLA_RANK, GLA_DV, D_MODEL, D_MODEL)

LANES = 128
HEAD_PAD = LANES
ROPE_LO = MLA_NOPE
ROPE_HALF = MLA_ROPE // 2
GLR_LO = ROPE_LO + MLA_ROPE

OFF_CQ = 0
OFF_CKV = OFF_CQ + MLA_Q_RANK
OFF_GQ = OFF_CKV + MLA_KV_RANK
OFF_GK = OFF_GQ + GLA_DK
OFF_GV = OFF_GK + GLA_DK
OFF_OG = OFF_GV + GLA_DV
OFF_GA = OFF_OG + GLA_DV
OFF_GB = OFF_GA + D_MODEL
OFF_MISC = OFF_GB + D_MODEL
D_IN_PACKED = OFF_MISC + LANES

REPACK_COLS = 256
PROJ_TILE = 512
POST_TILE = 512
POST_SUB = 256
ATTN_TILE = 256
ATTN_UNROLL = 12
BF16_ROWS = 16
MLA_VA = MLA_V + BF16_ROWS
MASKED = -1e30
GLA_CHUNK = 64
GLA_SUB = 8
GLA_PAR = 4
VMEM_LIMIT = 56 * 1024 * 1024


def _dot(a, b):
    return jnp.dot(a, b, preferred_element_type=F32)


def _dot_nt(a, b):
    return lax.dot_general(a, b, (((1,), (1,)), ((), ())), preferred_element_type=F32)


def _dot_tn(a, b):
    return lax.dot_general(a, b, (((0,), (0,)), ((), ())), preferred_element_type=F32)


def _rms(x, w):
    return x * lax.rsqrt(jnp.mean(x * x, axis=-1, keepdims=True) + EPS) * w


def _single_spec(a):
    return pl.BlockSpec(a.shape, lambda *_: (0,) * a.ndim, pipeline_mode=pl.Buffered(1))


def _gla_constants():
    c_, s_ = GLA_CHUNK, GLA_SUB
    ri = lax.broadcasted_iota(jnp.int32, (c_, c_), 0)
    ci = lax.broadcasted_iota(jnp.int32, (c_, c_), 1)
    rl = lax.broadcasted_iota(jnp.int32, (c_, GLA_DK), 0)
    ll = lax.broadcasted_iota(jnp.int32, (c_, GLA_DK), 1)
    row_minus_col = rl - ll % GLA_HK
    di = lax.broadcasted_iota(jnp.int32, (GLA_DK, GLA_DK), 0) // GLA_HK
    dj = lax.broadcasted_iota(jnp.int32, (GLA_DK, GLA_DK), 1) // GLA_HK
    return dict(
        tri=jnp.where((ri // s_ == ci // s_) & (ci <= ri), 1.0, 0.0).astype(BF16),
        diag_key=jnp.where((row_minus_col >= 0) & (rl % s_ >= row_minus_col), row_minus_col, -1),
        head_of_lane=ll // GLA_HK,
        odd_head_lane=(lax.broadcasted_iota(jnp.int32, (c_, 2 * GLA_HK), 1) // GLA_HK) == 1,
        head_sum=jnp.where(di == dj, 1.0, 0.0).astype(BF16))


def _gla_slab_stages(base, par, consts, gq_ref, gk_ref, gv_ref, la_ref, og_ref, norm_w, o_ref, state_ref,
                     kpad_ref, lapad_ref):
    c_, s_ = GLA_CHUNK, GLA_SUB
    nsub = c_ // s_
    pair = 2 * GLA_HK
    tri, diag_key, head_of_lane = consts["tri"], consts["diag_key"], consts["head_of_lane"]
    odd_head_lane, head_sum = consts["odd_head_lane"], consts["head_sum"]
    chunks = [dict(rows=slice(base + c * c_, base + (c + 1) * c_), off=s_ + c * c_) for c in range(par)]

    def blocks(fn):
        return jnp.concatenate([fn(b) for b in range(nsub)], axis=0)

    def sub(t, b):
        return t[b * s_:(b + 1) * s_, :]

    def cumsums():
        kpad_ref[0:s_, :] = jnp.zeros((s_, GLA_DK), F32)
        lapad_ref[0:s_, :] = jnp.zeros((s_, GLA_DK), F32)
        kpad_ref[s_:, :] = gk_ref[base:base + par * c_, :].astype(F32)
        lapad_ref[s_:, :] = la_ref[base:base + par * c_, :]
        for ch in chunks:
            la = la_ref[ch["rows"], :]
            hi = la.astype(BF16)
            r1 = la - hi.astype(F32)
            mid = r1.astype(BF16)
            lo = (r1 - mid.astype(F32)).astype(BF16)
            ch["lsub"] = _dot(tri, hi) + _dot(tri, mid) + _dot(tri, lo)

    def operands():
        for ch in chunks:
            q = gq_ref[ch["rows"], :].astype(F32)
            k = gk_ref[ch["rows"], :].astype(F32)
            lsub = ch["lsub"]
            tot = [lsub[(b + 1) * s_ - 1:(b + 1) * s_, :] for b in range(nsub)]
            q_sub = q * jnp.exp2(lsub)
            k_sub = k * jnp.exp2(blocks(lambda b: jnp.broadcast_to(tot[b], (s_, GLA_DK))) - lsub)
            zero = jnp.zeros((s_, GLA_DK), F32)
            e = [jnp.exp2(t) for t in tot]

            def decayed(t, b, over):
                piece = sub(t, b)
                for i in over:
                    piece = piece * e[i]
                return piece

            ch["qd"] = blocks(lambda b: decayed(q_sub, b, range(b))).astype(BF16)
            kd = blocks(lambda b: decayed(k_sub, b, range(b + 1, nsub))).astype(BF16)
            ch["decay"] = jnp.exp2(sum(tot[1:], tot[0]))
            v = gv_ref[ch["rows"], :]
            ch["upd"] = sum(_dot_tn(v[:, h * GLA_HV:(h + 1) * GLA_HV], jnp.where(head_of_lane == h, kd, 0.0))
                            for h in range(GLA_HEADS))
            ch["q_lv"], ch["k_lv"] = [], []
            width = 1
            while width < nsub:
                for odd in range(1, nsub // width, 2):
                    q_lo, k_lo = odd * width, (odd - 1) * width
                    ch["q_lv"].append(blocks(lambda b: decayed(q_sub, b, range(q_lo, b))
                                             if q_lo <= b < q_lo + width else zero))
                    ch["k_lv"].append(blocks(lambda b: decayed(k_sub, b, range(b + 1, k_lo + width))
                                             if k_lo <= b < k_lo + width else zero))
                width *= 2
            own = q * k
            own_hi = own.astype(BF16)
            prods = [own_hi]
            decay = jnp.zeros((c_, GLA_DK), F32)
            for d in range(1, s_):
                decay = decay + lapad_ref[ch["off"] - d + 1:ch["off"] - d + 1 + c_, :]
                k_sh = kpad_ref[ch["off"] - d:ch["off"] - d + c_, :]
                prods.append((q * k_sh * jnp.exp2(decay)).astype(BF16))
            prods.append((own - own_hi.astype(F32)).astype(BF16))
            summed = _dot(jnp.concatenate(prods, axis=0), head_sum)
            ch["summed"] = [summed[0:c_, :] + summed[s_ * c_:, :]] + [summed[d * c_:(d + 1) * c_, :]
                                                                        for d in range(1, s_)]

    def pair_scores():
        for ch in chunks:
            ch["attn_off"] = []
            for tile in range(GLA_HEADS // 2):
                lanes = slice(tile * pair, (tile + 1) * pair)

                def split(groups):
                    tiles = [g[:, lanes] for g in groups]
                    return tiles[0::2], [pltpu.roll(g, GLA_HK, axis=1) for g in tiles[1::2]]

                q_even, q_odd = split(ch["q_lv"])
                k_even, k_odd = split(ch["k_lv"])
                for own in (~odd_head_lane, odd_head_lane):
                    def packed(even, odd):
                        return jnp.concatenate(
                            [jnp.where(own, a, odd[n] if n < len(odd) else 0.0).astype(BF16)
                             for n, a in enumerate(even)], axis=1)

                    ch["attn_off"].append(_dot_nt(packed(q_even, q_odd), packed(k_even, k_odd)))

    def outputs():
        state = state_ref[...]
        for ch in chunks:
            ch["state_b"] = state.astype(BF16)
            state = state * ch["decay"] + ch["upd"]
        state_ref[...] = state
        for ch in chunks:
            attn_diag = jnp.zeros((c_, GLA_DK), F32)
            for d in range(s_):
                attn_diag = jnp.where(diag_key == d, ch["summed"][d], attn_diag)
            v = gv_ref[ch["rows"], :]
            for h in range(GLA_HEADS):
                hv = slice(h * GLA_HV, (h + 1) * GLA_HV)
                attn = attn_diag[:, h * GLA_HK:(h + 1) * GLA_HK] + ch["attn_off"][h]
                o = _dot(attn.astype(BF16), v[:, hv])
                o = o + _dot_nt(jnp.where(head_of_lane == h, ch["qd"], 0.0), ch["state_b"])
                o = _rms(o, norm_w)
                g = og_ref[ch["rows"], hv].astype(F32)
                o_ref[ch["rows"], hv] = (o * (g * jax.nn.sigmoid(g))).astype(BF16)

    return [cumsums, operands, pair_scores, outputs]


def _proj_gla_kernel(x_ref, pos_ref, lnmix_ref, win_ref, ncq_ref, wuq_ref, nckv_ref, wk_ref, wvt_ref,
                     wg2_ref, bg_ref, invf_ref, place_ref, gnorm_ref,
                     q_ref, k_ref, vt_ref, sa_ref, sb_ref, ob_ref,
                     gq_ref, gk_ref, gv_ref, la_ref, og_ref, state_ref, kpad_ref, lapad_ref,
                     *, tkv, steps_per_seq, par):
    tm = x_ref.shape[0]

    @pl.when(pl.program_id(0) % steps_per_seq == 0)
    def _():
        state_ref[...] = jnp.zeros_like(state_ref)

    ub = _rms(x_ref[...], lnmix_ref[...]).astype(BF16)

    def seg(lo, width):
        return _dot_nt(ub, win_ref[lo:lo + width, :])

    misc = seg(OFF_MISC, LANES)
    xg = _dot(misc.astype(BF16), wg2_ref[...]) + bg_ref[...]
    la_ref[...] = (jnp.minimum(xg, 0.0) - jnp.log1p(jnp.exp(-jnp.abs(xg)))) * (LOG2E / GLA_TAU)
    gq_ref[...] = (seg(OFF_GQ, GLA_DK) * (GLA_HK ** -0.5)).astype(BF16)
    gk_ref[...] = seg(OFF_GK, GLA_DK).astype(BF16)
    gv_ref[...] = seg(OFF_GV, GLA_DV).astype(BF16)
    og_ref[...] = seg(OFF_OG, GLA_DV).astype(BF16)

    consts = _gla_constants()
    slab_rows = par * GLA_CHUNK
    scan = []
    for sl in range(tm // slab_rows):
        scan += _gla_slab_stages(sl * slab_rows, par, consts, gq_ref, gk_ref, gv_ref, la_ref, og_ref,
                                 gnorm_ref[...], ob_ref, state_ref, kpad_ref.at[sl], lapad_ref.at[sl])

    lane = lax.broadcasted_iota(jnp.int32, (1, LANES), 1)
    in_rope = (lane >= ROPE_LO) & (lane < ROPE_LO + MLA_ROPE)
    t = {}

    def latents():
        t["cq"] = seg(OFF_CQ, MLA_Q_RANK)
        t["ckv"] = seg(OFF_CKV, MLA_KV_RANK)

    def rope_tables():
        ang = invf_ref[...] * pos_ref[0]
        cs = jnp.concatenate([jnp.cos(ang), jnp.sin(ang)], axis=0)
        cs_hi = cs.astype(BF16)
        cs_lo = (cs - cs_hi.astype(F32)).astype(BF16)
        tabs = _dot_tn(jnp.concatenate([cs_hi, cs_lo], axis=0), place_ref[...])
        t["cos"] = tabs[:, :LANES] + jnp.where(in_rope, 0.0, 1.0)
        t["sin_x1"] = tabs[:, LANES:2 * LANES]
        t["sin_x2"] = tabs[:, 2 * LANES:]

    def gate_a():
        sa_ref[...] = jax.nn.sigmoid(seg(OFF_GA, D_MODEL)).astype(BF16)

    def gate_b():
        sb_ref[...] = jax.nn.sigmoid(seg(OFF_GB, D_MODEL)).astype(BF16)

    def latent_norms():
        t["cqn"] = _rms(t["cq"], ncq_ref[...]).astype(BF16)
        t["ckvn"] = _rms(t["ckv"], nckv_ref[...]).astype(BF16)

    def up_q():
        t["q"] = _dot(t["cqn"], wuq_ref[...]) * (MLA_QK ** -0.5 * LOG2E)

    def up_kv():
        t["k_nope"] = _dot(t["ckvn"], wk_ref[...])
        for c in range(tm // tkv):
            vt = _dot_nt(wvt_ref[...], t["ckvn"][c * tkv:(c + 1) * tkv, :]).astype(BF16)
            for h in range(MLA_HEADS):
                vt_ref[c, h * MLA_VA:h * MLA_VA + MLA_V, :] = vt[h * MLA_V:(h + 1) * MLA_V, :]
                vt_ref[c, h * MLA_VA + MLA_V:(h + 1) * MLA_VA, :] = jnp.ones((MLA_VA - MLA_V, tkv), BF16)

    def rope_store():
        def rope(a):
            return (a * t["cos"] + pltpu.roll(a, LANES - ROPE_HALF, axis=1) * t["sin_x1"]
                    + pltpu.roll(a, ROPE_HALF, axis=1) * t["sin_x2"])

        k_rope = rope(jnp.where(in_rope, misc, 0.0))
        for h in range(MLA_HEADS):
            sl = slice(h * HEAD_PAD, (h + 1) * HEAD_PAD)
            q_ref[:, sl] = rope(t["q"][:, sl]).astype(BF16)
            k_ref[:, sl] = (t["k_nope"][:, sl] + k_rope).astype(BF16)

    rest = [latents, gate_a, rope_tables, latent_norms, gate_b, up_q, up_kv, rope_store]
    for n in range(max(len(scan), len(rest))):
        if n < len(scan):
            scan[n]()
        if n < len(rest):
            rest[n]()


def _proj_gla_call(x2, pos3, lnmix, winp, ncq, wuqp, nckv, wkp, wvt, wg2p, bg, invf, place, gnorm, seq, tm, tkv):
    t = x2.shape[0]
    par = math.gcd(tm // GLA_CHUNK, GLA_PAR)
    nslab = tm // (par * GLA_CHUNK)
    row = lambda w: pl.BlockSpec((tm, w), lambda i: (i, 0))
    consts = (lnmix, winp, ncq, wuqp, nckv, wkp, wvt, wg2p, bg, invf, place, gnorm)
    out_widths = (MLA_HEADS * HEAD_PAD, MLA_HEADS * HEAD_PAD, None, D_MODEL, D_MODEL, GLA_DV)
    vt_spec = pl.BlockSpec((tm // tkv, MLA_HEADS * MLA_VA, tkv), lambda i: (i, 0, 0))
    vt_shape = jax.ShapeDtypeStruct((t // tkv, MLA_HEADS * MLA_VA, tkv), BF16)
    pad_rows = GLA_SUB + par * GLA_CHUNK
    return pl.pallas_call(
        functools.partial(_proj_gla_kernel, tkv=tkv, steps_per_seq=seq // tm, par=par),
        grid=(t // tm,),
        in_specs=[row(D_MODEL), pl.BlockSpec((1, 1, tm), lambda i: (i, 0, 0))] + [_single_spec(c) for c in consts],
        out_specs=[vt_spec if w is None else row(w) for w in out_widths],
        out_shape=[vt_shape if w is None else jax.ShapeDtypeStruct((t, w), BF16) for w in out_widths],
        scratch_shapes=[pltpu.VMEM((tm, GLA_DK), BF16), pltpu.VMEM((tm, GLA_DK), BF16),
                        pltpu.VMEM((tm, GLA_DV), BF16), pltpu.VMEM((tm, GLA_DK), F32),
                        pltpu.VMEM((tm, GLA_DV), BF16), pltpu.VMEM((GLA_HV, GLA_DK), F32),
                        pltpu.VMEM((nslab, pad_rows, GLA_DK), F32), pltpu.VMEM((nslab, pad_rows, GLA_DK), F32)],
        compiler_params=pltpu.CompilerParams(dimension_semantics=("arbitrary",), vmem_limit_bytes=VMEM_LIMIT),
        name="proj_gla",
    )(x2, pos3, *consts)


def _attn_kernel(qi_ref, kj_ref, q_ref, k_ref, vt_ref, *refs, tq, npairs, ncast):
    cast_in, o_ref, cast_out = refs[:ncast], refs[ncast], refs[ncast + 1:2 * ncast + 1]
    acc_ref, ot_ref, m_ref, s0_ref, s1_ref, smax0_ref, smax1_ref = refs[2 * ncast + 1:]
    for src, dst in zip(cast_in, cast_out):
        dst[...] = src[...].astype(BF16)
    nq = q_ref.shape[0] // tq
    key = lax.broadcasted_iota(jnp.int32, (tq, tq), 0)
    qry = lax.broadcasted_iota(jnp.int32, (tq, tq), 1)
    causal = key <= qry
    slots = ((s0_ref, smax0_ref), (s1_ref, smax1_ref))

    def scores(n, slot, heads, masked):
        s_ref, smax_ref = slots[slot]
        qi, kj = qi_ref[n], kj_ref[n]
        qrows = pl.ds(pl.multiple_of(qi * tq, tq), tq)
        krows = pl.ds(pl.multiple_of(kj * tq, tq), tq)
        for h in heads:
            hq = slice(h * HEAD_PAD, (h + 1) * HEAD_PAD)
            s = _dot_nt(k_ref[krows, hq], q_ref[qrows, hq])
            if masked:
                s = jnp.where(causal, s, MASKED)
            s_ref[h] = s
            smax_ref[h:h + 1, :] = jnp.max(s, axis=0, keepdims=True)

    def accumulate(n, slot, heads, first):
        s_ref, smax_ref = slots[slot]
        qi, kj = qi_ref[n], kj_ref[n]
        for h in heads:
            hv = slice(h * MLA_VA, (h + 1) * MLA_VA)
            m_new = smax_ref[h:h + 1, :]
            if not first:
                m = m_ref[qi, h:h + 1, :]
                m_new = jnp.maximum(m, m_new)
                alpha = jnp.exp2(m - m_new)
            p = jnp.exp2(s_ref[h] - m_new)
            m_ref[qi, h:h + 1, :] = m_new
            pv = _dot(vt_ref[kj, hv, :], p.astype(BF16))
            acc_ref[qi, hv, :] = pv if first else alpha * acc_ref[qi, hv, :] + pv

    all_heads = range(MLA_HEADS)

    def step(k, parity, diag_next, diag_now):
        for h in all_heads:
            scores(k + 1, 1 - parity, (h,), diag_next)
            accumulate(k, parity, (h,), diag_now)

    def steps(lo, hi, diag_next, diag_now):
        while (hi - lo) % ATTN_UNROLL:
            step(lo, lo % 2, diag_next, diag_now)
            lo += 1
        if hi > lo:
            def unrolled(t, carry):
                for u in range(ATTN_UNROLL):
                    step(lo + ATTN_UNROLL * t + u, (lo + u) % 2, diag_next, diag_now)
                return carry

            lax.fori_loop(0, (hi - lo) // ATTN_UNROLL, unrolled, 0)

    scores(0, 0, all_heads, True)
    steps(0, nq - 1, True, True)
    if npairs > nq:
        steps(nq - 1, nq, False, True)
        steps(nq, npairs - 1, False, False)
    accumulate(npairs - 1, (npairs - 1) % 2, all_heads, npairs == nq)

    for qt in range(nq):
        for h in range(MLA_HEADS):
            lo = h * MLA_VA
            inv_denominator = 1.0 / acc_ref[qt, lo + MLA_V:lo + MLA_V + 1, :]
            ot_ref[h * MLA_V:(h + 1) * MLA_V, :] = acc_ref[qt, lo:lo + MLA_V, :] * inv_denominator
        o_ref[qt * tq:(qt + 1) * tq, :] = ot_ref[...].T.astype(BF16)


def _attn_call(q, k, vt, casts, batch, seq, tq):
    nq = seq // tq
    for a in casts:
        assert a.shape[0] % (batch * BF16_ROWS) == 0, a.shape
    cast_specs = [pl.BlockSpec((a.shape[0] // batch, a.shape[1]), lambda b, *_: (b, 0)) for a in casts]
    pairs = [(i, i) for i in range(nq)] + [(i, j) for i in range(nq) for j in range(i)]
    qi = jnp.asarray([p[0] for p in pairs], jnp.int32)
    kj = jnp.asarray([p[1] for p in pairs], jnp.int32)
    grid_spec = pltpu.PrefetchScalarGridSpec(
        num_scalar_prefetch=2,
        grid=(batch,),
        in_specs=[pl.BlockSpec((seq, MLA_HEADS * HEAD_PAD), lambda b, *_: (b, 0)),
                  pl.BlockSpec((seq, MLA_HEADS * HEAD_PAD), lambda b, *_: (b, 0)),
                  pl.BlockSpec((nq, MLA_HEADS * MLA_VA, tq), lambda b, *_: (b, 0, 0))] + cast_specs,
        out_specs=[pl.BlockSpec((seq, MLA_WIDTH), lambda b, *_: (b, 0))] + cast_specs,
        scratch_shapes=[pltpu.VMEM((nq, MLA_HEADS * MLA_VA, tq), F32), pltpu.VMEM((MLA_WIDTH, tq), F32),
                        pltpu.VMEM((nq, MLA_HEADS, tq), F32),
                        pltpu.VMEM((MLA_HEADS, tq, tq), F32), pltpu.VMEM((MLA_HEADS, tq, tq), F32),
                        pltpu.VMEM((MLA_HEADS, tq), F32), pltpu.VMEM((MLA_HEADS, tq), F32)])
    return pl.pallas_call(
        functools.partial(_attn_kernel, tq=tq, npairs=len(pairs), ncast=len(casts)),
        grid_spec=grid_spec,
        out_shape=[jax.ShapeDtypeStruct((batch * seq, MLA_WIDTH), BF16)]
                  + [jax.ShapeDtypeStruct(a.shape, BF16) for a in casts],
        compiler_params=pltpu.CompilerParams(dimension_semantics=("parallel",), vmem_limit_bytes=VMEM_LIMIT),
        name="mla_attn",
    )(qi, kj, q, k, vt, *casts)


def _post_kernel(x_ref, oa_ref, ob_ref, sa_ref, sb_ref, woa_ref, wob_ref, wout_ref, lnffn_ref,
                 wg_ref, wu_ref, wd_ref, fnorm_ref, out_ref):
    tm = x_ref.shape[0]
    sub = math.gcd(tm, POST_SUB)
    subs = [dict(rows=slice(r, r + sub)) for r in range(0, tm, sub)]

    def merge(t):
        ya = _dot(oa_ref[t["rows"], :], woa_ref[...])
        yb = _dot(ob_ref[t["rows"], :], wob_ref[...])
        t["mix"] = (sa_ref[t["rows"], :].astype(F32) * ya + sb_ref[t["rows"], :].astype(F32) * yb).astype(BF16)

    def residual(t):
        t["h"] = x_ref[t["rows"], :] + _dot(t["mix"], wout_ref[...])
        t["u"] = _rms(t["h"], lnffn_ref[...]).astype(BF16)

    def hidden(t):
        g = _dot(t["u"], wg_ref[...])
        t["act"] = (g * jax.nn.sigmoid(g) * _dot(t["u"], wu_ref[...])).astype(BF16)

    def output(t):
        out_ref[t["rows"], :] = _rms(t["h"] + _dot(t["act"], wd_ref[...]), fnorm_ref[...])

    for stage in (merge, residual, hidden, output):
        for t in subs:
            stage(t)


def _post_call(x2, oa, ob, sa, sb, woa, wob, wout, lnffn, wg, wu, wd, fnorm, tm):
    t = x2.shape[0]
    row = lambda w: pl.BlockSpec((tm, w), lambda i: (i, 0))
    consts = (woa, wob, wout, lnffn, wg, wu, wd, fnorm)
    return pl.pallas_call(
        _post_kernel,
        grid=(t // tm,),
        in_specs=[row(D_MODEL), row(MLA_WIDTH), row(GLA_DV), row(D_MODEL), row(D_MODEL)]
                 + [_single_spec(c) for c in consts],
        out_specs=row(D_MODEL),
        out_shape=jax.ShapeDtypeStruct((t, D_MODEL), F32),
        compiler_params=pltpu.CompilerParams(dimension_semantics=("parallel",), vmem_limit_bytes=VMEM_LIMIT),
        name="post",
    )(x2, oa, ob, sa, sb, *consts)


def _repack_kernel(wt_ref, o_ref):
    offs = [0]
    for s in IN_SIZES:
        offs.append(offs[-1] + s)
    cols = wt_ref.shape[1]
    dst = 0
    for idx in (0, 1, 3, 4, 5, 7, 8, 9):
        n = IN_SIZES[idx]
        o_ref[dst:dst + n, :] = wt_ref[offs[idx]:offs[idx + 1], :].astype(BF16)
        dst += n
    o_ref[dst:dst + ROPE_LO, :] = jnp.zeros((ROPE_LO, cols), BF16)
    o_ref[dst + ROPE_LO:dst + GLR_LO, :] = wt_ref[offs[2]:offs[3], :].astype(BF16)
    o_ref[dst + GLR_LO:dst + GLR_LO + GLA_RANK, :] = wt_ref[offs[6]:offs[7], :].astype(BF16)
    o_ref[dst + GLR_LO + GLA_RANK:dst + LANES, :] = jnp.zeros((LANES - GLR_LO - GLA_RANK, cols), BF16)


def _repack_call(w_in_t):
    n, d = w_in_t.shape
    cols = _tile(d, REPACK_COLS)
    return pl.pallas_call(
        _repack_kernel,
        grid=(d // cols,),
        in_specs=[pl.BlockSpec((n, cols), lambda i: (0, i))],
        out_specs=pl.BlockSpec((D_IN_PACKED, cols), lambda i: (0, i)),
        out_shape=jax.ShapeDtypeStruct((D_IN_PACKED, d), BF16),
        compiler_params=pltpu.CompilerParams(dimension_semantics=("parallel",), vmem_limit_bytes=VMEM_LIMIT),
        name="repack_w_in",
    )(w_in_t)


def _pack_weights(w_in, mla_w_uq, mla_w_ukv, gla_w_gate2):
    winp = _repack_call(w_in.T)
    wuq = mla_w_uq.reshape(MLA_Q_RANK, MLA_HEADS, MLA_QK)
    wuqp = jnp.pad(wuq, ((0, 0), (0, 0), (0, HEAD_PAD - MLA_QK))).reshape(MLA_Q_RANK, MLA_HEADS * HEAD_PAD)
    wukv = mla_w_ukv.reshape(MLA_KV_RANK, MLA_HEADS, MLA_NOPE + MLA_V)
    wkp = jnp.pad(wukv[:, :, :MLA_NOPE], ((0, 0), (0, 0), (0, HEAD_PAD - MLA_NOPE)))
    wkp = wkp.reshape(MLA_KV_RANK, MLA_HEADS * HEAD_PAD)
    wvt = wukv[:, :, MLA_NOPE:].reshape(MLA_KV_RANK, MLA_WIDTH).T
    wg2p = jnp.pad(gla_w_gate2, ((GLR_LO, LANES - GLR_LO - GLA_RANK), (0, 0)))
    return winp, wuqp.astype(BF16), wkp.astype(BF16), wvt.astype(BF16), wg2p.astype(BF16)


def _rope_inv_freq():
    half = ROPE_HALF
    inv = 1.0 / (ROPE_THETA ** (jnp.arange(half, dtype=F32) / half))
    return inv.reshape(half, 1)


def _rope_placement():
    place = np.zeros((4 * ROPE_HALF, 3 * LANES), np.float32)
    for piece in range(2):
        for i in range(ROPE_HALF):
            cos_row = 2 * piece * ROPE_HALF + i
            sin_row = cos_row + ROPE_HALF
            place[cos_row, ROPE_LO + i] = 1.0
            place[cos_row, ROPE_LO + ROPE_HALF + i] = 1.0
            place[sin_row, LANES + ROPE_LO + i] = -1.0
            place[sin_row, 2 * LANES + ROPE_LO + ROPE_HALF + i] = 1.0
    return jnp.asarray(place, BF16)


def _tile(n, pref):
    while n % pref:
        pref //= 2
    return pref


def kernel(x, positions, ln_mix, w_in, mla_norm_cq, mla_w_uq, mla_norm_ckv, mla_w_ukv, mla_w_o, gla_w_gate2,
           gla_b_gate, gla_norm, gla_w_o, w_out, ln_ffn, ffn_w_gate, ffn_w_up, ffn_w_down, final_norm):
    batch, seq, d = x.shape
    assert d == D_MODEL and w_in.shape[0] == 1 and seq % GLA_CHUNK == 0
    t = batch * seq
    x2 = x.reshape(t, d)
    winp, wuqp, wkp, wvt, wg2p = _pack_weights(w_in[0], mla_w_uq[0], mla_w_ukv[0], gla_w_gate2[0])
    r = lambda a: a.reshape(1, -1)

    tq = _tile(seq, ATTN_TILE)
    tm = max(tq, _tile(seq, PROJ_TILE))
    pos3 = positions.reshape(t // tm, 1, tm).astype(F32)
    q, k, vt, sa, sb, ob = _proj_gla_call(
        x2, pos3, r(ln_mix[0]), winp, r(mla_norm_cq[0]), wuqp, r(mla_norm_ckv[0]), wkp, wvt, wg2p,
        r(gla_b_gate[0]), _rope_inv_freq(), _rope_placement(), r(gla_norm[0]), seq, tm, tq)
    post_weights = (mla_w_o[0], gla_w_o[0], w_out[0], ffn_w_gate[0], ffn_w_up[0], ffn_w_down[0])
    oa, woa, wob, wout, wg, wu, wd = _attn_call(q, k, vt, post_weights, batch, seq, tq)
    out = _post_call(x2, oa, ob, sa, sb, woa, wob, wout, r(ln_ffn[0]), wg, wu, wd, r(final_norm),
                     _tile(t, POST_TILE))
    return out.reshape(batch, seq, d)
```

```python
import functools
import math

import jax
import jax.numpy as jnp
import numpy as np
from jax import lax
from jax.experimental import pallas as pl
from jax.experimental.pallas import tpu as pltpu

F32 = jnp.float32
BF16 = jnp.bfloat16

D_MODEL = 1024
MLA_HEADS = 8
MLA_NOPE = 64
MLA_ROPE = 32
MLA_QK = MLA_NOPE + MLA_ROPE
MLA_V = 64
MLA_WIDTH = MLA_HEADS * MLA_V
MLA_Q_RANK = 384
MLA_KV_RANK = 256
ROPE_THETA = 10000.0
GLA_HEADS = 4
GLA_DV = 512
GLA_DK = 256
GLA_HK = GLA_DK // GLA_HEADS
GLA_HV = GLA_DV // GLA_HEADS
GLA_RANK = 16
GLA_TAU = 16.0
EPS = 1e-6
LOG2E = 1.4426950408889634
IN_SIZES = (MLA_Q_RANK, MLA_KV_RANK, MLA_ROPE, GLA_DK, GLA_DK, GLA_DV, GLA_RANK, GLA_DV, D_MODEL, D_MODEL)

LANES = 128
HEAD_PAD = LANES
ROPE_LO = MLA_NOPE
ROPE_HALF = MLA_ROPE // 2
GLR_LO = ROPE_LO + MLA_ROPE

OFF_CQ = 0
OFF_CKV = OFF_CQ + MLA_Q_RANK
OFF_GQ = OFF_CKV + MLA_KV_RANK
OFF_GK = OFF_GQ + GLA_DK
OFF_GV = OFF_GK + GLA_DK
OFF_OG = OFF_GV + GLA_DV
OFF_GA = OFF_OG + GLA_DV
OFF_GB = OFF_GA + D_MODEL
OFF_MISC = OFF_GB + D_MODEL
D_IN_PACKED = OFF_MISC + LANES

REPACK_COLS = 256
PROJ_TILE = 512
POST_TILE = 512
POST_SUB = 256
ATTN_TILE = 256
ATTN_UNROLL = 12
BF16_ROWS = 16
MLA_VA = MLA_V + BF16_ROWS
MASKED = -1e30
GLA_CHUNK = 64
GLA_SUB = 8
GLA_PAR = 4
VMEM_LIMIT = 56 * 1024 * 1024


def _dot(a, b):
    return jnp.dot(a, b, preferred_element_type=F32)


def _dot_nt(a, b):
    return lax.dot_general(a, b, (((1,), (1,)), ((), ())), preferred_element_type=F32)


def _dot_tn(a, b):
    return lax.dot_general(a, b, (((0,), (0,)), ((), ())), preferred_element_type=F32)


def _rms(x, w):
    return x * lax.rsqrt(jnp.mean(x * x, axis=-1, keepdims=True) + EPS) * w


def _single_spec(a):
    return pl.BlockSpec(a.shape, lambda *_: (0,) * a.ndim, pipeline_mode=pl.Buffered(1))


def _gla_constants():
    c_, s_ = GLA_CHUNK, GLA_SUB
    ri = lax.broadcasted_iota(jnp.int32, (c_, c_), 0)
    ci = lax.broadcasted_iota(jnp.int32, (c_, c_), 1)
    rl = lax.broadcasted_iota(jnp.int32, (c_, GLA_DK), 0)
    ll = lax.broadcasted_iota(jnp.int32, (c_, GLA_DK), 1)
    row_minus_col = rl - ll % GLA_HK
    di = lax.broadcasted_iota(jnp.int32, (GLA_DK, GLA_DK), 0) // GLA_HK
    dj = lax.broadcasted_iota(jnp.int32, (GLA_DK, GLA_DK), 1) // GLA_HK
    return dict(
        tri=jnp.where((ri // s_ == ci // s_) & (ci <= ri), 1.0, 0.0).astype(BF16),
        diag_key=jnp.where((row_minus_col >= 0) & (rl % s_ >= row_minus_col), row_minus_col, -1),
        head_of_lane=ll // GLA_HK,
        odd_head_lane=(lax.broadcasted_iota(jnp.int32, (c_, 2 * GLA_HK), 1) // GLA_HK) == 1,
        head_sum=jnp.where(di == dj, 1.0, 0.0).astype(BF16))


def _gla_slab_stages(base, par, consts, gq_ref, gk_ref, gv_ref, la_ref, og_ref, norm_w, o_ref, state_ref,
                     kpad_ref, lapad_ref):
    c_, s_ = GLA_CHUNK, GLA_SUB
    nsub = c_ // s_
    pair = 2 * GLA_HK
    tri, diag_key, head_of_lane = consts["tri"], consts["diag_key"], consts["head_of_lane"]
    odd_head_lane, head_sum = consts["odd_head_lane"], consts["head_sum"]
    chunks = [dict(rows=slice(base + c * c_, base + (c + 1) * c_), off=s_ + c * c_) for c in range(par)]

    def blocks(fn):
        return jnp.concatenate([fn(b) for b in range(nsub)], axis=0)

    def sub(t, b):
        return t[b * s_:(b + 1) * s_, :]

    def cumsums():
        kpad_ref[0:s_, :] = jnp.zeros((s_, GLA_DK), F32)
        lapad_ref[0:s_, :] = jnp.zeros((s_, GLA_DK), F32)
        kpad_ref[s_:, :] = gk_ref[base:base + par * c_, :].astype(F32)
        lapad_ref[s_:, :] = la_ref[base:base + par * c_, :]
        for ch in chunks:
            la = la_ref[ch["rows"], :]
            hi = la.astype(BF16)
            r1 = la - hi.astype(F32)
            mid = r1.astype(BF16)
            lo = (r1 - mid.astype(F32)).astype(BF16)
            ch["lsub"] = _dot(tri, hi) + _dot(tri, mid) + _dot(tri, lo)

    def operands():
        for ch in chunks:
            q = gq_ref[ch["rows"], :].astype(F32)
            k = gk_ref[ch["rows"], :].astype(F32)
            lsub = ch["lsub"]
            tot = [lsub[(b + 1) * s_ - 1:(b + 1) * s_, :] for b in range(nsub)]
            q_sub = q * jnp.exp2(lsub)
            k_sub = k * jnp.exp2(blocks(lambda b: jnp.broadcast_to(tot[b], (s_, GLA_DK))) - lsub)
            zero = jnp.zeros((s_, GLA_DK), F32)
            e = [jnp.exp2(t) for t in tot]

            def decayed(t, b, over):
                piece = sub(t, b)
                for i in over:
                    piece = piece * e[i]
                return piece

            ch["qd"] = blocks(lambda b: decayed(q_sub, b, range(b))).astype(BF16)
            kd = blocks(lambda b: decayed(k_sub, b, range(b + 1, nsub))).astype(BF16)
            ch["decay"] = jnp.exp2(sum(tot[1:], tot[0]))
            v = gv_ref[ch["rows"], :]
            ch["upd"] = sum(_dot_tn(v[:, h * GLA_HV:(h + 1) * GLA_HV], jnp.where(head_of_lane == h, kd, 0.0))
                            for h in range(GLA_HEADS))
            ch["q_lv"], ch["k_lv"] = [], []
            width = 1
            while width < nsub:
                for odd in range(1, nsub // width, 2):
                    q_lo, k_lo = odd * width, (odd - 1) * width
                    ch["q_lv"].append(blocks(lambda b: decayed(q_sub, b, range(q_lo, b))
                                             if q_lo <= b < q_lo + width else zero))
                    ch["k_lv"].append(blocks(lambda b: decayed(k_sub, b, range(b + 1, k_lo + width))
                                             if k_lo <= b < k_lo + width else zero))
                width *= 2
            own = q * k
            own_hi = own.astype(BF16)
            prods = [own_hi]
            decay = jnp.zeros((c_, GLA_DK), F32)
            for d in range(1, s_):
                decay = decay + lapad_ref[ch["off"] - d + 1:ch["off"] - d + 1 + c_, :]
                k_sh = kpad_ref[ch["off"] - d:ch["off"] - d + c_, :]
                prods.append((q * k_sh * jnp.exp2(decay)).astype(BF16))
            prods.append((own - own_hi.astype(F32)).astype(BF16))
            summed = _dot(jnp.concatenate(prods, axis=0), head_sum)
            ch["summed"] = [summed[0:c_, :] + summed[s_ * c_:, :]] + [summed[d * c_:(d + 1) * c_, :]
                                                                        for d in range(1, s_)]

    def pair_scores():
        for ch in chunks:
            ch["attn_off"] = []
            for tile in range(GLA_HEADS // 2):
                lanes = slice(tile * pair, (tile + 1) * pair)

                def split(groups):
                    tiles = [g[:, lanes] for g in groups]
                    return tiles[0::2], [pltpu.roll(g, GLA_HK, axis=1) for g in tiles[1::2]]

                q_even, q_odd = split(ch["q_lv"])
                k_even, k_odd = split(ch["k_lv"])
                for own in (~odd_head_lane, odd_head_lane):
                    def packed(even, odd):
                        return jnp.concatenate(
                            [jnp.where(own, a, odd[n] if n < len(odd) else 0.0).astype(BF16)
                             for n, a in enumerate(even)], axis=1)

                    ch["attn_off"].append(_dot_nt(packed(q_even, q_odd), packed(k_even, k_odd)))

    def outputs():
        state = state_ref[...]
        for ch in chunks:
            ch["state_b"] = state.astype(BF16)
            state = state * ch["decay"] + ch["upd"]
        state_ref[...] = state
        for ch in chunks:
            attn_diag = jnp.zeros((c_, GLA_DK), F32)
            for d in range(s_):
                attn_diag = jnp.where(diag_key == d, ch["summed"][d], attn_diag)
            v = gv_ref[ch["rows"], :]
            for h in range(GLA_HEADS):
                hv = slice(h * GLA_HV, (h + 1) * GLA_HV)
                attn = attn_diag[:, h * GLA_HK:(h + 1) * GLA_HK] + ch["attn_off"][h]
                o = _dot(attn.astype(BF16), v[:, hv])
                o = o + _dot_nt(jnp.where(head_of_lane == h, ch["qd"], 0.0), ch["state_b"])
                o = _rms(o, norm_w)
                g = og_ref[ch["rows"], hv].astype(F32)
                o_ref[ch["rows"], hv] = (o * (g * jax.nn.sigmoid(g))).astype(BF16)

    return [cumsums, operands, pair_scores, outputs]


def _proj_gla_kernel(x_ref, pos_ref, lnmix_ref, win_ref, ncq_ref, wuq_ref, nckv_ref, wk_ref, wvt_ref,
                     wg2_ref, bg_ref, invf_ref, place_ref, gnorm_ref,
                     q_ref, k_ref, vt_ref, sa_ref, sb_ref, ob_ref,
                     gq_ref, gk_ref, gv_ref, la_ref, og_ref, state_ref, kpad_ref, lapad_ref,
                     *, tkv, steps_per_seq, par):
    tm = x_ref.shape[0]

    @pl.when(pl.program_id(0) % steps_per_seq == 0)
    def _():
        state_ref[...] = jnp.zeros_like(state_ref)

    ub = _rms(x_ref[...], lnmix_ref[...]).astype(BF16)

    def seg(lo, width):
        return _dot_nt(ub, win_ref[lo:lo + width, :])

    misc = seg(OFF_MISC, LANES)
    xg = _dot(misc.astype(BF16), wg2_ref[...]) + bg_ref[...]
    la_ref[...] = (jnp.minimum(xg, 0.0) - jnp.log1p(jnp.exp(-jnp.abs(xg)))) * (LOG2E / GLA_TAU)
    gq_ref[...] = (seg(OFF_GQ, GLA_DK) * (GLA_HK ** -0.5)).astype(BF16)
    gk_ref[...] = seg(OFF_GK, GLA_DK).astype(BF16)
    gv_ref[...] = seg(OFF_GV, GLA_DV).astype(BF16)
    og_ref[...] = seg(OFF_OG, GLA_DV).astype(BF16)

    consts = _gla_constants()
    slab_rows = par * GLA_CHUNK
    scan = []
    for sl in range(tm // slab_rows):
        scan += _gla_slab_stages(sl * slab_rows, par, consts, gq_ref, gk_ref, gv_ref, la_ref, og_ref,
                                 gnorm_ref[...], ob_ref, state_ref, kpad_ref.at[sl], lapad_ref.at[sl])

    lane = lax.broadcasted_iota(jnp.int32, (1, LANES), 1)
    in_rope = (lane >= ROPE_LO) & (lane < ROPE_LO + MLA_ROPE)
    t = {}

    def latents():
        t["cq"] = seg(OFF_CQ, MLA_Q_RANK)
        t["ckv"] = seg(OFF_CKV, MLA_KV_RANK)

    def rope_tables():
        ang = invf_ref[...] * pos_ref[0]
        cs = jnp.concatenate([jnp.cos(ang), jnp.sin(ang)], axis=0)
        cs_hi = cs.astype(BF16)
        cs_lo = (cs - cs_hi.astype(F32)).astype(BF16)
        tabs = _dot_tn(jnp.concatenate([cs_hi, cs_lo], axis=0), place_ref[...])
        t["cos"] = tabs[:, :LANES] + jnp.where(in_rope, 0.0, 1.0)
        t["sin_x1"] = tabs[:, LANES:2 * LANES]
        t["sin_x2"] = tabs[:, 2 * LANES:]

    def gate_a():
        sa_ref[...] = jax.nn.sigmoid(seg(OFF_GA, D_MODEL)).astype(BF16)

    def gate_b():
        sb_ref[...] = jax.nn.sigmoid(seg(OFF_GB, D_MODEL)).astype(BF16)

    def latent_norms():
        t["cqn"] = _rms(t["cq"], ncq_ref[...]).astype(BF16)
        t["ckvn"] = _rms(t["ckv"], nckv_ref[...]).astype(BF16)

    def up_q():
        t["q"] = _dot(t["cqn"], wuq_ref[...]) * (MLA_QK ** -0.5 * LOG2E)

    def up_kv():
        t["k_nope"] = _dot(t["ckvn"], wk_ref[...])
        for c in range(tm // tkv):
            vt = _dot_nt(wvt_ref[...], t["ckvn"][c * tkv:(c + 1) * tkv, :]).astype(BF16)
            for h in range(MLA_HEADS):
                vt_ref[c, h * MLA_VA:h * MLA_VA + MLA_V, :] = vt[h * MLA_V:(h + 1) * MLA_V, :]
                vt_ref[c, h * MLA_VA + MLA_V:(h + 1) * MLA_VA, :] = jnp.ones((MLA_VA - MLA_V, tkv), BF16)

    def rope_store():
        def rope(a):
            return (a * t["cos"] + pltpu.roll(a, LANES - ROPE_HALF, axis=1) * t["sin_x1"]
                    + pltpu.roll(a, ROPE_HALF, axis=1) * t["sin_x2"])

        k_rope = rope(jnp.where(in_rope, misc, 0.0))
        for h in range(MLA_HEADS):
            sl = slice(h * HEAD_PAD, (h + 1) * HEAD_PAD)
            q_ref[:, sl] = rope(t["q"][:, sl]).astype(BF16)
            k_ref[:, sl] = (t["k_nope"][:, sl] + k_rope).astype(BF16)

    rest = [latents, gate_a, rope_tables, latent_norms, gate_b, up_q, up_kv, rope_store]
    for n in range(max(len(scan), len(rest))):
        if n < len(scan):
            scan[n]()
        if n < len(rest):
            rest[n]()


def _proj_gla_call(x2, pos3, lnmix, winp, ncq, wuqp, nckv, wkp, wvt, wg2p, bg, invf, place, gnorm, seq, tm, tkv):
    t = x2.shape[0]
    par = math.gcd(tm // GLA_CHUNK, GLA_PAR)
    nslab = tm // (par * GLA_CHUNK)
    row = lambda w: pl.BlockSpec((tm, w), lambda i: (i, 0))
    consts = (lnmix, winp, ncq, wuqp, nckv, wkp, wvt, wg2p, bg, invf, place, gnorm)
    out_widths = (MLA_HEADS * HEAD_PAD, MLA_HEADS * HEAD_PAD, None, D_MODEL, D_MODEL, GLA_DV)
    vt_spec = pl.BlockSpec((tm // tkv, MLA_HEADS * MLA_VA, tkv), lambda i: (i, 0, 0))
    vt_shape = jax.ShapeDtypeStruct((t // tkv, MLA_HEADS * MLA_VA, tkv), BF16)
    pad_rows = GLA_SUB + par * GLA_CHUNK
    return pl.pallas_call(
        functools.partial(_proj_gla_kernel, tkv=tkv, steps_per_seq=seq // tm, par=par),
        grid=(t // tm,),
        in_specs=[row(D_MODEL), pl.BlockSpec((1, 1, tm), lambda i: (i, 0, 0))] + [_single_spec(c) for c in consts],
        out_specs=[vt_spec if w is None else row(w) for w in out_widths],
        out_shape=[vt_shape if w is None else jax.ShapeDtypeStruct((t, w), BF16) for w in out_widths],
        scratch_shapes=[pltpu.VMEM((tm, GLA_DK), BF16), pltpu.VMEM((tm, GLA_DK), BF16),
                        pltpu.VMEM((tm, GLA_DV), BF16), pltpu.VMEM((tm, GLA_DK), F32),
                        pltpu.VMEM((tm, GLA_DV), BF16), pltpu.VMEM((GLA_HV, GLA_DK), F32),
                        pltpu.VMEM((nslab, pad_rows, GLA_DK), F32), pltpu.VMEM((nslab, pad_rows, GLA_DK), F32)],
        compiler_params=pltpu.CompilerParams(dimension_semantics=("arbitrary",), vmem_limit_bytes=VMEM_LIMIT),
        name="proj_gla",
    )(x2, pos3, *consts)


def _attn_kernel(qi_ref, kj_ref, q_ref, k_ref, vt_ref, *refs, tq, npairs, ncast):
    cast_in, o_ref, cast_out = refs[:ncast], refs[ncast], refs[ncast + 1:2 * ncast + 1]
    acc_ref, ot_ref, m_ref, s0_ref, s1_ref, smax0_ref, smax1_ref = refs[2 * ncast + 1:]
    for src, dst in zip(cast_in, cast_out):
        dst[...] = src[...].astype(BF16)
    nq = q_ref.shape[0] // tq
    key = lax.broadcasted_iota(jnp.int32, (tq, tq), 0)
    qry = lax.broadcasted_iota(jnp.int32, (tq, tq), 1)
    causal = key <= qry
    slots = ((s0_ref, smax0_ref), (s1_ref, smax1_ref))

    def scores(n, slot, heads, masked):
        s_ref, smax_ref = slots[slot]
        qi, kj = qi_ref[n], kj_ref[n]
        qrows = pl.ds(pl.multiple_of(qi * tq, tq), tq)
        krows = pl.ds(pl.multiple_of(kj * tq, tq), tq)
        for h in heads:
            hq = slice(h * HEAD_PAD, (h + 1) * HEAD_PAD)
            s = _dot_nt(k_ref[krows, hq], q_ref[qrows, hq])
            if masked:
                s = jnp.where(causal, s, MASKED)
            s_ref[h] = s
            smax_ref[h:h + 1, :] = jnp.max(s, axis=0, keepdims=True)

    def accumulate(n, slot, heads, first):
        s_ref, smax_ref = slots[slot]
        qi, kj = qi_ref[n], kj_ref[n]
        for h in heads:
            hv = slice(h * MLA_VA, (h + 1) * MLA_VA)
            m_new = smax_ref[h:h + 1, :]
            if not first:
                m = m_ref[qi, h:h + 1, :]
                m_new = jnp.maximum(m, m_new)
                alpha = jnp.exp2(m - m_new)
            p = jnp.exp2(s_ref[h] - m_new)
            m_ref[qi, h:h + 1, :] = m_new
            pv = _dot(vt_ref[kj, hv, :], p.astype(BF16))
            acc_ref[qi, hv, :] = pv if first else alpha * acc_ref[qi, hv, :] + pv

    all_heads = range(MLA_HEADS)

    def step(k, parity, diag_next, diag_now):
        for h in all_heads:
            scores(k + 1, 1 - parity, (h,), diag_next)
            accumulate(k, parity, (h,), diag_now)

    def steps(lo, hi, diag_next, diag_now):
        while (hi - lo) % ATTN_UNROLL:
            step(lo, lo % 2, diag_next, diag_now)
            lo += 1
        if hi > lo:
            def unrolled(t, carry):
                for u in range(ATTN_UNROLL):
                    step(lo + ATTN_UNROLL * t + u, (lo + u) % 2, diag_next, diag_now)
                return carry

            lax.fori_loop(0, (hi - lo) // ATTN_UNROLL, unrolled, 0)

    scores(0, 0, all_heads, True)
    steps(0, nq - 1, True, True)
    if npairs > nq:
        steps(nq - 1, nq, False, True)
        steps(nq, npairs - 1, False, False)
    accumulate(npairs - 1, (npairs - 1) % 2, all_heads, npairs == nq)

    for qt in range(nq):
        for h in range(MLA_HEADS):
            lo = h * MLA_VA
            inv_denominator = 1.0 / acc_ref[qt, lo + MLA_V:lo + MLA_V + 1, :]
            ot_ref[h * MLA_V:(h + 1) * MLA_V, :] = acc_ref[qt, lo:lo + MLA_V, :] * inv_denominator
        o_ref[qt * tq:(qt + 1) * tq, :] = ot_ref[...].T.astype(BF16)


def _attn_call(q, k, vt, casts, batch, seq, tq):
    nq = seq // tq
    for a in casts:
        assert a.shape[0] % (batch * BF16_ROWS) == 0, a.shape
    cast_specs = [pl.BlockSpec((a.shape[0] // batch, a.shape[1]), lambda b, *_: (b, 0)) for a in casts]
    pairs = [(i, i) for i in range(nq)] + [(i, j) for i in range(nq) for j in range(i)]
    qi = jnp.asarray([p[0] for p in pairs], jnp.int32)
    kj = jnp.asarray([p[1] for p in pairs], jnp.int32)
    grid_spec = pltpu.PrefetchScalarGridSpec(
        num_scalar_prefetch=2,
        grid=(batch,),
        in_specs=[pl.BlockSpec((seq, MLA_HEADS * HEAD_PAD), lambda b, *_: (b, 0)),
                  pl.BlockSpec((seq, MLA_HEADS * HEAD_PAD), lambda b, *_: (b, 0)),
                  pl.BlockSpec((nq, MLA_HEADS * MLA_VA, tq), lambda b, *_: (b, 0, 0))] + cast_specs,
        out_specs=[pl.BlockSpec((seq, MLA_WIDTH), lambda b, *_: (b, 0))] + cast_specs,
        scratch_shapes=[pltpu.VMEM((nq, MLA_HEADS * MLA_VA, tq), F32), pltpu.VMEM((MLA_WIDTH, tq), F32),
                        pltpu.VMEM((nq, MLA_HEADS, tq), F32),
                        pltpu.VMEM((MLA_HEADS, tq, tq), F32), pltpu.VMEM((MLA_HEADS, tq, tq), F32),
                        pltpu.VMEM((MLA_HEADS, tq), F32), pltpu.VMEM((MLA_HEADS, tq), F32)])
    return pl.pallas_call(
        functools.partial(_attn_kernel, tq=tq, npairs=len(pairs), ncast=len(casts)),
        grid_spec=grid_spec,
        out_shape=[jax.ShapeDtypeStruct((batch * seq, MLA_WIDTH), BF16)]
                  + [jax.ShapeDtypeStruct(a.shape, BF16) for a in casts],
        compiler_params=pltpu.CompilerParams(dimension_semantics=("parallel",), vmem_limit_bytes=VMEM_LIMIT),
        name="mla_attn",
    )(qi, kj, q, k, vt, *casts)


def _post_kernel(x_ref, oa_ref, ob_ref, sa_ref, sb_ref, woa_ref, wob_ref, wout_ref, lnffn_ref,
                 wg_ref, wu_ref, wd_ref, fnorm_ref, out_ref):
    tm = x_ref.shape[0]
    sub = math.gcd(tm, POST_SUB)
    subs = [dict(rows=slice(r, r + sub)) for r in range(0, tm, sub)]

    def merge(t):
        ya = _dot(oa_ref[t["rows"], :], woa_ref[...])
        yb = _dot(ob_ref[t["rows"], :], wob_ref[...])
        t["mix"] = (sa_ref[t["rows"], :].astype(F32) * ya + sb_ref[t["rows"], :].astype(F32) * yb).astype(BF16)

    def residual(t):
        t["h"] = x_ref[t["rows"], :] + _dot(t["mix"], wout_ref[...])
        t["u"] = _rms(t["h"], lnffn_ref[...]).astype(BF16)

    def hidden(t):
        g = _dot(t["u"], wg_ref[...])
        t["act"] = (g * jax.nn.sigmoid(g) * _dot(t["u"], wu_ref[...])).astype(BF16)

    def output(t):
        out_ref[t["rows"], :] = _rms(t["h"] + _dot(t["act"], wd_ref[...]), fnorm_ref[...])

    for stage in (merge, residual, hidden, output):
        for t in subs:
            stage(t)


def _post_call(x2, oa, ob, sa, sb, woa, wob, wout, lnffn, wg, wu, wd, fnorm, tm):
    t = x2.shape[0]
    row = lambda w: pl.BlockSpec((tm, w), lambda i: (i, 0))
    consts = (woa, wob, wout, lnffn, wg, wu, wd, fnorm)
    return pl.pallas_call(
        _post_kernel,
        grid=(t // tm,),
        in_specs=[row(D_MODEL), row(MLA_WIDTH), row(GLA_DV), row(D_MODEL), row(D_MODEL)]
                 + [_single_spec(c) for c in consts],
        out_specs=row(D_MODEL),
        out_shape=jax.ShapeDtypeStruct((t, D_MODEL), F32),
        compiler_params=pltpu.CompilerParams(dimension_semantics=("parallel",), vmem_limit_bytes=VMEM_LIMIT),
        name="post",
    )(x2, oa, ob, sa, sb, *consts)


def _repack_kernel(wt_ref, o_ref):
    offs = [0]
    for s in IN_SIZES:
        offs.append(offs[-1] + s)
    cols = wt_ref.shape[1]
    dst = 0
    for idx in (0, 1, 3, 4, 5, 7, 8, 9):
        n = IN_SIZES[idx]
        o_ref[dst:dst + n, :] = wt_ref[offs[idx]:offs[idx + 1], :].astype(BF16)
        dst += n
    o_ref[dst:dst + ROPE_LO, :] = jnp.zeros((ROPE_LO, cols), BF16)
    o_ref[dst + ROPE_LO:dst + GLR_LO, :] = wt_ref[offs[2]:offs[3], :].astype(BF16)
    o_ref[dst + GLR_LO:dst + GLR_LO + GLA_RANK, :] = wt_ref[offs[6]:offs[7], :].astype(BF16)
    o_ref[dst + GLR_LO + GLA_RANK:dst + LANES, :] = jnp.zeros((LANES - GLR_LO - GLA_RANK, cols), BF16)


def _repack_call(w_in_t):
    n, d = w_in_t.shape
    cols = _tile(d, REPACK_COLS)
    return pl.pallas_call(
        _repack_kernel,
        grid=(d // cols,),
        in_specs=[pl.BlockSpec((n, cols), lambda i: (0, i))],
        out_specs=pl.BlockSpec((D_IN_PACKED, cols), lambda i: (0, i)),
        out_shape=jax.ShapeDtypeStruct((D_IN_PACKED, d), BF16),
        compiler_params=pltpu.CompilerParams(dimension_semantics=("parallel",), vmem_limit_bytes=VMEM_LIMIT),
        name="repack_w_in",
    )(w_in_t)


def _mla_weights_kernel(wuq_ref, wukv_ref, wg2_ref, wuqp_ref, wkp_ref, wvt_ref, wg2p_ref):
    wuq, wukv = wuq_ref[...], wukv_ref[...]
    q_pad = jnp.zeros((MLA_Q_RANK, HEAD_PAD - MLA_QK), F32)
    k_pad = jnp.zeros((MLA_KV_RANK, HEAD_PAD - MLA_NOPE), F32)
    per_head = MLA_NOPE + MLA_V
    wuqp_ref[...] = jnp.concatenate(
        [p for h in range(MLA_HEADS) for p in (wuq[:, h * MLA_QK:(h + 1) * MLA_QK], q_pad)], axis=1).astype(BF16)
    wkp_ref[...] = jnp.concatenate(
        [p for h in range(MLA_HEADS) for p in (wukv[:, h * per_head:h * per_head + MLA_NOPE], k_pad)],
        axis=1).astype(BF16)
    values = jnp.concatenate([wukv[:, h * per_head + MLA_NOPE:(h + 1) * per_head] for h in range(MLA_HEADS)], axis=1)
    wvt_ref[...] = values.T.astype(BF16)
    wg2p_ref[...] = jnp.concatenate([jnp.zeros((GLR_LO, GLA_DK), F32), wg2_ref[...],
                                     jnp.zeros((LANES - GLR_LO - GLA_RANK, GLA_DK), F32)], axis=0).astype(BF16)


def _pack_weights(w_in, mla_w_uq, mla_w_ukv, gla_w_gate2):
    winp = _repack_call(w_in.T)
    shapes = ((MLA_Q_RANK, MLA_HEADS * HEAD_PAD), (MLA_KV_RANK, MLA_HEADS * HEAD_PAD), (MLA_WIDTH, MLA_KV_RANK),
              (LANES, GLA_DK))
    wuqp, wkp, wvt, wg2p = pl.pallas_call(
        _mla_weights_kernel,
        out_shape=[jax.ShapeDtypeStruct(sh, BF16) for sh in shapes],
        compiler_params=pltpu.CompilerParams(vmem_limit_bytes=VMEM_LIMIT),
        name="mla_weights",
    )(mla_w_uq, mla_w_ukv, gla_w_gate2)
    return winp, wuqp, wkp, wvt, wg2p


def _rope_inv_freq():
    half = ROPE_HALF
    inv = 1.0 / (ROPE_THETA ** (jnp.arange(half, dtype=F32) / half))
    return inv.reshape(half, 1)


def _rope_placement():
    place = np.zeros((4 * ROPE_HALF, 3 * LANES), np.float32)
    for piece in range(2):
        for i in range(ROPE_HALF):
            cos_row = 2 * piece * ROPE_HALF + i
            sin_row = cos_row + ROPE_HALF
            place[cos_row, ROPE_LO + i] = 1.0
            place[cos_row, ROPE_LO + ROPE_HALF + i] = 1.0
            place[sin_row, LANES + ROPE_LO + i] = -1.0
            place[sin_row, 2 * LANES + ROPE_LO + ROPE_HALF + i] = 1.0
    return jnp.asarray(place, BF16)


def _tile(n, pref):
    while n % pref:
        pref //= 2
    return pref


def kernel(x, positions, ln_mix, w_in, mla_norm_cq, mla_w_uq, mla_norm_ckv, mla_w_ukv, mla_w_o, gla_w_gate2,
           gla_b_gate, gla_norm, gla_w_o, w_out, ln_ffn, ffn_w_gate, ffn_w_up, ffn_w_down, final_norm):
    batch, seq, d = x.shape
    assert d == D_MODEL and w_in.shape[0] == 1 and seq % GLA_CHUNK == 0
    t = batch * seq
    x2 = x.reshape(t, d)
    winp, wuqp, wkp, wvt, wg2p = _pack_weights(w_in[0], mla_w_uq[0], mla_w_ukv[0], gla_w_gate2[0])
    r = lambda a: a.reshape(1, -1)

    tq = _tile(seq, ATTN_TILE)
    tm = max(tq, _tile(seq, PROJ_TILE))
    pos3 = positions.reshape(t // tm, 1, tm).astype(F32)
    q, k, vt, sa, sb, ob = _proj_gla_call(
        x2, pos3, r(ln_mix[0]), winp, r(mla_norm_cq[0]), wuqp, r(mla_norm_ckv[0]), wkp, wvt, wg2p,
        r(gla_b_gate[0]), _rope_inv_freq(), _rope_placement(), r(gla_norm[0]), seq, tm, tq)
    post_weights = (mla_w_o[0], gla_w_o[0], w_out[0], ffn_w_gate[0], ffn_w_up[0], ffn_w_down[0])
    oa, woa, wob, wout, wg, wu, wd = _attn_call(q, k, vt, post_weights, batch, seq, tq)
    out = _post_call(x2, oa, ob, sa, sb, woa, wob, wout, r(ln_ffn[0]), wg, wu, wd, r(final_norm),
                     _tile(t, POST_TILE))
    return out.reshape(batch, seq, d)
```

```python
import functools
import math

import jax
import jax.numpy as jnp
import numpy as np
from jax import lax
from jax.experimental import pallas as pl
from jax.experimental.pallas import tpu as pltpu

F32 = jnp.float32
BF16 = jnp.bfloat16

D_MODEL = 1024
MLA_HEADS = 8
MLA_NOPE = 64
MLA_ROPE = 32
MLA_QK = MLA_NOPE + MLA_ROPE
MLA_V = 64
MLA_WIDTH = MLA_HEADS * MLA_V
MLA_Q_RANK = 384
MLA_KV_RANK = 256
ROPE_THETA = 10000.0
GLA_HEADS = 4
GLA_DV = 512
GLA_DK = 256
GLA_HK = GLA_DK // GLA_HEADS
GLA_HV = GLA_DV // GLA_HEADS
GLA_RANK = 16
GLA_TAU = 16.0
EPS = 1e-6
LOG2E = 1.4426950408889634
IN_SIZES = (MLA_Q_RANK, MLA_KV_RANK, MLA_ROPE, GLA_DK, GLA_DK, GLA_DV, GLA_RANK, GLA_DV, D_MODEL, D_MODEL)

LANES = 128
HEAD_PAD = LANES
ROPE_LO = MLA_NOPE
ROPE_HALF = MLA_ROPE // 2
GLR_LO = ROPE_LO + MLA_ROPE

OFF_CQ = 0
OFF_CKV = OFF_CQ + MLA_Q_RANK
OFF_GQ = OFF_CKV + MLA_KV_RANK
OFF_GK = OFF_GQ + GLA_DK
OFF_GV = OFF_GK + GLA_DK
OFF_OG = OFF_GV + GLA_DV
OFF_GA = OFF_OG + GLA_DV
OFF_GB = OFF_GA + D_MODEL
OFF_MISC = OFF_GB + D_MODEL
D_IN_PACKED = OFF_MISC + LANES

REPACK_COLS = 256
PROJ_TILE = 512
POST_TILE = 512
POST_SUB = 256
ATTN_TILE = 256
ATTN_UNROLL = 12
BF16_ROWS = 16
MLA_VA = MLA_V + BF16_ROWS
MASKED = -1e30
GLA_CHUNK = 64
GLA_SUB = 8
GLA_PAR = 4
VMEM_LIMIT = 56 * 1024 * 1024


def _dot(a, b):
    return jnp.dot(a, b, preferred_element_type=F32)


def _dot_nt(a, b):
    return lax.dot_general(a, b, (((1,), (1,)), ((), ())), preferred_element_type=F32)


def _dot_tn(a, b):
    return lax.dot_general(a, b, (((0,), (0,)), ((), ())), preferred_element_type=F32)


def _rms(x, w):
    return x * lax.rsqrt(jnp.mean(x * x, axis=-1, keepdims=True) + EPS) * w


def _single_spec(a):
    return pl.BlockSpec(a.shape, lambda *_: (0,) * a.ndim, pipeline_mode=pl.Buffered(1))


def _gla_constants():
    c_, s_ = GLA_CHUNK, GLA_SUB
    ri = lax.broadcasted_iota(jnp.int32, (c_, c_), 0)
    ci = lax.broadcasted_iota(jnp.int32, (c_, c_), 1)
    rl = lax.broadcasted_iota(jnp.int32, (c_, GLA_DK), 0)
    ll = lax.broadcasted_iota(jnp.int32, (c_, GLA_DK), 1)
    row_minus_col = rl - ll % GLA_HK
    di = lax.broadcasted_iota(jnp.int32, (GLA_DK, GLA_DK), 0) // GLA_HK
    dj = lax.broadcasted_iota(jnp.int32, (GLA_DK, GLA_DK), 1) // GLA_HK
    return dict(
        tri=jnp.where((ri // s_ == ci // s_) & (ci <= ri), 1.0, 0.0).astype(BF16),
        diag_key=jnp.where((row_minus_col >= 0) & (rl % s_ >= row_minus_col), row_minus_col, -1),
        head_of_lane=ll // GLA_HK,
        odd_head_lane=(lax.broadcasted_iota(jnp.int32, (c_, 2 * GLA_HK), 1) // GLA_HK) == 1,
        head_sum=jnp.where(di == dj, 1.0, 0.0).astype(BF16))


def _gla_slab_stages(base, par, consts, gq_ref, gk_ref, gv_ref, la_ref, og_ref, norm_w, o_ref, state_ref,
                     kpad_ref, lapad_ref):
    c_, s_ = GLA_CHUNK, GLA_SUB
    nsub = c_ // s_
    pair = 2 * GLA_HK
    tri, diag_key, head_of_lane = consts["tri"], consts["diag_key"], consts["head_of_lane"]
    odd_head_lane, head_sum = consts["odd_head_lane"], consts["head_sum"]
    chunks = [dict(rows=slice(base + c * c_, base + (c + 1) * c_), off=s_ + c * c_) for c in range(par)]

    def blocks(fn):
        return jnp.concatenate([fn(b) for b in range(nsub)], axis=0)

    def sub(t, b):
        return t[b * s_:(b + 1) * s_, :]

    def cumsums():
        kpad_ref[0:s_, :] = jnp.zeros((s_, GLA_DK), F32)
        lapad_ref[0:s_, :] = jnp.zeros((s_, GLA_DK), F32)
        kpad_ref[s_:, :] = gk_ref[base:base + par * c_, :].astype(F32)
        lapad_ref[s_:, :] = la_ref[base:base + par * c_, :]
        for ch in chunks:
            la = la_ref[ch["rows"], :]
            hi = la.astype(BF16)
            r1 = la - hi.astype(F32)
            mid = r1.astype(BF16)
            lo = (r1 - mid.astype(F32)).astype(BF16)
            ch["lsub"] = _dot(tri, hi) + _dot(tri, mid) + _dot(tri, lo)

    def operands():
        for ch in chunks:
            q = gq_ref[ch["rows"], :].astype(F32)
            k = gk_ref[ch["rows"], :].astype(F32)
            lsub = ch["lsub"]
            tot = [lsub[(b + 1) * s_ - 1:(b + 1) * s_, :] for b in range(nsub)]
            q_sub = q * jnp.exp2(lsub)
            k_sub = k * jnp.exp2(blocks(lambda b: jnp.broadcast_to(tot[b], (s_, GLA_DK))) - lsub)
            zero = jnp.zeros((s_, GLA_DK), F32)
            e = [jnp.exp2(t) for t in tot]

            def decayed(t, b, over):
                piece = sub(t, b)
                for i in over:
                    piece = piece * e[i]
                return piece

            ch["qd"] = blocks(lambda b: decayed(q_sub, b, range(b))).astype(BF16)
            kd = blocks(lambda b: decayed(k_sub, b, range(b + 1, nsub))).astype(BF16)
            ch["decay"] = jnp.exp2(sum(tot[1:], tot[0]))
            v = gv_ref[ch["rows"], :]
            ch["upd"] = sum(_dot_tn(v[:, h * GLA_HV:(h + 1) * GLA_HV], jnp.where(head_of_lane == h, kd, 0.0))
                            for h in range(GLA_HEADS))
            ch["q_lv"], ch["k_lv"] = [], []
            width = 1
            while width < nsub:
                for odd in range(1, nsub // width, 2):
                    q_lo, k_lo = odd * width, (odd - 1) * width
                    ch["q_lv"].append(blocks(lambda b: decayed(q_sub, b, range(q_lo, b))
                                             if q_lo <= b < q_lo + width else zero))
                    ch["k_lv"].append(blocks(lambda b: decayed(k_sub, b, range(b + 1, k_lo + width))
                                             if k_lo <= b < k_lo + width else zero))
                width *= 2
            own = q * k
            own_hi = own.astype(BF16)
            prods = [own_hi]
            decay = jnp.zeros((c_, GLA_DK), F32)
            for d in range(1, s_):
                decay = decay + lapad_ref[ch["off"] - d + 1:ch["off"] - d + 1 + c_, :]
                k_sh = kpad_ref[ch["off"] - d:ch["off"] - d + c_, :]
                prods.append((q * k_sh * jnp.exp2(decay)).astype(BF16))
            prods.append((own - own_hi.astype(F32)).astype(BF16))
            summed = _dot(jnp.concatenate(prods, axis=0), head_sum)
            ch["summed"] = [summed[0:c_, :] + summed[s_ * c_:, :]] + [summed[d * c_:(d + 1) * c_, :]
                                                                        for d in range(1, s_)]

    def pair_scores():
        for ch in chunks:
            ch["attn_off"] = []
            for tile in range(GLA_HEADS // 2):
                lanes = slice(tile * pair, (tile + 1) * pair)

                def split(groups):
                    tiles = [g[:, lanes] for g in groups]
                    return tiles[0::2], [pltpu.roll(g, GLA_HK, axis=1) for g in tiles[1::2]]

                q_even, q_odd = split(ch["q_lv"])
                k_even, k_odd = split(ch["k_lv"])
                for own in (~odd_head_lane, odd_head_lane):
                    def packed(even, odd):
                        return jnp.concatenate(
                            [jnp.where(own, a, odd[n] if n < len(odd) else 0.0).astype(BF16)
                             for n, a in enumerate(even)], axis=1)

                    ch["attn_off"].append(_dot_nt(packed(q_even, q_odd), packed(k_even, k_odd)))

    def outputs():
        state = state_ref[...]
        for ch in chunks:
            ch["state_b"] = state.astype(BF16)
            state = state * ch["decay"] + ch["upd"]
        state_ref[...] = state
        for ch in chunks:
            attn_diag = jnp.zeros((c_, GLA_DK), F32)
            for d in range(s_):
                attn_diag = jnp.where(diag_key == d, ch["summed"][d], attn_diag)
            v = gv_ref[ch["rows"], :]
            for h in range(GLA_HEADS):
                hv = slice(h * GLA_HV, (h + 1) * GLA_HV)
                attn = attn_diag[:, h * GLA_HK:(h + 1) * GLA_HK] + ch["attn_off"][h]
                o = _dot(attn.astype(BF16), v[:, hv])
                o = o + _dot_nt(jnp.where(head_of_lane == h, ch["qd"], 0.0), ch["state_b"])
                o = _rms(o, norm_w)
                g = og_ref[ch["rows"], hv].astype(F32)
                o_ref[ch["rows"], hv] = (o * (g * jax.nn.sigmoid(g))).astype(BF16)

    return [cumsums, operands, pair_scores, outputs]


def _proj_gla_kernel(x_ref, pos_ref, lnmix_ref, win_ref, ncq_ref, wuq_ref, nckv_ref, wk_ref, wvt_ref,
                     wg2_ref, bg_ref, invf_ref, place_ref, gnorm_ref,
                     q_ref, k_ref, vt_ref, sa_ref, sb_ref, ob_ref,
                     gq_ref, gk_ref, gv_ref, la_ref, og_ref, state_ref, kpad_ref, lapad_ref,
                     *, tkv, steps_per_seq, par):
    tm = x_ref.shape[0]

    @pl.when(pl.program_id(0) % steps_per_seq == 0)
    def _():
        state_ref[...] = jnp.zeros_like(state_ref)

    ub = _rms(x_ref[...], lnmix_ref[...]).astype(BF16)

    def seg(lo, width):
        return _dot_nt(ub, win_ref[lo:lo + width, :])

    misc = seg(OFF_MISC, LANES)
    xg = _dot(misc.astype(BF16), wg2_ref[...]) + bg_ref[...]
    la_ref[...] = (jnp.minimum(xg, 0.0) - jnp.log1p(jnp.exp(-jnp.abs(xg)))) * (LOG2E / GLA_TAU)
    gq_ref[...] = (seg(OFF_GQ, GLA_DK) * (GLA_HK ** -0.5)).astype(BF16)
    gk_ref[...] = seg(OFF_GK, GLA_DK).astype(BF16)
    gv_ref[...] = seg(OFF_GV, GLA_DV).astype(BF16)
    og_ref[...] = seg(OFF_OG, GLA_DV).astype(BF16)

    consts = _gla_constants()
    slab_rows = par * GLA_CHUNK
    scan = []
    for sl in range(tm // slab_rows):
        scan += _gla_slab_stages(sl * slab_rows, par, consts, gq_ref, gk_ref, gv_ref, la_ref, og_ref,
                                 gnorm_ref[...], ob_ref, state_ref, kpad_ref.at[sl], lapad_ref.at[sl])

    lane = lax.broadcasted_iota(jnp.int32, (1, LANES), 1)
    in_rope = (lane >= ROPE_LO) & (lane < ROPE_LO + MLA_ROPE)
    t = {}

    def latents():
        t["cq"] = seg(OFF_CQ, MLA_Q_RANK)
        t["ckv"] = seg(OFF_CKV, MLA_KV_RANK)

    def rope_tables():
        ang = invf_ref[...] * pos_ref[0]
        cs = jnp.concatenate([jnp.cos(ang), jnp.sin(ang)], axis=0)
        cs_hi = cs.astype(BF16)
        cs_lo = (cs - cs_hi.astype(F32)).astype(BF16)
        tabs = _dot_tn(jnp.concatenate([cs_hi, cs_lo], axis=0), place_ref[...])
        t["cos"] = tabs[:, :LANES] + jnp.where(in_rope, 0.0, 1.0)
        t["sin_x1"] = tabs[:, LANES:2 * LANES]
        t["sin_x2"] = tabs[:, 2 * LANES:]

    def gate_a():
        sa_ref[...] = jax.nn.sigmoid(seg(OFF_GA, D_MODEL)).astype(BF16)

    def gate_b():
        sb_ref[...] = jax.nn.sigmoid(seg(OFF_GB, D_MODEL)).astype(BF16)

    def latent_norms():
        t["cqn"] = _rms(t["cq"], ncq_ref[...]).astype(BF16)
        t["ckvn"] = _rms(t["ckv"], nckv_ref[...]).astype(BF16)

    def up_q():
        t["q"] = _dot(t["cqn"], wuq_ref[...]) * (MLA_QK ** -0.5 * LOG2E)

    def up_kv():
        t["k_nope"] = _dot(t["ckvn"], wk_ref[...])
        for c in range(tm // tkv):
            vt = _dot_nt(wvt_ref[...], t["ckvn"][c * tkv:(c + 1) * tkv, :]).astype(BF16)
            for h in range(MLA_HEADS):
                vt_ref[c, h * MLA_VA:h * MLA_VA + MLA_V, :] = vt[h * MLA_V:(h + 1) * MLA_V, :]
                vt_ref[c, h * MLA_VA + MLA_V:(h + 1) * MLA_VA, :] = jnp.ones((MLA_VA - MLA_V, tkv), BF16)

    def rope_store():
        def rope(a):
            return (a * t["cos"] + pltpu.roll(a, LANES - ROPE_HALF, axis=1) * t["sin_x1"]
                    + pltpu.roll(a, ROPE_HALF, axis=1) * t["sin_x2"])

        k_rope = rope(jnp.where(in_rope, misc, 0.0))
        for h in range(MLA_HEADS):
            sl = slice(h * HEAD_PAD, (h + 1) * HEAD_PAD)
            q_ref[:, sl] = rope(t["q"][:, sl]).astype(BF16)
            k_ref[:, sl] = (t["k_nope"][:, sl] + k_rope).astype(BF16)

    rest = [gate_a, latents, rope_tables, latent_norms, gate_b, up_q, up_kv, rope_store]
    for n in range(max(len(scan), len(rest))):
        if n < len(scan):
            scan[n]()
        if n < len(rest):
            rest[n]()


def _proj_gla_call(x2, pos3, lnmix, winp, ncq, wuqp, nckv, wkp, wvt, wg2p, bg, invf, place, gnorm, seq, tm, tkv):
    t = x2.shape[0]
    par = math.gcd(tm // GLA_CHUNK, GLA_PAR)
    nslab = tm // (par * GLA_CHUNK)
    row = lambda w: pl.BlockSpec((tm, w), lambda i: (i, 0))
    consts = (lnmix, winp, ncq, wuqp, nckv, wkp, wvt, wg2p, bg, invf, place, gnorm)
    out_widths = (MLA_HEADS * HEAD_PAD, MLA_HEADS * HEAD_PAD, None, D_MODEL, D_MODEL, GLA_DV)
    vt_spec = pl.BlockSpec((tm // tkv, MLA_HEADS * MLA_VA, tkv), lambda i: (i, 0, 0))
    vt_shape = jax.ShapeDtypeStruct((t // tkv, MLA_HEADS * MLA_VA, tkv), BF16)
    pad_rows = GLA_SUB + par * GLA_CHUNK
    return pl.pallas_call(
        functools.partial(_proj_gla_kernel, tkv=tkv, steps_per_seq=seq // tm, par=par),
        grid=(t // tm,),
        in_specs=[row(D_MODEL), pl.BlockSpec((1, 1, tm), lambda i: (i, 0, 0))] + [_single_spec(c) for c in consts],
        out_specs=[vt_spec if w is None else row(w) for w in out_widths],
        out_shape=[vt_shape if w is None else jax.ShapeDtypeStruct((t, w), BF16) for w in out_widths],
        scratch_shapes=[pltpu.VMEM((tm, GLA_DK), BF16), pltpu.VMEM((tm, GLA_DK), BF16),
                        pltpu.VMEM((tm, GLA_DV), BF16), pltpu.VMEM((tm, GLA_DK), F32),
                        pltpu.VMEM((tm, GLA_DV), BF16), pltpu.VMEM((GLA_HV, GLA_DK), F32),
                        pltpu.VMEM((nslab, pad_rows, GLA_DK), F32), pltpu.VMEM((nslab, pad_rows, GLA_DK), F32)],
        compiler_params=pltpu.CompilerParams(dimension_semantics=("arbitrary",), vmem_limit_bytes=VMEM_LIMIT),
        name="proj_gla",
    )(x2, pos3, *consts)


def _attn_kernel(qi_ref, kj_ref, q_ref, k_ref, vt_ref, *refs, tq, npairs, ncast):
    cast_in, o_ref, cast_out = refs[:ncast], refs[ncast], refs[ncast + 1:2 * ncast + 1]
    acc_ref, ot_ref, m_ref, s0_ref, s1_ref, smax0_ref, smax1_ref = refs[2 * ncast + 1:]
    for src, dst in zip(cast_in, cast_out):
        dst[...] = src[...].astype(BF16)
    nq = q_ref.shape[0] // tq
    key = lax.broadcasted_iota(jnp.int32, (tq, tq), 0)
    qry = lax.broadcasted_iota(jnp.int32, (tq, tq), 1)
    causal = key <= qry
    slots = ((s0_ref, smax0_ref), (s1_ref, smax1_ref))

    def scores(n, slot, heads, masked):
        s_ref, smax_ref = slots[slot]
        qi, kj = qi_ref[n], kj_ref[n]
        qrows = pl.ds(pl.multiple_of(qi * tq, tq), tq)
        krows = pl.ds(pl.multiple_of(kj * tq, tq), tq)
        for h in heads:
            hq = slice(h * HEAD_PAD, (h + 1) * HEAD_PAD)
            s = _dot_nt(k_ref[krows, hq], q_ref[qrows, hq])
            if masked:
                s = jnp.where(causal, s, MASKED)
            s_ref[h] = s
            smax_ref[h:h + 1, :] = jnp.max(s, axis=0, keepdims=True)

    def accumulate(n, slot, heads, first):
        s_ref, smax_ref = slots[slot]
        qi, kj = qi_ref[n], kj_ref[n]
        for h in heads:
            hv = slice(h * MLA_VA, (h + 1) * MLA_VA)
            m_new = smax_ref[h:h + 1, :]
            if not first:
                m = m_ref[qi, h:h + 1, :]
                m_new = jnp.maximum(m, m_new)
                alpha = jnp.exp2(m - m_new)
            p = jnp.exp2(s_ref[h] - m_new)
            m_ref[qi, h:h + 1, :] = m_new
            pv = _dot(vt_ref[kj, hv, :], p.astype(BF16))
            acc_ref[qi, hv, :] = pv if first else alpha * acc_ref[qi, hv, :] + pv

    all_heads = range(MLA_HEADS)

    def step(k, parity, diag_next, diag_now):
        for h in all_heads:
            scores(k + 1, 1 - parity, (h,), diag_next)
            accumulate(k, parity, (h,), diag_now)

    def steps(lo, hi, diag_next, diag_now):
        while (hi - lo) % ATTN_UNROLL:
            step(lo, lo % 2, diag_next, diag_now)
            lo += 1
        if hi > lo:
            def unrolled(t, carry):
                for u in range(ATTN_UNROLL):
                    step(lo + ATTN_UNROLL * t + u, (lo + u) % 2, diag_next, diag_now)
                return carry

            lax.fori_loop(0, (hi - lo) // ATTN_UNROLL, unrolled, 0)

    scores(0, 0, all_heads, True)
    steps(0, nq - 1, True, True)
    if npairs > nq:
        steps(nq - 1, nq, False, True)
        steps(nq, npairs - 1, False, False)
    accumulate(npairs - 1, (npairs - 1) % 2, all_heads, npairs == nq)

    for qt in range(nq):
        for h in range(MLA_HEADS):
            lo = h * MLA_VA
            inv_denominator = 1.0 / acc_ref[qt, lo + MLA_V:lo + MLA_V + 1, :]
            ot_ref[h * MLA_V:(h + 1) * MLA_V, :] = acc_ref[qt, lo:lo + MLA_V, :] * inv_denominator
        o_ref[qt * tq:(qt + 1) * tq, :] = ot_ref[...].T.astype(BF16)


def _attn_call(q, k, vt, casts, batch, seq, tq):
    nq = seq // tq
    for a in casts:
        assert a.shape[0] % (batch * BF16_ROWS) == 0, a.shape
    cast_specs = [pl.BlockSpec((a.shape[0] // batch, a.shape[1]), lambda b, *_: (b, 0)) for a in casts]
    pairs = [(i, i) for i in range(nq)] + [(i, j) for i in range(nq) for j in range(i)]
    qi = jnp.asarray([p[0] for p in pairs], jnp.int32)
    kj = jnp.asarray([p[1] for p in pairs], jnp.int32)
    grid_spec = pltpu.PrefetchScalarGridSpec(
        num_scalar_prefetch=2,
        grid=(batch,),
        in_specs=[pl.BlockSpec((seq, MLA_HEADS * HEAD_PAD), lambda b, *_: (b, 0)),
                  pl.BlockSpec((seq, MLA_HEADS * HEAD_PAD), lambda b, *_: (b, 0)),
                  pl.BlockSpec((nq, MLA_HEADS * MLA_VA, tq), lambda b, *_: (b, 0, 0))] + cast_specs,
        out_specs=[pl.BlockSpec((seq, MLA_WIDTH), lambda b, *_: (b, 0))] + cast_specs,
        scratch_shapes=[pltpu.VMEM((nq, MLA_HEADS * MLA_VA, tq), F32), pltpu.VMEM((MLA_WIDTH, tq), F32),
                        pltpu.VMEM((nq, MLA_HEADS, tq), F32),
                        pltpu.VMEM((MLA_HEADS, tq, tq), F32), pltpu.VMEM((MLA_HEADS, tq, tq), F32),
                        pltpu.VMEM((MLA_HEADS, tq), F32), pltpu.VMEM((MLA_HEADS, tq), F32)])
    return pl.pallas_call(
        functools.partial(_attn_kernel, tq=tq, npairs=len(pairs), ncast=len(casts)),
        grid_spec=grid_spec,
        out_shape=[jax.ShapeDtypeStruct((batch * seq, MLA_WIDTH), BF16)]
                  + [jax.ShapeDtypeStruct(a.shape, BF16) for a in casts],
        compiler_params=pltpu.CompilerParams(dimension_semantics=("parallel",), vmem_limit_bytes=VMEM_LIMIT),
        name="mla_attn",
    )(qi, kj, q, k, vt, *casts)


def _post_kernel(x_ref, oa_ref, ob_ref, sa_ref, sb_ref, woa_ref, wob_ref, wout_ref, lnffn_ref,
                 wg_ref, wu_ref, wd_ref, fnorm_ref, out_ref):
    tm = x_ref.shape[0]
    sub = math.gcd(tm, POST_SUB)
    subs = [dict(rows=slice(r, r + sub)) for r in range(0, tm, sub)]

    def merge(t):
        ya = _dot(oa_ref[t["rows"], :], woa_ref[...])
        yb = _dot(ob_ref[t["rows"], :], wob_ref[...])
        t["mix"] = (sa_ref[t["rows"], :].astype(F32) * ya + sb_ref[t["rows"], :].astype(F32) * yb).astype(BF16)

    def residual(t):
        t["h"] = x_ref[t["rows"], :] + _dot(t["mix"], wout_ref[...])
        t["u"] = _rms(t["h"], lnffn_ref[...]).astype(BF16)

    def hidden(t):
        g = _dot(t["u"], wg_ref[...])
        t["act"] = (g * jax.nn.sigmoid(g) * _dot(t["u"], wu_ref[...])).astype(BF16)

    def output(t):
        out_ref[t["rows"], :] = _rms(t["h"] + _dot(t["act"], wd_ref[...]), fnorm_ref[...])

    for stage in (merge, residual, hidden, output):
        for t in subs:
            stage(t)


def _post_call(x2, oa, ob, sa, sb, woa, wob, wout, lnffn, wg, wu, wd, fnorm, tm):
    t = x2.shape[0]
    row = lambda w: pl.BlockSpec((tm, w), lambda i: (i, 0))
    consts = (woa, wob, wout, lnffn, wg, wu, wd, fnorm)
    return pl.pallas_call(
        _post_kernel,
        grid=(t // tm,),
        in_specs=[row(D_MODEL), row(MLA_WIDTH), row(GLA_DV), row(D_MODEL), row(D_MODEL)]
                 + [_single_spec(c) for c in consts],
        out_specs=row(D_MODEL),
        out_shape=jax.ShapeDtypeStruct((t, D_MODEL), F32),
        compiler_params=pltpu.CompilerParams(dimension_semantics=("parallel",), vmem_limit_bytes=VMEM_LIMIT),
        name="post",
    )(x2, oa, ob, sa, sb, *consts)


def _repack_kernel(wt_ref, o_ref):
    offs = [0]
    for s in IN_SIZES:
        offs.append(offs[-1] + s)
    cols = wt_ref.shape[1]
    dst = 0
    for idx in (0, 1, 3, 4, 5, 7, 8, 9):
        n = IN_SIZES[idx]
        o_ref[dst:dst + n, :] = wt_ref[offs[idx]:offs[idx + 1], :].astype(BF16)
        dst += n
    o_ref[dst:dst + ROPE_LO, :] = jnp.zeros((ROPE_LO, cols), BF16)
    o_ref[dst + ROPE_LO:dst + GLR_LO, :] = wt_ref[offs[2]:offs[3], :].astype(BF16)
    o_ref[dst + GLR_LO:dst + GLR_LO + GLA_RANK, :] = wt_ref[offs[6]:offs[7], :].astype(BF16)
    o_ref[dst + GLR_LO + GLA_RANK:dst + LANES, :] = jnp.zeros((LANES - GLR_LO - GLA_RANK, cols), BF16)


def _repack_call(w_in_t):
    n, d = w_in_t.shape
    cols = _tile(d, REPACK_COLS)
    return pl.pallas_call(
        _repack_kernel,
        grid=(d // cols,),
        in_specs=[pl.BlockSpec((n, cols), lambda i: (0, i))],
        out_specs=pl.BlockSpec((D_IN_PACKED, cols), lambda i: (0, i)),
        out_shape=jax.ShapeDtypeStruct((D_IN_PACKED, d), BF16),
        compiler_params=pltpu.CompilerParams(dimension_semantics=("parallel",), vmem_limit_bytes=VMEM_LIMIT),
        name="repack_w_in",
    )(w_in_t)


def _mla_weights_kernel(wuq_ref, wukv_ref, wg2_ref, wuqp_ref, wkp_ref, wvt_ref, wg2p_ref):
    wuq, wukv = wuq_ref[...], wukv_ref[...]
    q_pad = jnp.zeros((MLA_Q_RANK, HEAD_PAD - MLA_QK), F32)
    k_pad = jnp.zeros((MLA_KV_RANK, HEAD_PAD - MLA_NOPE), F32)
    per_head = MLA_NOPE + MLA_V
    wuqp_ref[...] = jnp.concatenate(
        [p for h in range(MLA_HEADS) for p in (wuq[:, h * MLA_QK:(h + 1) * MLA_QK], q_pad)], axis=1).astype(BF16)
    wkp_ref[...] = jnp.concatenate(
        [p for h in range(MLA_HEADS) for p in (wukv[:, h * per_head:h * per_head + MLA_NOPE], k_pad)],
        axis=1).astype(BF16)
    values = jnp.concatenate([wukv[:, h * per_head + MLA_NOPE:(h + 1) * per_head] for h in range(MLA_HEADS)], axis=1)
    wvt_ref[...] = values.T.astype(BF16)
    wg2p_ref[...] = jnp.concatenate([jnp.zeros((GLR_LO, GLA_DK), F32), wg2_ref[...],
                                     jnp.zeros((LANES - GLR_LO - GLA_RANK, GLA_DK), F32)], axis=0).astype(BF16)


def _pack_weights(w_in, mla_w_uq, mla_w_ukv, gla_w_gate2):
    winp = _repack_call(w_in.T)
    shapes = ((MLA_Q_RANK, MLA_HEADS * HEAD_PAD), (MLA_KV_RANK, MLA_HEADS * HEAD_PAD), (MLA_WIDTH, MLA_KV_RANK),
              (LANES, GLA_DK))
    wuqp, wkp, wvt, wg2p = pl.pallas_call(
        _mla_weights_kernel,
        out_shape=[jax.ShapeDtypeStruct(sh, BF16) for sh in shapes],
        compiler_params=pltpu.CompilerParams(vmem_limit_bytes=VMEM_LIMIT),
        name="mla_weights",
    )(mla_w_uq, mla_w_ukv, gla_w_gate2)
    return winp, wuqp, wkp, wvt, wg2p


def _rope_inv_freq():
    half = ROPE_HALF
    inv = 1.0 / (ROPE_THETA ** (jnp.arange(half, dtype=F32) / half))
    return inv.reshape(half, 1)


def _rope_placement():
    place = np.zeros((4 * ROPE_HALF, 3 * LANES), np.float32)
    for piece in range(2):
        for i in range(ROPE_HALF):
            cos_row = 2 * piece * ROPE_HALF + i
            sin_row = cos_row + ROPE_HALF
            place[cos_row, ROPE_LO + i] = 1.0
            place[cos_row, ROPE_LO + ROPE_HALF + i] = 1.0
            place[sin_row, LANES + ROPE_LO + i] = -1.0
            place[sin_row, 2 * LANES + ROPE_LO + ROPE_HALF + i] = 1.0
    return jnp.asarray(place, BF16)


def _tile(n, pref):
    while n % pref:
        pref //= 2
    return pref


def kernel(x, positions, ln_mix, w_in, mla_norm_cq, mla_w_uq, mla_norm_ckv, mla_w_ukv, mla_w_o, gla_w_gate2,
           gla_b_gate, gla_norm, gla_w_o, w_out, ln_ffn, ffn_w_gate, ffn_w_up, ffn_w_down, final_norm):
    batch, seq, d = x.shape
    assert d == D_MODEL and w_in.shape[0] == 1 and seq % GLA_CHUNK == 0
    t = batch * seq
    x2 = x.reshape(t, d)
    winp, wuqp, wkp, wvt, wg2p = _pack_weights(w_in[0], mla_w_uq[0], mla_w_ukv[0], gla_w_gate2[0])
    r = lambda a: a.reshape(1, -1)

    tq = _tile(seq, ATTN_TILE)
    tm = max(tq, _tile(seq, PROJ_TILE))
    pos3 = positions.reshape(t // tm, 1, tm).astype(F32)
    q, k, vt, sa, sb, ob = _proj_gla_call(
        x2, pos3, r(ln_mix[0]), winp, r(mla_norm_cq[0]), wuqp, r(mla_norm_ckv[0]), wkp, wvt, wg2p,
        r(gla_b_gate[0]), _rope_inv_freq(), _rope_placement(), r(gla_norm[0]), seq, tm, tq)
    post_weights = (mla_w_o[0], gla_w_o[0], w_out[0], ffn_w_gate[0], ffn_w_up[0], ffn_w_down[0])
    oa, woa, wob, wout, wg, wu, wd = _attn_call(q, k, vt, post_weights, batch, seq, tq)
    out = _post_call(x2, oa, ob, sa, sb, woa, wob, wout, r(ln_ffn[0]), wg, wu, wd, r(final_norm),
                     _tile(t, POST_TILE))
    return out.reshape(batch, seq, d)
```

```python
import functools
import math

import jax
import jax.numpy as jnp
import numpy as np
from jax import lax
from jax.experimental import pallas as pl
from jax.experimental.pallas import tpu as pltpu

F32 = jnp.float32
BF16 = jnp.bfloat16

D_MODEL = 1024
MLA_HEADS = 8
MLA_NOPE = 64
MLA_ROPE = 32
MLA_QK = MLA_NOPE + MLA_ROPE
MLA_V = 64
MLA_WIDTH = MLA_HEADS * MLA_V
MLA_Q_RANK = 384
MLA_KV_RANK = 256
ROPE_THETA = 10000.0
GLA_HEADS = 4
GLA_DV = 512
GLA_DK = 256
GLA_HK = GLA_DK // GLA_HEADS
GLA_HV = GLA_DV // GLA_HEADS
GLA_RANK = 16
GLA_TAU = 16.0
EPS = 1e-6
LOG2E = 1.4426950408889634
IN_SIZES = (MLA_Q_RANK, MLA_KV_RANK, MLA_ROPE, GLA_DK, GLA_DK, GLA_DV, GLA_RANK, GLA_DV, D_MODEL, D_MODEL)

LANES = 128
HEAD_PAD = LANES
ROPE_LO = MLA_NOPE
ROPE_HALF = MLA_ROPE // 2
GLR_LO = ROPE_LO + MLA_ROPE

OFF_GA = 0
OFF_GB = OFF_GA + D_MODEL
OFF_CQ = OFF_GB + D_MODEL
OFF_CKV = OFF_CQ + MLA_Q_RANK
OFF_GQ = OFF_CKV + MLA_KV_RANK
OFF_GK = OFF_GQ + GLA_DK
OFF_GV = OFF_GK + GLA_DK
OFF_OG = OFF_GV + GLA_DV
OFF_MISC = OFF_OG + GLA_DV
D_IN_PACKED = OFF_MISC + LANES

REPACK_COLS = 256
PROJ_TILE = 512
POST_TILE = 512
POST_SUB = 256
ATTN_TILE = 256
ATTN_UNROLL = 12
BF16_ROWS = 16
MLA_VA = MLA_V + BF16_ROWS
MASKED = -1e30
GLA_CHUNK = 64
GLA_SUB = 8
GLA_PAR = 4
VMEM_LIMIT = 56 * 1024 * 1024


def _dot(a, b):
    return jnp.dot(a, b, preferred_element_type=F32)


def _dot_nt(a, b):
    return lax.dot_general(a, b, (((1,), (1,)), ((), ())), preferred_element_type=F32)


def _dot_tn(a, b):
    return lax.dot_general(a, b, (((0,), (0,)), ((), ())), preferred_element_type=F32)


def _rms(x, w):
    return x * lax.rsqrt(jnp.mean(x * x, axis=-1, keepdims=True) + EPS) * w


def _single_spec(a):
    return pl.BlockSpec(a.shape, lambda *_: (0,) * a.ndim, pipeline_mode=pl.Buffered(1))


def _gla_constants():
    c_, s_ = GLA_CHUNK, GLA_SUB
    ri = lax.broadcasted_iota(jnp.int32, (c_, c_), 0)
    ci = lax.broadcasted_iota(jnp.int32, (c_, c_), 1)
    rl = lax.broadcasted_iota(jnp.int32, (c_, GLA_DK), 0)
    ll = lax.broadcasted_iota(jnp.int32, (c_, GLA_DK), 1)
    row_minus_col = rl - ll % GLA_HK
    di = lax.broadcasted_iota(jnp.int32, (GLA_DK, GLA_DK), 0) // GLA_HK
    dj = lax.broadcasted_iota(jnp.int32, (GLA_DK, GLA_DK), 1) // GLA_HK
    return dict(
        tri=jnp.where((ri // s_ == ci // s_) & (ci <= ri), 1.0, 0.0).astype(BF16),
        diag_key=jnp.where((row_minus_col >= 0) & (rl % s_ >= row_minus_col), row_minus_col, -1),
        head_of_lane=ll // GLA_HK,
        odd_head_lane=(lax.broadcasted_iota(jnp.int32, (c_, 2 * GLA_HK), 1) // GLA_HK) == 1,
        head_sum=jnp.where(di == dj, 1.0, 0.0).astype(BF16))


def _gla_slab_stages(base, par, consts, gq_ref, gk_ref, gv_ref, la_ref, og_ref, norm_w, o_ref, state_ref,
                     kpad_ref, lapad_ref):
    c_, s_ = GLA_CHUNK, GLA_SUB
    nsub = c_ // s_
    pair = 2 * GLA_HK
    tri, diag_key, head_of_lane = consts["tri"], consts["diag_key"], consts["head_of_lane"]
    odd_head_lane, head_sum = consts["odd_head_lane"], consts["head_sum"]
    chunks = [dict(rows=slice(base + c * c_, base + (c + 1) * c_), off=s_ + c * c_) for c in range(par)]

    def blocks(fn):
        return jnp.concatenate([fn(b) for b in range(nsub)], axis=0)

    def sub(t, b):
        return t[b * s_:(b + 1) * s_, :]

    def cumsums():
        kpad_ref[0:s_, :] = jnp.zeros((s_, GLA_DK), F32)
        lapad_ref[0:s_, :] = jnp.zeros((s_, GLA_DK), F32)
        kpad_ref[s_:, :] = gk_ref[base:base + par * c_, :].astype(F32)
        lapad_ref[s_:, :] = la_ref[base:base + par * c_, :]
        for ch in chunks:
            la = la_ref[ch["rows"], :]
            hi = la.astype(BF16)
            r1 = la - hi.astype(F32)
            mid = r1.astype(BF16)
            lo = (r1 - mid.astype(F32)).astype(BF16)
            ch["lsub"] = _dot(tri, hi) + _dot(tri, mid) + _dot(tri, lo)

    def operands():
        for ch in chunks:
            q = gq_ref[ch["rows"], :].astype(F32)
            k = gk_ref[ch["rows"], :].astype(F32)
            lsub = ch["lsub"]
            tot = [lsub[(b + 1) * s_ - 1:(b + 1) * s_, :] for b in range(nsub)]
            q_sub = q * jnp.exp2(lsub)
            k_sub = k * jnp.exp2(blocks(lambda b: jnp.broadcast_to(tot[b], (s_, GLA_DK))) - lsub)
            zero = jnp.zeros((s_, GLA_DK), F32)
            e = [jnp.exp2(t) for t in tot]

            def decayed(t, b, over):
                piece = sub(t, b)
                for i in over:
                    piece = piece * e[i]
                return piece

            ch["qd"] = blocks(lambda b: decayed(q_sub, b, range(b))).astype(BF16)
            kd = blocks(lambda b: decayed(k_sub, b, range(b + 1, nsub))).astype(BF16)
            ch["decay"] = jnp.exp2(sum(tot[1:], tot[0]))
            v = gv_ref[ch["rows"], :]
            ch["upd"] = sum(_dot_tn(v[:, h * GLA_HV:(h + 1) * GLA_HV], jnp.where(head_of_lane == h, kd, 0.0))
                            for h in range(GLA_HEADS))
            ch["q_lv"], ch["k_lv"] = [], []
            width = 1
            while width < nsub:
                for odd in range(1, nsub // width, 2):
                    q_lo, k_lo = odd * width, (odd - 1) * width
                    ch["q_lv"].append(blocks(lambda b: decayed(q_sub, b, range(q_lo, b))
                                             if q_lo <= b < q_lo + width else zero))
                    ch["k_lv"].append(blocks(lambda b: decayed(k_sub, b, range(b + 1, k_lo + width))
                                             if k_lo <= b < k_lo + width else zero))
                width *= 2
            own = q * k
            own_hi = own.astype(BF16)
            prods = [own_hi]
            decay = jnp.zeros((c_, GLA_DK), F32)
            for d in range(1, s_):
                decay = decay + lapad_ref[ch["off"] - d + 1:ch["off"] - d + 1 + c_, :]
                k_sh = kpad_ref[ch["off"] - d:ch["off"] - d + c_, :]
                prods.append((q * k_sh * jnp.exp2(decay)).astype(BF16))
            prods.append((own - own_hi.astype(F32)).astype(BF16))
            summed = _dot(jnp.concatenate(prods, axis=0), head_sum)
            ch["summed"] = [summed[0:c_, :] + summed[s_ * c_:, :]] + [summed[d * c_:(d + 1) * c_, :]
                                                                        for d in range(1, s_)]

    def pair_scores():
        for ch in chunks:
            ch["attn_off"] = []
            for tile in range(GLA_HEADS // 2):
                lanes = slice(tile * pair, (tile + 1) * pair)

                def split(groups):
                    tiles = [g[:, lanes] for g in groups]
                    return tiles[0::2], [pltpu.roll(g, GLA_HK, axis=1) for g in tiles[1::2]]

                q_even, q_odd = split(ch["q_lv"])
                k_even, k_odd = split(ch["k_lv"])
                for own in (~odd_head_lane, odd_head_lane):
                    def packed(even, odd):
                        return jnp.concatenate(
                            [jnp.where(own, a, odd[n] if n < len(odd) else 0.0).astype(BF16)
                             for n, a in enumerate(even)], axis=1)

                    ch["attn_off"].append(_dot_nt(packed(q_even, q_odd), packed(k_even, k_odd)))

    def outputs():
        state = state_ref[...]
        for ch in chunks:
            ch["state_b"] = state.astype(BF16)
            state = state * ch["decay"] + ch["upd"]
        state_ref[...] = state
        for ch in chunks:
            attn_diag = jnp.zeros((c_, GLA_DK), F32)
            for d in range(s_):
                attn_diag = jnp.where(diag_key == d, ch["summed"][d], attn_diag)
            v = gv_ref[ch["rows"], :]
            for h in range(GLA_HEADS):
                hv = slice(h * GLA_HV, (h + 1) * GLA_HV)
                attn = attn_diag[:, h * GLA_HK:(h + 1) * GLA_HK] + ch["attn_off"][h]
                o = _dot(attn.astype(BF16), v[:, hv])
                o = o + _dot_nt(jnp.where(head_of_lane == h, ch["qd"], 0.0), ch["state_b"])
                o = _rms(o, norm_w)
                g = og_ref[ch["rows"], hv].astype(F32)
                o_ref[ch["rows"], hv] = (o * (g * jax.nn.sigmoid(g))).astype(BF16)

    return [cumsums, operands, pair_scores, outputs]


def _proj_gla_kernel(x_ref, pos_ref, lnmix_ref, win_ref, ncq_ref, wuq_ref, nckv_ref, wk_ref, wvt_ref,
                     wg2_ref, bg_ref, invf_ref, place_ref, gnorm_ref,
                     q_ref, k_ref, vt_ref, ob_ref,
                     gq_ref, gk_ref, gv_ref, la_ref, og_ref, state_ref, kpad_ref, lapad_ref,
                     *, tkv, steps_per_seq, par):
    tm = x_ref.shape[0]

    @pl.when(pl.program_id(0) % steps_per_seq == 0)
    def _():
        state_ref[...] = jnp.zeros_like(state_ref)

    ub = _rms(x_ref[...], lnmix_ref[...]).astype(BF16)

    def seg(lo, width):
        return _dot_nt(ub, win_ref[lo:lo + width, :])

    misc = seg(OFF_MISC, LANES)
    xg = _dot(misc.astype(BF16), wg2_ref[...]) + bg_ref[...]
    la_ref[...] = (jnp.minimum(xg, 0.0) - jnp.log1p(jnp.exp(-jnp.abs(xg)))) * (LOG2E / GLA_TAU)
    gq_ref[...] = (seg(OFF_GQ, GLA_DK) * (GLA_HK ** -0.5)).astype(BF16)
    gk_ref[...] = seg(OFF_GK, GLA_DK).astype(BF16)
    gv_ref[...] = seg(OFF_GV, GLA_DV).astype(BF16)
    og_ref[...] = seg(OFF_OG, GLA_DV).astype(BF16)

    consts = _gla_constants()
    slab_rows = par * GLA_CHUNK
    scan = []
    for sl in range(tm // slab_rows):
        scan += _gla_slab_stages(sl * slab_rows, par, consts, gq_ref, gk_ref, gv_ref, la_ref, og_ref,
                                 gnorm_ref[...], ob_ref, state_ref, kpad_ref.at[sl], lapad_ref.at[sl])

    lane = lax.broadcasted_iota(jnp.int32, (1, LANES), 1)
    in_rope = (lane >= ROPE_LO) & (lane < ROPE_LO + MLA_ROPE)
    t = {}

    def latents():
        t["cq"] = seg(OFF_CQ, MLA_Q_RANK)
        t["ckv"] = seg(OFF_CKV, MLA_KV_RANK)

    def rope_tables():
        ang = invf_ref[...] * pos_ref[0]
        cs = jnp.concatenate([jnp.cos(ang), jnp.sin(ang)], axis=0)
        cs_hi = cs.astype(BF16)
        cs_lo = (cs - cs_hi.astype(F32)).astype(BF16)
        tabs = _dot_tn(jnp.concatenate([cs_hi, cs_lo], axis=0), place_ref[...])
        t["cos"] = tabs[:, :LANES] + jnp.where(in_rope, 0.0, 1.0)
        t["sin_x1"] = tabs[:, LANES:2 * LANES]
        t["sin_x2"] = tabs[:, 2 * LANES:]

    def latent_norms():
        t["cqn"] = _rms(t["cq"], ncq_ref[...]).astype(BF16)
        t["ckvn"] = _rms(t["ckv"], nckv_ref[...]).astype(BF16)

    def up_q():
        t["q"] = _dot(t["cqn"], wuq_ref[...]) * (MLA_QK ** -0.5 * LOG2E)

    def up_kv():
        t["k_nope"] = _dot(t["ckvn"], wk_ref[...])
        for c in range(tm // tkv):
            vt = _dot_nt(wvt_ref[...], t["ckvn"][c * tkv:(c + 1) * tkv, :]).astype(BF16)
            for h in range(MLA_HEADS):
                vt_ref[c, h * MLA_VA:h * MLA_VA + MLA_V, :] = vt[h * MLA_V:(h + 1) * MLA_V, :]
                vt_ref[c, h * MLA_VA + MLA_V:(h + 1) * MLA_VA, :] = jnp.ones((MLA_VA - MLA_V, tkv), BF16)

    def rope_store():
        def rope(a):
            return (a * t["cos"] + pltpu.roll(a, LANES - ROPE_HALF, axis=1) * t["sin_x1"]
                    + pltpu.roll(a, ROPE_HALF, axis=1) * t["sin_x2"])

        k_rope = rope(jnp.where(in_rope, misc, 0.0))
        for h in range(MLA_HEADS):
            sl = slice(h * HEAD_PAD, (h + 1) * HEAD_PAD)
            q_ref[:, sl] = rope(t["q"][:, sl]).astype(BF16)
            k_ref[:, sl] = (t["k_nope"][:, sl] + k_rope).astype(BF16)

    rest = [latents, rope_tables, latent_norms, up_q, up_kv, rope_store]
    for n in range(max(len(scan), len(rest))):
        if n < len(scan):
            scan[n]()
        if n < len(rest):
            rest[n]()


def _proj_gla_call(x2, pos3, lnmix, winp, ncq, wuqp, nckv, wkp, wvt, wg2p, bg, invf, place, gnorm, seq, tm, tkv):
    t = x2.shape[0]
    par = math.gcd(tm // GLA_CHUNK, GLA_PAR)
    nslab = tm // (par * GLA_CHUNK)
    row = lambda w: pl.BlockSpec((tm, w), lambda i: (i, 0))
    consts = (lnmix, winp, ncq, wuqp, nckv, wkp, wvt, wg2p, bg, invf, place, gnorm)
    out_widths = (MLA_HEADS * HEAD_PAD, MLA_HEADS * HEAD_PAD, None, GLA_DV)
    vt_spec = pl.BlockSpec((tm // tkv, MLA_HEADS * MLA_VA, tkv), lambda i: (i, 0, 0))
    vt_shape = jax.ShapeDtypeStruct((t // tkv, MLA_HEADS * MLA_VA, tkv), BF16)
    pad_rows = GLA_SUB + par * GLA_CHUNK
    return pl.pallas_call(
        functools.partial(_proj_gla_kernel, tkv=tkv, steps_per_seq=seq // tm, par=par),
        grid=(t // tm,),
        in_specs=[row(D_MODEL), pl.BlockSpec((1, 1, tm), lambda i: (i, 0, 0))] + [_single_spec(c) for c in consts],
        out_specs=[vt_spec if w is None else row(w) for w in out_widths],
        out_shape=[vt_shape if w is None else jax.ShapeDtypeStruct((t, w), BF16) for w in out_widths],
        scratch_shapes=[pltpu.VMEM((tm, GLA_DK), BF16), pltpu.VMEM((tm, GLA_DK), BF16),
                        pltpu.VMEM((tm, GLA_DV), BF16), pltpu.VMEM((tm, GLA_DK), F32),
                        pltpu.VMEM((tm, GLA_DV), BF16), pltpu.VMEM((GLA_HV, GLA_DK), F32),
                        pltpu.VMEM((nslab, pad_rows, GLA_DK), F32), pltpu.VMEM((nslab, pad_rows, GLA_DK), F32)],
        compiler_params=pltpu.CompilerParams(dimension_semantics=("arbitrary",), vmem_limit_bytes=VMEM_LIMIT),
        name="proj_gla",
    )(x2, pos3, *consts)


def _attn_kernel(qi_ref, kj_ref, q_ref, k_ref, vt_ref, *refs, tq, npairs, ncast):
    cast_in, o_ref, cast_out = refs[:ncast], refs[ncast], refs[ncast + 1:2 * ncast + 1]
    acc_ref, ot_ref, m_ref, s0_ref, s1_ref, smax0_ref, smax1_ref = refs[2 * ncast + 1:]
    for src, dst in zip(cast_in, cast_out):
        dst[...] = src[...].astype(BF16)
    nq = q_ref.shape[0] // tq
    key = lax.broadcasted_iota(jnp.int32, (tq, tq), 0)
    qry = lax.broadcasted_iota(jnp.int32, (tq, tq), 1)
    causal = key <= qry
    slots = ((s0_ref, smax0_ref), (s1_ref, smax1_ref))

    def scores(n, slot, heads, masked):
        s_ref, smax_ref = slots[slot]
        qi, kj = qi_ref[n], kj_ref[n]
        qrows = pl.ds(pl.multiple_of(qi * tq, tq), tq)
        krows = pl.ds(pl.multiple_of(kj * tq, tq), tq)
        for h in heads:
            hq = slice(h * HEAD_PAD, (h + 1) * HEAD_PAD)
            s = _dot_nt(k_ref[krows, hq], q_ref[qrows, hq])
            if masked:
                s = jnp.where(causal, s, MASKED)
            s_ref[h] = s
            smax_ref[h:h + 1, :] = jnp.max(s, axis=0, keepdims=True)

    def accumulate(n, slot, heads, first):
        s_ref, smax_ref = slots[slot]
        qi, kj = qi_ref[n], kj_ref[n]
        for h in heads:
            hv = slice(h * MLA_VA, (h + 1) * MLA_VA)
            m_new = smax_ref[h:h + 1, :]
            if not first:
                m = m_ref[qi, h:h + 1, :]
                m_new = jnp.maximum(m, m_new)
                alpha = jnp.exp2(m - m_new)
            p = jnp.exp2(s_ref[h] - m_new)
            m_ref[qi, h:h + 1, :] = m_new
            pv = _dot(vt_ref[kj, hv, :], p.astype(BF16))
            acc_ref[qi, hv, :] = pv if first else alpha * acc_ref[qi, hv, :] + pv

    all_heads = range(MLA_HEADS)

    def step(k, parity, diag_next, diag_now):
        for h in all_heads:
            scores(k + 1, 1 - parity, (h,), diag_next)
            accumulate(k, parity, (h,), diag_now)

    def steps(lo, hi, diag_next, diag_now):
        while (hi - lo) % ATTN_UNROLL:
            step(lo, lo % 2, diag_next, diag_now)
            lo += 1
        if hi > lo:
            def unrolled(t, carry):
                for u in range(ATTN_UNROLL):
                    step(lo + ATTN_UNROLL * t + u, (lo + u) % 2, diag_next, diag_now)
                return carry

            lax.fori_loop(0, (hi - lo) // ATTN_UNROLL, unrolled, 0)

    scores(0, 0, all_heads, True)
    steps(0, nq - 1, True, True)
    if npairs > nq:
        steps(nq - 1, nq, False, True)
        steps(nq, npairs - 1, False, False)
    accumulate(npairs - 1, (npairs - 1) % 2, all_heads, npairs == nq)

    for qt in range(nq):
        for h in range(MLA_HEADS):
            lo = h * MLA_VA
            inv_denominator = 1.0 / acc_ref[qt, lo + MLA_V:lo + MLA_V + 1, :]
            ot_ref[h * MLA_V:(h + 1) * MLA_V, :] = acc_ref[qt, lo:lo + MLA_V, :] * inv_denominator
        o_ref[qt * tq:(qt + 1) * tq, :] = ot_ref[...].T.astype(BF16)


def _attn_call(q, k, vt, casts, batch, seq, tq):
    nq = seq // tq
    for a in casts:
        assert a.shape[0] % (batch * BF16_ROWS) == 0, a.shape
    cast_specs = [pl.BlockSpec((a.shape[0] // batch, a.shape[1]), lambda b, *_: (b, 0)) for a in casts]
    pairs = [(i, i) for i in range(nq)] + [(i, j) for i in range(nq) for j in range(i)]
    qi = jnp.asarray([p[0] for p in pairs], jnp.int32)
    kj = jnp.asarray([p[1] for p in pairs], jnp.int32)
    grid_spec = pltpu.PrefetchScalarGridSpec(
        num_scalar_prefetch=2,
        grid=(batch,),
        in_specs=[pl.BlockSpec((seq, MLA_HEADS * HEAD_PAD), lambda b, *_: (b, 0)),
                  pl.BlockSpec((seq, MLA_HEADS * HEAD_PAD), lambda b, *_: (b, 0)),
                  pl.BlockSpec((nq, MLA_HEADS * MLA_VA, tq), lambda b, *_: (b, 0, 0))] + cast_specs,
        out_specs=[pl.BlockSpec((seq, MLA_WIDTH), lambda b, *_: (b, 0))] + cast_specs,
        scratch_shapes=[pltpu.VMEM((nq, MLA_HEADS * MLA_VA, tq), F32), pltpu.VMEM((MLA_WIDTH, tq), F32),
                        pltpu.VMEM((nq, MLA_HEADS, tq), F32),
                        pltpu.VMEM((MLA_HEADS, tq, tq), F32), pltpu.VMEM((MLA_HEADS, tq, tq), F32),
                        pltpu.VMEM((MLA_HEADS, tq), F32), pltpu.VMEM((MLA_HEADS, tq), F32)])
    return pl.pallas_call(
        functools.partial(_attn_kernel, tq=tq, npairs=len(pairs), ncast=len(casts)),
        grid_spec=grid_spec,
        out_shape=[jax.ShapeDtypeStruct((batch * seq, MLA_WIDTH), BF16)]
                  + [jax.ShapeDtypeStruct(a.shape, BF16) for a in casts],
        compiler_params=pltpu.CompilerParams(dimension_semantics=("parallel",), vmem_limit_bytes=VMEM_LIMIT),
        name="mla_attn",
    )(qi, kj, q, k, vt, *casts)


def _post_kernel(x_ref, oa_ref, ob_ref, lnmix_ref, wgates_ref, woa_ref, wob_ref, wout_ref, lnffn_ref,
                 wg_ref, wu_ref, wd_ref, fnorm_ref, out_ref):
    tm = x_ref.shape[0]
    sub = math.gcd(tm, POST_SUB)
    subs = [dict(rows=slice(r, r + sub)) for r in range(0, tm, sub)]

    def merge(t):
        ub = _rms(x_ref[t["rows"], :], lnmix_ref[...]).astype(BF16)
        ya = _dot(oa_ref[t["rows"], :], woa_ref[...])
        yb = _dot(ob_ref[t["rows"], :], wob_ref[...])
        gate_a = jax.nn.sigmoid(_dot_nt(ub, wgates_ref[OFF_GA:OFF_GA + D_MODEL, :]))
        gate_b = jax.nn.sigmoid(_dot_nt(ub, wgates_ref[OFF_GB:OFF_GB + D_MODEL, :]))
        t["mix"] = (gate_a * ya + gate_b * yb).astype(BF16)

    def residual(t):
        t["h"] = x_ref[t["rows"], :] + _dot(t["mix"], wout_ref[...])
        t["u"] = _rms(t["h"], lnffn_ref[...]).astype(BF16)

    def hidden(t):
        g = _dot(t["u"], wg_ref[...])
        t["act"] = (g * jax.nn.sigmoid(g) * _dot(t["u"], wu_ref[...])).astype(BF16)

    def output(t):
        out_ref[t["rows"], :] = _rms(t["h"] + _dot(t["act"], wd_ref[...]), fnorm_ref[...])

    for stage in (merge, residual, hidden, output):
        for t in subs:
            stage(t)


def _post_call(x2, oa, ob, lnmix, winp, woa, wob, wout, lnffn, wg, wu, wd, fnorm, tm):
    t = x2.shape[0]
    row = lambda w: pl.BlockSpec((tm, w), lambda i: (i, 0))
    consts = (woa, wob, wout, lnffn, wg, wu, wd, fnorm)
    gates_spec = pl.BlockSpec((2 * D_MODEL, D_MODEL), lambda i: (0, 0), pipeline_mode=pl.Buffered(1))
    return pl.pallas_call(
        _post_kernel,
        grid=(t // tm,),
        in_specs=[row(D_MODEL), row(MLA_WIDTH), row(GLA_DV), _single_spec(lnmix), gates_spec]
                 + [_single_spec(c) for c in consts],
        out_specs=row(D_MODEL),
        out_shape=jax.ShapeDtypeStruct((t, D_MODEL), F32),
        compiler_params=pltpu.CompilerParams(dimension_semantics=("parallel",), vmem_limit_bytes=VMEM_LIMIT),
        name="post",
    )(x2, oa, ob, lnmix, winp, *consts)


def _repack_kernel(wt_ref, o_ref):
    offs = [0]
    for s in IN_SIZES:
        offs.append(offs[-1] + s)
    cols = wt_ref.shape[1]
    dst = 0
    for idx in (8, 9, 0, 1, 3, 4, 5, 7):
        n = IN_SIZES[idx]
        o_ref[dst:dst + n, :] = wt_ref[offs[idx]:offs[idx + 1], :].astype(BF16)
        dst += n
    o_ref[dst:dst + ROPE_LO, :] = jnp.zeros((ROPE_LO, cols), BF16)
    o_ref[dst + ROPE_LO:dst + GLR_LO, :] = wt_ref[offs[2]:offs[3], :].astype(BF16)
    o_ref[dst + GLR_LO:dst + GLR_LO + GLA_RANK, :] = wt_ref[offs[6]:offs[7], :].astype(BF16)
    o_ref[dst + GLR_LO + GLA_RANK:dst + LANES, :] = jnp.zeros((LANES - GLR_LO - GLA_RANK, cols), BF16)


def _repack_call(w_in_t):
    n, d = w_in_t.shape
    cols = _tile(d, REPACK_COLS)
    return pl.pallas_call(
        _repack_kernel,
        grid=(d // cols,),
        in_specs=[pl.BlockSpec((n, cols), lambda i: (0, i))],
        out_specs=pl.BlockSpec((D_IN_PACKED, cols), lambda i: (0, i)),
        out_shape=jax.ShapeDtypeStruct((D_IN_PACKED, d), BF16),
        compiler_params=pltpu.CompilerParams(dimension_semantics=("parallel",), vmem_limit_bytes=VMEM_LIMIT),
        name="repack_w_in",
    )(w_in_t)


def _mla_weights_kernel(wuq_ref, wukv_ref, wg2_ref, wuqp_ref, wkp_ref, wvt_ref, wg2p_ref):
    wuq, wukv = wuq_ref[...], wukv_ref[...]
    q_pad = jnp.zeros((MLA_Q_RANK, HEAD_PAD - MLA_QK), F32)
    k_pad = jnp.zeros((MLA_KV_RANK, HEAD_PAD - MLA_NOPE), F32)
    per_head = MLA_NOPE + MLA_V
    wuqp_ref[...] = jnp.concatenate(
        [p for h in range(MLA_HEADS) for p in (wuq[:, h * MLA_QK:(h + 1) * MLA_QK], q_pad)], axis=1).astype(BF16)
    wkp_ref[...] = jnp.concatenate(
        [p for h in range(MLA_HEADS) for p in (wukv[:, h * per_head:h * per_head + MLA_NOPE], k_pad)],
        axis=1).astype(BF16)
    values = jnp.concatenate([wukv[:, h * per_head + MLA_NOPE:(h + 1) * per_head] for h in range(MLA_HEADS)], axis=1)
    wvt_ref[...] = values.T.astype(BF16)
    wg2p_ref[...] = jnp.concatenate([jnp.zeros((GLR_LO, GLA_DK), F32), wg2_ref[...],
                                     jnp.zeros((LANES - GLR_LO - GLA_RANK, GLA_DK), F32)], axis=0).astype(BF16)


def _pack_weights(w_in, mla_w_uq, mla_w_ukv, gla_w_gate2):
    winp = _repack_call(w_in.T)
    shapes = ((MLA_Q_RANK, MLA_HEADS * HEAD_PAD), (MLA_KV_RANK, MLA_HEADS * HEAD_PAD), (MLA_WIDTH, MLA_KV_RANK),
              (LANES, GLA_DK))
    wuqp, wkp, wvt, wg2p = pl.pallas_call(
        _mla_weights_kernel,
        out_shape=[jax.ShapeDtypeStruct(sh, BF16) for sh in shapes],
        compiler_params=pltpu.CompilerParams(vmem_limit_bytes=VMEM_LIMIT),
        name="mla_weights",
    )(mla_w_uq, mla_w_ukv, gla_w_gate2)
    return winp, wuqp, wkp, wvt, wg2p


def _rope_inv_freq():
    half = ROPE_HALF
    inv = 1.0 / (ROPE_THETA ** (jnp.arange(half, dtype=F32) / half))
    return inv.reshape(half, 1)


def _rope_placement():
    place = np.zeros((4 * ROPE_HALF, 3 * LANES), np.float32)
    for piece in range(2):
        for i in range(ROPE_HALF):
            cos_row = 2 * piece * ROPE_HALF + i
            sin_row = cos_row + ROPE_HALF
            place[cos_row, ROPE_LO + i] = 1.0
            place[cos_row, ROPE_LO + ROPE_HALF + i] = 1.0
            place[sin_row, LANES + ROPE_LO + i] = -1.0
            place[sin_row, 2 * LANES + ROPE_LO + ROPE_HALF + i] = 1.0
    return jnp.asarray(place, BF16)


def _tile(n, pref):
    while n % pref:
        pref //= 2
    return pref


def kernel(x, positions, ln_mix, w_in, mla_norm_cq, mla_w_uq, mla_norm_ckv, mla_w_ukv, mla_w_o, gla_w_gate2,
           gla_b_gate, gla_norm, gla_w_o, w_out, ln_ffn, ffn_w_gate, ffn_w_up, ffn_w_down, final_norm):
    batch, seq, d = x.shape
    assert d == D_MODEL and w_in.shape[0] == 1 and seq % GLA_CHUNK == 0
    t = batch * seq
    x2 = x.reshape(t, d)
    winp, wuqp, wkp, wvt, wg2p = _pack_weights(w_in[0], mla_w_uq[0], mla_w_ukv[0], gla_w_gate2[0])
    r = lambda a: a.reshape(1, -1)

    tq = _tile(seq, ATTN_TILE)
    tm = max(tq, _tile(seq, PROJ_TILE))
    pos3 = positions.reshape(t // tm, 1, tm).astype(F32)
    q, k, vt, ob = _proj_gla_call(
        x2, pos3, r(ln_mix[0]), winp, r(mla_norm_cq[0]), wuqp, r(mla_norm_ckv[0]), wkp, wvt, wg2p,
        r(gla_b_gate[0]), _rope_inv_freq(), _rope_placement(), r(gla_norm[0]), seq, tm, tq)
    post_weights = (mla_w_o[0], gla_w_o[0], w_out[0], ffn_w_gate[0], ffn_w_up[0], ffn_w_down[0])
    oa, woa, wob, wout, wg, wu, wd = _attn_call(q, k, vt, post_weights, batch, seq, tq)
    out = _post_call(x2, oa, ob, r(ln_mix[0]), winp, woa, wob, wout, r(ln_ffn[0]), wg, wu, wd, r(final_norm),
                     _tile(t, POST_TILE))
    return out.reshape(batch, seq, d)
```
